```python
import math
import jax, jax.numpy as jnp
from jax import lax
import numpy as np

D_MODEL = 1024
BATCH = 4
SEQ = 4096
DEPTH = 2
DEC_BATCH = 128
DEC_SEQ = 1
PAST_LEN = 2048
PAGE_SIZE = 128

D_MIX = D_MODEL
D_RNN = D_MIX // 2
RNN_BLOCKS = 8
RNN_BLOCK_W = D_RNN // RNN_BLOCKS
CONV_W = 4
LRU_C = 8.0
D_ATT = D_MIX - D_RNN
HEAD_DIM = 64
N_DIFF_HEADS = D_ATT // (2 * HEAD_DIM)
N_QK_HEADS = 2 * N_DIFF_HEADS
V_DIM = 2 * HEAD_DIM
ROPE_THETA = 10000.0
D_PROJ = 2 * D_RNN + 3 * D_ATT
N_GROUPS = 4
EXPERTS_PER_GROUP = 8
N_EXPERTS = N_GROUPS * EXPERTS_PER_GROUP
TOP_K = 2
D_EXPERT = 512
Q_BLOCK = 128
DISPATCH_BLOCK = 128
NORM_EPS = 1e-6
SUBLN_EPS = 1e-5
NEG_INF = -1e30

kernel_name = 'hymba_rglru_diffattn_hmoe_step'


def rmsnorm(x, g, eps=NORM_EPS):
    xf = x.astype(jnp.float32)
    y = xf * lax.rsqrt(jnp.mean(xf * xf, axis=-1, keepdims=True) + eps)
    return (y * g.astype(jnp.float32)).astype(x.dtype)


def rope(x, pos):
    half = HEAD_DIM // 2
    inv = ROPE_THETA ** (-jnp.arange(half, dtype=jnp.float32) / half)
    ang = pos.astype(jnp.float32)[:, None] * inv[None, :]
    cos = jnp.cos(ang)[None, :, None, :]
    sin = jnp.sin(ang)[None, :, None, :]
    xf = x.astype(jnp.float32)
    x1, x2 = xf[..., :half], xf[..., half:]
    return jnp.concatenate([x1 * cos - x2 * sin, x2 * cos + x1 * sin], axis=-1).astype(x.dtype)


def causal_conv(x, buf, w, b):
    S = x.shape[1]
    xp = jnp.concatenate([buf.astype(x.dtype), x], axis=1)
    y = b.astype(x.dtype)
    for j in range(CONV_W):
        y = y + xp[:, j:j + S] * w[j]
    return y, xp[:, -(CONV_W - 1):]


def rg_lru(xc, pos, h0, w_ga, b_ga, w_gx, b_gx, lru_lambda):
    B, S, _ = xc.shape
    xb = xc.reshape(B, S, RNN_BLOCKS, RNN_BLOCK_W)
    r = jax.nn.sigmoid((jnp.einsum('bsni,nij->bsnj', xb, w_ga) + b_ga).astype(jnp.float32)).reshape(B, S, D_RNN)
    i = jax.nn.sigmoid((jnp.einsum('bsni,nij->bsnj', xb, w_gx) + b_gx).astype(jnp.float32)).reshape(B, S, D_RNN)
    log_a = -LRU_C * r * jax.nn.softplus(-lru_lambda.astype(jnp.float32))
    a = jnp.exp(log_a)
    mult = jnp.sqrt(-jnp.expm1(2.0 * log_a))
    mult = jnp.where((pos == 0)[None, :, None], 1.0, mult)
    bt = mult * i * xc.astype(jnp.float32)
    bt = bt.at[:, 0].add(a[:, 0] * h0.astype(jnp.float32))

    def comb(lhs, rhs):
        a1, b1 = lhs
        a2, b2 = rhs
        return a1 * a2, a2 * b1 + b2

    _, hs = lax.associative_scan(comb, (a, bt), axis=1)
    return hs, hs[:, -1]


def diff_attn_block(q, qpos, k, v, kpos, lam):
    B, Q = q.shape[:2]
    s = jnp.einsum('bqhd,bkhd->bhqk', q, k).astype(jnp.float32) * (HEAD_DIM ** -0.5)
    s = jnp.where((kpos[None, :] <= qpos[:, None])[None, None], s, NEG_INF)
    p = jax.nn.softmax(s, axis=-1).reshape(B, N_DIFF_HEADS, 2, Q, -1)
    amap = p[:, :, 0] - lam * p[:, :, 1]
    return jnp.einsum('bgqk,bkgd->bqgd', amap.astype(v.dtype), v)


def prompt_attention(q, k, v, pos, lam):
    B, S = q.shape[:2]
    nb = S // Q_BLOCK
    qb = q.reshape(B, nb, Q_BLOCK, N_QK_HEADS, HEAD_DIM).transpose(1, 0, 2, 3, 4)
    pb = pos.reshape(nb, Q_BLOCK)
    out = lax.map(lambda a: diff_attn_block(a[0], a[1], k, v, pos, lam), (qb, pb))
    return out.transpose(1, 0, 2, 3, 4).reshape(B, S, N_DIFF_HEADS, V_DIM)


def mixer(x, pos, conv_buf, h0, attend, ln1_g, w_in, conv_w, conv_b, w_ga, b_ga, w_gx, b_gx,
          lru_lambda, lam_q1, lam_k1, lam_q2, lam_k2, subln_g, lambda_init, w_out):
    B, S, _ = x.shape
    h = rmsnorm(x, ln1_g)
    proj = h @ w_in
    x_rnn, g_rnn, q, k, v = jnp.split(proj, [D_RNN, 2 * D_RNN, 2 * D_RNN + D_ATT, 2 * D_RNN + 2 * D_ATT], axis=-1)
    xc, new_buf = causal_conv(x_rnn, conv_buf, conv_w, conv_b)
    hs, h_last = rg_lru(xc, pos, h0, w_ga, b_ga, w_gx, b_gx, lru_lambda)
    y_rnn = hs.astype(x.dtype) * jax.nn.gelu(g_rnn)
    q = rope(q.reshape(B, S, N_QK_HEADS, HEAD_DIM), pos)
    k = rope(k.reshape(B, S, N_QK_HEADS, HEAD_DIM), pos)
    v = v.reshape(B, S, N_DIFF_HEADS, V_DIM)
    lam = (jnp.exp(jnp.sum(lam_q1.astype(jnp.float32) * lam_k1.astype(jnp.float32)))
           - jnp.exp(jnp.sum(lam_q2.astype(jnp.float32) * lam_k2.astype(jnp.float32))) + lambda_init)
    o = attend(q, k, v, lam)
    o = rmsnorm(o, subln_g, SUBLN_EPS) * (1.0 - lambda_init)
    y = jnp.concatenate([y_rnn, o.reshape(B, S, D_ATT)], axis=-1) @ w_out
    return x + y, k, v, h_last, new_buf


def hier_route(h, w_rg, b_rg, w_re, b_re):
    N = h.shape[0]
    hf = h.astype(jnp.float32)
    gp = jax.nn.softmax(hf @ w_rg.astype(jnp.float32) + b_rg.astype(jnp.float32), axis=-1)
    g = jnp.argmax(gp, axis=-1).astype(jnp.int32)
    pg = jnp.max(gp, axis=-1)
    el = (hf @ w_re.astype(jnp.float32) + b_re.astype(jnp.float32)).reshape(N, N_GROUPS, EXPERTS_PER_GROUP)
    sel = jnp.take_along_axis(el, g[:, None, None], axis=1)[:, 0]
    ep = jax.nn.softmax(sel, axis=-1)
    vals, idx = lax.top_k(ep, TOP_K)
    gate = pg[:, None] * vals / jnp.sum(vals, axis=-1, keepdims=True)
    return g[:, None] * EXPERTS_PER_GROUP + idx.astype(jnp.int32), gate


def moe_ffn(h, expert_idx, gate, w_e_in, w_e_out):
    N, D = h.shape
    A = N * TOP_K
    flat_e = expert_idx.reshape(A)
    order = jnp.argsort(flat_e)
    sorted_e = flat_e[order]
    tok = order // TOP_K
    counts = jnp.bincount(flat_e, length=N_EXPERTS)
    padded = (counts + DISPATCH_BLOCK - 1) // DISPATCH_BLOCK * DISPATCH_BLOCK
    pad_end = jnp.cumsum(padded)
    pad_start = pad_end - padded
    seg_start = jnp.cumsum(counts) - counts
    dest = pad_start[sorted_e] + jnp.arange(A) - seg_start[sorted_e]
    n_blocks = -(-A // DISPATCH_BLOCK) + N_EXPERTS
    rows = jnp.full((n_blocks * DISPATCH_BLOCK,), N, dtype=jnp.int32).at[dest].set(tok.astype(jnp.int32))
    block_expert = jnp.minimum(jnp.searchsorted(pad_end // DISPATCH_BLOCK, jnp.arange(n_blocks), side='right'),
                               N_EXPERTS - 1).astype(jnp.int32)
    hpad = jnp.concatenate([h, jnp.zeros((1, D), h.dtype)], axis=0)
    xb = hpad[rows].reshape(n_blocks, DISPATCH_BLOCK, D)

    def expert_block(args):
        xblk, e = args
        gt, up = jnp.split(xblk @ w_e_in[e], 2, axis=-1)
        return (jax.nn.silu(gt) * up) @ w_e_out[e]

    yb = lax.map(expert_block, (xb, block_expert)).reshape(-1, D)
    contrib = yb[dest] * gate.reshape(A)[order][:, None].astype(yb.dtype)
    return jnp.zeros((N, D), h.dtype).at[tok].add(contrib.astype(h.dtype))


def channel_mixer(x, ln2_g, w_rg, b_rg, w_re, b_re, w_e_in, w_e_out):
    B, S, D = x.shape
    h = rmsnorm(x, ln2_g).reshape(B * S, D)
    eidx, gate = hier_route(h, w_rg, b_rg, w_re, b_re)
    return x + moe_ffn(h, eidx, gate, w_e_in, w_e_out).reshape(B, S, D)


def setup_inputs(seed: int = 0) -> dict:
    key = jax.random.key(seed)
    ks = jax.random.split(key, 32)
    f32 = jnp.float32
    n_pages = PAST_LEN // PAGE_SIZE
    n_pool = (DEC_BATCH * n_pages * 5) // 4
    nrm = lambda k, shape, s: jax.random.normal(k, shape, f32) * s
    u = jax.random.uniform(ks[10], (DEPTH, D_RNN), f32, 0.9, 0.999)
    s_l = u ** (1.0 / LRU_C)
    page_table = jax.random.permutation(ks[7], n_pool)[:DEC_BATCH * n_pages].reshape(DEC_BATCH, n_pages).astype(jnp.int32)
    return {
        'x_prompt': nrm(ks[0], (BATCH, SEQ, D_MODEL), 1.0),
        'x_sample': nrm(ks[1], (DEC_BATCH, DEC_SEQ, D_MODEL), 1.0),
        'cache_k': nrm(ks[2], (DEPTH, n_pool, PAGE_SIZE, N_QK_HEADS, HEAD_DIM), 1.0),
        'cache_v': nrm(ks[3], (DEPTH, n_pool, PAGE_SIZE, N_DIFF_HEADS, V_DIM), 1.0),
        'state_h': nrm(ks[4], (DEPTH, DEC_BATCH, D_RNN), 0.5),
        'state_conv': nrm(ks[5], (DEPTH, DEC_BATCH, CONV_W - 1, D_RNN), 1.0),
        'page_table': page_table,
        'ln1_g': 1.0 + nrm(ks[6], (DEPTH, D_MODEL), 0.02),
        'w_in': nrm(ks[8], (DEPTH, D_MODEL, D_PROJ), D_MODEL ** -0.5),
        'conv_w': nrm(ks[9], (DEPTH, CONV_W, D_RNN), CONV_W ** -0.5),
        'conv_b': nrm(ks[11], (DEPTH, D_RNN), 0.01),
        'w_ga': nrm(ks[12], (DEPTH, RNN_BLOCKS, RNN_BLOCK_W, RNN_BLOCK_W), RNN_BLOCK_W ** -0.5),
        'b_ga': nrm(ks[13], (DEPTH, RNN_BLOCKS, RNN_BLOCK_W), 0.01),
        'w_gx': nrm(ks[14], (DEPTH, RNN_BLOCKS, RNN_BLOCK_W, RNN_BLOCK_W), RNN_BLOCK_W ** -0.5),
        'b_gx': nrm(ks[15], (DEPTH, RNN_BLOCKS, RNN_BLOCK_W), 0.01),
        'lru_lambda': jnp.log(s_l) - jnp.log1p(-s_l),
        'lam_q1': nrm(ks[16], (DEPTH, HEAD_DIM), 0.1),
        'lam_k1': nrm(ks[17], (DEPTH, HEAD_DIM), 0.1),
        'lam_q2': nrm(ks[18], (DEPTH, HEAD_DIM), 0.1),
        'lam_k2': nrm(ks[19], (DEPTH, HEAD_DIM), 0.1),
        'subln_g': 1.0 + nrm(ks[20], (DEPTH, V_DIM), 0.02),
        'w_out': nrm(ks[21], (DEPTH, D_MIX, D_MODEL), D_MIX ** -0.5),
        'ln2_g': 1.0 + nrm(ks[22], (DEPTH, D_MODEL), 0.02),
        'w_route_group': nrm(ks[23], (DEPTH, D_MODEL, N_GROUPS), D_MODEL ** -0.5),
        'b_route_group': nrm(ks[24], (DEPTH, N_GROUPS), 0.01),
        'w_route_expert': nrm(ks[25], (DEPTH, D_MODEL, N_EXPERTS), D_MODEL ** -0.5),
        'b_route_expert': nrm(ks[26], (DEPTH, N_EXPERTS), 0.01),
        'w_exp_in': nrm(ks[27], (DEPTH, N_EXPERTS, D_MODEL, 2 * D_EXPERT), D_MODEL ** -0.5),
        'w_exp_out': nrm(ks[28], (DEPTH, N_EXPERTS, D_EXPERT, D_MODEL), D_EXPERT ** -0.5),
        'ln_f': 1.0 + nrm(ks[29], (D_MODEL,), 0.02),
    }


def reference(x_prompt, x_sample, cache_k, cache_v, state_h, state_conv, page_table,
              ln1_g, w_in, conv_w, conv_b, w_ga, b_ga, w_gx, b_gx, lru_lambda,
              lam_q1, lam_k1, lam_q2, lam_k2, subln_g, w_out, ln2_g,
              w_route_group, b_route_group, w_route_expert, b_route_expert,
              w_exp_in, w_exp_out, ln_f):
    Bp, Sp, _ = x_prompt.shape
    Bs, Ss, _ = x_sample.shape
    n_pages = page_table.shape[1]
    past_len = n_pages * cache_k.shape[2]
    pos_p = jnp.arange(Sp)
    pos_s = past_len + jnp.arange(Ss)
    kpos_s = jnp.arange(past_len + Ss)
    xp, xs = x_prompt, x_sample
    kp_l, vp_l, hp_l, cp_l, ks_l, vs_l, hs_l, cs_l = [], [], [], [], [], [], [], []
    for l in range(DEPTH):
        lambda_init = 0.8 - 0.6 * math.exp(-0.3 * l)
        w = (ln1_g[l], w_in[l], conv_w[l], conv_b[l], w_ga[l], b_ga[l], w_gx[l], b_gx[l], lru_lambda[l],
             lam_q1[l], lam_k1[l], lam_q2[l], lam_k2[l], subln_g[l], lambda_init, w_out[l])
        attend_p = lambda q, k, v, lam: prompt_attention(q, k, v, pos_p, lam)
        xp, kp, vp, hp, cp = mixer(xp, pos_p, jnp.zeros((Bp, CONV_W - 1, D_RNN), xp.dtype),
                                   jnp.zeros((Bp, D_RNN), xp.dtype), attend_p, *w)
        past_k = cache_k[l][page_table].reshape(Bs, past_len, N_QK_HEADS, HEAD_DIM)
        past_v = cache_v[l][page_table].reshape(Bs, past_len, N_DIFF_HEADS, V_DIM)

        def attend_s(q, k, v, lam, past_k=past_k, past_v=past_v):
            kk = jnp.concatenate([past_k, k.astype(past_k.dtype)], axis=1)
            vv = jnp.concatenate([past_v, v.astype(past_v.dtype)], axis=1)
            return diff_attn_block(q, pos_s, kk, vv, kpos_s, lam)

        xs, k_s, v_s, h_s, c_s = mixer(xs, pos_s, state_conv[l], state_h[l], attend_s, *w)
        xp = channel_mixer(xp, ln2_g[l], w_route_group[l], b_route_group[l], w_route_expert[l],
                           b_route_expert[l], w_exp_in[l], w_exp_out[l])
        xs = channel_mixer(xs, ln2_g[l], w_route_group[l], b_route_group[l], w_route_expert[l],
                           b_route_expert[l], w_exp_in[l], w_exp_out[l])
        kp_l.append(kp); vp_l.append(vp); hp_l.append(hp.astype(xp.dtype)); cp_l.append(cp)
        ks_l.append(k_s.astype(cache_k.dtype)); vs_l.append(v_s.astype(cache_v.dtype))
        hs_l.append(h_s.astype(state_h.dtype)); cs_l.append(c_s.astype(state_conv.dtype))
    y_prompt = rmsnorm(xp, ln_f)
    y_sample = rmsnorm(xs, ln_f)
    return (y_prompt, y_sample,
            jnp.stack(kp_l), jnp.stack(vp_l), jnp.stack(hp_l), jnp.stack(cp_l),
            jnp.stack(ks_l), jnp.stack(vs_l), jnp.stack(hs_l), jnp.stack(cs_l))
```

```python
import functools
import math

import jax
import jax.numpy as jnp
from jax import lax
from jax.experimental import pallas as pl
from jax.experimental.pallas import tpu as pltpu

F32 = jnp.float32
BF16 = jnp.bfloat16

HEAD_DIM = 64
V_DIM = 2 * HEAD_DIM
HALF = HEAD_DIM // 2
CONV_W = 4
LRU_C = 8.0
ROPE_THETA = 10000.0
NORM_EPS = 1e-6
SUBLN_EPS = 1e-5
NEG_INF = -1e30
TOP_K = 2

LANES = 128
SUBLANES = 8
VMEM_LIMIT = 48 * 1024 * 1024

ROW_TILE = 512
LRU_TILE = 256
ATT_TILE = 512
EXP_BLOCK = 256
DISPATCH_TILE = 512
COMBINE_TILE = 256


def _cparams(sem):
    return pltpu.CompilerParams(dimension_semantics=sem, vmem_limit_bytes=VMEM_LIMIT)


def _inproj_kernel(x_ref, g_ref, wm_ref, wt_ref, cq_ref, sq_ref, ct_ref, st_ref, *outs, sample):
    x = x_ref[...]
    ms = jnp.mean(x * x, axis=-1, keepdims=True)
    h = (x * lax.rsqrt(ms + NORM_EPS) * g_ref[...]).astype(BF16)
    main = jnp.dot(h, wm_ref[...], preferred_element_type=F32)
    d_rnn = outs[0].shape[1]
    ct = ct_ref[...]
    st = st_ref[...]
    tr = lax.dot_general(wt_ref[...], h, (((1,), (1,)), ((), ())), preferred_element_type=F32)

    def rope_t(blk):
        x1 = blk[:HALF]
        x2 = blk[HALF:]
        return x1 * ct - x2 * st, x2 * ct + x1 * st

    if sample:
        xr_ref, gr_ref, v_ref, kt_ref, qt_ref = outs
        xr_ref[...] = main[:, :d_rnn]
        gr_ref[...] = main[:, d_rnn:2 * d_rnn]
        v_ref[...] = main[:, 2 * d_rnn:]
        n_heads = kt_ref.shape[0] // HEAD_DIM
        for hh in range(n_heads):
            o1, o2 = rope_t(tr[hh * HEAD_DIM:(hh + 1) * HEAD_DIM])
            kt_ref[hh * HEAD_DIM:hh * HEAD_DIM + HALF, :] = o1
            kt_ref[hh * HEAD_DIM + HALF:(hh + 1) * HEAD_DIM, :] = o2
            base = (n_heads + hh) * HEAD_DIM
            p1, p2 = rope_t(tr[base:base + HEAD_DIM])
            qt_ref[hh * HEAD_DIM:hh * HEAD_DIM + HALF, :] = p1
            qt_ref[hh * HEAD_DIM + HALF:(hh + 1) * HEAD_DIM, :] = p2
    else:
        xr_ref, gr_ref, q_ref, v_ref, vb_ref, kt_ref, ktb_ref = outs
        xr_ref[...] = main[:, :d_rnn]
        gr_ref[...] = main[:, d_rnn:2 * d_rnn]
        d_att = q_ref.shape[1]
        q = main[:, 2 * d_rnn:2 * d_rnn + d_att]
        v = main[:, 2 * d_rnn + d_att:]
        cq = cq_ref[...]
        sq = sq_ref[...]
        for c in range(d_att // LANES):
            qc = q[:, c * LANES:(c + 1) * LANES]
            q_ref[:, c * LANES:(c + 1) * LANES] = (qc * cq + pltpu.roll(qc, HEAD_DIM, 1) * sq).astype(BF16)
        v_ref[...] = v
        vb_ref[...] = v.astype(BF16)
        n_heads = kt_ref.shape[0] // HEAD_DIM
        for hh in range(n_heads):
            o1, o2 = rope_t(tr[hh * HEAD_DIM:(hh + 1) * HEAD_DIM])
            kt_ref[hh * HEAD_DIM:hh * HEAD_DIM + HALF, :] = o1
            kt_ref[hh * HEAD_DIM + HALF:(hh + 1) * HEAD_DIM, :] = o2
            g, jj = divmod(hh, 2)
            r0 = g * V_DIM + jj * HALF
            ktb_ref[r0:r0 + HALF, :] = o1.astype(BF16)
            ktb_ref[r0 + HEAD_DIM:r0 + HEAD_DIM + HALF, :] = o2.astype(BF16)


def _inproj_prompt(x, g, wm, wt, cq, sq, ct, st):
    B, S, D = x.shape
    T = min(ROW_TILE, S)
    nt = S // T
    d_rnn = 512
    d_att = wt.shape[0]
    row = lambda w: pl.BlockSpec((None, T, w), lambda b, t: (b, t, 0))
    full = lambda a: pl.BlockSpec(a.shape, lambda b, t: (0,) * a.ndim)
    tr_spec = pl.BlockSpec((None, d_att, T), lambda b, t: (b, 0, t))
    return pl.pallas_call(
        functools.partial(_inproj_kernel, sample=False),
        grid=(B, nt),
        in_specs=[row(D), full(g), full(wm), full(wt),
                  pl.BlockSpec((T, LANES), lambda b, t: (t, 0)),
                  pl.BlockSpec((T, LANES), lambda b, t: (t, 0)),
                  pl.BlockSpec((HALF, T), lambda b, t: (0, t)),
                  pl.BlockSpec((HALF, T), lambda b, t: (0, t))],
        out_specs=[row(d_rnn), row(d_rnn), row(d_att), row(d_att), row(d_att), tr_spec, tr_spec],
        out_shape=[jax.ShapeDtypeStruct((B, S, d_rnn), F32), jax.ShapeDtypeStruct((B, S, d_rnn), F32),
                   jax.ShapeDtypeStruct((B, S, d_att), BF16), jax.ShapeDtypeStruct((B, S, d_att), F32),
                   jax.ShapeDtypeStruct((B, S, d_att), BF16), jax.ShapeDtypeStruct((B, d_att, S), F32),
                   jax.ShapeDtypeStruct((B, d_att, S), BF16)],
        compiler_params=_cparams(("parallel", "arbitrary")),
        name="inproj_prompt",
    )(x, g, wm, wt, cq, sq, ct, st)


def _inproj_sample(x, g, wm, wt, ct, st):
    n, D = x.shape
    d_rnn = 512
    d_att = wt.shape[0] // 2
    full = lambda a: pl.BlockSpec(a.shape, lambda i: (0,) * a.ndim)
    dummy = jnp.zeros((SUBLANES, LANES), F32)
    out_shapes = [jax.ShapeDtypeStruct((n, d_rnn), F32), jax.ShapeDtypeStruct((n, d_rnn), F32),
                  jax.ShapeDtypeStruct((n, d_att), F32), jax.ShapeDtypeStruct((d_att, n), F32),
                  jax.ShapeDtypeStruct((d_att, n), F32)]
    return pl.pallas_call(
        functools.partial(_inproj_kernel, sample=True),
        grid=(1,),
        in_specs=[full(x), full(g), full(wm), full(wt), full(dummy), full(dummy), full(ct), full(st)],
        out_specs=[pl.BlockSpec(s.shape, lambda i: (0, 0)) for s in out_shapes],
        out_shape=out_shapes,
        compiler_params=_cparams(("arbitrary",)),
        name="inproj_sample",
    )(x, g, wm, wt, dummy, dummy, ct, st)


def _gates(xc, wg_ref, bga, bgx, lam):
    half = xc.shape[1] // 2
    xcb = xc.astype(BF16)
    g0 = jnp.dot(xcb[:, :half], wg_ref[0], preferred_element_type=F32)
    g1 = jnp.dot(xcb[:, half:], wg_ref[1], preferred_element_type=F32)
    r = jax.nn.sigmoid(jnp.concatenate([g0[:, :half], g1[:, :half]], axis=1) + bga)
    i = jax.nn.sigmoid(jnp.concatenate([g0[:, half:], g1[:, half:]], axis=1) + bgx)
    neg = -lam
    softplus = jnp.maximum(neg, 0.0) + jnp.log1p(jnp.exp(-jnp.abs(neg)))
    log_a = -LRU_C * r * softplus
    a = jnp.exp(log_a)
    mult = jnp.sqrt(1.0 - a * a)
    return a, i, mult


def _lru_kernel(xr_ref, gr_ref, cw_ref, cb_ref, wg_ref, bga_ref, bgx_ref, lam_ref,
                y_ref, hlast_ref, ctail_ref, xp_scr, a_scr, b_scr, h_scr):
    t = pl.program_id(1)
    T, C = xr_ref.shape

    @pl.when(t == 0)
    def _():
        xp_scr[0:SUBLANES, :] = jnp.zeros((SUBLANES, C), F32)
        h_scr[...] = jnp.zeros_like(h_scr)

    x = xr_ref[...]
    xp_scr[SUBLANES:SUBLANES + T, :] = x
    cw = cw_ref[...]
    xc = cb_ref[...]
    for j in range(CONV_W - 1):
        xc = xc + xp_scr[pl.ds(SUBLANES - (CONV_W - 1) + j, T), :] * cw[j:j + 1]
    xc = xc + x * cw[CONV_W - 1:CONV_W]
    tail = x[T - (CONV_W - 1):, :]
    xp_scr[SUBLANES - (CONV_W - 1):SUBLANES, :] = tail
    ctail_ref[...] = tail

    a, i, mult = _gates(xc, wg_ref, bga_ref[...], bgx_ref[...], lam_ref[...])
    row = lax.broadcasted_iota(jnp.int32, (T, 1), 0)
    mult = jnp.where(jnp.logical_and(row == 0, t == 0), 1.0, mult)
    a_scr[...] = a
    b_scr[...] = mult * i * xc

    rowi = lax.broadcasted_iota(jnp.int32, (SUBLANES, C), 0)

    def body(gi, h):
        off = pl.multiple_of(gi * SUBLANES, SUBLANES)
        A = a_scr[pl.ds(off, SUBLANES), :]
        Bv = b_scr[pl.ds(off, SUBLANES), :]
        for d in (1, 2, 4):
            keep = rowi >= d
            Bv = jnp.where(keep, Bv + A * pltpu.roll(Bv, d, 0), Bv)
            A = jnp.where(keep, A * pltpu.roll(A, d, 0), A)
        hrows = Bv + A * h
        b_scr[pl.ds(off, SUBLANES), :] = hrows
        return hrows[SUBLANES - 1:SUBLANES, :]

    h = lax.fori_loop(0, T // SUBLANES, body, h_scr[...], unroll=2)
    h_scr[...] = h
    hlast_ref[...] = h
    y_ref[...] = (b_scr[...] * jax.nn.gelu(gr_ref[...])).astype(BF16)


def _lru_prompt(xr, gr, cw, cb, wg, bga, bgx, lam):
    B, S, C = xr.shape
    T = min(LRU_TILE, S)
    nt = S // T
    row = pl.BlockSpec((None, T, C), lambda b, t: (b, t, 0))
    full = lambda a: pl.BlockSpec(a.shape, lambda b, t: (0,) * a.ndim)
    return pl.pallas_call(
        _lru_kernel,
        grid=(B, nt),
        in_specs=[row, row, full(cw), full(cb), full(wg), full(bga), full(bgx), full(lam)],
        out_specs=[row, pl.BlockSpec((None, 1, C), lambda b, t: (b, 0, 0)),
                   pl.BlockSpec((None, CONV_W - 1, C), lambda b, t: (b, 0, 0))],
        out_shape=[jax.ShapeDtypeStruct((B, S, C), BF16), jax.ShapeDtypeStruct((B, 1, C), F32),
                   jax.ShapeDtypeStruct((B, CONV_W - 1, C), F32)],
        scratch_shapes=[pltpu.VMEM((T + SUBLANES, C), F32), pltpu.VMEM((T, C), F32),
                        pltpu.VMEM((T, C), F32), pltpu.VMEM((1, C), F32)],
        compiler_params=_cparams(("parallel", "arbitrary")),
        name="rglru_prompt",
    )(xr, gr, cw, cb, wg, bga, bgx, lam)


def _lru_step_kernel(xr_ref, gr_ref, conv_ref, h0_ref, cw_ref, cb_ref, wg_ref, bga_ref, bgx_ref, lam_ref,
                     y_ref, hnew_ref, cnew_ref):
    x = xr_ref[...]
    cw = cw_ref[...]
    xc = cb_ref[...]
    for j in range(CONV_W - 1):
        xc = xc + conv_ref[j] * cw[j:j + 1]
    xc = xc + x * cw[CONV_W - 1:CONV_W]
    for j in range(CONV_W - 2):
        cnew_ref[j] = conv_ref[j + 1]
    cnew_ref[CONV_W - 2] = x
    a, i, mult = _gates(xc, wg_ref, bga_ref[...], bgx_ref[...], lam_ref[...])
    h = a * h0_ref[...] + mult * i * xc
    hnew_ref[...] = h
    y_ref[...] = (h * jax.nn.gelu(gr_ref[...])).astype(BF16)


def _lru_step(xr, gr, conv, h0, cw, cb, wg, bga, bgx, lam):
    n, C = xr.shape
    args = (xr, gr, conv, h0, cw, cb, wg, bga, bgx, lam)
    full = lambda a: pl.BlockSpec(a.shape, lambda i: (0,) * a.ndim)
    out_shapes = [jax.ShapeDtypeStruct((n, C), BF16), jax.ShapeDtypeStruct((n, C), F32),
                  jax.ShapeDtypeStruct((CONV_W - 1, n, C), F32)]
    return pl.pallas_call(
        _lru_step_kernel,
        grid=(1,),
        in_specs=[full(a) for a in args],
        out_specs=[pl.BlockSpec(s.shape, lambda i, nd=len(s.shape): (0,) * nd) for s in out_shapes],
        out_shape=out_shapes,
        compiler_params=_cparams(("arbitrary",)),
        name="rglru_step",
    )(*args)


def _subln(o, g, scale):
    ms = jnp.mean(o * o, axis=-1, keepdims=True)
    return o * lax.rsqrt(ms + SUBLN_EPS) * g * scale


def _attn_kernel(qi_ref, kj_ref, lam_ref, q_ref, kt_ref, v_ref, g_ref, o_ref,
                 m1, l1, acc1, m2, l2, acc2, *, out_scale):
    t = pl.program_id(1)
    qi = qi_ref[t]
    kj = kj_ref[t]
    Tq = q_ref.shape[0]
    Tk = kt_ref.shape[1]

    @pl.when(kj == 0)
    def _():
        for m, l, acc in ((m1, l1, acc1), (m2, l2, acc2)):
            m[...] = jnp.full_like(m, NEG_INF)
            l[...] = jnp.zeros_like(l)
            acc[...] = jnp.zeros_like(acc)

    q = q_ref[...]
    lane = lax.broadcasted_iota(jnp.int32, q.shape, 1)
    second = (lane & HALF) != 0
    zero = jnp.zeros_like(q)
    kt = kt_ref[...]
    v = v_ref[...]
    rows = qi * Tq + lax.broadcasted_iota(jnp.int32, (Tq, Tk), 0)
    cols = kj * Tk + lax.broadcasted_iota(jnp.int32, (Tq, Tk), 1)
    visible = cols <= rows

    for qs, m, l, acc in ((jnp.where(second, zero, q), m1, l1, acc1),
                          (jnp.where(second, q, zero), m2, l2, acc2)):
        s = jnp.dot(qs, kt, preferred_element_type=F32)
        s = jnp.where(visible, s, NEG_INF)
        m_old = m[...]
        m_new = jnp.maximum(m_old, jnp.max(s, axis=-1, keepdims=True))
        alpha = jnp.exp(m_old - m_new)
        p = jnp.exp(s - m_new)
        l[...] = alpha * l[...] + jnp.sum(p, axis=-1, keepdims=True)
        acc[...] = alpha * acc[...] + jnp.dot(p.astype(BF16), v, preferred_element_type=F32)
        m[...] = m_new

    @pl.when(kj == qi)
    def _():
        o = acc1[...] / l1[...] - lam_ref[0, 0] * (acc2[...] / l2[...])
        o_ref[...] = _subln(o, g_ref[...], out_scale).astype(o_ref.dtype)


def _attn_prompt(q, ktb, vb, subln_g, lam, out_scale):
    B, S, d_att = q.shape
    G = d_att // V_DIM
    T = min(ATT_TILE, S)
    nq = S // T
    tri = [(i, j) for i in range(nq) for j in range(i + 1)]
    qi = jnp.asarray([p[0] for p in tri], jnp.int32)
    kj = jnp.asarray([p[1] for p in tri], jnp.int32)
    grid_spec = pltpu.PrefetchScalarGridSpec(
        num_scalar_prefetch=2,
        grid=(B * G, len(tri)),
        in_specs=[
            pl.BlockSpec(memory_space=pltpu.SMEM),
            pl.BlockSpec((None, T, V_DIM), lambda bg, t, qi, kj: (bg // G, qi[t], bg % G)),
            pl.BlockSpec((None, V_DIM, T), lambda bg, t, qi, kj: (bg // G, bg % G, kj[t])),
            pl.BlockSpec((None, T, V_DIM), lambda bg, t, qi, kj: (bg // G, kj[t], bg % G)),
            pl.BlockSpec((1, V_DIM), lambda bg, t, qi, kj: (0, 0)),
        ],
        out_specs=pl.BlockSpec((None, T, V_DIM), lambda bg, t, qi, kj: (bg // G, qi[t], bg % G)),
        scratch_shapes=[pltpu.VMEM((T, 1), F32), pltpu.VMEM((T, 1), F32), pltpu.VMEM((T, V_DIM), F32),
                        pltpu.VMEM((T, 1), F32), pltpu.VMEM((T, 1), F32), pltpu.VMEM((T, V_DIM), F32)],
    )
    return pl.pallas_call(
        functools.partial(_attn_kernel, out_scale=out_scale),
        grid_spec=grid_spec,
        out_shape=jax.ShapeDtypeStruct((B, S, d_att), BF16),
        compiler_params=_cparams(("parallel", "arbitrary")),
        name="diff_attn_prompt",
    )(qi, kj, lam, q, ktb, vb, subln_g)


def _attn_sample_kernel(pt_ref, lam_ref, q_ref, kc_ref, vc_ref, g_ref, *rest, n_pages, out_scale):
    k_refs = rest[:n_pages]
    v_refs = rest[n_pages:2 * n_pages]
    o_ref = rest[2 * n_pages]
    lam = lam_ref[0, 0]
    q = q_ref[...]
    d_att = q.shape[1]
    n_heads = d_att // HEAD_DIM
    G = d_att // V_DIM
    R = 2 * SUBLANES
    rowi = lax.broadcasted_iota(jnp.int32, (R, d_att), 0)
    coli = lax.broadcasted_iota(jnp.int32, (R, d_att), 1)
    qblk = jnp.where(coli // HEAD_DIM == rowi, jnp.broadcast_to(q, (R, d_att)), 0.0)
    qb = qblk.astype(BF16)
    s = jnp.concatenate(
        [jnp.dot(qb, k_refs[j][...].astype(BF16), preferred_element_type=F32) for j in range(n_pages)], axis=1)
    s = s[:n_heads]
    s_cur = jnp.sum(qblk[:n_heads] * kc_ref[...], axis=-1, keepdims=True)
    m = jnp.maximum(jnp.max(s, axis=-1, keepdims=True), s_cur)
    p = jnp.exp(s - m)
    p_cur = jnp.exp(s_cur - m)
    inv_l = 1.0 / (jnp.sum(p, axis=-1, keepdims=True) + p_cur)
    p = p * inv_l
    p_cur = p_cur * inv_l
    amap = p - lam * pltpu.roll(p, n_heads - 1, 0)
    amap = jnp.concatenate([amap, jnp.zeros_like(amap)], axis=0).astype(BF16)
    p_cur = jnp.broadcast_to(p_cur, (n_heads, LANES))
    amap_cur = p_cur - lam * pltpu.roll(p_cur, n_heads - 1, 0)
    P = k_refs[0].shape[1]
    outs = []
    for g in range(G):
        acc = jnp.zeros((R, V_DIM), F32)
        for j in range(n_pages):
            vjg = v_refs[j][pl.ds(g, P, stride=G), :].astype(BF16)
            acc = acc + jnp.dot(amap[:, j * P:(j + 1) * P], vjg, preferred_element_type=F32)
        o = acc[2 * g:2 * g + 1, :] + amap_cur[2 * g:2 * g + 1, :] * vc_ref[:, g * V_DIM:(g + 1) * V_DIM]
        outs.append(_subln(o, g_ref[...], out_scale))
    o_ref[...] = jnp.concatenate(outs, axis=1).astype(o_ref.dtype)


def _attn_sample(pt_flat, lam, q, k_cur, v_cur, subln_g, ck, cv, layer, out_scale):
    n, d_att = q.shape
    n_pages = pt_flat.shape[0] // n
    P = ck.shape[-1]
    q3 = q.reshape(n, 1, d_att)
    kc3 = k_cur.reshape(n, 1, d_att)
    vc3 = v_cur.reshape(n, 1, d_att)
    tok = lambda: pl.BlockSpec((None, 1, d_att), lambda b, pt: (b, 0, 0))
    k_specs = [pl.BlockSpec((None, None, d_att, P), functools.partial(
        lambda b, pt, j: (layer, pt[b * n_pages + j], 0, 0), j=j)) for j in range(n_pages)]
    v_specs = [pl.BlockSpec((None, None, cv.shape[2], V_DIM), functools.partial(
        lambda b, pt, j: (layer, pt[b * n_pages + j], 0, 0), j=j)) for j in range(n_pages)]
    grid_spec = pltpu.PrefetchScalarGridSpec(
        num_scalar_prefetch=1,
        grid=(n,),
        in_specs=[pl.BlockSpec(memory_space=pltpu.SMEM), tok(), tok(), tok(),
                  pl.BlockSpec((1, V_DIM), lambda b, pt: (0, 0))] + k_specs + v_specs,
        out_specs=pl.BlockSpec((None, 1, d_att), lambda b, pt: (b, 0, 0)),
    )
    out = pl.pallas_call(
        functools.partial(_attn_sample_kernel, n_pages=n_pages, out_scale=out_scale),
        grid_spec=grid_spec,
        out_shape=jax.ShapeDtypeStruct((n, 1, d_att), BF16),
        compiler_params=_cparams(("arbitrary",)),
        name="diff_attn_sample",
    )(pt_flat, lam, q3, kc3, vc3, subln_g, *([ck] * n_pages), *([cv] * n_pages))
    return out.reshape(n, d_att)


def _outproj_kernel(yr_ref, at_ref, x_ref, wo_ref, g2_ref, wr_ref, br_ref, cnt_in_ref,
                    x1_ref, h2_ref, route_ref, cnt_out_ref, cnt_scr, *, n_groups, per_group):
    step = pl.program_id(0)

    @pl.when(step == 0)
    def _():
        cnt_scr[...] = cnt_in_ref[...]

    d_rnn = yr_ref.shape[1]
    y = (jnp.dot(yr_ref[...], wo_ref[:d_rnn, :], preferred_element_type=F32)
         + jnp.dot(at_ref[...], wo_ref[d_rnn:, :], preferred_element_type=F32))
    x1 = x_ref[...] + y
    x1_ref[...] = x1
    ms = jnp.mean(x1 * x1, axis=-1, keepdims=True)
    h2 = x1 * lax.rsqrt(ms + NORM_EPS) * g2_ref[...]
    h2_ref[...] = h2
    logits = jnp.dot(h2, wr_ref[...], preferred_element_type=F32, precision=lax.Precision.HIGHEST) + br_ref[...]
    T = logits.shape[0]
    lane = lax.broadcasted_iota(jnp.int32, (T, LANES), 1).astype(F32)
    first = lambda hit: jnp.min(jnp.where(hit, lane, float(LANES)), axis=-1, keepdims=True)
    gl = jnp.where(lane < n_groups, logits, NEG_INF)
    gmax = jnp.max(gl, axis=-1, keepdims=True)
    gidx = first(gl == gmax)
    pg = 1.0 / jnp.sum(jnp.exp(gl - gmax), axis=-1, keepdims=True)
    lo = n_groups + per_group * gidx
    el = jnp.where(jnp.logical_and(lane >= lo, lane < lo + per_group), logits, NEG_INF)
    ma = jnp.max(el, axis=-1, keepdims=True)
    ia = first(el == ma)
    el2 = jnp.where(lane == ia, NEG_INF, el)
    mb = jnp.max(el2, axis=-1, keepdims=True)
    ib = first(el2 == mb)
    eb = jnp.exp(mb - ma)
    gate_a = pg / (1.0 + eb)
    gate_b = pg * eb / (1.0 + eb)
    hit_a = lane == ia
    hit_b = lane == ib
    onehot = jnp.where(jnp.logical_or(hit_a, hit_b), 1.0, 0.0)
    r_i = lax.broadcasted_iota(jnp.int32, (T, T), 0)
    c_i = lax.broadcasted_iota(jnp.int32, (T, T), 1)
    lower = jnp.where(c_i < r_i, 1.0, 0.0).astype(BF16)
    before = jnp.dot(lower, onehot.astype(BF16), preferred_element_type=F32) + cnt_scr[...]
    rank_a = jnp.sum(jnp.where(hit_a, before, 0.0), axis=-1, keepdims=True)
    rank_b = jnp.sum(jnp.where(hit_b, before, 0.0), axis=-1, keepdims=True)
    cnt_scr[...] = cnt_scr[...] + jnp.sum(onehot, axis=0, keepdims=True)
    cnt_out_ref[...] = cnt_scr[...]
    cols = (ia - n_groups, ib - n_groups, gate_a, gate_b, rank_a, rank_b)
    route = jnp.zeros((T, LANES), F32)
    for c, val in enumerate(cols):
        route = jnp.where(lane == float(c), val, route)
    route_ref[...] = route


def _outproj_route(yr, at, x, wo, g2, wr, br, cnt_in, n_groups, per_group):
    N, D = x.shape
    T = min(ROW_TILE, N)
    d_rnn = yr.shape[1]
    row = lambda w: pl.BlockSpec((T, w), lambda i: (i, 0))
    full = lambda a: pl.BlockSpec(a.shape, lambda i: (0,) * a.ndim)
    return pl.pallas_call(
        functools.partial(_outproj_kernel, n_groups=n_groups, per_group=per_group),
        grid=(N // T,),
        in_specs=[row(d_rnn), row(at.shape[1]), row(D), full(wo), full(g2), full(wr), full(br), full(cnt_in)],
        out_specs=[row(D), row(D), row(LANES), pl.BlockSpec((1, LANES), lambda i: (0, 0))],
        out_shape=[jax.ShapeDtypeStruct((N, D), F32), jax.ShapeDtypeStruct((N, D), F32),
                   jax.ShapeDtypeStruct((N, LANES), F32), jax.ShapeDtypeStruct((1, LANES), F32)],
        scratch_shapes=[pltpu.VMEM((1, LANES), F32)],
        compiler_params=_cparams(("arbitrary",)),
        name="outproj_route",
    )(yr, at, x, wo, g2, wr, br, cnt_in)


def _dispatch_kernel(zb_ref, dest_ref, h_hbm, *rest, zero_fill):
    xb_hbm, zbuf, sem = rest[-3], rest[-2], rest[-1]
    T = dest_ref.shape[1] // TOP_K
    base = pl.program_id(0) * T

    if zero_fill:
        @pl.when(pl.program_id(0) == 0)
        def _():
            zbuf[...] = jnp.zeros_like(zbuf)

            def zcopy(i):
                row0 = pl.multiple_of(zb_ref[i] * EXP_BLOCK, EXP_BLOCK)
                return pltpu.make_async_copy(zbuf, xb_hbm.at[pl.ds(row0, EXP_BLOCK)], sem)

            def zstart(i, c):
                @pl.when(zb_ref[i] >= 0)
                def _():
                    zcopy(i).start()
                return c

            def zwait(i, c):
                @pl.when(zb_ref[i] >= 0)
                def _():
                    zcopy(i).wait()
                return c

            lax.fori_loop(0, zb_ref.shape[0], zstart, 0)
            lax.fori_loop(0, zb_ref.shape[0], zwait, 0)

    def body(i, c):
        for s in range(TOP_K):
            pltpu.make_async_copy(h_hbm.at[pl.ds(base + i, 1)],
                                  xb_hbm.at[pl.ds(dest_ref[0, TOP_K * i + s], 1)], sem).start()
        return c

    lax.fori_loop(0, T, body, 0)
    for s in range(TOP_K):
        pltpu.make_async_copy(h_hbm.at[pl.ds(0, T)], xb_hbm.at[pl.ds(0, T)], sem).wait()


def _dispatch(zb, dest, h, xb, xb_rows):
    N, D = h.shape
    T = min(DISPATCH_TILE, N)
    dest3 = dest.reshape(N // T, 1, TOP_K * T)
    aliased = xb is not None
    in_specs = [pl.BlockSpec((None, 1, TOP_K * T), lambda i, zb: (i, 0, 0), memory_space=pltpu.SMEM),
                pl.BlockSpec(memory_space=pl.ANY)]
    args = [zb, dest3, h]
    if aliased:
        in_specs.append(pl.BlockSpec(memory_space=pl.ANY))
        args.append(xb)
    grid_spec = pltpu.PrefetchScalarGridSpec(
        num_scalar_prefetch=1,
        grid=(N // T,),
        in_specs=in_specs,
        out_specs=pl.BlockSpec(memory_space=pl.ANY),
        scratch_shapes=[pltpu.VMEM((EXP_BLOCK, D), F32), pltpu.SemaphoreType.DMA(())],
    )
    return pl.pallas_call(
        functools.partial(_dispatch_kernel, zero_fill=not aliased),
        grid_spec=grid_spec,
        out_shape=jax.ShapeDtypeStruct((xb_rows, D), F32),
        input_output_aliases={3: 0} if aliased else {},
        compiler_params=pltpu.CompilerParams(dimension_semantics=("arbitrary",), has_side_effects=True),
        name="moe_dispatch",
    )(*args)


def _expert_kernel(be_ref, bi_ref, br_ref, bf_ref, x_ref, w1_ref, w2_ref, y_ref, w1b, w2b):
    b = pl.program_id(0)
    rows = br_ref[b]

    @pl.when(bf_ref[b] == 1)
    def _():
        w1b[...] = w1_ref[...].astype(BF16)
        w2b[...] = w2_ref[...].astype(BF16)

    @pl.when(rows > 0)
    def _():
        h = jnp.dot(x_ref[...].astype(BF16), w1b[...], preferred_element_type=F32)
        de = h.shape[1] // 2
        act = (jax.nn.silu(h[:, :de]) * h[:, de:]).astype(BF16)
        y_ref[...] = jnp.dot(act, w2b[...], preferred_element_type=F32)

    @pl.when(rows == 0)
    def _():
        y_ref[...] = jnp.zeros_like(y_ref)


def _experts(be, bi, br, bf, xb, w1, w2, layer):
    R, D = xb.shape
    nb = be.shape[0]
    d2 = w1.shape[-1]
    de = w2.shape[-2]
    grid_spec = pltpu.PrefetchScalarGridSpec(
        num_scalar_prefetch=4,
        grid=(nb,),
        in_specs=[pl.BlockSpec((EXP_BLOCK, D), lambda b, be, bi, br, bf: (bi[b], 0)),
                  pl.BlockSpec((None, None, D, d2), lambda b, be, bi, br, bf: (layer, be[b], 0, 0)),
                  pl.BlockSpec((None, None, de, D), lambda b, be, bi, br, bf: (layer, be[b], 0, 0))],
        out_specs=pl.BlockSpec((EXP_BLOCK, D), lambda b, be, bi, br, bf: (b, 0)),
        scratch_shapes=[pltpu.VMEM((D, d2), BF16), pltpu.VMEM((de, D), BF16)],
    )
    return pl.pallas_call(
        _expert_kernel,
        grid_spec=grid_spec,
        out_shape=jax.ShapeDtypeStruct((R, D), F32),
        compiler_params=_cparams(("arbitrary",)),
        name="moe_experts",
    )(be, bi, br, bf, xb, w1, w2)


def _combine_kernel(dest_ref, x1_ref, route_ref, gf_ref, yb_hbm, o_ref, ybuf, sem, *, final):
    T = x1_ref.shape[0]

    def body(i, c):
        for s in range(TOP_K):
            pltpu.make_async_copy(yb_hbm.at[pl.ds(dest_ref[0, TOP_K * i + s], 1)],
                                  ybuf.at[s, pl.ds(i, 1)], sem).start()
        return c

    lax.fori_loop(0, T, body, 0)
    for s in range(TOP_K):
        pltpu.make_async_copy(yb_hbm.at[pl.ds(0, T)], ybuf.at[s], sem).wait()
    route = route_ref[...]
    out = x1_ref[...] + ybuf[0] * route[:, 2:3] + ybuf[1] * route[:, 3:4]
    if final:
        ms = jnp.mean(out * out, axis=-1, keepdims=True)
        out = out * lax.rsqrt(ms + NORM_EPS) * gf_ref[...]
    o_ref[...] = out


def _combine(dest, x1, route, gf, yb, final):
    N, D = x1.shape
    T = min(COMBINE_TILE, N)
    dest3 = dest.reshape(N // T, 1, TOP_K * T)
    return pl.pallas_call(
        functools.partial(_combine_kernel, final=final),
        grid=(N // T,),
        in_specs=[pl.BlockSpec((None, 1, TOP_K * T), lambda i: (i, 0, 0), memory_space=pltpu.SMEM),
                  pl.BlockSpec((T, D), lambda i: (i, 0)),
                  pl.BlockSpec((T, LANES), lambda i: (i, 0)),
                  pl.BlockSpec((1, D), lambda i: (0, 0)),
                  pl.BlockSpec(memory_space=pl.ANY)],
        out_specs=pl.BlockSpec((T, D), lambda i: (i, 0)),
        out_shape=jax.ShapeDtypeStruct((N, D), F32),
        scratch_shapes=[pltpu.VMEM((TOP_K, T, D), F32), pltpu.SemaphoreType.DMA(())],
        compiler_params=_cparams(("arbitrary",)),
        name="moe_combine",
    )(dest3, x1, route, gf, yb)


def _rope_tables(pos):
    inv = ROPE_THETA ** (-jnp.arange(HALF, dtype=F32) / HALF)
    ang = pos.astype(F32)[:, None] * inv[None, :]
    cos, sin = jnp.cos(ang), jnp.sin(ang)
    cq = jnp.tile(cos, (1, LANES // HALF))
    sq = jnp.concatenate([-sin, -sin, sin, sin], axis=1)
    return cq, sq, cos.T, sin.T


def _q_perm(d_att):
    idx = []
    for g in range(d_att // V_DIM):
        for part in range(2):
            for jj in range(2):
                base = HEAD_DIM * (2 * g + jj) + HALF * part
                idx.extend(range(base, base + HALF))
    return jnp.asarray(idx, jnp.int32)


def _gate_weights(w_ga, w_gx):
    nb, bw, _ = w_ga.shape
    half = nb // 2
    eye = jnp.eye(half, dtype=w_ga.dtype)

    def dense(w):
        return jnp.einsum('nij,nm->nimj', w, eye).reshape(half * bw, half * bw)

    return jnp.stack([jnp.concatenate([dense(w_ga[h * half:(h + 1) * half]),
                                       dense(w_gx[h * half:(h + 1) * half])], axis=1)
                      for h in range(2)]).astype(BF16)


def _block_tables(counts, nb):
    ne = counts.shape[0]
    nblk = (counts + EXP_BLOCK - 1) // EXP_BLOCK
    cum = jnp.cumsum(nblk)
    total = cum[-1]
    b = jnp.arange(nb, dtype=jnp.int32)
    bc = jnp.minimum(b, total - 1)
    e = jnp.minimum(jnp.searchsorted(cum, bc, side='right'), ne - 1).astype(jnp.int32)
    off = bc - (cum[e] - nblk[e])
    valid = b < total
    rows = jnp.where(valid, jnp.clip(counts[e] - off * EXP_BLOCK, 0, EXP_BLOCK), 0)
    first = jnp.where(jnp.logical_and(valid, off == 0), 1, 0)
    last_blk = jnp.where(nblk > 0, cum - 1, -1)
    trail = total + jnp.arange(ne, dtype=jnp.int32)
    trail = jnp.where(trail < nb, trail, -1)
    zb = jnp.concatenate([last_blk, trail]).astype(jnp.int32)
    seg_start = ((cum - nblk) * EXP_BLOCK).astype(jnp.int32)
    return seg_start, e, bc.astype(jnp.int32), rows.astype(jnp.int32), first.astype(jnp.int32), zb


def kernel(x_prompt, x_sample, cache_k, cache_v, state_h, state_conv, page_table, ln1_g, w_in, conv_w, conv_b,
           w_ga, b_ga, w_gx, b_gx, lru_lambda, lam_q1, lam_k1, lam_q2, lam_k2, subln_g, w_out, ln2_g,
           w_route_group, b_route_group, w_route_expert, b_route_expert, w_exp_in, w_exp_out, ln_f):
    Bp, Sp, D = x_prompt.shape
    Bs = x_sample.shape[0]
    depth = w_in.shape[0]
    d_rnn = state_h.shape[-1]
    d_att = cache_k.shape[-2] * cache_k.shape[-1]
    n_groups = w_route_group.shape[-1]
    n_experts = w_route_expert.shape[-1]
    per_group = n_experts // n_groups
    n_pool, page = cache_k.shape[1], cache_k.shape[2]
    n_pages = page_table.shape[1]
    past_len = n_pages * page
    Np = Bp * Sp
    assert x_sample.shape[1] == 1 and n_groups + n_experts <= LANES

    cq, sq, ct, st = _rope_tables(jnp.arange(Sp))
    _, _, ct_s, st_s = _rope_tables(jnp.full((Bs,), past_len))
    perm = _q_perm(d_att)
    qscale = HEAD_DIM ** -0.5
    ck = jnp.transpose(cache_k, (0, 1, 3, 4, 2)).reshape(depth, n_pool, d_att, page)
    cv = cache_v.reshape(depth, n_pool, page * (d_att // V_DIM), V_DIM)
    pt_flat = page_table.reshape(-1).astype(jnp.int32)
    conv_s = jnp.transpose(state_conv, (0, 2, 1, 3))

    n_blocks = (TOP_K * (Np + Bs)) // EXP_BLOCK + n_experts

    xp = x_prompt
    xs = x_sample.reshape(Bs, D)
    kp_l, vp_l, hp_l, cp_l, ks_l, vs_l, hs_l, cs_l = [], [], [], [], [], [], [], []
    for l in range(depth):
        lambda_init = 0.8 - 0.6 * math.exp(-0.3 * l)
        out_scale = 1.0 - lambda_init
        lam = (jnp.exp(jnp.sum(lam_q1[l] * lam_k1[l])) - jnp.exp(jnp.sum(lam_q2[l] * lam_k2[l]))
               + lambda_init).reshape(1, 1).astype(F32)
        wl = w_in[l]
        w_xg = wl[:, :2 * d_rnn]
        w_q = wl[:, 2 * d_rnn:2 * d_rnn + d_att] * qscale
        w_k = wl[:, 2 * d_rnn + d_att:2 * d_rnn + 2 * d_att]
        w_v = wl[:, 2 * d_rnn + 2 * d_att:]
        wm_p = jnp.concatenate([w_xg, w_q[:, perm], w_v], axis=1).astype(BF16)
        wt_p = w_k.T.astype(BF16)
        wm_s = jnp.concatenate([w_xg, w_v], axis=1).astype(BF16)
        wt_s = jnp.concatenate([w_k, w_q], axis=1).T.astype(BF16)
        g1 = ln1_g[l].reshape(1, D)
        wg = _gate_weights(w_ga[l], w_gx[l])
        bga = b_ga[l].reshape(1, d_rnn)
        bgx = b_gx[l].reshape(1, d_rnn)
        lru_l = lru_lambda[l].reshape(1, d_rnn)
        cw = conv_w[l]
        cb = conv_b[l].reshape(1, d_rnn)
        sg = subln_g[l].reshape(1, V_DIM)
        wo = w_out[l].astype(BF16)
        g2 = ln2_g[l].reshape(1, D)
        wr = jnp.zeros((D, LANES), F32).at[:, :n_groups].set(w_route_group[l]).at[
            :, n_groups:n_groups + n_experts].set(w_route_expert[l])
        br = jnp.zeros((1, LANES), F32).at[0, :n_groups].set(b_route_group[l]).at[
            0, n_groups:n_groups + n_experts].set(b_route_expert[l])

        xr, gr, q, v, vb, kt, ktb = _inproj_prompt(xp, g1, wm_p, wt_p, cq, sq, ct, st)
        y_rnn, h_last, c_tail = _lru_prompt(xr, gr, cw, cb, wg, bga, bgx, lru_l)
        att = _attn_prompt(q, ktb, vb, sg, lam, out_scale)
        xr_s, gr_s, v_s, kt_s, qt_s = _inproj_sample(xs, g1, wm_s, wt_s, ct_s, st_s)
        y_rnn_s, h_s, c_s = _lru_step(xr_s, gr_s, conv_s[l], state_h[l], cw, cb, wg, bga, bgx, lru_l)
        k_s = kt_s.T
        att_s = _attn_sample(pt_flat, lam, qt_s.T, k_s, v_s, sg, ck, cv, l, out_scale)
        cnt0 = jnp.zeros((1, LANES), F32)
        x1, h2, route, cnt1 = _outproj_route(y_rnn.reshape(Np, d_rnn), att.reshape(Np, d_att),
                                             xp.reshape(Np, D), wo, g2, wr, br, cnt0, n_groups, per_group)
        x1_s, h2_s, route_s, cnt2 = _outproj_route(y_rnn_s, att_s, xs, wo, g2, wr, br, cnt1,
                                                   n_groups, per_group)
        counts = cnt2[0, n_groups:n_groups + n_experts].astype(jnp.int32)
        seg_start, be, bi, brows, bfirst, zb = _block_tables(counts, n_blocks)
        dest = lambda r: (seg_start[r[:, 0:2].astype(jnp.int32)] + r[:, 4:6].astype(jnp.int32)).reshape(-1)
        dest_p, dest_s = dest(route), dest(route_s)
        xb = _dispatch(zb, dest_p, h2, None, n_blocks * EXP_BLOCK)
        xb = _dispatch(zb, dest_s, h2_s, xb, n_blocks * EXP_BLOCK)
        yb = _experts(be, bi, brows, bfirst, xb, w_exp_in, w_exp_out, l)
        final = l == depth - 1
        gf = ln_f.reshape(1, D)
        xp_new = _combine(dest_p, x1, route, gf, yb, final)
        xs = _combine(dest_s, x1_s, route_s, gf, yb, final)
        xp = xp_new.reshape(Bp, Sp, D)

        n_qk = d_att // HEAD_DIM
        kp_l.append(jnp.transpose(kt.reshape(Bp, n_qk, HEAD_DIM, Sp), (0, 3, 1, 2)))
        vp_l.append(v.reshape(Bp, Sp, d_att // V_DIM, V_DIM))
        hp_l.append(h_last.reshape(Bp, d_rnn))
        cp_l.append(c_tail)
        ks_l.append(k_s.reshape(Bs, 1, n_qk, HEAD_DIM))
        vs_l.append(v_s.reshape(Bs, 1, d_att // V_DIM, V_DIM))
        hs_l.append(h_s)
        cs_l.append(jnp.transpose(c_s, (1, 0, 2)))
    return (xp, xs.reshape(Bs, 1, D),
            jnp.stack(kp_l), jnp.stack(vp_l), jnp.stack(hp_l), jnp.stack(cp_l),
            jnp.stack(ks_l), jnp.stack(vs_l), jnp.stack(hs_l), jnp.stack(cs_l))
```

```python
import functools
import math

import jax
import jax.numpy as jnp
from jax import lax
from jax.experimental import pallas as pl
from jax.experimental.pallas import tpu as pltpu

F32 = jnp.float32
BF16 = jnp.bfloat16

HEAD_DIM = 64
V_DIM = 2 * HEAD_DIM
HALF = HEAD_DIM // 2
CONV_W = 4
LRU_C = 8.0
ROPE_THETA = 10000.0
NORM_EPS = 1e-6
SUBLN_EPS = 1e-5
NEG_INF = -1e30
TOP_K = 2

LANES = 128
SUBLANES = 8
VMEM_LIMIT = 48 * 1024 * 1024

ROW_TILE = 512
LRU_TILE = 256
ATT_TILE = 512
EXP_BLOCK = 256
DISPATCH_TILE = 512
COMBINE_TILE = 256


def _cparams(sem):
    return pltpu.CompilerParams(dimension_semantics=sem, vmem_limit_bytes=VMEM_LIMIT)


def _inproj_kernel(x_ref, g_ref, wm_ref, wt_ref, cq_ref, sq_ref, ct_ref, st_ref, *outs, sample):
    x = x_ref[...]
    ms = jnp.mean(x * x, axis=-1, keepdims=True)
    h = (x * lax.rsqrt(ms + NORM_EPS) * g_ref[...]).astype(BF16)
    main = jnp.dot(h, wm_ref[...], preferred_element_type=F32)
    d_rnn = outs[0].shape[1]
    ct = ct_ref[...]
    st = st_ref[...]
    tr = lax.dot_general(wt_ref[...], h, (((1,), (1,)), ((), ())), preferred_element_type=F32)

    def rope_t(blk):
        x1 = blk[:HALF]
        x2 = blk[HALF:]
        return x1 * ct - x2 * st, x2 * ct + x1 * st

    if sample:
        xr_ref, gr_ref, v_ref, kt_ref, qt_ref = outs
        xr_ref[...] = main[:, :d_rnn]
        gr_ref[...] = main[:, d_rnn:2 * d_rnn]
        v_ref[...] = main[:, 2 * d_rnn:]
        n_heads = kt_ref.shape[0] // HEAD_DIM
        for hh in range(n_heads):
            o1, o2 = rope_t(tr[hh * HEAD_DIM:(hh + 1) * HEAD_DIM])
            kt_ref[hh * HEAD_DIM:hh * HEAD_DIM + HALF, :] = o1
            kt_ref[hh * HEAD_DIM + HALF:(hh + 1) * HEAD_DIM, :] = o2
            base = (n_heads + hh) * HEAD_DIM
            p1, p2 = rope_t(tr[base:base + HEAD_DIM])
            qt_ref[hh * HEAD_DIM:hh * HEAD_DIM + HALF, :] = p1
            qt_ref[hh * HEAD_DIM + HALF:(hh + 1) * HEAD_DIM, :] = p2
    else:
        xr_ref, gr_ref, kb_ref, v_ref, qt_ref, kt_ref, vt_ref = outs
        xr_ref[...] = main[:, :d_rnn]
        gr_ref[...] = main[:, d_rnn:2 * d_rnn]
        d_att = kb_ref.shape[1]
        k = main[:, 2 * d_rnn:2 * d_rnn + d_att]
        cq = cq_ref[...]
        sq = sq_ref[...]
        for c in range(d_att // LANES):
            kc = k[:, c * LANES:(c + 1) * LANES]
            kb_ref[:, c * LANES:(c + 1) * LANES] = (kc * cq + pltpu.roll(kc, HEAD_DIM, 1) * sq).astype(BF16)
        v_ref[...] = main[:, 2 * d_rnn + d_att:]
        n_heads = d_att // HEAD_DIM
        for hh in range(n_heads):
            g, jj = divmod(hh, 2)
            r1 = g * V_DIM + jj * HALF
            x1 = tr[r1:r1 + HALF]
            x2 = tr[r1 + HEAD_DIM:r1 + HEAD_DIM + HALF]
            qt_ref[r1:r1 + HALF, :] = (x1 * ct - x2 * st).astype(BF16)
            qt_ref[r1 + HEAD_DIM:r1 + HEAD_DIM + HALF, :] = (x2 * ct + x1 * st).astype(BF16)
            base = d_att + hh * HEAD_DIM
            o1, o2 = rope_t(tr[base:base + HEAD_DIM])
            kt_ref[hh * HEAD_DIM:hh * HEAD_DIM + HALF, :] = o1
            kt_ref[hh * HEAD_DIM + HALF:(hh + 1) * HEAD_DIM, :] = o2
        vt_ref[...] = tr[2 * d_att:].astype(BF16)


def _inproj_prompt(x, g, wm, wt, cq, sq, ct, st):
    B, S, D = x.shape
    T = min(ROW_TILE, S)
    nt = S // T
    d_att = wt.shape[0] // 3
    d_rnn = (wm.shape[1] - 2 * d_att) // 2
    row = lambda w: pl.BlockSpec((None, T, w), lambda b, t: (b, t, 0))
    full = lambda a: pl.BlockSpec(a.shape, lambda b, t: (0,) * a.ndim)
    once = lambda a: pl.BlockSpec(a.shape, lambda b, t: (0,) * a.ndim, pipeline_mode=pl.Buffered(1))
    tr_spec = pl.BlockSpec((None, d_att, T), lambda b, t: (b, 0, t))
    return pl.pallas_call(
        functools.partial(_inproj_kernel, sample=False),
        grid=(B, nt),
        in_specs=[row(D), full(g), once(wm), once(wt),
                  pl.BlockSpec((T, LANES), lambda b, t: (t, 0)),
                  pl.BlockSpec((T, LANES), lambda b, t: (t, 0)),
                  pl.BlockSpec((HALF, T), lambda b, t: (0, t)),
                  pl.BlockSpec((HALF, T), lambda b, t: (0, t))],
        out_specs=[row(d_rnn), row(d_rnn), row(d_att), row(d_att), tr_spec, tr_spec, tr_spec],
        out_shape=[jax.ShapeDtypeStruct((B, S, d_rnn), F32), jax.ShapeDtypeStruct((B, S, d_rnn), F32),
                   jax.ShapeDtypeStruct((B, S, d_att), BF16), jax.ShapeDtypeStruct((B, S, d_att), F32),
                   jax.ShapeDtypeStruct((B, d_att, S), BF16), jax.ShapeDtypeStruct((B, d_att, S), F32),
                   jax.ShapeDtypeStruct((B, d_att, S), BF16)],
        compiler_params=_cparams(("parallel", "arbitrary")),
        name="inproj_prompt",
    )(x, g, wm, wt, cq, sq, ct, st)


def _inproj_sample(x, g, wm, wt, ct, st):
    n, D = x.shape
    d_att = wt.shape[0] // 2
    d_rnn = (wm.shape[1] - d_att) // 2
    full = lambda a: pl.BlockSpec(a.shape, lambda i: (0,) * a.ndim)
    dummy = jnp.zeros((SUBLANES, LANES), F32)
    out_shapes = [jax.ShapeDtypeStruct((n, d_rnn), F32), jax.ShapeDtypeStruct((n, d_rnn), F32),
                  jax.ShapeDtypeStruct((n, d_att), F32), jax.ShapeDtypeStruct((d_att, n), F32),
                  jax.ShapeDtypeStruct((d_att, n), F32)]
    return pl.pallas_call(
        functools.partial(_inproj_kernel, sample=True),
        grid=(1,),
        in_specs=[full(x), full(g), full(wm), full(wt), full(dummy), full(dummy), full(ct), full(st)],
        out_specs=[pl.BlockSpec(s.shape, lambda i: (0, 0)) for s in out_shapes],
        out_shape=out_shapes,
        compiler_params=_cparams(("arbitrary",)),
        name="inproj_sample",
    )(x, g, wm, wt, dummy, dummy, ct, st)


def _gates(xc, wg_ref, bga, bgx, lam):
    half = xc.shape[1] // 2
    xcb = xc.astype(BF16)
    g0 = jnp.dot(xcb[:, :half], wg_ref[0], preferred_element_type=F32)
    g1 = jnp.dot(xcb[:, half:], wg_ref[1], preferred_element_type=F32)
    r = jax.nn.sigmoid(jnp.concatenate([g0[:, :half], g1[:, :half]], axis=1) + bga)
    i = jax.nn.sigmoid(jnp.concatenate([g0[:, half:], g1[:, half:]], axis=1) + bgx)
    neg = -lam
    softplus = jnp.maximum(neg, 0.0) + jnp.log1p(jnp.exp(-jnp.abs(neg)))
    log_a = -LRU_C * r * softplus
    a = jnp.exp(log_a)
    mult = jnp.sqrt(1.0 - a * a)
    return a, i, mult


def _lru_kernel(xr_ref, gr_ref, cw_ref, cb_ref, wg_ref, bga_ref, bgx_ref, lam_ref,
                y_ref, hlast_ref, ctail_ref, xp_scr, a_scr, b_scr, h_scr):
    t = pl.program_id(1)
    T, C = xr_ref.shape

    @pl.when(t == 0)
    def _():
        xp_scr[0:SUBLANES, :] = jnp.zeros((SUBLANES, C), F32)
        h_scr[...] = jnp.zeros_like(h_scr)

    x = xr_ref[...]
    xp_scr[SUBLANES:SUBLANES + T, :] = x
    cw = cw_ref[...]
    xc = cb_ref[...]
    for j in range(CONV_W - 1):
        xc = xc + xp_scr[pl.ds(SUBLANES - (CONV_W - 1) + j, T), :] * cw[j:j + 1]
    xc = xc + x * cw[CONV_W - 1:CONV_W]
    tail = x[T - (CONV_W - 1):, :]
    xp_scr[SUBLANES - (CONV_W - 1):SUBLANES, :] = tail
    ctail_ref[...] = tail

    a, i, mult = _gates(xc, wg_ref, bga_ref[...], bgx_ref[...], lam_ref[...])
    row = lax.broadcasted_iota(jnp.int32, (T, 1), 0)
    mult = jnp.where(jnp.logical_and(row == 0, t == 0), 1.0, mult)
    a_scr[...] = a
    b_scr[...] = mult * i * xc

    rowi = lax.broadcasted_iota(jnp.int32, (SUBLANES, C), 0)

    def body(gi, h):
        off = pl.multiple_of(gi * SUBLANES, SUBLANES)
        A = a_scr[pl.ds(off, SUBLANES), :]
        Bv = b_scr[pl.ds(off, SUBLANES), :]
        for d in (1, 2, 4):
            keep = rowi >= d
            Bv = jnp.where(keep, Bv + A * pltpu.roll(Bv, d, 0), Bv)
            A = jnp.where(keep, A * pltpu.roll(A, d, 0), A)
        hrows = Bv + A * h
        b_scr[pl.ds(off, SUBLANES), :] = hrows
        return hrows[SUBLANES - 1:SUBLANES, :]

    h = lax.fori_loop(0, T // SUBLANES, body, h_scr[...], unroll=2)
    h_scr[...] = h
    hlast_ref[...] = h
    y_ref[...] = (b_scr[...] * jax.nn.gelu(gr_ref[...])).astype(BF16)


def _lru_prompt(xr, gr, cw, cb, wg, bga, bgx, lam):
    B, S, C = xr.shape
    T = min(LRU_TILE, S)
    nt = S // T
    row = pl.BlockSpec((None, T, C), lambda b, t: (b, t, 0))
    full = lambda a: pl.BlockSpec(a.shape, lambda b, t: (0,) * a.ndim)
    return pl.pallas_call(
        _lru_kernel,
        grid=(B, nt),
        in_specs=[row, row, full(cw), full(cb), full(wg), full(bga), full(bgx), full(lam)],
        out_specs=[row, pl.BlockSpec((None, 1, C), lambda b, t: (b, 0, 0)),
                   pl.BlockSpec((None, CONV_W - 1, C), lambda b, t: (b, 0, 0))],
        out_shape=[jax.ShapeDtypeStruct((B, S, C), BF16), jax.ShapeDtypeStruct((B, 1, C), F32),
                   jax.ShapeDtypeStruct((B, CONV_W - 1, C), F32)],
        scratch_shapes=[pltpu.VMEM((T + SUBLANES, C), F32), pltpu.VMEM((T, C), F32),
                        pltpu.VMEM((T, C), F32), pltpu.VMEM((1, C), F32)],
        compiler_params=_cparams(("parallel", "arbitrary")),
        name="rglru_prompt",
    )(xr, gr, cw, cb, wg, bga, bgx, lam)


def _lru_step_kernel(xr_ref, gr_ref, conv_ref, h0_ref, cw_ref, cb_ref, wg_ref, bga_ref, bgx_ref, lam_ref,
                     y_ref, hnew_ref, cnew_ref):
    x = xr_ref[...]
    cw = cw_ref[...]
    xc = cb_ref[...]
    for j in range(CONV_W - 1):
        xc = xc + conv_ref[j] * cw[j:j + 1]
    xc = xc + x * cw[CONV_W - 1:CONV_W]
    for j in range(CONV_W - 2):
        cnew_ref[j] = conv_ref[j + 1]
    cnew_ref[CONV_W - 2] = x
    a, i, mult = _gates(xc, wg_ref, bga_ref[...], bgx_ref[...], lam_ref[...])
    h = a * h0_ref[...] + mult * i * xc
    hnew_ref[...] = h
    y_ref[...] = (h * jax.nn.gelu(gr_ref[...])).astype(BF16)


def _lru_step(xr, gr, conv, h0, cw, cb, wg, bga, bgx, lam):
    n, C = xr.shape
    args = (xr, gr, conv, h0, cw, cb, wg, bga, bgx, lam)
    full = lambda a: pl.BlockSpec(a.shape, lambda i: (0,) * a.ndim)
    out_shapes = [jax.ShapeDtypeStruct((n, C), BF16), jax.ShapeDtypeStruct((n, C), F32),
                  jax.ShapeDtypeStruct((CONV_W - 1, n, C), F32)]
    return pl.pallas_call(
        _lru_step_kernel,
        grid=(1,),
        in_specs=[full(a) for a in args],
        out_specs=[pl.BlockSpec(s.shape, lambda i, nd=len(s.shape): (0,) * nd) for s in out_shapes],
        out_shape=out_shapes,
        compiler_params=_cparams(("arbitrary",)),
        name="rglru_step",
    )(*args)


def _subln(o, g, scale):
    ms = jnp.mean(o * o, axis=-1, keepdims=True)
    return o * lax.rsqrt(ms + SUBLN_EPS) * g * scale


def _attn_kernel(qi_ref, kj_ref, lam_ref, qt_ref, kb_ref, vt_ref, g_ref, o_ref,
                 qcat, m, l, acc, *, out_scale):
    t = pl.program_id(1)
    qi = qi_ref[t]
    kj = kj_ref[t]
    Tq = qt_ref.shape[1]
    Tk = kb_ref.shape[0]

    @pl.when(kj == 0)
    def _():
        qt = qt_ref[...]
        row = lax.broadcasted_iota(jnp.int32, qt.shape, 0)
        second = (row & HALF) != 0
        zero = jnp.zeros_like(qt)
        qcat[:, :Tq] = jnp.where(second, zero, qt)
        qcat[:, Tq:] = jnp.where(second, qt, zero)
        m[...] = jnp.full_like(m, NEG_INF)
        l[...] = jnp.zeros_like(l)
        acc[...] = jnp.zeros_like(acc)

    def step(diagonal):
        s = jnp.dot(kb_ref[...], qcat[...], preferred_element_type=F32)
        if diagonal:
            key = lax.broadcasted_iota(jnp.int32, s.shape, 0)
            qry = lax.broadcasted_iota(jnp.int32, s.shape, 1)
            qry = jnp.where(qry >= Tq, qry - Tq, qry)
            s = jnp.where(key <= qry, s, NEG_INF)
        m_old = m[...]
        m_new = jnp.maximum(m_old, jnp.max(s, axis=0, keepdims=True))
        alpha = jnp.exp(m_old - m_new)
        p = jnp.exp(s - m_new)
        l[...] = alpha * l[...] + jnp.sum(p, axis=0, keepdims=True)
        acc[...] = alpha * acc[...] + jnp.dot(vt_ref[...], p.astype(BF16), preferred_element_type=F32)
        m[...] = m_new

    @pl.when(kj < qi)
    def _():
        step(False)

    @pl.when(kj == qi)
    def _():
        step(True)
        a = acc[...] / l[...]
        o = a[:, :Tq] - lam_ref[0, 0] * a[:, Tq:]
        ms = jnp.mean(o * o, axis=0, keepdims=True)
        o = o * lax.rsqrt(ms + SUBLN_EPS) * (g_ref[...] * out_scale)
        o_ref[...] = o.T.astype(o_ref.dtype)


def _attn_prompt(qt, kb, vt, subln_g, lam, out_scale):
    B, S, d_att = kb.shape
    G = d_att // V_DIM
    T = min(ATT_TILE, S)
    nq = S // T
    tri = [(i, j) for i in range(nq) for j in range(i + 1)]
    qi = jnp.asarray([p[0] for p in tri], jnp.int32)
    kj = jnp.asarray([p[1] for p in tri], jnp.int32)
    grid_spec = pltpu.PrefetchScalarGridSpec(
        num_scalar_prefetch=2,
        grid=(B * G, len(tri)),
        in_specs=[
            pl.BlockSpec(memory_space=pltpu.SMEM),
            pl.BlockSpec((None, V_DIM, T), lambda bg, t, qi, kj: (bg // G, bg % G, qi[t])),
            pl.BlockSpec((None, T, V_DIM), lambda bg, t, qi, kj: (bg // G, kj[t], bg % G)),
            pl.BlockSpec((None, V_DIM, T), lambda bg, t, qi, kj: (bg // G, bg % G, kj[t])),
            pl.BlockSpec((V_DIM, 1), lambda bg, t, qi, kj: (0, 0)),
        ],
        out_specs=pl.BlockSpec((None, T, V_DIM), lambda bg, t, qi, kj: (bg // G, qi[t], bg % G)),
        scratch_shapes=[pltpu.VMEM((V_DIM, 2 * T), BF16), pltpu.VMEM((1, 2 * T), F32),
                        pltpu.VMEM((1, 2 * T), F32), pltpu.VMEM((V_DIM, 2 * T), F32)],
    )
    return pl.pallas_call(
        functools.partial(_attn_kernel, out_scale=out_scale),
        grid_spec=grid_spec,
        out_shape=jax.ShapeDtypeStruct((B, S, d_att), BF16),
        compiler_params=_cparams(("parallel", "arbitrary")),
        name="diff_attn_prompt",
    )(qi, kj, lam, qt, kb, vt, subln_g.reshape(V_DIM, 1))


def _attn_sample_kernel(pt_ref, lam_ref, q_ref, kc_ref, vc_ref, g_ref, *rest, n_pages, out_scale):
    k_refs = rest[:n_pages]
    v_refs = rest[n_pages:2 * n_pages]
    o_ref = rest[2 * n_pages]
    lam = lam_ref[0, 0]
    q = q_ref[...]
    d_att = q.shape[1]
    n_heads = d_att // HEAD_DIM
    G = d_att // V_DIM
    R = 2 * SUBLANES
    rowi = lax.broadcasted_iota(jnp.int32, (R, d_att), 0)
    coli = lax.broadcasted_iota(jnp.int32, (R, d_att), 1)
    qblk = jnp.where(coli // HEAD_DIM == rowi, jnp.broadcast_to(q, (R, d_att)), 0.0)
    qb = qblk.astype(BF16)
    s = jnp.concatenate(
        [jnp.dot(qb, k_refs[j][...].astype(BF16), preferred_element_type=F32) for j in range(n_pages)], axis=1)
    s = s[:n_heads]
    s_cur = jnp.sum(qblk[:n_heads] * kc_ref[...], axis=-1, keepdims=True)
    m = jnp.maximum(jnp.max(s, axis=-1, keepdims=True), s_cur)
    p = jnp.exp(s - m)
    p_cur = jnp.exp(s_cur - m)
    inv_l = 1.0 / (jnp.sum(p, axis=-1, keepdims=True) + p_cur)
    p = p * inv_l
    p_cur = p_cur * inv_l
    amap = p - lam * pltpu.roll(p, n_heads - 1, 0)
    amap = jnp.concatenate([amap, jnp.zeros_like(amap)], axis=0).astype(BF16)
    p_cur = jnp.broadcast_to(p_cur, (n_heads, LANES))
    amap_cur = p_cur - lam * pltpu.roll(p_cur, n_heads - 1, 0)
    P = k_refs[0].shape[1]
    outs = []
    for g in range(G):
        acc = jnp.zeros((R, V_DIM), F32)
        for j in range(n_pages):
            vjg = v_refs[j][pl.ds(g, P, stride=G), :].astype(BF16)
            acc = acc + jnp.dot(amap[:, j * P:(j + 1) * P], vjg, preferred_element_type=F32)
        o = acc[2 * g:2 * g + 1, :] + amap_cur[2 * g:2 * g + 1, :] * vc_ref[:, g * V_DIM:(g + 1) * V_DIM]
        outs.append(_subln(o, g_ref[...], out_scale))
    o_ref[...] = jnp.concatenate(outs, axis=1).astype(o_ref.dtype)


def _attn_sample(pt_flat, lam, q, k_cur, v_cur, subln_g, ck, cv, layer, out_scale):
    n, d_att = q.shape
    n_pages = pt_flat.shape[0] // n
    P = ck.shape[-1]
    q3 = q.reshape(n, 1, d_att)
    kc3 = k_cur.reshape(n, 1, d_att)
    vc3 = v_cur.reshape(n, 1, d_att)
    tok = lambda: pl.BlockSpec((None, 1, d_att), lambda b, pt: (b, 0, 0))
    k_specs = [pl.BlockSpec((None, None, d_att, P), functools.partial(
        lambda b, pt, j: (layer, pt[b * n_pages + j], 0, 0), j=j)) for j in range(n_pages)]
    v_specs = [pl.BlockSpec((None, None, cv.shape[2], V_DIM), functools.partial(
        lambda b, pt, j: (layer, pt[b * n_pages + j], 0, 0), j=j)) for j in range(n_pages)]
    grid_spec = pltpu.PrefetchScalarGridSpec(
        num_scalar_prefetch=1,
        grid=(n,),
        in_specs=[pl.BlockSpec(memory_space=pltpu.SMEM), tok(), tok(), tok(),
                  pl.BlockSpec((1, V_DIM), lambda b, pt: (0, 0))] + k_specs + v_specs,
        out_specs=pl.BlockSpec((None, 1, d_att), lambda b, pt: (b, 0, 0)),
    )
    out = pl.pallas_call(
        functools.partial(_attn_sample_kernel, n_pages=n_pages, out_scale=out_scale),
        grid_spec=grid_spec,
        out_shape=jax.ShapeDtypeStruct((n, 1, d_att), BF16),
        compiler_params=_cparams(("arbitrary",)),
        name="diff_attn_sample",
    )(pt_flat, lam, q3, kc3, vc3, subln_g, *([ck] * n_pages), *([cv] * n_pages))
    return out.reshape(n, d_att)


def _outproj_kernel(yr_ref, at_ref, x_ref, wo_ref, g2_ref, wr_ref, br_ref, cnt_in_ref,
                    x1_ref, h2_ref, route_ref, cnt_out_ref, cnt_scr, *, n_groups, per_group):
    step = pl.program_id(0)

    @pl.when(step == 0)
    def _():
        cnt_scr[...] = cnt_in_ref[...]

    d_rnn = yr_ref.shape[1]
    y = (jnp.dot(yr_ref[...], wo_ref[:d_rnn, :], preferred_element_type=F32)
         + jnp.dot(at_ref[...], wo_ref[d_rnn:, :], preferred_element_type=F32))
    x1 = x_ref[...] + y
    x1_ref[...] = x1
    ms = jnp.mean(x1 * x1, axis=-1, keepdims=True)
    h2 = x1 * lax.rsqrt(ms + NORM_EPS) * g2_ref[...]
    h2_ref[...] = h2
    logits = jnp.dot(h2.astype(BF16), wr_ref[...], preferred_element_type=F32) + br_ref[...]
    T = logits.shape[0]
    lane = lax.broadcasted_iota(jnp.int32, (T, LANES), 1).astype(F32)
    first = lambda hit: jnp.min(jnp.where(hit, lane, float(LANES)), axis=-1, keepdims=True)
    gl = jnp.where(lane < n_groups, logits, NEG_INF)
    gmax = jnp.max(gl, axis=-1, keepdims=True)
    gidx = first(gl == gmax)
    pg = 1.0 / jnp.sum(jnp.exp(gl - gmax), axis=-1, keepdims=True)
    lo = n_groups + per_group * gidx
    el = jnp.where(jnp.logical_and(lane >= lo, lane < lo + per_group), logits, NEG_INF)
    ma = jnp.max(el, axis=-1, keepdims=True)
    ia = first(el == ma)
    el2 = jnp.where(lane == ia, NEG_INF, el)
    mb = jnp.max(el2, axis=-1, keepdims=True)
    ib = first(el2 == mb)
    eb = jnp.exp(mb - ma)
    gate_a = pg / (1.0 + eb)
    gate_b = pg * eb / (1.0 + eb)
    hit_a = lane == ia
    hit_b = lane == ib
    onehot = jnp.where(jnp.logical_or(hit_a, hit_b), 1.0, 0.0)
    r_i = lax.broadcasted_iota(jnp.int32, (T, T), 0)
    c_i = lax.broadcasted_iota(jnp.int32, (T, T), 1)
    lower = jnp.where(c_i < r_i, 1.0, 0.0).astype(BF16)
    before = jnp.dot(lower, onehot.astype(BF16), preferred_element_type=F32) + cnt_scr[...]
    rank_a = jnp.sum(jnp.where(hit_a, before, 0.0), axis=-1, keepdims=True)
    rank_b = jnp.sum(jnp.where(hit_b, before, 0.0), axis=-1, keepdims=True)
    cnt_scr[...] = cnt_scr[...] + jnp.sum(onehot, axis=0, keepdims=True)
    cnt_out_ref[...] = cnt_scr[...]
    cols = (ia - n_groups, ib - n_groups, gate_a, gate_b, rank_a, rank_b)
    route = jnp.zeros((T, LANES), F32)
    for c, val in enumerate(cols):
        route = jnp.where(lane == float(c), val, route)
    route_ref[...] = route


def _outproj_route(yr, at, x, wo, g2, wr, br, cnt_in, n_groups, per_group):
    N, D = x.shape
    T = min(ROW_TILE, N)
    d_rnn = yr.shape[1]
    row = lambda w: pl.BlockSpec((T, w), lambda i: (i, 0))
    full = lambda a: pl.BlockSpec(a.shape, lambda i: (0,) * a.ndim)
    return pl.pallas_call(
        functools.partial(_outproj_kernel, n_groups=n_groups, per_group=per_group),
        grid=(N // T,),
        in_specs=[row(d_rnn), row(at.shape[1]), row(D), full(wo), full(g2), full(wr), full(br), full(cnt_in)],
        out_specs=[row(D), row(D), row(LANES), pl.BlockSpec((1, LANES), lambda i: (0, 0))],
        out_shape=[jax.ShapeDtypeStruct((N, D), F32), jax.ShapeDtypeStruct((N, D), F32),
                   jax.ShapeDtypeStruct((N, LANES), F32), jax.ShapeDtypeStruct((1, LANES), F32)],
        scratch_shapes=[pltpu.VMEM((1, LANES), F32)],
        compiler_params=_cparams(("arbitrary",)),
        name="outproj_route",
    )(yr, at, x, wo, g2, wr, br, cnt_in)


def _dispatch_kernel(zb_ref, dest_ref, h_ref, *rest, zero_fill):
    xb_hbm, zbuf, sem = rest[-3], rest[-2], rest[-1]
    T = h_ref.shape[0]

    if zero_fill:
        @pl.when(pl.program_id(0) == 0)
        def _():
            zbuf[...] = jnp.zeros_like(zbuf)

            def zcopy(i):
                row0 = pl.multiple_of(zb_ref[i] * EXP_BLOCK, EXP_BLOCK)
                return pltpu.make_async_copy(zbuf, xb_hbm.at[pl.ds(row0, EXP_BLOCK)], sem)

            def zstart(i, c):
                @pl.when(zb_ref[i] >= 0)
                def _():
                    zcopy(i).start()
                return c

            def zwait(i, c):
                @pl.when(zb_ref[i] >= 0)
                def _():
                    zcopy(i).wait()
                return c

            lax.fori_loop(0, zb_ref.shape[0], zstart, 0)
            lax.fori_loop(0, zb_ref.shape[0], zwait, 0)

    def body(i, c):
        for s in range(TOP_K):
            pltpu.make_async_copy(h_ref.at[pl.ds(i, 1)],
                                  xb_hbm.at[pl.ds(dest_ref[0, TOP_K * i + s], 1)], sem).start(priority=s)
        return c

    lax.fori_loop(0, T, body, 0, unroll=8)
    for s in range(TOP_K):
        pltpu.make_async_copy(h_ref, xb_hbm.at[pl.ds(0, T)], sem).wait()


def _dispatch(zb, dest, h, xb, xb_rows):
    N, D = h.shape
    T = min(DISPATCH_TILE, N)
    dest3 = dest.reshape(N // T, 1, TOP_K * T)
    aliased = xb is not None
    in_specs = [pl.BlockSpec((None, 1, TOP_K * T), lambda i, zb: (i, 0, 0), memory_space=pltpu.SMEM),
                pl.BlockSpec((T, D), lambda i, zb: (i, 0))]
    args = [zb, dest3, h]
    if aliased:
        in_specs.append(pl.BlockSpec(memory_space=pl.ANY))
        args.append(xb)
    grid_spec = pltpu.PrefetchScalarGridSpec(
        num_scalar_prefetch=1,
        grid=(N // T,),
        in_specs=in_specs,
        out_specs=pl.BlockSpec(memory_space=pl.ANY),
        scratch_shapes=[pltpu.VMEM((EXP_BLOCK, D), F32), pltpu.SemaphoreType.DMA(())],
    )
    return pl.pallas_call(
        functools.partial(_dispatch_kernel, zero_fill=not aliased),
        grid_spec=grid_spec,
        out_shape=jax.ShapeDtypeStruct((xb_rows, D), F32),
        input_output_aliases={3: 0} if aliased else {},
        compiler_params=pltpu.CompilerParams(dimension_semantics=("arbitrary",), has_side_effects=True),
        name="moe_dispatch",
    )(*args)


def _expert_kernel(be_ref, bi_ref, br_ref, bf_ref, x_ref, w1_ref, w2_ref, y_ref, w1b, w2b):
    b = pl.program_id(0)
    rows = br_ref[b]

    @pl.when(bf_ref[b] == 1)
    def _():
        w1b[...] = w1_ref[...].astype(BF16)
        w2b[...] = w2_ref[...].astype(BF16)

    @pl.when(rows > 0)
    def _():
        h = jnp.dot(x_ref[...].astype(BF16), w1b[...], preferred_element_type=F32)
        de = h.shape[1] // 2
        act = (jax.nn.silu(h[:, :de]) * h[:, de:]).astype(BF16)
        y_ref[...] = jnp.dot(act, w2b[...], preferred_element_type=F32)

    @pl.when(rows == 0)
    def _():
        y_ref[...] = jnp.zeros_like(y_ref)


def _experts(be, bi, br, bf, xb, w1, w2, layer):
    R, D = xb.shape
    nb = be.shape[0]
    d2 = w1.shape[-1]
    de = w2.shape[-2]
    grid_spec = pltpu.PrefetchScalarGridSpec(
        num_scalar_prefetch=4,
        grid=(nb,),
        in_specs=[pl.BlockSpec((EXP_BLOCK, D), lambda b, be, bi, br, bf: (bi[b], 0)),
                  pl.BlockSpec((None, None, D, d2), lambda b, be, bi, br, bf: (layer, be[b], 0, 0)),
                  pl.BlockSpec((None, None, de, D), lambda b, be, bi, br, bf: (layer, be[b], 0, 0))],
        out_specs=pl.BlockSpec((EXP_BLOCK, D), lambda b, be, bi, br, bf: (b, 0)),
        scratch_shapes=[pltpu.VMEM((D, d2), BF16), pltpu.VMEM((de, D), BF16)],
    )
    return pl.pallas_call(
        _expert_kernel,
        grid_spec=grid_spec,
        out_shape=jax.ShapeDtypeStruct((R, D), F32),
        compiler_params=_cparams(("arbitrary",)),
        name="moe_experts",
    )(be, bi, br, bf, xb, w1, w2)


def _combine_kernel(dest_ref, x1_ref, route_ref, gf_ref, yb_hbm, o_ref, ybuf, sem, *, final):
    T = x1_ref.shape[0]

    def body(i, c):
        for s in range(TOP_K):
            pltpu.make_async_copy(yb_hbm.at[pl.ds(dest_ref[0, TOP_K * i + s], 1)],
                                  ybuf.at[s, pl.ds(i, 1)], sem).start()
        return c

    lax.fori_loop(0, T, body, 0)
    for s in range(TOP_K):
        pltpu.make_async_copy(yb_hbm.at[pl.ds(0, T)], ybuf.at[s], sem).wait()
    route = route_ref[...]
    out = x1_ref[...] + ybuf[0] * route[:, 2:3] + ybuf[1] * route[:, 3:4]
    if final:
        ms = jnp.mean(out * out, axis=-1, keepdims=True)
        out = out * lax.rsqrt(ms + NORM_EPS) * gf_ref[...]
    o_ref[...] = out


def _combine(dest, x1, route, gf, yb, final):
    N, D = x1.shape
    T = min(COMBINE_TILE, N)
    dest3 = dest.reshape(N // T, 1, TOP_K * T)
    return pl.pallas_call(
        functools.partial(_combine_kernel, final=final),
        grid=(N // T,),
        in_specs=[pl.BlockSpec((None, 1, TOP_K * T), lambda i: (i, 0, 0), memory_space=pltpu.SMEM),
                  pl.BlockSpec((T, D), lambda i: (i, 0)),
                  pl.BlockSpec((T, LANES), lambda i: (i, 0)),
                  pl.BlockSpec((1, D), lambda i: (0, 0)),
                  pl.BlockSpec(memory_space=pl.ANY)],
        out_specs=pl.BlockSpec((T, D), lambda i: (i, 0)),
        out_shape=jax.ShapeDtypeStruct((N, D), F32),
        scratch_shapes=[pltpu.VMEM((TOP_K, T, D), F32), pltpu.SemaphoreType.DMA(())],
        compiler_params=_cparams(("arbitrary",)),
        name="moe_combine",
    )(dest3, x1, route, gf, yb)


def _rope_tables(pos):
    inv = ROPE_THETA ** (-jnp.arange(HALF, dtype=F32) / HALF)
    ang = pos.astype(F32)[:, None] * inv[None, :]
    cos, sin = jnp.cos(ang), jnp.sin(ang)
    cq = jnp.tile(cos, (1, LANES // HALF))
    sq = jnp.concatenate([-sin, -sin, sin, sin], axis=1)
    return cq, sq, cos.T, sin.T


def _q_perm(d_att):
    idx = []
    for g in range(d_att // V_DIM):
        for part in range(2):
            for jj in range(2):
                base = HEAD_DIM * (2 * g + jj) + HALF * part
                idx.extend(range(base, base + HALF))
    return jnp.asarray(idx, jnp.int32)


def _gate_weights(w_ga, w_gx):
    nb, bw, _ = w_ga.shape
    half = nb // 2
    eye = jnp.eye(half, dtype=w_ga.dtype)

    def dense(w):
        return jnp.einsum('nij,nm->nimj', w, eye).reshape(half * bw, half * bw)

    return jnp.stack([jnp.concatenate([dense(w_ga[h * half:(h + 1) * half]),
                                       dense(w_gx[h * half:(h + 1) * half])], axis=1)
                      for h in range(2)]).astype(BF16)


def _block_tables(counts, nb):
    ne = counts.shape[0]
    nblk = (counts + EXP_BLOCK - 1) // EXP_BLOCK
    cum = jnp.cumsum(nblk)
    total = cum[-1]
    b = jnp.arange(nb, dtype=jnp.int32)
    bc = jnp.minimum(b, total - 1)
    e = jnp.minimum(jnp.sum(bc[:, None] >= cum[None, :], axis=1), ne - 1).astype(jnp.int32)
    off = bc - (cum[e] - nblk[e])
    valid = b < total
    rows = jnp.where(valid, jnp.clip(counts[e] - off * EXP_BLOCK, 0, EXP_BLOCK), 0)
    first = jnp.where(jnp.logical_and(valid, off == 0), 1, 0)
    last_blk = jnp.where(nblk > 0, cum - 1, -1)
    trail = total + jnp.arange(ne, dtype=jnp.int32)
    trail = jnp.where(trail < nb, trail, -1)
    zb = jnp.concatenate([last_blk, trail]).astype(jnp.int32)
    seg_start = ((cum - nblk) * EXP_BLOCK).astype(jnp.int32)
    return seg_start, e, bc.astype(jnp.int32), rows.astype(jnp.int32), first.astype(jnp.int32), zb


def kernel(x_prompt, x_sample, cache_k, cache_v, state_h, state_conv, page_table, ln1_g, w_in, conv_w, conv_b,
           w_ga, b_ga, w_gx, b_gx, lru_lambda, lam_q1, lam_k1, lam_q2, lam_k2, subln_g, w_out, ln2_g,
           w_route_group, b_route_group, w_route_expert, b_route_expert, w_exp_in, w_exp_out, ln_f):
    Bp, Sp, D = x_prompt.shape
    Bs = x_sample.shape[0]
    depth = w_in.shape[0]
    d_rnn = state_h.shape[-1]
    d_att = cache_k.shape[-2] * cache_k.shape[-1]
    n_groups = w_route_group.shape[-1]
    n_experts = w_route_expert.shape[-1]
    per_group = n_experts // n_groups
    n_pool, page = cache_k.shape[1], cache_k.shape[2]
    n_pages = page_table.shape[1]
    past_len = n_pages * page
    Np = Bp * Sp
    assert x_sample.shape[1] == 1 and n_groups + n_experts <= LANES

    cq, sq, ct, st = _rope_tables(jnp.arange(Sp))
    _, _, ct_s, st_s = _rope_tables(jnp.full((Bs,), past_len))
    perm = _q_perm(d_att)
    qscale = HEAD_DIM ** -0.5
    ck = jnp.transpose(cache_k, (0, 1, 3, 4, 2)).reshape(depth, n_pool, d_att, page)
    cv = cache_v.reshape(depth, n_pool, page * (d_att // V_DIM), V_DIM)
    pt_flat = page_table.reshape(-1).astype(jnp.int32)
    conv_s = jnp.transpose(state_conv, (0, 2, 1, 3))

    n_blocks = (TOP_K * (Np + Bs)) // EXP_BLOCK + n_experts

    xp = x_prompt
    xs = x_sample.reshape(Bs, D)
    kp_l, vp_l, hp_l, cp_l, ks_l, vs_l, hs_l, cs_l = [], [], [], [], [], [], [], []
    for l in range(depth):
        lambda_init = 0.8 - 0.6 * math.exp(-0.3 * l)
        out_scale = 1.0 - lambda_init
        lam = (jnp.exp(jnp.sum(lam_q1[l] * lam_k1[l])) - jnp.exp(jnp.sum(lam_q2[l] * lam_k2[l]))
               + lambda_init).reshape(1, 1).astype(F32)
        wl = w_in[l]
        w_xg = wl[:, :2 * d_rnn]
        w_q = wl[:, 2 * d_rnn:2 * d_rnn + d_att] * qscale
        w_k = wl[:, 2 * d_rnn + d_att:2 * d_rnn + 2 * d_att]
        w_v = wl[:, 2 * d_rnn + 2 * d_att:]
        wm_p = jnp.concatenate([w_xg, w_k[:, perm], w_v], axis=1).astype(BF16)
        wt_p = jnp.concatenate([w_q[:, perm], w_k, w_v], axis=1).T.astype(BF16)
        wm_s = jnp.concatenate([w_xg, w_v], axis=1).astype(BF16)
        wt_s = jnp.concatenate([w_k, w_q], axis=1).T.astype(BF16)
        g1 = ln1_g[l].reshape(1, D)
        wg = _gate_weights(w_ga[l], w_gx[l])
        bga = b_ga[l].reshape(1, d_rnn)
        bgx = b_gx[l].reshape(1, d_rnn)
        lru_l = lru_lambda[l].reshape(1, d_rnn)
        cw = conv_w[l]
        cb = conv_b[l].reshape(1, d_rnn)
        sg = subln_g[l].reshape(1, V_DIM)
        wo = w_out[l].astype(BF16)
        g2 = ln2_g[l].reshape(1, D)
        wr = jnp.zeros((D, LANES), F32).at[:, :n_groups].set(w_route_group[l]).at[
            :, n_groups:n_groups + n_experts].set(w_route_expert[l])
        wr = wr.astype(BF16)
        br =jnp.zeros((1, LANES), F32).at[0, :n_groups].set(b_route_group[l]).at[
            0, n_groups:n_groups + n_experts].set(b_route_expert[l])

        xr, gr, kb, v, qt, kt, vt = _inproj_prompt(xp, g1, wm_p, wt_p, cq, sq, ct, st)
        y_rnn, h_last, c_tail = _lru_prompt(xr, gr, cw, cb, wg, bga, bgx, lru_l)
        att = _attn_prompt(qt, kb, vt, sg, lam, out_scale)
        xr_s, gr_s, v_s, kt_s, qt_s = _inproj_sample(xs, g1, wm_s, wt_s, ct_s, st_s)
        y_rnn_s, h_s, c_s = _lru_step(xr_s, gr_s, conv_s[l], state_h[l], cw, cb, wg, bga, bgx, lru_l)
        k_s = kt_s.T
        att_s = _attn_sample(pt_flat, lam, qt_s.T, k_s, v_s, sg, ck, cv, l, out_scale)
        cnt0 = jnp.zeros((1, LANES), F32)
        x1, h2, route, cnt1 = _outproj_route(y_rnn.reshape(Np, d_rnn), att.reshape(Np, d_att),
                                             xp.reshape(Np, D), wo, g2, wr, br, cnt0, n_groups, per_group)
        x1_s, h2_s, route_s, cnt2 = _outproj_route(y_rnn_s, att_s, xs, wo, g2, wr, br, cnt1,
                                                   n_groups, per_group)
        counts = cnt2[0, n_groups:n_groups + n_experts].astype(jnp.int32)
        seg_start, be, bi, brows, bfirst, zb = _block_tables(counts, n_blocks)
        dest = lambda r: (seg_start[r[:, 0:2].astype(jnp.int32)] + r[:, 4:6].astype(jnp.int32)).reshape(-1)
        dest_p, dest_s = dest(route), dest(route_s)
        xb = _dispatch(zb, dest_p, h2, None, n_blocks * EXP_BLOCK)
        xb = _dispatch(zb, dest_s, h2_s, xb, n_blocks * EXP_BLOCK)
        yb = _experts(be, bi, brows, bfirst, xb, w_exp_in, w_exp_out, l)
        final = l == depth - 1
        gf = ln_f.reshape(1, D)
        xp_new = _combine(dest_p, x1, route, gf, yb, final)
        xs = _combine(dest_s, x1_s, route_s, gf, yb, final)
        xp = xp_new.reshape(Bp, Sp, D)

        n_qk = d_att // HEAD_DIM
        kp_l.append(jnp.transpose(kt.reshape(Bp, n_qk, HEAD_DIM, Sp), (0, 3, 1, 2)))
        vp_l.append(v.reshape(Bp, Sp, d_att // V_DIM, V_DIM))
        hp_l.append(h_last.reshape(Bp, d_rnn))
        cp_l.append(c_tail)
        ks_l.append(k_s.reshape(Bs, 1, n_qk, HEAD_DIM))
        vs_l.append(v_s.reshape(Bs, 1, d_att // V_DIM, V_DIM))
        hs_l.append(h_s)
        cs_l.append(jnp.transpose(c_s, (1, 0, 2)))
    return (xp, xs.reshape(Bs, 1, D),
            jnp.stack(kp_l), jnp.stack(vp_l), jnp.stack(hp_l), jnp.stack(cp_l),
            jnp.stack(ks_l), jnp.stack(vs_l), jnp.stack(hs_l), jnp.stack(cs_l))
```

```python
import functools
import math

import jax
import jax.numpy as jnp
from jax import lax
from jax.experimental import pallas as pl
from jax.experimental.pallas import tpu as pltpu

F32 = jnp.float32
BF16 = jnp.bfloat16

HEAD_DIM = 64
V_DIM = 2 * HEAD_DIM
HALF = HEAD_DIM // 2
CONV_W = 4
LRU_C = 8.0
ROPE_THETA = 10000.0
NORM_EPS = 1e-6
SUBLN_EPS = 1e-5
NEG_INF = -1e30
TOP_K = 2

LANES = 128
SUBLANES = 8
VMEM_LIMIT = 48 * 1024 * 1024
INPROJ_VMEM_LIMIT = 56 * 1024 * 1024

ROW_TILE = 512
LRU_TILE = 256
ATT_TILE = 512
EXP_BLOCK = 256
DISPATCH_TILE = 512
COMBINE_TILE = 256


def _cparams(sem, vmem_limit=VMEM_LIMIT):
    return pltpu.CompilerParams(dimension_semantics=sem, vmem_limit_bytes=vmem_limit)


def _inproj_kernel(x_ref, g_ref, wm_ref, wt_ref, cq_ref, sq_ref, ct_ref, st_ref, *rest, sample, n_prev=0):
    prev, outs = rest[:2 if n_prev else 0], rest[2 if n_prev else 0:]
    x = x_ref[...]
    ms = jnp.mean(x * x, axis=-1, keepdims=True)
    h = (x * lax.rsqrt(ms + NORM_EPS) * g_ref[...]).astype(BF16)
    main = jnp.dot(h, wm_ref[...], preferred_element_type=F32)
    d_rnn = outs[0].shape[1]
    ct = ct_ref[...]
    st = st_ref[...]
    tr = lax.dot_general(wt_ref[...], h, (((1,), (1,)), ((), ())), preferred_element_type=F32)

    def rope_t(blk):
        x1 = blk[:HALF]
        x2 = blk[HALF:]
        return x1 * ct - x2 * st, x2 * ct + x1 * st

    if sample:
        xr_ref, gr_ref, v_ref, kt_ref, qt_ref = outs
        xr_ref[...] = main[:, :d_rnn]
        gr_ref[...] = main[:, d_rnn:2 * d_rnn]
        v_ref[...] = main[:, 2 * d_rnn:]
        n_heads = kt_ref.shape[0] // HEAD_DIM
        for hh in range(n_heads):
            o1, o2 = rope_t(tr[hh * HEAD_DIM:(hh + 1) * HEAD_DIM])
            kt_ref[hh * HEAD_DIM:hh * HEAD_DIM + HALF, :] = o1
            kt_ref[hh * HEAD_DIM + HALF:(hh + 1) * HEAD_DIM, :] = o2
            base = (n_heads + hh) * HEAD_DIM
            p1, p2 = rope_t(tr[base:base + HEAD_DIM])
            qt_ref[hh * HEAD_DIM:hh * HEAD_DIM + HALF, :] = p1
            qt_ref[hh * HEAD_DIM + HALF:(hh + 1) * HEAD_DIM, :] = p2
    else:
        xr_ref, gr_ref, kb_ref, v_ref, qt_ref, kt_ref, vt_ref = outs
        xr_ref[...] = main[:, :d_rnn]
        gr_ref[...] = main[:, d_rnn:2 * d_rnn]
        d_att = kb_ref.shape[1]
        k = main[:, 2 * d_rnn:2 * d_rnn + d_att]
        cq = cq_ref[...]
        sq = sq_ref[...]
        for c in range(d_att // LANES):
            kc = k[:, c * LANES:(c + 1) * LANES]
            kb_ref[:, c * LANES:(c + 1) * LANES] = (kc * cq + pltpu.roll(kc, HEAD_DIM, 1) * sq).astype(BF16)
        if n_prev:
            kt_ref[:n_prev] = prev[0][...]
            v_ref[:n_prev] = prev[1][...]
        v = main[:, 2 * d_rnn + d_att:]
        T = v.shape[0]
        G = d_att // V_DIM
        for g in range(G):
            v_ref[n_prev, pl.ds(g, T, stride=G), :] = v[:, g * V_DIM:(g + 1) * V_DIM]
        n_heads = d_att // HEAD_DIM
        for hh in range(n_heads):
            g, jj = divmod(hh, 2)
            r1 = g * V_DIM + jj * HALF
            x1 = tr[r1:r1 + HALF]
            x2 = tr[r1 + HEAD_DIM:r1 + HEAD_DIM + HALF]
            qt_ref[r1:r1 + HALF, :] = (x1 * ct - x2 * st).astype(BF16)
            qt_ref[r1 + HEAD_DIM:r1 + HEAD_DIM + HALF, :] = (x2 * ct + x1 * st).astype(BF16)
            base = d_att + hh * HEAD_DIM
            o1, o2 = rope_t(tr[base:base + HEAD_DIM])
            kt_ref[n_prev, hh * HEAD_DIM:hh * HEAD_DIM + HALF, :] = o1
            kt_ref[n_prev, hh * HEAD_DIM + HALF:(hh + 1) * HEAD_DIM, :] = o2
        vt_ref[...] = tr[2 * d_att:].astype(BF16)


def _inproj_prompt(x, g, wm, wt, cq, sq, ct, st, k_prev, v_prev):
    B, S, D = x.shape
    T = min(ROW_TILE, S)
    nt = S // T
    d_att = wt.shape[0] // 3
    d_rnn = (wm.shape[1] - 2 * d_att) // 2
    G = d_att // V_DIM
    n_prev = 0 if k_prev is None else k_prev.shape[0]
    row = lambda w: pl.BlockSpec((None, T, w), lambda b, t: (b, t, 0))
    full = lambda a: pl.BlockSpec(a.shape, lambda b, t: (0,) * a.ndim)
    once = lambda a: pl.BlockSpec(a.shape, lambda b, t: (0,) * a.ndim, pipeline_mode=pl.Buffered(1))
    tr_spec = pl.BlockSpec((None, d_att, T), lambda b, t: (b, 0, t))
    kstack = lambda n: pl.BlockSpec((n, None, d_att, T), lambda b, t: (0, b, 0, t))
    vstack = lambda n: pl.BlockSpec((n, None, T * G, V_DIM), lambda b, t: (0, b, t, 0))
    prev_specs = [kstack(n_prev), vstack(n_prev)] if n_prev else []
    prev_args = [k_prev, v_prev] if n_prev else []
    return pl.pallas_call(
        functools.partial(_inproj_kernel, sample=False, n_prev=n_prev),
        grid=(B, nt),
        in_specs=[row(D), full(g), once(wm), once(wt),
                  pl.BlockSpec((T, LANES), lambda b, t: (t, 0)),
                  pl.BlockSpec((T, LANES), lambda b, t: (t, 0)),
                  pl.BlockSpec((HALF, T), lambda b, t: (0, t)),
                  pl.BlockSpec((HALF, T), lambda b, t: (0, t))] + prev_specs,
        out_specs=[row(d_rnn), row(d_rnn), row(d_att), vstack(n_prev + 1), tr_spec, kstack(n_prev + 1), tr_spec],
        out_shape=[jax.ShapeDtypeStruct((B, S, d_rnn), F32), jax.ShapeDtypeStruct((B, S, d_rnn), F32),
                   jax.ShapeDtypeStruct((B, S, d_att), BF16),
                   jax.ShapeDtypeStruct((n_prev + 1, B, S * G, V_DIM), F32),
                   jax.ShapeDtypeStruct((B, d_att, S), BF16),
                   jax.ShapeDtypeStruct((n_prev + 1, B, d_att, S), F32),
                   jax.ShapeDtypeStruct((B, d_att, S), BF16)],
        compiler_params=_cparams(("parallel", "arbitrary"), INPROJ_VMEM_LIMIT),
        name="inproj_prompt",
    )(x, g, wm, wt, cq, sq, ct, st, *prev_args)


def _inproj_sample(x, g, wm, wt, ct, st):
    n, D = x.shape
    d_att = wt.shape[0] // 2
    d_rnn = (wm.shape[1] - d_att) // 2
    full = lambda a: pl.BlockSpec(a.shape, lambda i: (0,) * a.ndim)
    dummy = jnp.zeros((SUBLANES, LANES), F32)
    out_shapes = [jax.ShapeDtypeStruct((n, d_rnn), F32), jax.ShapeDtypeStruct((n, d_rnn), F32),
                  jax.ShapeDtypeStruct((n, d_att), F32), jax.ShapeDtypeStruct((d_att, n), F32),
                  jax.ShapeDtypeStruct((d_att, n), F32)]
    return pl.pallas_call(
        functools.partial(_inproj_kernel, sample=True),
        grid=(1,),
        in_specs=[full(x), full(g), full(wm), full(wt), full(dummy), full(dummy), full(ct), full(st)],
        out_specs=[pl.BlockSpec(s.shape, lambda i: (0, 0)) for s in out_shapes],
        out_shape=out_shapes,
        compiler_params=_cparams(("arbitrary",)),
        name="inproj_sample",
    )(x, g, wm, wt, dummy, dummy, ct, st)


def _gates(xc, wg_ref, bga, bgx, lam):
    half = xc.shape[1] // 2
    xcb = xc.astype(BF16)
    g0 = jnp.dot(xcb[:, :half], wg_ref[0], preferred_element_type=F32)
    g1 = jnp.dot(xcb[:, half:], wg_ref[1], preferred_element_type=F32)
    r = jax.nn.sigmoid(jnp.concatenate([g0[:, :half], g1[:, :half]], axis=1) + bga)
    i = jax.nn.sigmoid(jnp.concatenate([g0[:, half:], g1[:, half:]], axis=1) + bgx)
    neg = -lam
    softplus = jnp.maximum(neg, 0.0) + jnp.log1p(jnp.exp(-jnp.abs(neg)))
    log_a = -LRU_C * r * softplus
    a = jnp.exp(log_a)
    mult = jnp.sqrt(1.0 - a * a)
    return a, i, mult


def _lru_kernel(xr_ref, gr_ref, cw_ref, cb_ref, wg_ref, bga_ref, bgx_ref, lam_ref,
                y_ref, hlast_ref, ctail_ref, xp_scr, a_scr, b_scr, h_scr):
    t = pl.program_id(1)
    T, C = xr_ref.shape

    @pl.when(t == 0)
    def _():
        xp_scr[0:SUBLANES, :] = jnp.zeros((SUBLANES, C), F32)
        h_scr[...] = jnp.zeros_like(h_scr)

    x = xr_ref[...]
    xp_scr[SUBLANES:SUBLANES + T, :] = x
    cw = cw_ref[...]
    xc = cb_ref[...]
    for j in range(CONV_W - 1):
        xc = xc + xp_scr[pl.ds(SUBLANES - (CONV_W - 1) + j, T), :] * cw[j:j + 1]
    xc = xc + x * cw[CONV_W - 1:CONV_W]
    tail = x[T - (CONV_W - 1):, :]
    xp_scr[SUBLANES - (CONV_W - 1):SUBLANES, :] = tail
    ctail_ref[...] = tail

    a, i, mult = _gates(xc, wg_ref, bga_ref[...], bgx_ref[...], lam_ref[...])
    row = lax.broadcasted_iota(jnp.int32, (T, 1), 0)
    mult = jnp.where(jnp.logical_and(row == 0, t == 0), 1.0, mult)
    a_scr[...] = a
    b_scr[...] = mult * i * xc

    rowi = lax.broadcasted_iota(jnp.int32, (SUBLANES, C), 0)

    def body(gi, h):
        off = pl.multiple_of(gi * SUBLANES, SUBLANES)
        A = a_scr[pl.ds(off, SUBLANES), :]
        Bv = b_scr[pl.ds(off, SUBLANES), :]
        for d in (1, 2, 4):
            keep = rowi >= d
            Bv = jnp.where(keep, Bv + A * pltpu.roll(Bv, d, 0), Bv)
            A = jnp.where(keep, A * pltpu.roll(A, d, 0), A)
        hrows = Bv + A * h
        b_scr[pl.ds(off, SUBLANES), :] = hrows
        return hrows[SUBLANES - 1:SUBLANES, :]

    h = lax.fori_loop(0, T // SUBLANES, body, h_scr[...], unroll=2)
    h_scr[...] = h
    hlast_ref[...] = h
    y_ref[...] = (b_scr[...] * jax.nn.gelu(gr_ref[...])).astype(BF16)


def _lru_prompt(xr, gr, cw, cb, wg, bga, bgx, lam):
    B, S, C = xr.shape
    T = min(LRU_TILE, S)
    nt = S // T
    row = pl.BlockSpec((None, T, C), lambda b, t: (b, t, 0))
    full = lambda a: pl.BlockSpec(a.shape, lambda b, t: (0,) * a.ndim)
    return pl.pallas_call(
        _lru_kernel,
        grid=(B, nt),
        in_specs=[row, row, full(cw), full(cb), full(wg), full(bga), full(bgx), full(lam)],
        out_specs=[row, pl.BlockSpec((None, 1, C), lambda b, t: (b, 0, 0)),
                   pl.BlockSpec((None, CONV_W - 1, C), lambda b, t: (b, 0, 0))],
        out_shape=[jax.ShapeDtypeStruct((B, S, C), BF16), jax.ShapeDtypeStruct((B, 1, C), F32),
                   jax.ShapeDtypeStruct((B, CONV_W - 1, C), F32)],
        scratch_shapes=[pltpu.VMEM((T + SUBLANES, C), F32), pltpu.VMEM((T, C), F32),
                        pltpu.VMEM((T, C), F32), pltpu.VMEM((1, C), F32)],
        compiler_params=_cparams(("parallel", "arbitrary")),
        name="rglru_prompt",
    )(xr, gr, cw, cb, wg, bga, bgx, lam)


def _lru_step_kernel(xr_ref, gr_ref, conv_ref, h0_ref, cw_ref, cb_ref, wg_ref, bga_ref, bgx_ref, lam_ref,
                     y_ref, hnew_ref, cnew_ref):
    x = xr_ref[...]
    cw = cw_ref[...]
    xc = cb_ref[...]
    for j in range(CONV_W - 1):
        xc = xc + conv_ref[j] * cw[j:j + 1]
    xc = xc + x * cw[CONV_W - 1:CONV_W]
    for j in range(CONV_W - 2):
        cnew_ref[j] = conv_ref[j + 1]
    cnew_ref[CONV_W - 2] = x
    a, i, mult = _gates(xc, wg_ref, bga_ref[...], bgx_ref[...], lam_ref[...])
    h = a * h0_ref[...] + mult * i * xc
    hnew_ref[...] = h
    y_ref[...] = (h * jax.nn.gelu(gr_ref[...])).astype(BF16)


def _lru_step(xr, gr, conv, h0, cw, cb, wg, bga, bgx, lam):
    n, C = xr.shape
    args = (xr, gr, conv, h0, cw, cb, wg, bga, bgx, lam)
    full = lambda a: pl.BlockSpec(a.shape, lambda i: (0,) * a.ndim)
    out_shapes = [jax.ShapeDtypeStruct((n, C), BF16), jax.ShapeDtypeStruct((n, C), F32),
                  jax.ShapeDtypeStruct((CONV_W - 1, n, C), F32)]
    return pl.pallas_call(
        _lru_step_kernel,
        grid=(1,),
        in_specs=[full(a) for a in args],
        out_specs=[pl.BlockSpec(s.shape, lambda i, nd=len(s.shape): (0,) * nd) for s in out_shapes],
        out_shape=out_shapes,
        compiler_params=_cparams(("arbitrary",)),
        name="rglru_step",
    )(*args)


def _subln(o, g, scale):
    ms = jnp.mean(o * o, axis=-1, keepdims=True)
    return o * lax.rsqrt(ms + SUBLN_EPS) * g * scale


def _attn_kernel(lam_ref, qt_ref, kb_ref, vt_ref, g_ref, o_ref, qcat, m, l, acc, *, out_scale):
    qi = pl.program_id(1)
    Tq = qt_ref.shape[1]
    Tk = Tq

    qt = qt_ref[...]
    row = lax.broadcasted_iota(jnp.int32, qt.shape, 0)
    second = (row & HALF) != 0
    zero = jnp.zeros_like(qt)
    qcat[:, :Tq] = jnp.where(second, zero, qt)
    qcat[:, Tq:] = jnp.where(second, qt, zero)
    m[...] = jnp.full_like(m, NEG_INF)
    l[...] = jnp.zeros_like(l)
    acc[...] = jnp.zeros_like(acc)

    def step(kj, diagonal):
        k0 = pl.multiple_of(kj * Tk, Tk)
        s = jnp.dot(kb_ref[pl.ds(k0, Tk), :], qcat[...], preferred_element_type=F32)
        if diagonal:
            key = lax.broadcasted_iota(jnp.int32, s.shape, 0)
            qry = lax.broadcasted_iota(jnp.int32, s.shape, 1)
            qry = jnp.where(qry >= Tq, qry - Tq, qry)
            s = jnp.where(key <= qry, s, NEG_INF)
        m_old = m[...]
        m_new = jnp.maximum(m_old, jnp.max(s, axis=0, keepdims=True))
        alpha = jnp.exp(m_old - m_new)
        p = jnp.exp(s - m_new)
        l[...] = alpha * l[...] + jnp.sum(p, axis=0, keepdims=True)
        acc[...] = alpha * acc[...] + jnp.dot(vt_ref[:, pl.ds(k0, Tk)], p.astype(BF16),
                                              preferred_element_type=F32)
        m[...] = m_new

    def body(kj, c):
        step(kj, False)
        return c

    lax.fori_loop(0, qi, body, 0)
    step(qi, True)
    a = acc[...] / l[...]
    o = a[:, :Tq] - lam_ref[0, 0] * a[:, Tq:]
    ms = jnp.mean(o * o, axis=0, keepdims=True)
    o = o * lax.rsqrt(ms + SUBLN_EPS) * (g_ref[...] * out_scale)
    o_ref[...] = o.T.astype(o_ref.dtype)


def _attn_prompt(qt, kb, vt, subln_g, lam, out_scale):
    B, S, d_att = kb.shape
    G = d_att // V_DIM
    T = min(ATT_TILE, S)
    return pl.pallas_call(
        functools.partial(_attn_kernel, out_scale=out_scale),
        grid=(B * G, S // T),
        in_specs=[
            pl.BlockSpec(memory_space=pltpu.SMEM),
            pl.BlockSpec((None, V_DIM, T), lambda bg, i: (bg // G, bg % G, i)),
            pl.BlockSpec((None, S, V_DIM), lambda bg, i: (bg // G, 0, bg % G)),
            pl.BlockSpec((None, V_DIM, S), lambda bg, i: (bg // G, bg % G, 0)),
            pl.BlockSpec((V_DIM, 1), lambda bg, i: (0, 0)),
        ],
        out_specs=pl.BlockSpec((None, T, V_DIM), lambda bg, i: (bg // G, i, bg % G)),
        out_shape=jax.ShapeDtypeStruct((B, S, d_att), BF16),
        scratch_shapes=[pltpu.VMEM((V_DIM, 2 * T), BF16), pltpu.VMEM((1, 2 * T), F32),
                        pltpu.VMEM((1, 2 * T), F32), pltpu.VMEM((V_DIM, 2 * T), F32)],
        compiler_params=_cparams(("parallel", "arbitrary")),
        name="diff_attn_prompt",
    )(lam, qt, kb, vt, subln_g.reshape(V_DIM, 1))


def _attn_sample_kernel(pt_ref, lam_ref, q_ref, kc_ref, vc_ref, g_ref, *rest, n_pages, out_scale):
    k_refs = rest[:n_pages]
    v_refs = rest[n_pages:2 * n_pages]
    o_ref = rest[2 * n_pages]
    lam = lam_ref[0, 0]
    q = q_ref[...]
    d_att = q.shape[1]
    n_heads = d_att // HEAD_DIM
    G = d_att // V_DIM
    R = 2 * SUBLANES
    rowi = lax.broadcasted_iota(jnp.int32, (R, d_att), 0)
    coli = lax.broadcasted_iota(jnp.int32, (R, d_att), 1)
    qblk = jnp.where(coli // HEAD_DIM == rowi, jnp.broadcast_to(q, (R, d_att)), 0.0)
    qb = qblk.astype(BF16)
    s = jnp.concatenate(
        [jnp.dot(qb, k_refs[j][...].astype(BF16), preferred_element_type=F32) for j in range(n_pages)], axis=1)
    s = s[:n_heads]
    s_cur = jnp.sum(qblk[:n_heads] * kc_ref[...], axis=-1, keepdims=True)
    m = jnp.maximum(jnp.max(s, axis=-1, keepdims=True), s_cur)
    p = jnp.exp(s - m)
    p_cur = jnp.exp(s_cur - m)
    inv_l = 1.0 / (jnp.sum(p, axis=-1, keepdims=True) + p_cur)
    p = p * inv_l
    p_cur = p_cur * inv_l
    amap = p - lam * pltpu.roll(p, n_heads - 1, 0)
    amap = jnp.concatenate([amap, jnp.zeros_like(amap)], axis=0).astype(BF16)
    p_cur = jnp.broadcast_to(p_cur, (n_heads, LANES))
    amap_cur = p_cur - lam * pltpu.roll(p_cur, n_heads - 1, 0)
    P = k_refs[0].shape[1]
    outs = []
    for g in range(G):
        acc = jnp.zeros((R, V_DIM), F32)
        for j in range(n_pages):
            vjg = v_refs[j][pl.ds(g, P, stride=G), :].astype(BF16)
            acc = acc + jnp.dot(amap[:, j * P:(j + 1) * P], vjg, preferred_element_type=F32)
        o = acc[2 * g:2 * g + 1, :] + amap_cur[2 * g:2 * g + 1, :] * vc_ref[:, g * V_DIM:(g + 1) * V_DIM]
        outs.append(_subln(o, g_ref[...], out_scale))
    o_ref[...] = jnp.concatenate(outs, axis=1).astype(o_ref.dtype)


def _attn_sample(pt_flat, lam, q, k_cur, v_cur, subln_g, ck, cv, layer, out_scale):
    n, d_att = q.shape
    n_pages = pt_flat.shape[0] // n
    P = ck.shape[-1]
    q3 = q.reshape(n, 1, d_att)
    kc3 = k_cur.reshape(n, 1, d_att)
    vc3 = v_cur.reshape(n, 1, d_att)
    tok = lambda: pl.BlockSpec((None, 1, d_att), lambda b, pt: (b, 0, 0))
    k_specs = [pl.BlockSpec((None, None, d_att, P), functools.partial(
        lambda b, pt, j: (layer, pt[b * n_pages + j], 0, 0), j=j)) for j in range(n_pages)]
    v_specs = [pl.BlockSpec((None, None, cv.shape[2], V_DIM), functools.partial(
        lambda b, pt, j: (layer, pt[b * n_pages + j], 0, 0), j=j)) for j in range(n_pages)]
    grid_spec = pltpu.PrefetchScalarGridSpec(
        num_scalar_prefetch=1,
        grid=(n,),
        in_specs=[pl.BlockSpec(memory_space=pltpu.SMEM), tok(), tok(), tok(),
                  pl.BlockSpec((1, V_DIM), lambda b, pt: (0, 0))] + k_specs + v_specs,
        out_specs=pl.BlockSpec((None, 1, d_att), lambda b, pt: (b, 0, 0)),
    )
    out = pl.pallas_call(
        functools.partial(_attn_sample_kernel, n_pages=n_pages, out_scale=out_scale),
        grid_spec=grid_spec,
        out_shape=jax.ShapeDtypeStruct((n, 1, d_att), BF16),
        compiler_params=_cparams(("arbitrary",)),
        name="diff_attn_sample",
    )(pt_flat, lam, q3, kc3, vc3, subln_g, *([ck] * n_pages), *([cv] * n_pages))
    return out.reshape(n, d_att)


def _outproj_kernel(yr_ref, at_ref, x_ref, wo_ref, g2_ref, wr_ref, br_ref, cnt_in_ref,
                    x1_ref, h2_ref, route_ref, route_t_ref, cnt_out_ref, cnt_scr, *, n_groups, per_group):
    step = pl.program_id(0)

    @pl.when(step == 0)
    def _():
        cnt_scr[...] = cnt_in_ref[...]

    d_rnn = yr_ref.shape[1]
    y = (jnp.dot(yr_ref[...], wo_ref[:d_rnn, :], preferred_element_type=F32)
         + jnp.dot(at_ref[...], wo_ref[d_rnn:, :], preferred_element_type=F32))
    x1 = x_ref[...] + y
    x1_ref[...] = x1
    ms = jnp.mean(x1 * x1, axis=-1, keepdims=True)
    h2 = x1 * lax.rsqrt(ms + NORM_EPS) * g2_ref[...]
    h2_ref[...] = h2
    logits = jnp.dot(h2.astype(BF16), wr_ref[...], preferred_element_type=F32) + br_ref[...]
    T = logits.shape[0]
    lane = lax.broadcasted_iota(jnp.int32, (T, LANES), 1).astype(F32)
    first = lambda hit: jnp.min(jnp.where(hit, lane, float(LANES)), axis=-1, keepdims=True)
    gl = jnp.where(lane < n_groups, logits, NEG_INF)
    gmax = jnp.max(gl, axis=-1, keepdims=True)
    gidx = first(gl == gmax)
    pg = 1.0 / jnp.sum(jnp.exp(gl - gmax), axis=-1, keepdims=True)
    lo = n_groups + per_group * gidx
    el = jnp.where(jnp.logical_and(lane >= lo, lane < lo + per_group), logits, NEG_INF)
    ma = jnp.max(el, axis=-1, keepdims=True)
    ia = first(el == ma)
    el2 = jnp.where(lane == ia, NEG_INF, el)
    mb = jnp.max(el2, axis=-1, keepdims=True)
    ib = first(el2 == mb)
    eb = jnp.exp(mb - ma)
    gate_a = pg / (1.0 + eb)
    gate_b = pg * eb / (1.0 + eb)
    hit_a = lane == ia
    hit_b = lane == ib
    onehot = jnp.where(jnp.logical_or(hit_a, hit_b), 1.0, 0.0)
    r_i = lax.broadcasted_iota(jnp.int32, (T, T), 0)
    c_i = lax.broadcasted_iota(jnp.int32, (T, T), 1)
    lower = jnp.where(c_i < r_i, 1.0, 0.0).astype(BF16)
    before = jnp.dot(lower, onehot.astype(BF16), preferred_element_type=F32) + cnt_scr[...]
    rank_a = jnp.sum(jnp.where(hit_a, before, 0.0), axis=-1, keepdims=True)
    rank_b = jnp.sum(jnp.where(hit_b, before, 0.0), axis=-1, keepdims=True)
    cnt_scr[...] = cnt_scr[...] + jnp.sum(onehot, axis=0, keepdims=True)
    cnt_out_ref[...] = cnt_scr[...]
    cols = (ia - n_groups, ib - n_groups, gate_a, gate_b, rank_a, rank_b)
    route = jnp.zeros((T, LANES), F32)
    for c, val in enumerate(cols):
        route = jnp.where(lane == float(c), val, route)
    route_ref[...] = route
    route_t_ref[...] = route.T[:SUBLANES]


def _outproj_route(yr, at, x, wo, g2, wr, br, cnt_in, n_groups, per_group):
    N, D = x.shape
    T = min(ROW_TILE, N)
    d_rnn = yr.shape[1]
    row = lambda w: pl.BlockSpec((T, w), lambda i: (i, 0))
    full = lambda a: pl.BlockSpec(a.shape, lambda i: (0,) * a.ndim)
    return pl.pallas_call(
        functools.partial(_outproj_kernel, n_groups=n_groups, per_group=per_group),
        grid=(N // T,),
        in_specs=[row(d_rnn), row(at.shape[1]), row(D), full(wo), full(g2), full(wr), full(br), full(cnt_in)],
        out_specs=[row(D), row(D), row(LANES), pl.BlockSpec((SUBLANES, T), lambda i: (0, i)),
                   pl.BlockSpec((1, LANES), lambda i: (0, 0))],
        out_shape=[jax.ShapeDtypeStruct((N, D), F32), jax.ShapeDtypeStruct((N, D), F32),
                   jax.ShapeDtypeStruct((N, LANES), F32), jax.ShapeDtypeStruct((SUBLANES, N), F32),
                   jax.ShapeDtypeStruct((1, LANES), F32)],
        scratch_shapes=[pltpu.VMEM((1, LANES), F32)],
        compiler_params=_cparams(("arbitrary",)),
        name="outproj_route",
    )(yr, at, x, wo, g2, wr, br, cnt_in)


def _dispatch_kernel(zb_ref, dest_ref, h_ref, *rest, zero_fill):
    xb_hbm, zbuf, sem = rest[-3], rest[-2], rest[-1]
    T = h_ref.shape[0]

    if zero_fill:
        @pl.when(pl.program_id(0) == 0)
        def _():
            zbuf[...] = jnp.zeros_like(zbuf)

            def zcopy(i):
                row0 = pl.multiple_of(zb_ref[i] * EXP_BLOCK, EXP_BLOCK)
                return pltpu.make_async_copy(zbuf, xb_hbm.at[pl.ds(row0, EXP_BLOCK)], sem)

            def zstart(i, c):
                @pl.when(zb_ref[i] >= 0)
                def _():
                    zcopy(i).start()
                return c

            def zwait(i, c):
                @pl.when(zb_ref[i] >= 0)
                def _():
                    zcopy(i).wait()
                return c

            lax.fori_loop(0, zb_ref.shape[0], zstart, 0)
            lax.fori_loop(0, zb_ref.shape[0], zwait, 0)

    def body(i, c):
        for s in range(TOP_K):
            pltpu.make_async_copy(h_ref.at[pl.ds(i, 1)],
                                  xb_hbm.at[pl.ds(dest_ref[s, i], 1)], sem).start(priority=s)
        return c

    lax.fori_loop(0, T, body, 0, unroll=8)
    for s in range(TOP_K):
        pltpu.make_async_copy(h_ref, xb_hbm.at[pl.ds(0, T)], sem).wait()


def _dispatch(zb, dest, h, xb, xb_rows):
    N, D = h.shape
    T = min(DISPATCH_TILE, N)
    dest3 = jnp.transpose(dest.reshape(TOP_K, N // T, T), (1, 0, 2))
    aliased = xb is not None
    in_specs = [pl.BlockSpec((None, TOP_K, T), lambda i, zb: (i, 0, 0), memory_space=pltpu.SMEM),
                pl.BlockSpec((T, D), lambda i, zb: (i, 0))]
    args = [zb, dest3, h]
    if aliased:
        in_specs.append(pl.BlockSpec(memory_space=pl.ANY))
        args.append(xb)
    grid_spec = pltpu.PrefetchScalarGridSpec(
        num_scalar_prefetch=1,
        grid=(N // T,),
        in_specs=in_specs,
        out_specs=pl.BlockSpec(memory_space=pl.ANY),
        scratch_shapes=[pltpu.VMEM((EXP_BLOCK, D), F32), pltpu.SemaphoreType.DMA(())],
    )
    return pl.pallas_call(
        functools.partial(_dispatch_kernel, zero_fill=not aliased),
        grid_spec=grid_spec,
        out_shape=jax.ShapeDtypeStruct((xb_rows, D), F32),
        input_output_aliases={3: 0} if aliased else {},
        compiler_params=pltpu.CompilerParams(dimension_semantics=("arbitrary",), has_side_effects=True),
        name="moe_dispatch",
    )(*args)


def _expert_kernel(be_ref, bi_ref, br_ref, bf_ref, x_ref, w1_ref, w2_ref, y_ref, w1b, w2b):
    b = pl.program_id(0)
    rows = br_ref[b]

    @pl.when(bf_ref[b] == 1)
    def _():
        w1b[...] = w1_ref[...].astype(BF16)
        w2b[...] = w2_ref[...].astype(BF16)

    @pl.when(rows > 0)
    def _():
        h = jnp.dot(x_ref[...].astype(BF16), w1b[...], preferred_element_type=F32)
        de = h.shape[1] // 2
        act = (jax.nn.silu(h[:, :de]) * h[:, de:]).astype(BF16)
        y_ref[...] = jnp.dot(act, w2b[...], preferred_element_type=F32)

    @pl.when(rows == 0)
    def _():
        y_ref[...] = jnp.zeros_like(y_ref)


def _experts(be, bi, br, bf, xb, w1, w2, layer):
    R, D = xb.shape
    nb = be.shape[0]
    d2 = w1.shape[-1]
    de = w2.shape[-2]
    grid_spec = pltpu.PrefetchScalarGridSpec(
        num_scalar_prefetch=4,
        grid=(nb,),
        in_specs=[pl.BlockSpec((EXP_BLOCK, D), lambda b, be, bi, br, bf: (bi[b], 0)),
                  pl.BlockSpec((None, None, D, d2), lambda b, be, bi, br, bf: (layer, be[b], 0, 0)),
                  pl.BlockSpec((None, None, de, D), lambda b, be, bi, br, bf: (layer, be[b], 0, 0))],
        out_specs=pl.BlockSpec((EXP_BLOCK, D), lambda b, be, bi, br, bf: (b, 0)),
        scratch_shapes=[pltpu.VMEM((D, d2), BF16), pltpu.VMEM((de, D), BF16)],
    )
    return pl.pallas_call(
        _expert_kernel,
        grid_spec=grid_spec,
        out_shape=jax.ShapeDtypeStruct((R, D), F32),
        compiler_params=_cparams(("arbitrary",)),
        name="moe_experts",
    )(be, bi, br, bf, xb, w1, w2)


def _combine_kernel(dest_ref, x1_ref, route_ref, gf_ref, yb_hbm, o_ref, ybuf, sem, *, final):
    T = x1_ref.shape[0]

    def body(i, c):
        for s in range(TOP_K):
            pltpu.make_async_copy(yb_hbm.at[pl.ds(dest_ref[s, i], 1)],
                                  ybuf.at[s, pl.ds(i, 1)], sem).start(priority=s)
        return c

    lax.fori_loop(0, T, body, 0, unroll=8)
    for s in range(TOP_K):
        pltpu.make_async_copy(yb_hbm.at[pl.ds(0, T)], ybuf.at[s], sem).wait()
    route = route_ref[...]
    out = x1_ref[...] + ybuf[0] * route[:, 2:3] + ybuf[1] * route[:, 3:4]
    if final:
        ms = jnp.mean(out * out, axis=-1, keepdims=True)
        out = out * lax.rsqrt(ms + NORM_EPS) * gf_ref[...]
    o_ref[...] = out


def _combine(dest, x1, route, gf, yb, final):
    N, D = x1.shape
    T = min(COMBINE_TILE, N)
    dest3 = jnp.transpose(dest.reshape(TOP_K, N // T, T), (1, 0, 2))
    return pl.pallas_call(
        functools.partial(_combine_kernel, final=final),
        grid=(N // T,),
        in_specs=[pl.BlockSpec((None, TOP_K, T), lambda i: (i, 0, 0), memory_space=pltpu.SMEM),
                  pl.BlockSpec((T, D), lambda i: (i, 0)),
                  pl.BlockSpec((T, LANES), lambda i: (i, 0)),
                  pl.BlockSpec((1, D), lambda i: (0, 0)),
                  pl.BlockSpec(memory_space=pl.ANY)],
        out_specs=pl.BlockSpec((T, D), lambda i: (i, 0)),
        out_shape=jax.ShapeDtypeStruct((N, D), F32),
        scratch_shapes=[pltpu.VMEM((TOP_K, T, D), F32), pltpu.SemaphoreType.DMA(())],
        compiler_params=_cparams(("arbitrary",)),
        name="moe_combine",
    )(dest3, x1, route, gf, yb)


def _rope_tables(pos):
    inv = ROPE_THETA ** (-jnp.arange(HALF, dtype=F32) / HALF)
    ang = pos.astype(F32)[:, None] * inv[None, :]
    cos, sin = jnp.cos(ang), jnp.sin(ang)
    cq = jnp.tile(cos, (1, LANES // HALF))
    sq = jnp.concatenate([-sin, -sin, sin, sin], axis=1)
    return cq, sq, cos.T, sin.T


def _q_perm(d_att):
    idx = []
    for g in range(d_att // V_DIM):
        for part in range(2):
            for jj in range(2):
                base = HEAD_DIM * (2 * g + jj) + HALF * part
                idx.extend(range(base, base + HALF))
    return jnp.asarray(idx, jnp.int32)


def _gate_weights(w_ga, w_gx):
    nb, bw, _ = w_ga.shape
    half = nb // 2
    eye = jnp.eye(half, dtype=w_ga.dtype)

    def dense(w):
        return jnp.einsum('nij,nm->nimj', w, eye).reshape(half * bw, half * bw)

    return jnp.stack([jnp.concatenate([dense(w_ga[h * half:(h + 1) * half]),
                                       dense(w_gx[h * half:(h + 1) * half])], axis=1)
                      for h in range(2)]).astype(BF16)


def _block_tables(counts, nb):
    ne = counts.shape[0]
    nblk = (counts + EXP_BLOCK - 1) // EXP_BLOCK
    cum = jnp.sum(jnp.where(jnp.arange(ne)[None, :] <= jnp.arange(ne)[:, None], nblk[None, :], 0), axis=1)
    total = cum[-1]
    b = jnp.arange(nb, dtype=jnp.int32)
    bc = jnp.minimum(b, total - 1)
    e = jnp.minimum(jnp.sum(bc[:, None] >= cum[None, :], axis=1), ne - 1).astype(jnp.int32)
    pick = lambda tab: jnp.sum(jnp.where(e[:, None] == jnp.arange(ne)[None, :], tab[None, :], 0), axis=1)
    off = bc - (pick(cum) - pick(nblk))
    valid = b < total
    rows = jnp.where(valid, jnp.clip(pick(counts) - off * EXP_BLOCK, 0, EXP_BLOCK), 0)
    first = jnp.where(jnp.logical_and(valid, off == 0), 1, 0)
    last_blk = jnp.where(nblk > 0, cum - 1, -1)
    trail = total + jnp.arange(ne, dtype=jnp.int32)
    trail = jnp.where(trail < nb, trail, -1)
    zb = jnp.concatenate([last_blk, trail]).astype(jnp.int32)
    seg_start = ((cum - nblk) * EXP_BLOCK).astype(jnp.int32)
    return seg_start, e, bc.astype(jnp.int32), rows.astype(jnp.int32), first.astype(jnp.int32), zb


def kernel(x_prompt, x_sample, cache_k, cache_v, state_h, state_conv, page_table, ln1_g, w_in, conv_w, conv_b,
           w_ga, b_ga, w_gx, b_gx, lru_lambda, lam_q1, lam_k1, lam_q2, lam_k2, subln_g, w_out, ln2_g,
           w_route_group, b_route_group, w_route_expert, b_route_expert, w_exp_in, w_exp_out, ln_f):
    Bp, Sp, D = x_prompt.shape
    Bs = x_sample.shape[0]
    depth = w_in.shape[0]
    d_rnn = state_h.shape[-1]
    d_att = cache_k.shape[-2] * cache_k.shape[-1]
    n_groups = w_route_group.shape[-1]
    n_experts = w_route_expert.shape[-1]
    per_group = n_experts // n_groups
    n_pool, page = cache_k.shape[1], cache_k.shape[2]
    n_pages = page_table.shape[1]
    past_len = n_pages * page
    Np = Bp * Sp
    assert x_sample.shape[1] == 1 and n_groups + n_experts <= LANES

    cq, sq, ct, st = _rope_tables(jnp.arange(Sp))
    _, _, ct_s, st_s = _rope_tables(jnp.full((Bs,), past_len))
    perm = _q_perm(d_att)
    qscale = HEAD_DIM ** -0.5
    ck = jnp.transpose(cache_k, (0, 1, 3, 4, 2)).reshape(depth, n_pool, d_att, page)
    cv = cache_v.reshape(depth, n_pool, page * (d_att // V_DIM), V_DIM)
    pt_flat = page_table.reshape(-1).astype(jnp.int32)
    conv_s = jnp.transpose(state_conv, (0, 2, 1, 3))

    n_blocks = (TOP_K * (Np + Bs)) // EXP_BLOCK + n_experts

    xp = x_prompt
    xs = x_sample.reshape(Bs, D)
    kt_all = v_all = None
    hp_l, cp_l, ks_l, vs_l, hs_l, cs_l = [], [], [], [], [], []
    for l in range(depth):
        lambda_init = 0.8 - 0.6 * math.exp(-0.3 * l)
        out_scale = 1.0 - lambda_init
        lam = (jnp.exp(jnp.sum(lam_q1[l] * lam_k1[l])) - jnp.exp(jnp.sum(lam_q2[l] * lam_k2[l]))
               + lambda_init).reshape(1, 1).astype(F32)
        wl = w_in[l]
        w_xg = wl[:, :2 * d_rnn]
        w_q = wl[:, 2 * d_rnn:2 * d_rnn + d_att] * qscale
        w_k = wl[:, 2 * d_rnn + d_att:2 * d_rnn + 2 * d_att]
        w_v = wl[:, 2 * d_rnn + 2 * d_att:]
        wm_p = jnp.concatenate([w_xg, w_k[:, perm], w_v], axis=1).astype(BF16)
        wt_p = jnp.concatenate([w_q[:, perm], w_k, w_v], axis=1).T.astype(BF16)
        wm_s = jnp.concatenate([w_xg, w_v], axis=1).astype(BF16)
        wt_s = jnp.concatenate([w_k, w_q], axis=1).T.astype(BF16)
        g1 = ln1_g[l].reshape(1, D)
        wg = _gate_weights(w_ga[l], w_gx[l])
        bga = b_ga[l].reshape(1, d_rnn)
        bgx = b_gx[l].reshape(1, d_rnn)
        lru_l = lru_lambda[l].reshape(1, d_rnn)
        cw = conv_w[l]
        cb = conv_b[l].reshape(1, d_rnn)
        sg = subln_g[l].reshape(1, V_DIM)
        wo = w_out[l].astype(BF16)
        g2 = ln2_g[l].reshape(1, D)
        wr = jnp.zeros((D, LANES), F32).at[:, :n_groups].set(w_route_group[l]).at[
            :, n_groups:n_groups + n_experts].set(w_route_expert[l])
        wr = wr.astype(BF16)
        br =jnp.zeros((1, LANES), F32).at[0, :n_groups].set(b_route_group[l]).at[
            0, n_groups:n_groups + n_experts].set(b_route_expert[l])

        xr, gr, kb, v_all, qt, kt_all, vt = _inproj_prompt(xp, g1, wm_p, wt_p, cq, sq, ct, st, kt_all, v_all)
        y_rnn, h_last, c_tail = _lru_prompt(xr, gr, cw, cb, wg, bga, bgx, lru_l)
        att = _attn_prompt(qt, kb, vt, sg, lam, out_scale)
        xr_s, gr_s, v_s, kt_s, qt_s = _inproj_sample(xs, g1, wm_s, wt_s, ct_s, st_s)
        y_rnn_s, h_s, c_s = _lru_step(xr_s, gr_s, conv_s[l], state_h[l], cw, cb, wg, bga, bgx, lru_l)
        k_s = kt_s.T
        att_s = _attn_sample(pt_flat, lam, qt_s.T, k_s, v_s, sg, ck, cv, l, out_scale)
        cnt0 = jnp.zeros((1, LANES), F32)
        x1, h2, route, route_t, cnt1 = _outproj_route(
            y_rnn.reshape(Np, d_rnn), att.reshape(Np, d_att), xp.reshape(Np, D), wo, g2, wr, br, cnt0,
            n_groups, per_group)
        x1_s, h2_s, route_s, route_ts, cnt2 = _outproj_route(y_rnn_s, att_s, xs, wo, g2, wr, br, cnt1,
                                                             n_groups, per_group)
        counts = cnt2[0, n_groups:n_groups + n_experts].astype(jnp.int32)
        seg_start, be, bi, brows, bfirst, zb = _block_tables(counts, n_blocks)

        def dest(rt):
            e = rt[0:TOP_K].astype(jnp.int32)
            start = jnp.zeros_like(e)
            for ex in range(n_experts):
                start = jnp.where(e == ex, seg_start[ex], start)
            return start + rt[4:4 + TOP_K].astype(jnp.int32)

        dest_p, dest_s = dest(route_t), dest(route_ts)
        xb = _dispatch(zb, dest_p, h2, None, n_blocks * EXP_BLOCK)
        xb = _dispatch(zb, dest_s, h2_s, xb, n_blocks * EXP_BLOCK)
        yb = _experts(be, bi, brows, bfirst, xb, w_exp_in, w_exp_out, l)
        final = l == depth - 1
        gf = ln_f.reshape(1, D)
        xp_new = _combine(dest_p, x1, route, gf, yb, final)
        xs = _combine(dest_s, x1_s, route_s, gf, yb, final)
        xp = xp_new.reshape(Bp, Sp, D)

        n_qk = d_att // HEAD_DIM
        hp_l.append(h_last.reshape(Bp, d_rnn))
        cp_l.append(c_tail)
        ks_l.append(k_s.reshape(Bs, 1, n_qk, HEAD_DIM))
        vs_l.append(v_s.reshape(Bs, 1, d_att // V_DIM, V_DIM))
        hs_l.append(h_s)
        cs_l.append(jnp.transpose(c_s, (1, 0, 2)))
    k_prompt = jnp.transpose(kt_all.reshape(depth, Bp, n_qk, HEAD_DIM, Sp), (0, 1, 4, 2, 3))
    v_prompt = v_all.reshape(depth, Bp, Sp, d_att // V_DIM, V_DIM)
    return (xp, xs.reshape(Bs, 1, D),
            k_prompt, v_prompt, jnp.stack(hp_l), jnp.stack(cp_l),
            jnp.stack(ks_l), jnp.stack(vs_l), jnp.stack(hs_l), jnp.stack(cs_l))
```

```python
import functools
import math

import jax
import jax.numpy as jnp
from jax import lax
from jax.experimental import pallas as pl
from jax.experimental.pallas import tpu as pltpu

F32 = jnp.float32
BF16 = jnp.bfloat16

HEAD_DIM = 64
V_DIM = 2 * HEAD_DIM
HALF = HEAD_DIM // 2
ONES_ROWS = 16
CONV_W = 4
LRU_C = 8.0
ROPE_THETA = 10000.0
NORM_EPS = 1e-6
SUBLN_EPS = 1e-5
NEG_INF = -1e30
TOP_K = 2

LANES = 128
SUBLANES = 8
VMEM_LIMIT = 48 * 1024 * 1024
INPROJ_VMEM_LIMIT = 56 * 1024 * 1024

ROW_TILE = 512
LRU_TILE = 256
ATT_TILE = 512
EXP_BLOCK = 256
DISPATCH_TILE = 512
COMBINE_TILE = 256


def _cparams(sem, vmem_limit=VMEM_LIMIT):
    return pltpu.CompilerParams(dimension_semantics=sem, vmem_limit_bytes=vmem_limit)


def _inproj_kernel(x_ref, g_ref, wm_ref, wt_ref, cq_ref, sq_ref, ct_ref, st_ref, *rest, sample, n_prev=0):
    prev, outs = rest[:2 if n_prev else 0], rest[2 if n_prev else 0:]
    x = x_ref[...]
    ms = jnp.mean(x * x, axis=-1, keepdims=True)
    h = (x * lax.rsqrt(ms + NORM_EPS) * g_ref[...]).astype(BF16)
    main = jnp.dot(h, wm_ref[...], preferred_element_type=F32)
    d_rnn = outs[0].shape[1]
    ct = ct_ref[...]
    st = st_ref[...]
    tr = lax.dot_general(wt_ref[...], h, (((1,), (1,)), ((), ())), preferred_element_type=F32)

    def rope_t(blk):
        x1 = blk[:HALF]
        x2 = blk[HALF:]
        return x1 * ct - x2 * st, x2 * ct + x1 * st

    if sample:
        xr_ref, gr_ref, v_ref, kt_ref, qt_ref = outs
        xr_ref[...] = main[:, :d_rnn]
        gr_ref[...] = main[:, d_rnn:2 * d_rnn]
        v_ref[...] = main[:, 2 * d_rnn:]
        n_heads = kt_ref.shape[0] // HEAD_DIM
        for hh in range(n_heads):
            o1, o2 = rope_t(tr[hh * HEAD_DIM:(hh + 1) * HEAD_DIM])
            kt_ref[hh * HEAD_DIM:hh * HEAD_DIM + HALF, :] = o1
            kt_ref[hh * HEAD_DIM + HALF:(hh + 1) * HEAD_DIM, :] = o2
            base = (n_heads + hh) * HEAD_DIM
            p1, p2 = rope_t(tr[base:base + HEAD_DIM])
            qt_ref[hh * HEAD_DIM:hh * HEAD_DIM + HALF, :] = p1
            qt_ref[hh * HEAD_DIM + HALF:(hh + 1) * HEAD_DIM, :] = p2
    else:
        xr_ref, gr_ref, kb_ref, v_ref, qt_ref, kt_ref, vt_ref = outs
        xr_ref[...] = main[:, :d_rnn]
        gr_ref[...] = main[:, d_rnn:2 * d_rnn]
        d_att = kb_ref.shape[1]
        k = main[:, 2 * d_rnn:2 * d_rnn + d_att]
        cq = cq_ref[...]
        sq = sq_ref[...]
        for c in range(d_att // LANES):
            kc = k[:, c * LANES:(c + 1) * LANES]
            kb_ref[:, c * LANES:(c + 1) * LANES] = (kc * cq + pltpu.roll(kc, HEAD_DIM, 1) * sq).astype(BF16)
        if n_prev:
            kt_ref[:n_prev] = prev[0][...]
            v_ref[:n_prev] = prev[1][...]
        v = main[:, 2 * d_rnn + d_att:]
        T = v.shape[0]
        G = d_att // V_DIM
        for g in range(G):
            v_ref[n_prev, pl.ds(g, T, stride=G), :] = v[:, g * V_DIM:(g + 1) * V_DIM]
        n_heads = d_att // HEAD_DIM
        for hh in range(n_heads):
            g, jj = divmod(hh, 2)
            r1 = g * V_DIM + jj * HALF
            x1 = tr[r1:r1 + HALF]
            x2 = tr[r1 + HEAD_DIM:r1 + HEAD_DIM + HALF]
            qt_ref[r1:r1 + HALF, :] = (x1 * ct - x2 * st).astype(BF16)
            qt_ref[r1 + HEAD_DIM:r1 + HEAD_DIM + HALF, :] = (x2 * ct + x1 * st).astype(BF16)
            base = d_att + hh * HEAD_DIM
            o1, o2 = rope_t(tr[base:base + HEAD_DIM])
            kt_ref[n_prev, hh * HEAD_DIM:hh * HEAD_DIM + HALF, :] = o1
            kt_ref[n_prev, hh * HEAD_DIM + HALF:(hh + 1) * HEAD_DIM, :] = o2
        for g in range(G):
            r0 = g * (V_DIM + ONES_ROWS)
            vt_ref[r0:r0 + V_DIM, :] = tr[2 * d_att + g * V_DIM:2 * d_att + (g + 1) * V_DIM].astype(BF16)
            vt_ref[r0 + V_DIM:r0 + V_DIM + ONES_ROWS, :] = jnp.ones((ONES_ROWS, T), BF16)


def _inproj_prompt(x, g, wm, wt, cq, sq, ct, st, k_prev, v_prev):
    B, S, D = x.shape
    T = min(ROW_TILE, S)
    nt = S // T
    d_att = wt.shape[0] // 3
    d_rnn = (wm.shape[1] - 2 * d_att) // 2
    G = d_att // V_DIM
    n_prev = 0 if k_prev is None else k_prev.shape[0]
    row = lambda w: pl.BlockSpec((None, T, w), lambda b, t: (b, t, 0))
    full = lambda a: pl.BlockSpec(a.shape, lambda b, t: (0,) * a.ndim)
    once = lambda a: pl.BlockSpec(a.shape, lambda b, t: (0,) * a.ndim, pipeline_mode=pl.Buffered(1))
    tr_spec = pl.BlockSpec((None, d_att, T), lambda b, t: (b, 0, t))
    kstack = lambda n: pl.BlockSpec((n, None, d_att, T), lambda b, t: (0, b, 0, t))
    vstack = lambda n: pl.BlockSpec((n, None, T * G, V_DIM), lambda b, t: (0, b, t, 0))
    prev_specs = [kstack(n_prev), vstack(n_prev)] if n_prev else []
    prev_args = [k_prev, v_prev] if n_prev else []
    return pl.pallas_call(
        functools.partial(_inproj_kernel, sample=False, n_prev=n_prev),
        grid=(B, nt),
        in_specs=[row(D), full(g), once(wm), once(wt),
                  pl.BlockSpec((T, LANES), lambda b, t: (t, 0)),
                  pl.BlockSpec((T, LANES), lambda b, t: (t, 0)),
                  pl.BlockSpec((HALF, T), lambda b, t: (0, t)),
                  pl.BlockSpec((HALF, T), lambda b, t: (0, t))] + prev_specs,
        out_specs=[row(d_rnn), row(d_rnn), row(d_att), vstack(n_prev + 1), tr_spec, kstack(n_prev + 1),
                   pl.BlockSpec((None, G * (V_DIM + ONES_ROWS), T), lambda b, t: (b, 0, t))],
        out_shape=[jax.ShapeDtypeStruct((B, S, d_rnn), F32), jax.ShapeDtypeStruct((B, S, d_rnn), F32),
                   jax.ShapeDtypeStruct((B, S, d_att), BF16),
                   jax.ShapeDtypeStruct((n_prev + 1, B, S * G, V_DIM), F32),
                   jax.ShapeDtypeStruct((B, d_att, S), BF16),
                   jax.ShapeDtypeStruct((n_prev + 1, B, d_att, S), F32),
                   jax.ShapeDtypeStruct((B, G * (V_DIM + ONES_ROWS), S), BF16)],
        compiler_params=_cparams(("parallel", "arbitrary"), INPROJ_VMEM_LIMIT),
        name="inproj_prompt",
    )(x, g, wm, wt, cq, sq, ct, st, *prev_args)


def _inproj_sample(x, g, wm, wt, ct, st):
    n, D = x.shape
    d_att = wt.shape[0] // 2
    d_rnn = (wm.shape[1] - d_att) // 2
    full = lambda a: pl.BlockSpec(a.shape, lambda i: (0,) * a.ndim)
    dummy = jnp.zeros((SUBLANES, LANES), F32)
    out_shapes = [jax.ShapeDtypeStruct((n, d_rnn), F32), jax.ShapeDtypeStruct((n, d_rnn), F32),
                  jax.ShapeDtypeStruct((n, d_att), F32), jax.ShapeDtypeStruct((d_att, n), F32),
                  jax.ShapeDtypeStruct((d_att, n), F32)]
    return pl.pallas_call(
        functools.partial(_inproj_kernel, sample=True),
        grid=(1,),
        in_specs=[full(x), full(g), full(wm), full(wt), full(dummy), full(dummy), full(ct), full(st)],
        out_specs=[pl.BlockSpec(s.shape, lambda i: (0, 0)) for s in out_shapes],
        out_shape=out_shapes,
        compiler_params=_cparams(("arbitrary",)),
        name="inproj_sample",
    )(x, g, wm, wt, dummy, dummy, ct, st)


def _gates(xc, wg_ref, bga, bgx, lam):
    half = xc.shape[1] // 2
    xcb = xc.astype(BF16)
    g0 = jnp.dot(xcb[:, :half], wg_ref[0], preferred_element_type=F32)
    g1 = jnp.dot(xcb[:, half:], wg_ref[1], preferred_element_type=F32)
    r = jax.nn.sigmoid(jnp.concatenate([g0[:, :half], g1[:, :half]], axis=1) + bga)
    i = jax.nn.sigmoid(jnp.concatenate([g0[:, half:], g1[:, half:]], axis=1) + bgx)
    neg = -lam
    softplus = jnp.maximum(neg, 0.0) + jnp.log1p(jnp.exp(-jnp.abs(neg)))
    log_a = -LRU_C * r * softplus
    a = jnp.exp(log_a)
    mult = jnp.sqrt(1.0 - a * a)
    return a, i, mult


def _lru_kernel(xr_ref, gr_ref, cw_ref, cb_ref, wg_ref, bga_ref, bgx_ref, lam_ref,
                y_ref, hlast_ref, ctail_ref, xp_scr, a_scr, b_scr, h_scr):
    t = pl.program_id(1)
    T, C = xr_ref.shape

    @pl.when(t == 0)
    def _():
        xp_scr[0:SUBLANES, :] = jnp.zeros((SUBLANES, C), F32)
        h_scr[...] = jnp.zeros_like(h_scr)

    x = xr_ref[...]
    xp_scr[SUBLANES:SUBLANES + T, :] = x
    cw = cw_ref[...]
    xc = cb_ref[...]
    for j in range(CONV_W - 1):
        xc = xc + xp_scr[pl.ds(SUBLANES - (CONV_W - 1) + j, T), :] * cw[j:j + 1]
    xc = xc + x * cw[CONV_W - 1:CONV_W]
    tail = x[T - (CONV_W - 1):, :]
    xp_scr[SUBLANES - (CONV_W - 1):SUBLANES, :] = tail
    ctail_ref[...] = tail

    a, i, mult = _gates(xc, wg_ref, bga_ref[...], bgx_ref[...], lam_ref[...])
    row = lax.broadcasted_iota(jnp.int32, (T, 1), 0)
    mult = jnp.where(jnp.logical_and(row == 0, t == 0), 1.0, mult)
    a_scr[...] = a
    b_scr[...] = mult * i * xc

    rowi = lax.broadcasted_iota(jnp.int32, (SUBLANES, C), 0)

    def body(gi, h):
        off = pl.multiple_of(gi * SUBLANES, SUBLANES)
        A = a_scr[pl.ds(off, SUBLANES), :]
        Bv = b_scr[pl.ds(off, SUBLANES), :]
        for d in (1, 2, 4):
            keep = rowi >= d
            Bv = jnp.where(keep, Bv + A * pltpu.roll(Bv, d, 0), Bv)
            A = jnp.where(keep, A * pltpu.roll(A, d, 0), A)
        hrows = Bv + A * h
        b_scr[pl.ds(off, SUBLANES), :] = hrows
        return hrows[SUBLANES - 1:SUBLANES, :]

    h = lax.fori_loop(0, T // SUBLANES, body, h_scr[...], unroll=2)
    h_scr[...] = h
    hlast_ref[...] = h
    y_ref[...] = (b_scr[...] * jax.nn.gelu(gr_ref[...])).astype(BF16)


def _lru_prompt(xr, gr, cw, cb, wg, bga, bgx, lam):
    B, S, C = xr.shape
    T = min(LRU_TILE, S)
    nt = S // T
    row = pl.BlockSpec((None, T, C), lambda b, t: (b, t, 0))
    full = lambda a: pl.BlockSpec(a.shape, lambda b, t: (0,) * a.ndim)
    return pl.pallas_call(
        _lru_kernel,
        grid=(B, nt),
        in_specs=[row, row, full(cw), full(cb), full(wg), full(bga), full(bgx), full(lam)],
        out_specs=[row, pl.BlockSpec((None, 1, C), lambda b, t: (b, 0, 0)),
                   pl.BlockSpec((None, CONV_W - 1, C), lambda b, t: (b, 0, 0))],
        out_shape=[jax.ShapeDtypeStruct((B, S, C), BF16), jax.ShapeDtypeStruct((B, 1, C), F32),
                   jax.ShapeDtypeStruct((B, CONV_W - 1, C), F32)],
        scratch_shapes=[pltpu.VMEM((T + SUBLANES, C), F32), pltpu.VMEM((T, C), F32),
                        pltpu.VMEM((T, C), F32), pltpu.VMEM((1, C), F32)],
        compiler_params=_cparams(("parallel", "arbitrary")),
        name="rglru_prompt",
    )(xr, gr, cw, cb, wg, bga, bgx, lam)


def _lru_step_kernel(xr_ref, gr_ref, conv_ref, h0_ref, cw_ref, cb_ref, wg_ref, bga_ref, bgx_ref, lam_ref,
                     y_ref, hnew_ref, cnew_ref):
    x = xr_ref[...]
    cw = cw_ref[...]
    xc = cb_ref[...]
    for j in range(CONV_W - 1):
        xc = xc + conv_ref[j] * cw[j:j + 1]
    xc = xc + x * cw[CONV_W - 1:CONV_W]
    for j in range(CONV_W - 2):
        cnew_ref[j] = conv_ref[j + 1]
    cnew_ref[CONV_W - 2] = x
    a, i, mult = _gates(xc, wg_ref, bga_ref[...], bgx_ref[...], lam_ref[...])
    h = a * h0_ref[...] + mult * i * xc
    hnew_ref[...] = h
    y_ref[...] = (h * jax.nn.gelu(gr_ref[...])).astype(BF16)


def _lru_step(xr, gr, conv, h0, cw, cb, wg, bga, bgx, lam):
    n, C = xr.shape
    args = (xr, gr, conv, h0, cw, cb, wg, bga, bgx, lam)
    full = lambda a: pl.BlockSpec(a.shape, lambda i: (0,) * a.ndim)
    out_shapes = [jax.ShapeDtypeStruct((n, C), BF16), jax.ShapeDtypeStruct((n, C), F32),
                  jax.ShapeDtypeStruct((CONV_W - 1, n, C), F32)]
    return pl.pallas_call(
        _lru_step_kernel,
        grid=(1,),
        in_specs=[full(a) for a in args],
        out_specs=[pl.BlockSpec(s.shape, lambda i, nd=len(s.shape): (0,) * nd) for s in out_shapes],
        out_shape=out_shapes,
        compiler_params=_cparams(("arbitrary",)),
        name="rglru_step",
    )(*args)


def _subln(o, g, scale):
    ms = jnp.mean(o * o, axis=-1, keepdims=True)
    return o * lax.rsqrt(ms + SUBLN_EPS) * g * scale


def _attn_kernel(lam_ref, qt_ref, kb_ref, vt_ref, g_ref, o_ref, qcat, m, acc, s_even, s_odd, *, out_scale):
    qi = pl.program_id(1)
    Tq = qt_ref.shape[1]
    Tk = Tq

    qt = qt_ref[...]
    row = lax.broadcasted_iota(jnp.int32, qt.shape, 0)
    second = (row & HALF) != 0
    zero = jnp.zeros_like(qt)
    qcat[:, :Tq] = jnp.where(second, zero, qt)
    qcat[:, Tq:] = jnp.where(second, qt, zero)
    m[...] = jnp.full_like(m, NEG_INF)
    acc[...] = jnp.zeros_like(acc)

    def scores(kj, sbuf):
        k0 = pl.multiple_of(kj * Tk, Tk)
        sbuf[...] = jnp.dot(kb_ref[pl.ds(k0, Tk), :], qcat[...], preferred_element_type=F32)

    def update(kj, sbuf, diagonal):
        k0 = pl.multiple_of(kj * Tk, Tk)
        s = sbuf[...]
        if diagonal:
            key = lax.broadcasted_iota(jnp.int32, s.shape, 0)
            qry = lax.broadcasted_iota(jnp.int32, s.shape, 1)
            qry = jnp.where(qry >= Tq, qry - Tq, qry)
            s = jnp.where(key <= qry, s, NEG_INF)
        m_old = m[...]
        m_new = jnp.maximum(m_old, jnp.max(s, axis=0, keepdims=True))
        alpha = jnp.exp(m_old - m_new)
        p = jnp.exp(s - m_new)
        acc[...] = alpha * acc[...] + jnp.dot(vt_ref[:, pl.ds(k0, Tk)], p.astype(BF16),
                                              preferred_element_type=F32)
        m[...] = m_new

    def finish():
        a = acc[:V_DIM, :] / acc[V_DIM:V_DIM + 1, :]
        o = a[:, :Tq] - lam_ref[0, 0] * a[:, Tq:]
        ms = jnp.mean(o * o, axis=0, keepdims=True)
        o = o * lax.rsqrt(ms + SUBLN_EPS) * (g_ref[...] * out_scale)
        o_ref[...] = o.T.astype(o_ref.dtype)

    scores(0, s_even)

    def pair(i, c):
        t = 2 * i
        scores(t + 1, s_odd)
        update(t, s_even, False)
        scores(t + 2, s_even)
        update(t + 1, s_odd, False)
        return c

    lax.fori_loop(0, qi // 2, pair, 0)

    @pl.when(qi % 2 == 0)
    def _():
        update(qi, s_even, True)
        finish()

    @pl.when(qi % 2 == 1)
    def _():
        scores(qi, s_odd)
        update(qi - 1, s_even, False)
        update(qi, s_odd, True)
        finish()


def _attn_prompt(qt, kb, vt, subln_g, lam, out_scale):
    B, S, d_att = kb.shape
    G = d_att // V_DIM
    T = min(ATT_TILE, S)
    return pl.pallas_call(
        functools.partial(_attn_kernel, out_scale=out_scale),
        grid=(B * G, S // T),
        in_specs=[
            pl.BlockSpec(memory_space=pltpu.SMEM),
            pl.BlockSpec((None, V_DIM, T), lambda bg, i: (bg // G, bg % G, i)),
            pl.BlockSpec((None, S, V_DIM), lambda bg, i: (bg // G, 0, bg % G)),
            pl.BlockSpec((None, V_DIM + ONES_ROWS, S), lambda bg, i: (bg // G, bg % G, 0)),
            pl.BlockSpec((V_DIM, 1), lambda bg, i: (0, 0)),
        ],
        out_specs=pl.BlockSpec((None, T, V_DIM), lambda bg, i: (bg // G, i, bg % G)),
        out_shape=jax.ShapeDtypeStruct((B, S, d_att), BF16),
        scratch_shapes=[pltpu.VMEM((V_DIM, 2 * T), BF16), pltpu.VMEM((1, 2 * T), F32),
                        pltpu.VMEM((V_DIM + ONES_ROWS, 2 * T), F32),
                        pltpu.VMEM((T, 2 * T), F32), pltpu.VMEM((T, 2 * T), F32)],
        compiler_params=_cparams(("parallel", "arbitrary")),
        name="diff_attn_prompt",
    )(lam, qt, kb, vt, subln_g.reshape(V_DIM, 1))


def _attn_sample_kernel(pt_ref, lam_ref, q_ref, kc_ref, vc_ref, g_ref, *rest, n_pages, out_scale):
    k_refs = rest[:n_pages]
    v_refs = rest[n_pages:2 * n_pages]
    o_ref = rest[2 * n_pages]
    lam = lam_ref[0, 0]
    q = q_ref[...]
    d_att = q.shape[1]
    n_heads = d_att // HEAD_DIM
    G = d_att // V_DIM
    R = 2 * SUBLANES
    rowi = lax.broadcasted_iota(jnp.int32, (R, d_att), 0)
    coli = lax.broadcasted_iota(jnp.int32, (R, d_att), 1)
    qblk = jnp.where(coli // HEAD_DIM == rowi, jnp.broadcast_to(q, (R, d_att)), 0.0)
    qb = qblk.astype(BF16)
    s = jnp.concatenate(
        [jnp.dot(qb, k_refs[j][...].astype(BF16), preferred_element_type=F32) for j in range(n_pages)], axis=1)
    s = s[:n_heads]
    s_cur = jnp.sum(qblk[:n_heads] * kc_ref[...], axis=-1, keepdims=True)
    m = jnp.maximum(jnp.max(s, axis=-1, keepdims=True), s_cur)
    p = jnp.exp(s - m)
    p_cur = jnp.exp(s_cur - m)
    inv_l = 1.0 / (jnp.sum(p, axis=-1, keepdims=True) + p_cur)
    p = p * inv_l
    p_cur = p_cur * inv_l
    amap = p - lam * pltpu.roll(p, n_heads - 1, 0)
    amap = jnp.concatenate([amap, jnp.zeros_like(amap)], axis=0).astype(BF16)
    p_cur = jnp.broadcast_to(p_cur, (n_heads, LANES))
    amap_cur = p_cur - lam * pltpu.roll(p_cur, n_heads - 1, 0)
    P = k_refs[0].shape[1]
    outs = []
    for g in range(G):
        acc = jnp.zeros((R, V_DIM), F32)
        for j in range(n_pages):
            vjg = v_refs[j][pl.ds(g, P, stride=G), :].astype(BF16)
            acc = acc + jnp.dot(amap[:, j * P:(j + 1) * P], vjg, preferred_element_type=F32)
        o = acc[2 * g:2 * g + 1, :] + amap_cur[2 * g:2 * g + 1, :] * vc_ref[:, g * V_DIM:(g + 1) * V_DIM]
        outs.append(_subln(o, g_ref[...], out_scale))
    o_ref[...] = jnp.concatenate(outs, axis=1).astype(o_ref.dtype)


def _attn_sample(pt_flat, lam, q, k_cur, v_cur, subln_g, ck, cv, layer, out_scale):
    n, d_att = q.shape
    n_pages = pt_flat.shape[0] // n
    P = ck.shape[-1]
    q3 = q.reshape(n, 1, d_att)
    kc3 = k_cur.reshape(n, 1, d_att)
    vc3 = v_cur.reshape(n, 1, d_att)
    tok = lambda: pl.BlockSpec((None, 1, d_att), lambda b, pt: (b, 0, 0))
    k_specs = [pl.BlockSpec((None, None, d_att, P), functools.partial(
        lambda b, pt, j: (layer, pt[b * n_pages + j], 0, 0), j=j)) for j in range(n_pages)]
    v_specs = [pl.BlockSpec((None, None, cv.shape[2], V_DIM), functools.partial(
        lambda b, pt, j: (layer, pt[b * n_pages + j], 0, 0), j=j)) for j in range(n_pages)]
    grid_spec = pltpu.PrefetchScalarGridSpec(
        num_scalar_prefetch=1,
        grid=(n,),
        in_specs=[pl.BlockSpec(memory_space=pltpu.SMEM), tok(), tok(), tok(),
                  pl.BlockSpec((1, V_DIM), lambda b, pt: (0, 0))] + k_specs + v_specs,
        out_specs=pl.BlockSpec((None, 1, d_att), lambda b, pt: (b, 0, 0)),
    )
    out = pl.pallas_call(
        functools.partial(_attn_sample_kernel, n_pages=n_pages, out_scale=out_scale),
        grid_spec=grid_spec,
        out_shape=jax.ShapeDtypeStruct((n, 1, d_att), BF16),
        compiler_params=_cparams(("arbitrary",)),
        name="diff_attn_sample",
    )(pt_flat, lam, q3, kc3, vc3, subln_g, *([ck] * n_pages), *([cv] * n_pages))
    return out.reshape(n, d_att)


def _outproj_kernel(yr_ref, at_ref, x_ref, wo_ref, g2_ref, wr_ref, br_ref, cnt_in_ref,
                    x1_ref, h2_ref, route_ref, route_t_ref, cnt_out_ref, cnt_scr, *, n_groups, per_group):
    step = pl.program_id(0)

    @pl.when(step == 0)
    def _():
        cnt_scr[...] = cnt_in_ref[...]

    d_rnn = yr_ref.shape[1]
    y = (jnp.dot(yr_ref[...], wo_ref[:d_rnn, :], preferred_element_type=F32)
         + jnp.dot(at_ref[...], wo_ref[d_rnn:, :], preferred_element_type=F32))
    x1 = x_ref[...] + y
    x1_ref[...] = x1
    ms = jnp.mean(x1 * x1, axis=-1, keepdims=True)
    h2 = x1 * lax.rsqrt(ms + NORM_EPS) * g2_ref[...]
    h2_ref[...] = h2
    logits = jnp.dot(h2.astype(BF16), wr_ref[...], preferred_element_type=F32) + br_ref[...]
    T = logits.shape[0]
    lane = lax.broadcasted_iota(jnp.int32, (T, LANES), 1).astype(F32)
    first = lambda hit: jnp.min(jnp.where(hit, lane, float(LANES)), axis=-1, keepdims=True)
    gl = jnp.where(lane < n_groups, logits, NEG_INF)
    gmax = jnp.max(gl, axis=-1, keepdims=True)
    gidx = first(gl == gmax)
    pg = 1.0 / jnp.sum(jnp.exp(gl - gmax), axis=-1, keepdims=True)
    lo = n_groups + per_group * gidx
    el = jnp.where(jnp.logical_and(lane >= lo, lane < lo + per_group), logits, NEG_INF)
    ma = jnp.max(el, axis=-1, keepdims=True)
    ia = first(el == ma)
    el2 = jnp.where(lane == ia, NEG_INF, el)
    mb = jnp.max(el2, axis=-1, keepdims=True)
    ib = first(el2 == mb)
    eb = jnp.exp(mb - ma)
    gate_a = pg / (1.0 + eb)
    gate_b = pg * eb / (1.0 + eb)
    hit_a = lane == ia
    hit_b = lane == ib
    onehot = jnp.where(jnp.logical_or(hit_a, hit_b), 1.0, 0.0)
    r_i = lax.broadcasted_iota(jnp.int32, (T, T), 0)
    c_i = lax.broadcasted_iota(jnp.int32, (T, T), 1)
    lower = jnp.where(c_i < r_i, 1.0, 0.0).astype(BF16)
    before = jnp.dot(lower, onehot.astype(BF16), preferred_element_type=F32) + cnt_scr[...]
    rank_a = jnp.sum(jnp.where(hit_a, before, 0.0), axis=-1, keepdims=True)
    rank_b = jnp.sum(jnp.where(hit_b, before, 0.0), axis=-1, keepdims=True)
    cnt_scr[...] = cnt_scr[...] + jnp.sum(onehot, axis=0, keepdims=True)
    cnt_out_ref[...] = cnt_scr[...]
    cols = (ia - n_groups, ib - n_groups, gate_a, gate_b, rank_a, rank_b)
    route = jnp.zeros((T, LANES), F32)
    for c, val in enumerate(cols):
        route = jnp.where(lane == float(c), val, route)
    route_ref[...] = route
    route_t_ref[...] = route.T[:SUBLANES]


def _outproj_route(yr, at, x, wo, g2, wr, br, cnt_in, n_groups, per_group):
    N, D = x.shape
    T = min(ROW_TILE, N)
    d_rnn = yr.shape[1]
    row = lambda w: pl.BlockSpec((T, w), lambda i: (i, 0))
    full = lambda a: pl.BlockSpec(a.shape, lambda i: (0,) * a.ndim)
    return pl.pallas_call(
        functools.partial(_outproj_kernel, n_groups=n_groups, per_group=per_group),
        grid=(N // T,),
        in_specs=[row(d_rnn), row(at.shape[1]), row(D), full(wo), full(g2), full(wr), full(br), full(cnt_in)],
        out_specs=[row(D), row(D), row(LANES), pl.BlockSpec((SUBLANES, T), lambda i: (0, i)),
                   pl.BlockSpec((1, LANES), lambda i: (0, 0))],
        out_shape=[jax.ShapeDtypeStruct((N, D), F32), jax.ShapeDtypeStruct((N, D), F32),
                   jax.ShapeDtypeStruct((N, LANES), F32), jax.ShapeDtypeStruct((SUBLANES, N), F32),
                   jax.ShapeDtypeStruct((1, LANES), F32)],
        scratch_shapes=[pltpu.VMEM((1, LANES), F32)],
        compiler_params=_cparams(("arbitrary",)),
        name="outproj_route",
    )(yr, at, x, wo, g2, wr, br, cnt_in)


def _dispatch_kernel(zb_ref, dest_ref, h_ref, *rest, zero_fill):
    xb_hbm, zbuf, sem = rest[-3], rest[-2], rest[-1]
    T = h_ref.shape[0]

    if zero_fill:
        @pl.when(pl.program_id(0) == 0)
        def _():
            zbuf[...] = jnp.zeros_like(zbuf)

            def zcopy(i):
                row0 = pl.multiple_of(zb_ref[i] * EXP_BLOCK, EXP_BLOCK)
                return pltpu.make_async_copy(zbuf, xb_hbm.at[pl.ds(row0, EXP_BLOCK)], sem)

            def zstart(i, c):
                @pl.when(zb_ref[i] >= 0)
                def _():
                    zcopy(i).start()
                return c

            def zwait(i, c):
                @pl.when(zb_ref[i] >= 0)
                def _():
                    zcopy(i).wait()
                return c

            lax.fori_loop(0, zb_ref.shape[0], zstart, 0)
            lax.fori_loop(0, zb_ref.shape[0], zwait, 0)

    def body(i, c):
        for s in range(TOP_K):
            pltpu.make_async_copy(h_ref.at[pl.ds(i, 1)],
                                  xb_hbm.at[pl.ds(dest_ref[s, i], 1)], sem).start(priority=s)
        return c

    lax.fori_loop(0, T, body, 0, unroll=8)
    for s in range(TOP_K):
        pltpu.make_async_copy(h_ref, xb_hbm.at[pl.ds(0, T)], sem).wait()


def _dispatch(zb, dest, h, xb, xb_rows):
    N, D = h.shape
    T = min(DISPATCH_TILE, N)
    dest3 = jnp.transpose(dest.reshape(TOP_K, N // T, T), (1, 0, 2))
    aliased = xb is not None
    in_specs = [pl.BlockSpec((None, TOP_K, T), lambda i, zb: (i, 0, 0), memory_space=pltpu.SMEM),
                pl.BlockSpec((T, D), lambda i, zb: (i, 0))]
    args = [zb, dest3, h]
    if aliased:
        in_specs.append(pl.BlockSpec(memory_space=pl.ANY))
        args.append(xb)
    grid_spec = pltpu.PrefetchScalarGridSpec(
        num_scalar_prefetch=1,
        grid=(N // T,),
        in_specs=in_specs,
        out_specs=pl.BlockSpec(memory_space=pl.ANY),
        scratch_shapes=[pltpu.VMEM((EXP_BLOCK, D), F32), pltpu.SemaphoreType.DMA(())],
    )
    return pl.pallas_call(
        functools.partial(_dispatch_kernel, zero_fill=not aliased),
        grid_spec=grid_spec,
        out_shape=jax.ShapeDtypeStruct((xb_rows, D), F32),
        input_output_aliases={3: 0} if aliased else {},
        compiler_params=pltpu.CompilerParams(dimension_semantics=("arbitrary",), has_side_effects=True),
        name="moe_dispatch",
    )(*args)


def _expert_kernel(be_ref, bi_ref, br_ref, bf_ref, x_ref, w1_ref, w2_ref, y_ref, w1b, w2b):
    b = pl.program_id(0)
    rows = br_ref[b]

    @pl.when(bf_ref[b] == 1)
    def _():
        w1b[...] = w1_ref[...].astype(BF16)
        w2b[...] = w2_ref[...].astype(BF16)

    @pl.when(rows > 0)
    def _():
        h = jnp.dot(x_ref[...].astype(BF16), w1b[...], preferred_element_type=F32)
        de = h.shape[1] // 2
        act = (jax.nn.silu(h[:, :de]) * h[:, de:]).astype(BF16)
        y_ref[...] = jnp.dot(act, w2b[...], preferred_element_type=F32)

    @pl.when(rows == 0)
    def _():
        y_ref[...] = jnp.zeros_like(y_ref)


def _experts(be, bi, br, bf, xb, w1, w2, layer):
    R, D = xb.shape
    nb = be.shape[0]
    d2 = w1.shape[-1]
    de = w2.shape[-2]
    grid_spec = pltpu.PrefetchScalarGridSpec(
        num_scalar_prefetch=4,
        grid=(nb,),
        in_specs=[pl.BlockSpec((EXP_BLOCK, D), lambda b, be, bi, br, bf: (bi[b], 0)),
                  pl.BlockSpec((None, None, D, d2), lambda b, be, bi, br, bf: (layer, be[b], 0, 0)),
                  pl.BlockSpec((None, None, de, D), lambda b, be, bi, br, bf: (layer, be[b], 0, 0))],
        out_specs=pl.BlockSpec((EXP_BLOCK, D), lambda b, be, bi, br, bf: (b, 0)),
        scratch_shapes=[pltpu.VMEM((D, d2), BF16), pltpu.VMEM((de, D), BF16)],
    )
    return pl.pallas_call(
        _expert_kernel,
        grid_spec=grid_spec,
        out_shape=jax.ShapeDtypeStruct((R, D), F32),
        compiler_params=_cparams(("arbitrary",)),
        name="moe_experts",
    )(be, bi, br, bf, xb, w1, w2)


def _combine_kernel(dest_ref, x1_ref, route_ref, gf_ref, yb_hbm, o_ref, ybuf, sem, *, final):
    T = x1_ref.shape[0]

    def body(i, c):
        for s in range(TOP_K):
            pltpu.make_async_copy(yb_hbm.at[pl.ds(dest_ref[s, i], 1)],
                                  ybuf.at[s, pl.ds(i, 1)], sem).start(priority=s)
        return c

    lax.fori_loop(0, T, body, 0, unroll=8)
    for s in range(TOP_K):
        pltpu.make_async_copy(yb_hbm.at[pl.ds(0, T)], ybuf.at[s], sem).wait()
    route = route_ref[...]
    out = x1_ref[...] + ybuf[0] * route[:, 2:3] + ybuf[1] * route[:, 3:4]
    if final:
        ms = jnp.mean(out * out, axis=-1, keepdims=True)
        out = out * lax.rsqrt(ms + NORM_EPS) * gf_ref[...]
    o_ref[...] = out


def _combine(dest, x1, route, gf, yb, final):
    N, D = x1.shape
    T = min(COMBINE_TILE, N)
    dest3 = jnp.transpose(dest.reshape(TOP_K, N // T, T), (1, 0, 2))
    return pl.pallas_call(
        functools.partial(_combine_kernel, final=final),
        grid=(N // T,),
        in_specs=[pl.BlockSpec((None, TOP_K, T), lambda i: (i, 0, 0), memory_space=pltpu.SMEM),
                  pl.BlockSpec((T, D), lambda i: (i, 0)),
                  pl.BlockSpec((T, LANES), lambda i: (i, 0)),
                  pl.BlockSpec((1, D), lambda i: (0, 0)),
                  pl.BlockSpec(memory_space=pl.ANY)],
        out_specs=pl.BlockSpec((T, D), lambda i: (i, 0)),
        out_shape=jax.ShapeDtypeStruct((N, D), F32),
        scratch_shapes=[pltpu.VMEM((TOP_K, T, D), F32), pltpu.SemaphoreType.DMA(())],
        compiler_params=_cparams(("arbitrary",)),
        name="moe_combine",
    )(dest3, x1, route, gf, yb)


def _rope_tables(pos):
    inv = ROPE_THETA ** (-jnp.arange(HALF, dtype=F32) / HALF)
    ang = pos.astype(F32)[:, None] * inv[None, :]
    cos, sin = jnp.cos(ang), jnp.sin(ang)
    cq = jnp.tile(cos, (1, LANES // HALF))
    sq = jnp.concatenate([-sin, -sin, sin, sin], axis=1)
    return cq, sq, cos.T, sin.T


def _q_perm(d_att):
    idx = []
    for g in range(d_att // V_DIM):
        for part in range(2):
            for jj in range(2):
                base = HEAD_DIM * (2 * g + jj) + HALF * part
                idx.extend(range(base, base + HALF))
    return jnp.asarray(idx, jnp.int32)


def _gate_weights(w_ga, w_gx):
    nb, bw, _ = w_ga.shape
    half = nb // 2
    eye = jnp.eye(half, dtype=w_ga.dtype)

    def dense(w):
        return jnp.einsum('nij,nm->nimj', w, eye).reshape(half * bw, half * bw)

    return jnp.stack([jnp.concatenate([dense(w_ga[h * half:(h + 1) * half]),
                                       dense(w_gx[h * half:(h + 1) * half])], axis=1)
                      for h in range(2)]).astype(BF16)


def _block_tables(counts, nb):
    ne = counts.shape[0]
    nblk = (counts + EXP_BLOCK - 1) // EXP_BLOCK
    cum = jnp.sum(jnp.where(jnp.arange(ne)[None, :] <= jnp.arange(ne)[:, None], nblk[None, :], 0), axis=1)
    total = cum[-1]
    b = jnp.arange(nb, dtype=jnp.int32)
    bc = jnp.minimum(b, total - 1)
    e = jnp.minimum(jnp.sum(bc[:, None] >= cum[None, :], axis=1), ne - 1).astype(jnp.int32)
    pick = lambda tab: jnp.sum(jnp.where(e[:, None] == jnp.arange(ne)[None, :], tab[None, :], 0), axis=1)
    off = bc - (pick(cum) - pick(nblk))
    valid = b < total
    rows = jnp.where(valid, jnp.clip(pick(counts) - off * EXP_BLOCK, 0, EXP_BLOCK), 0)
    first = jnp.where(jnp.logical_and(valid, off == 0), 1, 0)
    last_blk = jnp.where(nblk > 0, cum - 1, -1)
    trail = total + jnp.arange(ne, dtype=jnp.int32)
    trail = jnp.where(trail < nb, trail, -1)
    zb = jnp.concatenate([last_blk, trail]).astype(jnp.int32)
    seg_start = ((cum - nblk) * EXP_BLOCK).astype(jnp.int32)
    return seg_start, e, bc.astype(jnp.int32), rows.astype(jnp.int32), first.astype(jnp.int32), zb


def kernel(x_prompt, x_sample, cache_k, cache_v, state_h, state_conv, page_table, ln1_g, w_in, conv_w, conv_b,
           w_ga, b_ga, w_gx, b_gx, lru_lambda, lam_q1, lam_k1, lam_q2, lam_k2, subln_g, w_out, ln2_g,
           w_route_group, b_route_group, w_route_expert, b_route_expert, w_exp_in, w_exp_out, ln_f):
    Bp, Sp, D = x_prompt.shape
    Bs = x_sample.shape[0]
    depth = w_in.shape[0]
    d_rnn = state_h.shape[-1]
    d_att = cache_k.shape[-2] * cache_k.shape[-1]
    n_groups = w_route_group.shape[-1]
    n_experts = w_route_expert.shape[-1]
    per_group = n_experts // n_groups
    n_pool, page = cache_k.shape[1], cache_k.shape[2]
    n_pages = page_table.shape[1]
    past_len = n_pages * page
    Np = Bp * Sp
    assert x_sample.shape[1] == 1 and n_groups + n_experts <= LANES

    cq, sq, ct, st = _rope_tables(jnp.arange(Sp))
    _, _, ct_s, st_s = _rope_tables(jnp.full((Bs,), past_len))
    perm = _q_perm(d_att)
    qscale = HEAD_DIM ** -0.5
    ck = jnp.transpose(cache_k, (0, 1, 3, 4, 2)).reshape(depth, n_pool, d_att, page)
    cv = cache_v.reshape(depth, n_pool, page * (d_att // V_DIM), V_DIM)
    pt_flat = page_table.reshape(-1).astype(jnp.int32)
    conv_s = jnp.transpose(state_conv, (0, 2, 1, 3))

    n_blocks = (TOP_K * (Np + Bs)) // EXP_BLOCK + n_experts

    xp = x_prompt
    xs = x_sample.reshape(Bs, D)
    kt_all = v_all = None
    hp_l, cp_l, ks_l, vs_l, hs_l, cs_l = [], [], [], [], [], []
    for l in range(depth):
        lambda_init = 0.8 - 0.6 * math.exp(-0.3 * l)
        out_scale = 1.0 - lambda_init
        lam = (jnp.exp(jnp.sum(lam_q1[l] * lam_k1[l])) - jnp.exp(jnp.sum(lam_q2[l] * lam_k2[l]))
               + lambda_init).reshape(1, 1).astype(F32)
        wl = w_in[l]
        w_xg = wl[:, :2 * d_rnn]
        w_q = wl[:, 2 * d_rnn:2 * d_rnn + d_att] * qscale
        w_k = wl[:, 2 * d_rnn + d_att:2 * d_rnn + 2 * d_att]
        w_v = wl[:, 2 * d_rnn + 2 * d_att:]
        wm_p = jnp.concatenate([w_xg, w_k[:, perm], w_v], axis=1).astype(BF16)
        wt_p = jnp.concatenate([w_q[:, perm], w_k, w_v], axis=1).T.astype(BF16)
        wm_s = jnp.concatenate([w_xg, w_v], axis=1).astype(BF16)
        wt_s = jnp.concatenate([w_k, w_q], axis=1).T.astype(BF16)
        g1 = ln1_g[l].reshape(1, D)
        wg = _gate_weights(w_ga[l], w_gx[l])
        bga = b_ga[l].reshape(1, d_rnn)
        bgx = b_gx[l].reshape(1, d_rnn)
        lru_l = lru_lambda[l].reshape(1, d_rnn)
        cw = conv_w[l]
        cb = conv_b[l].reshape(1, d_rnn)
        sg = subln_g[l].reshape(1, V_DIM)
        wo = w_out[l].astype(BF16)
        g2 = ln2_g[l].reshape(1, D)
        wr = jnp.zeros((D, LANES), F32).at[:, :n_groups].set(w_route_group[l]).at[
            :, n_groups:n_groups + n_experts].set(w_route_expert[l])
        wr = wr.astype(BF16)
        br =jnp.zeros((1, LANES), F32).at[0, :n_groups].set(b_route_group[l]).at[
            0, n_groups:n_groups + n_experts].set(b_route_expert[l])

        xr, gr, kb, v_all, qt, kt_all, vt = _inproj_prompt(xp, g1, wm_p, wt_p, cq, sq, ct, st, kt_all, v_all)
        y_rnn, h_last, c_tail = _lru_prompt(xr, gr, cw, cb, wg, bga, bgx, lru_l)
        att = _attn_prompt(qt, kb, vt, sg, lam, out_scale)
        xr_s, gr_s, v_s, kt_s, qt_s = _inproj_sample(xs, g1, wm_s, wt_s, ct_s, st_s)
        y_rnn_s, h_s, c_s = _lru_step(xr_s, gr_s, conv_s[l], state_h[l], cw, cb, wg, bga, bgx, lru_l)
        k_s = kt_s.T
        att_s = _attn_sample(pt_flat, lam, qt_s.T, k_s, v_s, sg, ck, cv, l, out_scale)
        cnt0 = jnp.zeros((1, LANES), F32)
        x1, h2, route, route_t, cnt1 = _outproj_route(
            y_rnn.reshape(Np, d_rnn), att.reshape(Np, d_att), xp.reshape(Np, D), wo, g2, wr, br, cnt0,
            n_groups, per_group)
        x1_s, h2_s, route_s, route_ts, cnt2 = _outproj_route(y_rnn_s, att_s, xs, wo, g2, wr, br, cnt1,
                                                             n_groups, per_group)
        counts = cnt2[0, n_groups:n_groups + n_experts].astype(jnp.int32)
        seg_start, be, bi, brows, bfirst, zb = _block_tables(counts, n_blocks)

        def dest(rt):
            e = rt[0:TOP_K].astype(jnp.int32)
            start = jnp.zeros_like(e)
            for ex in range(n_experts):
                start = jnp.where(e == ex, seg_start[ex], start)
            return start + rt[4:4 + TOP_K].astype(jnp.int32)

        dest_p, dest_s = dest(route_t), dest(route_ts)
        xb = _dispatch(zb, dest_p, h2, None, n_blocks * EXP_BLOCK)
        xb = _dispatch(zb, dest_s, h2_s, xb, n_blocks * EXP_BLOCK)
        yb = _experts(be, bi, brows, bfirst, xb, w_exp_in, w_exp_out, l)
        final = l == depth - 1
        gf = ln_f.reshape(1, D)
        xp_new = _combine(dest_p, x1, route, gf, yb, final)
        xs = _combine(dest_s, x1_s, route_s, gf, yb, final)
        xp = xp_new.reshape(Bp, Sp, D)

        n_qk = d_att // HEAD_DIM
        hp_l.append(h_last.reshape(Bp, d_rnn))
        cp_l.append(c_tail)
        ks_l.append(k_s.reshape(Bs, 1, n_qk, HEAD_DIM))
        vs_l.append(v_s.reshape(Bs, 1, d_att // V_DIM, V_DIM))
        hs_l.append(h_s)
        cs_l.append(jnp.transpose(c_s, (1, 0, 2)))
    k_prompt = jnp.transpose(kt_all.reshape(depth, Bp, n_qk, HEAD_DIM, Sp), (0, 1, 4, 2, 3))
    v_prompt = v_all.reshape(depth, Bp, Sp, d_att // V_DIM, V_DIM)
    return (xp, xs.reshape(Bs, 1, D),
            k_prompt, v_prompt, jnp.stack(hp_l), jnp.stack(cp_l),
            jnp.stack(ks_l), jnp.stack(vs_l), jnp.stack(hs_l), jnp.stack(cs_l))
```

```python
import functools
import math

import jax
import jax.numpy as jnp
from jax import lax
from jax.experimental import pallas as pl
from jax.experimental.pallas import tpu as pltpu

F32 = jnp.float32
BF16 = jnp.bfloat16

HEAD_DIM = 64
V_DIM = 2 * HEAD_DIM
HALF = HEAD_DIM // 2
ONES_ROWS = 16
CONV_W = 4
LRU_C = 8.0
ROPE_THETA = 10000.0
NORM_EPS = 1e-6
SUBLN_EPS = 1e-5
NEG_INF = -1e30
TOP_K = 2

LANES = 128
SUBLANES = 8
VMEM_LIMIT = 48 * 1024 * 1024
INPROJ_VMEM_LIMIT = 56 * 1024 * 1024

ROW_TILE = 512
LRU_TILE = 256
ATT_TILE = 512
EXP_BLOCK = 512
DISPATCH_TILE = 512
COMBINE_TILE = 256


def _cparams(sem, vmem_limit=VMEM_LIMIT):
    return pltpu.CompilerParams(dimension_semantics=sem, vmem_limit_bytes=vmem_limit)


def _inproj_kernel(x_ref, g_ref, wm_ref, wt_ref, cq_ref, sq_ref, ct_ref, st_ref, *rest, sample, n_prev=0):
    prev, outs = rest[:2 if n_prev else 0], rest[2 if n_prev else 0:]
    x = x_ref[...]
    ms = jnp.mean(x * x, axis=-1, keepdims=True)
    h = (x * lax.rsqrt(ms + NORM_EPS) * g_ref[...]).astype(BF16)
    main = jnp.dot(h, wm_ref[...], preferred_element_type=F32)
    d_rnn = outs[0].shape[1]
    ct = ct_ref[...]
    st = st_ref[...]
    tr = lax.dot_general(wt_ref[...], h, (((1,), (1,)), ((), ())), preferred_element_type=F32)

    def rope_t(blk):
        x1 = blk[:HALF]
        x2 = blk[HALF:]
        return x1 * ct - x2 * st, x2 * ct + x1 * st

    if sample:
        xr_ref, gr_ref, v_ref, kt_ref, qt_ref = outs
        xr_ref[...] = main[:, :d_rnn]
        gr_ref[...] = main[:, d_rnn:2 * d_rnn]
        v_ref[...] = main[:, 2 * d_rnn:]
        n_heads = kt_ref.shape[0] // HEAD_DIM
        for hh in range(n_heads):
            o1, o2 = rope_t(tr[hh * HEAD_DIM:(hh + 1) * HEAD_DIM])
            kt_ref[hh * HEAD_DIM:hh * HEAD_DIM + HALF, :] = o1
            kt_ref[hh * HEAD_DIM + HALF:(hh + 1) * HEAD_DIM, :] = o2
            base = (n_heads + hh) * HEAD_DIM
            p1, p2 = rope_t(tr[base:base + HEAD_DIM])
            qt_ref[hh * HEAD_DIM:hh * HEAD_DIM + HALF, :] = p1
            qt_ref[hh * HEAD_DIM + HALF:(hh + 1) * HEAD_DIM, :] = p2
    else:
        xr_ref, gr_ref, kb_ref, v_ref, qt_ref, kt_ref, vt_ref = outs
        xr_ref[...] = main[:, :d_rnn]
        gr_ref[...] = main[:, d_rnn:2 * d_rnn]
        d_att = kb_ref.shape[1]
        k = main[:, 2 * d_rnn:2 * d_rnn + d_att]
        cq = cq_ref[...]
        sq = sq_ref[...]
        for c in range(d_att // LANES):
            kc = k[:, c * LANES:(c + 1) * LANES]
            kb_ref[:, c * LANES:(c + 1) * LANES] = (kc * cq + pltpu.roll(kc, HEAD_DIM, 1) * sq).astype(BF16)
        if n_prev:
            kt_ref[:n_prev] = prev[0][...]
            v_ref[:n_prev] = prev[1][...]
        v = main[:, 2 * d_rnn + d_att:]
        T = v.shape[0]
        G = d_att // V_DIM
        for g in range(G):
            v_ref[n_prev, pl.ds(g, T, stride=G), :] = v[:, g * V_DIM:(g + 1) * V_DIM]
        n_heads = d_att // HEAD_DIM
        for hh in range(n_heads):
            g, jj = divmod(hh, 2)
            r1 = g * V_DIM + jj * HALF
            x1 = tr[r1:r1 + HALF]
            x2 = tr[r1 + HEAD_DIM:r1 + HEAD_DIM + HALF]
            qt_ref[r1:r1 + HALF, :] = (x1 * ct - x2 * st).astype(BF16)
            qt_ref[r1 + HEAD_DIM:r1 + HEAD_DIM + HALF, :] = (x2 * ct + x1 * st).astype(BF16)
            base = d_att + hh * HEAD_DIM
            o1, o2 = rope_t(tr[base:base + HEAD_DIM])
            kt_ref[n_prev, hh * HEAD_DIM:hh * HEAD_DIM + HALF, :] = o1
            kt_ref[n_prev, hh * HEAD_DIM + HALF:(hh + 1) * HEAD_DIM, :] = o2
        for g in range(G):
            r0 = g * (V_DIM + ONES_ROWS)
            vt_ref[r0:r0 + V_DIM, :] = tr[2 * d_att + g * V_DIM:2 * d_att + (g + 1) * V_DIM].astype(BF16)
            vt_ref[r0 + V_DIM:r0 + V_DIM + ONES_ROWS, :] = jnp.ones((ONES_ROWS, T), BF16)


def _inproj_prompt(x, g, wm, wt, cq, sq, ct, st, k_prev, v_prev):
    B, S, D = x.shape
    T = min(ROW_TILE, S)
    nt = S // T
    d_att = wt.shape[0] // 3
    d_rnn = (wm.shape[1] - 2 * d_att) // 2
    G = d_att // V_DIM
    n_prev = 0 if k_prev is None else k_prev.shape[0]
    row = lambda w: pl.BlockSpec((None, T, w), lambda b, t: (b, t, 0))
    full = lambda a: pl.BlockSpec(a.shape, lambda b, t: (0,) * a.ndim)
    once = lambda a: pl.BlockSpec(a.shape, lambda b, t: (0,) * a.ndim, pipeline_mode=pl.Buffered(1))
    tr_spec = pl.BlockSpec((None, d_att, T), lambda b, t: (b, 0, t))
    kstack = lambda n: pl.BlockSpec((n, None, d_att, T), lambda b, t: (0, b, 0, t))
    vstack = lambda n: pl.BlockSpec((n, None, T * G, V_DIM), lambda b, t: (0, b, t, 0))
    prev_specs = [kstack(n_prev), vstack(n_prev)] if n_prev else []
    prev_args = [k_prev, v_prev] if n_prev else []
    return pl.pallas_call(
        functools.partial(_inproj_kernel, sample=False, n_prev=n_prev),
        grid=(B, nt),
        in_specs=[row(D), full(g), once(wm), once(wt),
                  pl.BlockSpec((T, LANES), lambda b, t: (t, 0)),
                  pl.BlockSpec((T, LANES), lambda b, t: (t, 0)),
                  pl.BlockSpec((HALF, T), lambda b, t: (0, t)),
                  pl.BlockSpec((HALF, T), lambda b, t: (0, t))] + prev_specs,
        out_specs=[row(d_rnn), row(d_rnn), row(d_att), vstack(n_prev + 1), tr_spec, kstack(n_prev + 1),
                   pl.BlockSpec((None, G * (V_DIM + ONES_ROWS), T), lambda b, t: (b, 0, t))],
        out_shape=[jax.ShapeDtypeStruct((B, S, d_rnn), F32), jax.ShapeDtypeStruct((B, S, d_rnn), F32),
                   jax.ShapeDtypeStruct((B, S, d_att), BF16),
                   jax.ShapeDtypeStruct((n_prev + 1, B, S * G, V_DIM), F32),
                   jax.ShapeDtypeStruct((B, d_att, S), BF16),
                   jax.ShapeDtypeStruct((n_prev + 1, B, d_att, S), F32),
                   jax.ShapeDtypeStruct((B, G * (V_DIM + ONES_ROWS), S), BF16)],
        compiler_params=_cparams(("parallel", "arbitrary"), INPROJ_VMEM_LIMIT),
        name="inproj_prompt",
    )(x, g, wm, wt, cq, sq, ct, st, *prev_args)


def _inproj_sample(x, g, wm, wt, ct, st):
    n, D = x.shape
    d_att = wt.shape[0] // 2
    d_rnn = (wm.shape[1] - d_att) // 2
    full = lambda a: pl.BlockSpec(a.shape, lambda i: (0,) * a.ndim)
    dummy = jnp.zeros((SUBLANES, LANES), F32)
    out_shapes = [jax.ShapeDtypeStruct((n, d_rnn), F32), jax.ShapeDtypeStruct((n, d_rnn), F32),
                  jax.ShapeDtypeStruct((n, d_att), F32), jax.ShapeDtypeStruct((d_att, n), F32),
                  jax.ShapeDtypeStruct((d_att, n), F32)]
    return pl.pallas_call(
        functools.partial(_inproj_kernel, sample=True),
        grid=(1,),
        in_specs=[full(x), full(g), full(wm), full(wt), full(dummy), full(dummy), full(ct), full(st)],
        out_specs=[pl.BlockSpec(s.shape, lambda i: (0, 0)) for s in out_shapes],
        out_shape=out_shapes,
        compiler_params=_cparams(("arbitrary",)),
        name="inproj_sample",
    )(x, g, wm, wt, dummy, dummy, ct, st)


def _gates(xc, wg_ref, bga, bgx, lam):
    half = xc.shape[1] // 2
    xcb = xc.astype(BF16)
    g0 = jnp.dot(xcb[:, :half], wg_ref[0], preferred_element_type=F32)
    g1 = jnp.dot(xcb[:, half:], wg_ref[1], preferred_element_type=F32)
    r = jax.nn.sigmoid(jnp.concatenate([g0[:, :half], g1[:, :half]], axis=1) + bga)
    i = jax.nn.sigmoid(jnp.concatenate([g0[:, half:], g1[:, half:]], axis=1) + bgx)
    neg = -lam
    softplus = jnp.maximum(neg, 0.0) + jnp.log1p(jnp.exp(-jnp.abs(neg)))
    log_a = -LRU_C * r * softplus
    a = jnp.exp(log_a)
    mult = jnp.sqrt(1.0 - a * a)
    return a, i, mult


def _lru_kernel(xr_ref, gr_ref, cw_ref, cb_ref, wg_ref, bga_ref, bgx_ref, lam_ref,
                y_ref, hlast_ref, ctail_ref, xp_scr, a_scr, b_scr, h_scr):
    t = pl.program_id(1)
    T, C = xr_ref.shape

    @pl.when(t == 0)
    def _():
        xp_scr[0:SUBLANES, :] = jnp.zeros((SUBLANES, C), F32)
        h_scr[...] = jnp.zeros_like(h_scr)

    x = xr_ref[...]
    xp_scr[SUBLANES:SUBLANES + T, :] = x
    cw = cw_ref[...]
    xc = cb_ref[...]
    for j in range(CONV_W - 1):
        xc = xc + xp_scr[pl.ds(SUBLANES - (CONV_W - 1) + j, T), :] * cw[j:j + 1]
    xc = xc + x * cw[CONV_W - 1:CONV_W]
    tail = x[T - (CONV_W - 1):, :]
    xp_scr[SUBLANES - (CONV_W - 1):SUBLANES, :] = tail
    ctail_ref[...] = tail

    a, i, mult = _gates(xc, wg_ref, bga_ref[...], bgx_ref[...], lam_ref[...])
    row = lax.broadcasted_iota(jnp.int32, (T, 1), 0)
    mult = jnp.where(jnp.logical_and(row == 0, t == 0), 1.0, mult)
    a_scr[...] = a
    b_scr[...] = mult * i * xc

    rowi = lax.broadcasted_iota(jnp.int32, (SUBLANES, C), 0)

    def body(gi, h):
        off = pl.multiple_of(gi * SUBLANES, SUBLANES)
        A = a_scr[pl.ds(off, SUBLANES), :]
        Bv = b_scr[pl.ds(off, SUBLANES), :]
        for d in (1, 2, 4):
            keep = rowi >= d
            Bv = jnp.where(keep, Bv + A * pltpu.roll(Bv, d, 0), Bv)
            A = jnp.where(keep, A * pltpu.roll(A, d, 0), A)
        hrows = Bv + A * h
        b_scr[pl.ds(off, SUBLANES), :] = hrows
        return hrows[SUBLANES - 1:SUBLANES, :]

    h = lax.fori_loop(0, T // SUBLANES, body, h_scr[...], unroll=2)
    h_scr[...] = h
    hlast_ref[...] = h
    y_ref[...] = (b_scr[...] * jax.nn.gelu(gr_ref[...])).astype(BF16)


def _lru_prompt(xr, gr, cw, cb, wg, bga, bgx, lam):
    B, S, C = xr.shape
    T = min(LRU_TILE, S)
    nt = S // T
    row = pl.BlockSpec((None, T, C), lambda b, t: (b, t, 0))
    full = lambda a: pl.BlockSpec(a.shape, lambda b, t: (0,) * a.ndim)
    return pl.pallas_call(
        _lru_kernel,
        grid=(B, nt),
        in_specs=[row, row, full(cw), full(cb), full(wg), full(bga), full(bgx), full(lam)],
        out_specs=[row, pl.BlockSpec((None, 1, C), lambda b, t: (b, 0, 0)),
                   pl.BlockSpec((None, CONV_W - 1, C), lambda b, t: (b, 0, 0))],
        out_shape=[jax.ShapeDtypeStruct((B, S, C), BF16), jax.ShapeDtypeStruct((B, 1, C), F32),
                   jax.ShapeDtypeStruct((B, CONV_W - 1, C), F32)],
        scratch_shapes=[pltpu.VMEM((T + SUBLANES, C), F32), pltpu.VMEM((T, C), F32),
                        pltpu.VMEM((T, C), F32), pltpu.VMEM((1, C), F32)],
        compiler_params=_cparams(("parallel", "arbitrary")),
        name="rglru_prompt",
    )(xr, gr, cw, cb, wg, bga, bgx, lam)


def _lru_step_kernel(xr_ref, gr_ref, conv_ref, h0_ref, cw_ref, cb_ref, wg_ref, bga_ref, bgx_ref, lam_ref,
                     y_ref, hnew_ref, cnew_ref):
    x = xr_ref[...]
    cw = cw_ref[...]
    xc = cb_ref[...]
    for j in range(CONV_W - 1):
        xc = xc + conv_ref[j] * cw[j:j + 1]
    xc = xc + x * cw[CONV_W - 1:CONV_W]
    for j in range(CONV_W - 2):
        cnew_ref[j] = conv_ref[j + 1]
    cnew_ref[CONV_W - 2] = x
    a, i, mult = _gates(xc, wg_ref, bga_ref[...], bgx_ref[...], lam_ref[...])
    h = a * h0_ref[...] + mult * i * xc
    hnew_ref[...] = h
    y_ref[...] = (h * jax.nn.gelu(gr_ref[...])).astype(BF16)


def _lru_step(xr, gr, conv, h0, cw, cb, wg, bga, bgx, lam):
    n, C = xr.shape
    args = (xr, gr, conv, h0, cw, cb, wg, bga, bgx, lam)
    full = lambda a: pl.BlockSpec(a.shape, lambda i: (0,) * a.ndim)
    out_shapes = [jax.ShapeDtypeStruct((n, C), BF16), jax.ShapeDtypeStruct((n, C), F32),
                  jax.ShapeDtypeStruct((CONV_W - 1, n, C), F32)]
    return pl.pallas_call(
        _lru_step_kernel,
        grid=(1,),
        in_specs=[full(a) for a in args],
        out_specs=[pl.BlockSpec(s.shape, lambda i, nd=len(s.shape): (0,) * nd) for s in out_shapes],
        out_shape=out_shapes,
        compiler_params=_cparams(("arbitrary",)),
        name="rglru_step",
    )(*args)


def _subln(o, g, scale):
    ms = jnp.mean(o * o, axis=-1, keepdims=True)
    return o * lax.rsqrt(ms + SUBLN_EPS) * g * scale


def _attn_kernel(lam_ref, qt_ref, kb_ref, vt_ref, g_ref, o_ref, qcat, m, acc, s_even, s_odd, *, out_scale):
    qi = pl.program_id(1)
    Tq = qt_ref.shape[1]
    Tk = Tq

    qt = qt_ref[...]
    row = lax.broadcasted_iota(jnp.int32, qt.shape, 0)
    second = (row & HALF) != 0
    zero = jnp.zeros_like(qt)
    qcat[:, :Tq] = jnp.where(second, zero, qt)
    qcat[:, Tq:] = jnp.where(second, qt, zero)
    m[...] = jnp.full_like(m, NEG_INF)
    acc[...] = jnp.zeros_like(acc)

    def scores(kj, sbuf):
        k0 = pl.multiple_of(kj * Tk, Tk)
        sbuf[...] = jnp.dot(kb_ref[pl.ds(k0, Tk), :], qcat[...], preferred_element_type=F32)

    def update(kj, sbuf, diagonal):
        k0 = pl.multiple_of(kj * Tk, Tk)
        s = sbuf[...]
        if diagonal:
            key = lax.broadcasted_iota(jnp.int32, s.shape, 0)
            qry = lax.broadcasted_iota(jnp.int32, s.shape, 1)
            qry = jnp.where(qry >= Tq, qry - Tq, qry)
            s = jnp.where(key <= qry, s, NEG_INF)
        m_old = m[...]
        m_new = jnp.maximum(m_old, jnp.max(s, axis=0, keepdims=True))
        alpha = jnp.exp(m_old - m_new)
        p = jnp.exp(s - m_new)
        acc[...] = alpha * acc[...] + jnp.dot(vt_ref[:, pl.ds(k0, Tk)], p.astype(BF16),
                                              preferred_element_type=F32)
        m[...] = m_new

    def finish():
        a = acc[:V_DIM, :] / acc[V_DIM:V_DIM + 1, :]
        o = a[:, :Tq] - lam_ref[0, 0] * a[:, Tq:]
        ms = jnp.mean(o * o, axis=0, keepdims=True)
        o = o * lax.rsqrt(ms + SUBLN_EPS) * (g_ref[...] * out_scale)
        o_ref[...] = o.T.astype(o_ref.dtype)

    scores(0, s_even)

    def pair(i, c):
        t = 2 * i
        scores(t + 1, s_odd)
        update(t, s_even, False)
        scores(t + 2, s_even)
        update(t + 1, s_odd, False)
        return c

    lax.fori_loop(0, qi // 2, pair, 0)

    @pl.when(qi % 2 == 0)
    def _():
        update(qi, s_even, True)
        finish()

    @pl.when(qi % 2 == 1)
    def _():
        scores(qi, s_odd)
        update(qi - 1, s_even, False)
        update(qi, s_odd, True)
        finish()


def _attn_prompt(qt, kb, vt, subln_g, lam, out_scale):
    B, S, d_att = kb.shape
    G = d_att // V_DIM
    T = min(ATT_TILE, S)
    return pl.pallas_call(
        functools.partial(_attn_kernel, out_scale=out_scale),
        grid=(B * G, S // T),
        in_specs=[
            pl.BlockSpec(memory_space=pltpu.SMEM),
            pl.BlockSpec((None, V_DIM, T), lambda bg, i: (bg // G, bg % G, i)),
            pl.BlockSpec((None, S, V_DIM), lambda bg, i: (bg // G, 0, bg % G)),
            pl.BlockSpec((None, V_DIM + ONES_ROWS, S), lambda bg, i: (bg // G, bg % G, 0)),
            pl.BlockSpec((V_DIM, 1), lambda bg, i: (0, 0)),
        ],
        out_specs=pl.BlockSpec((None, T, V_DIM), lambda bg, i: (bg // G, i, bg % G)),
        out_shape=jax.ShapeDtypeStruct((B, S, d_att), BF16),
        scratch_shapes=[pltpu.VMEM((V_DIM, 2 * T), BF16), pltpu.VMEM((1, 2 * T), F32),
                        pltpu.VMEM((V_DIM + ONES_ROWS, 2 * T), F32),
                        pltpu.VMEM((T, 2 * T), F32), pltpu.VMEM((T, 2 * T), F32)],
        compiler_params=_cparams(("parallel", "arbitrary")),
        name="diff_attn_prompt",
    )(lam, qt, kb, vt, subln_g.reshape(V_DIM, 1))


def _attn_sample_kernel(pt_ref, lam_ref, q_ref, kc_ref, vc_ref, g_ref, *rest, n_pages, out_scale):
    k_refs = rest[:n_pages]
    v_refs = rest[n_pages:2 * n_pages]
    o_ref = rest[2 * n_pages]
    lam = lam_ref[0, 0]
    q = q_ref[...]
    d_att = q.shape[1]
    n_heads = d_att // HEAD_DIM
    G = d_att // V_DIM
    R = 2 * SUBLANES
    rowi = lax.broadcasted_iota(jnp.int32, (R, d_att), 0)
    coli = lax.broadcasted_iota(jnp.int32, (R, d_att), 1)
    qblk = jnp.where(coli // HEAD_DIM == rowi, jnp.broadcast_to(q, (R, d_att)), 0.0)
    qb = qblk.astype(BF16)
    s = jnp.concatenate(
        [jnp.dot(qb, k_refs[j][...].astype(BF16), preferred_element_type=F32) for j in range(n_pages)], axis=1)
    s = s[:n_heads]
    s_cur = jnp.sum(qblk[:n_heads] * kc_ref[...], axis=-1, keepdims=True)
    m = jnp.maximum(jnp.max(s, axis=-1, keepdims=True), s_cur)
    p = jnp.exp(s - m)
    p_cur = jnp.exp(s_cur - m)
    inv_l = 1.0 / (jnp.sum(p, axis=-1, keepdims=True) + p_cur)
    p = p * inv_l
    p_cur = p_cur * inv_l
    amap = p - lam * pltpu.roll(p, n_heads - 1, 0)
    amap = jnp.concatenate([amap, jnp.zeros_like(amap)], axis=0).astype(BF16)
    p_cur = jnp.broadcast_to(p_cur, (n_heads, LANES))
    amap_cur = p_cur - lam * pltpu.roll(p_cur, n_heads - 1, 0)
    P = k_refs[0].shape[1]
    outs = []
    for g in range(G):
        acc = jnp.zeros((R, V_DIM), F32)
        for j in range(n_pages):
            vjg = v_refs[j][pl.ds(g, P, stride=G), :].astype(BF16)
            acc = acc + jnp.dot(amap[:, j * P:(j + 1) * P], vjg, preferred_element_type=F32)
        o = acc[2 * g:2 * g + 1, :] + amap_cur[2 * g:2 * g + 1, :] * vc_ref[:, g * V_DIM:(g + 1) * V_DIM]
        outs.append(_subln(o, g_ref[...], out_scale))
    o_ref[...] = jnp.concatenate(outs, axis=1).astype(o_ref.dtype)


def _attn_sample(pt_flat, lam, q, k_cur, v_cur, subln_g, ck, cv, layer, out_scale):
    n, d_att = q.shape
    n_pages = pt_flat.shape[0] // n
    P = ck.shape[-1]
    q3 = q.reshape(n, 1, d_att)
    kc3 = k_cur.reshape(n, 1, d_att)
    vc3 = v_cur.reshape(n, 1, d_att)
    tok = lambda: pl.BlockSpec((None, 1, d_att), lambda b, pt: (b, 0, 0))
    k_specs = [pl.BlockSpec((None, None, d_att, P), functools.partial(
        lambda b, pt, j: (layer, pt[b * n_pages + j], 0, 0), j=j)) for j in range(n_pages)]
    v_specs = [pl.BlockSpec((None, None, cv.shape[2], V_DIM), functools.partial(
        lambda b, pt, j: (layer, pt[b * n_pages + j], 0, 0), j=j)) for j in range(n_pages)]
    grid_spec = pltpu.PrefetchScalarGridSpec(
        num_scalar_prefetch=1,
        grid=(n,),
        in_specs=[pl.BlockSpec(memory_space=pltpu.SMEM), tok(), tok(), tok(),
                  pl.BlockSpec((1, V_DIM), lambda b, pt: (0, 0))] + k_specs + v_specs,
        out_specs=pl.BlockSpec((None, 1, d_att), lambda b, pt: (b, 0, 0)),
    )
    out = pl.pallas_call(
        functools.partial(_attn_sample_kernel, n_pages=n_pages, out_scale=out_scale),
        grid_spec=grid_spec,
        out_shape=jax.ShapeDtypeStruct((n, 1, d_att), BF16),
        compiler_params=_cparams(("arbitrary",)),
        name="diff_attn_sample",
    )(pt_flat, lam, q3, kc3, vc3, subln_g, *([ck] * n_pages), *([cv] * n_pages))
    return out.reshape(n, d_att)


def _attn_fused_kernel(pt_ref, lam_ref, qt_ref, kb_ref, vt_ref, gcol_ref, q_ref, kc_ref, vc_ref, g_ref, *rest,
                       n_pages, out_scale):
    pages = rest[:2 * n_pages]
    o_ref, os_ref, qcat, m, acc, s_even, s_odd = rest[2 * n_pages:]
    _attn_kernel(lam_ref, qt_ref, kb_ref, vt_ref, gcol_ref, o_ref, qcat, m, acc, s_even, s_odd, out_scale=out_scale)
    _attn_sample_kernel(pt_ref, lam_ref, q_ref, kc_ref, vc_ref, g_ref, *pages, os_ref,
                        n_pages=n_pages, out_scale=out_scale)


def _attn_prompt_and_sample(qt, kb, vt, subln_g, lam, out_scale, pt_flat, q_s, k_cur, v_cur, ck, cv, layer):
    B, S, d_att = kb.shape
    G = d_att // V_DIM
    T = min(ATT_TILE, S)
    nq = S // T
    n = q_s.shape[0]
    if n != B * G * nq:
        return (_attn_prompt(qt, kb, vt, subln_g, lam, out_scale),
                _attn_sample(pt_flat, lam, q_s, k_cur, v_cur, subln_g, ck, cv, layer, out_scale))
    n_pages = pt_flat.shape[0] // n
    P = ck.shape[-1]
    seq = lambda bg, i: bg * nq + i
    tok = lambda: pl.BlockSpec((None, 1, d_att), lambda bg, i, pt: (seq(bg, i), 0, 0))
    k_specs = [pl.BlockSpec((None, None, d_att, P), functools.partial(
        lambda bg, i, pt, j: (layer, pt[seq(bg, i) * n_pages + j], 0, 0), j=j)) for j in range(n_pages)]
    v_specs = [pl.BlockSpec((None, None, cv.shape[2], V_DIM), functools.partial(
        lambda bg, i, pt, j: (layer, pt[seq(bg, i) * n_pages + j], 0, 0), j=j)) for j in range(n_pages)]
    grid_spec = pltpu.PrefetchScalarGridSpec(
        num_scalar_prefetch=1,
        grid=(B * G, nq),
        in_specs=[
            pl.BlockSpec(memory_space=pltpu.SMEM),
            pl.BlockSpec((None, V_DIM, T), lambda bg, i, pt: (bg // G, bg % G, i)),
            pl.BlockSpec((None, S, V_DIM), lambda bg, i, pt: (bg // G, 0, bg % G)),
            pl.BlockSpec((None, V_DIM + ONES_ROWS, S), lambda bg, i, pt: (bg // G, bg % G, 0)),
            pl.BlockSpec((V_DIM, 1), lambda bg, i, pt: (0, 0)),
            tok(), tok(), tok(),
            pl.BlockSpec((1, V_DIM), lambda bg, i, pt: (0, 0)),
        ] + k_specs + v_specs,
        out_specs=[pl.BlockSpec((None, T, V_DIM), lambda bg, i, pt: (bg // G, i, bg % G)),
                   pl.BlockSpec((None, 1, d_att), lambda bg, i, pt: (seq(bg, i), 0, 0))],
        scratch_shapes=[pltpu.VMEM((V_DIM, 2 * T), BF16), pltpu.VMEM((1, 2 * T), F32),
                        pltpu.VMEM((V_DIM + ONES_ROWS, 2 * T), F32),
                        pltpu.VMEM((T, 2 * T), F32), pltpu.VMEM((T, 2 * T), F32)],
    )
    att, att_s = pl.pallas_call(
        functools.partial(_attn_fused_kernel, n_pages=n_pages, out_scale=out_scale),
        grid_spec=grid_spec,
        out_shape=[jax.ShapeDtypeStruct((B, S, d_att), BF16), jax.ShapeDtypeStruct((n, 1, d_att), BF16)],
        compiler_params=_cparams(("parallel", "arbitrary")),
        name="diff_attn_fused",
    )(pt_flat, lam, qt, kb, vt, subln_g.reshape(V_DIM, 1),
      q_s.reshape(n, 1, d_att), k_cur.reshape(n, 1, d_att), v_cur.reshape(n, 1, d_att), subln_g,
      *([ck] * n_pages), *([cv] * n_pages))
    return att, att_s.reshape(n, d_att)


def _outproj_kernel(yr_ref, at_ref, x_ref, wo_ref, g2_ref, wr_ref, br_ref, cnt_in_ref,
                    x1_ref, h2_ref, route_ref, route_t_ref, cnt_out_ref, cnt_scr, *, n_groups, per_group):
    step = pl.program_id(0)

    @pl.when(step == 0)
    def _():
        cnt_scr[...] = cnt_in_ref[...]

    d_rnn = yr_ref.shape[1]
    y = (jnp.dot(yr_ref[...], wo_ref[:d_rnn, :], preferred_element_type=F32)
         + jnp.dot(at_ref[...], wo_ref[d_rnn:, :], preferred_element_type=F32))
    x1 = x_ref[...] + y
    x1_ref[...] = x1
    ms = jnp.mean(x1 * x1, axis=-1, keepdims=True)
    h2 = x1 * lax.rsqrt(ms + NORM_EPS) * g2_ref[...]
    h2_ref[...] = h2
    logits = jnp.dot(h2.astype(BF16), wr_ref[...], preferred_element_type=F32) + br_ref[...]
    T = logits.shape[0]
    lane = lax.broadcasted_iota(jnp.int32, (T, LANES), 1).astype(F32)
    first = lambda hit: jnp.min(jnp.where(hit, lane, float(LANES)), axis=-1, keepdims=True)
    gl = jnp.where(lane < n_groups, logits, NEG_INF)
    gmax = jnp.max(gl, axis=-1, keepdims=True)
    gidx = first(gl == gmax)
    pg = 1.0 / jnp.sum(jnp.exp(gl - gmax), axis=-1, keepdims=True)
    lo = n_groups + per_group * gidx
    el = jnp.where(jnp.logical_and(lane >= lo, lane < lo + per_group), logits, NEG_INF)
    ma = jnp.max(el, axis=-1, keepdims=True)
    ia = first(el == ma)
    el2 = jnp.where(lane == ia, NEG_INF, el)
    mb = jnp.max(el2, axis=-1, keepdims=True)
    ib = first(el2 == mb)
    eb = jnp.exp(mb - ma)
    gate_a = pg / (1.0 + eb)
    gate_b = pg * eb / (1.0 + eb)
    hit_a = lane == ia
    hit_b = lane == ib
    onehot = jnp.where(jnp.logical_or(hit_a, hit_b), 1.0, 0.0)
    r_i = lax.broadcasted_iota(jnp.int32, (T, T), 0)
    c_i = lax.broadcasted_iota(jnp.int32, (T, T), 1)
    lower = jnp.where(c_i < r_i, 1.0, 0.0).astype(BF16)
    before = jnp.dot(lower, onehot.astype(BF16), preferred_element_type=F32) + cnt_scr[...]
    rank_a = jnp.sum(jnp.where(hit_a, before, 0.0), axis=-1, keepdims=True)
    rank_b = jnp.sum(jnp.where(hit_b, before, 0.0), axis=-1, keepdims=True)
    cnt_scr[...] = cnt_scr[...] + jnp.sum(onehot, axis=0, keepdims=True)
    cnt_out_ref[...] = cnt_scr[...]
    cols = (ia - n_groups, ib - n_groups, gate_a, gate_b, rank_a, rank_b)
    route = jnp.zeros((T, LANES), F32)
    for c, val in enumerate(cols):
        route = jnp.where(lane == float(c), val, route)
    route_ref[...] = route
    route_t_ref[...] = route.T[:SUBLANES]


def _outproj_route(yr, at, x, wo, g2, wr, br, cnt_in, n_groups, per_group):
    N, D = x.shape
    T = min(ROW_TILE, N)
    d_rnn = yr.shape[1]
    row = lambda w: pl.BlockSpec((T, w), lambda i: (i, 0))
    full = lambda a: pl.BlockSpec(a.shape, lambda i: (0,) * a.ndim)
    return pl.pallas_call(
        functools.partial(_outproj_kernel, n_groups=n_groups, per_group=per_group),
        grid=(N // T,),
        in_specs=[row(d_rnn), row(at.shape[1]), row(D), full(wo), full(g2), full(wr), full(br), full(cnt_in)],
        out_specs=[row(D), row(D), row(LANES), pl.BlockSpec((SUBLANES, T), lambda i: (0, i)),
                   pl.BlockSpec((1, LANES), lambda i: (0, 0))],
        out_shape=[jax.ShapeDtypeStruct((N, D), F32), jax.ShapeDtypeStruct((N, D), F32),
                   jax.ShapeDtypeStruct((N, LANES), F32), jax.ShapeDtypeStruct((SUBLANES, N), F32),
                   jax.ShapeDtypeStruct((1, LANES), F32)],
        scratch_shapes=[pltpu.VMEM((1, LANES), F32)],
        compiler_params=_cparams(("arbitrary",)),
        name="outproj_route",
    )(yr, at, x, wo, g2, wr, br, cnt_in)


def _dest_kernel(seg_ref, rt_ref, o_ref):
    e = rt_ref[0:TOP_K, :].astype(jnp.int32)
    start = jnp.zeros_like(e)
    for ex in range(seg_ref.shape[0]):
        start = jnp.where(e == ex, seg_ref[ex], start)
    o_ref[...] = start + rt_ref[4:4 + TOP_K, :].astype(jnp.int32)


def _dest(seg_start, route_t):
    R, N = route_t.shape
    T = min(ROW_TILE, N)
    grid_spec = pltpu.PrefetchScalarGridSpec(
        num_scalar_prefetch=1,
        grid=(N // T,),
        in_specs=[pl.BlockSpec((R, T), lambda i, seg: (0, i))],
        out_specs=pl.BlockSpec((TOP_K, T), lambda i, seg: (0, i)),
    )
    return pl.pallas_call(
        _dest_kernel,
        grid_spec=grid_spec,
        out_shape=jax.ShapeDtypeStruct((TOP_K, N), jnp.int32),
        compiler_params=_cparams(("arbitrary",)),
        name="moe_dest",
    )(seg_start, route_t)


def _dispatch_kernel(zb_ref, dest_ref, h_ref, *rest, zero_fill):
    xb_hbm, zbuf, sem = rest[-3], rest[-2], rest[-1]
    T = h_ref.shape[0]

    if zero_fill:
        @pl.when(pl.program_id(0) == 0)
        def _():
            zbuf[...] = jnp.zeros_like(zbuf)

            def zcopy(i):
                row0 = pl.multiple_of(zb_ref[i] * EXP_BLOCK, EXP_BLOCK)
                return pltpu.make_async_copy(zbuf, xb_hbm.at[pl.ds(row0, EXP_BLOCK)], sem)

            def zstart(i, c):
                @pl.when(zb_ref[i] >= 0)
                def _():
                    zcopy(i).start()
                return c

            def zwait(i, c):
                @pl.when(zb_ref[i] >= 0)
                def _():
                    zcopy(i).wait()
                return c

            lax.fori_loop(0, zb_ref.shape[0], zstart, 0)
            lax.fori_loop(0, zb_ref.shape[0], zwait, 0)

    def body(k, c):
        r0 = pl.multiple_of(k * SUBLANES, SUBLANES)
        rows = h_ref.at[pl.ds(r0, SUBLANES)]
        for j in range(SUBLANES):
            for s in range(TOP_K):
                pltpu.make_async_copy(rows.at[pl.ds(j, 1)],
                                      xb_hbm.at[pl.ds(dest_ref[s, r0 + j], 1)], sem).start(priority=s)
        return c

    lax.fori_loop(0, T // SUBLANES, body, 0, unroll=2)
    for s in range(TOP_K):
        pltpu.make_async_copy(h_ref, xb_hbm.at[pl.ds(0, T)], sem).wait()


def _dispatch(zb, dest, h, xb, xb_rows):
    N, D = h.shape
    T = min(DISPATCH_TILE, N)
    dest3 = jnp.transpose(dest.reshape(TOP_K, N // T, T), (1, 0, 2))
    aliased = xb is not None
    in_specs = [pl.BlockSpec((None, TOP_K, T), lambda i, zb: (i, 0, 0), memory_space=pltpu.SMEM),
                pl.BlockSpec((T, D), lambda i, zb: (i, 0))]
    args = [zb, dest3, h]
    if aliased:
        in_specs.append(pl.BlockSpec(memory_space=pl.ANY))
        args.append(xb)
    grid_spec = pltpu.PrefetchScalarGridSpec(
        num_scalar_prefetch=1,
        grid=(N // T,),
        in_specs=in_specs,
        out_specs=pl.BlockSpec(memory_space=pl.ANY),
        scratch_shapes=[pltpu.VMEM((EXP_BLOCK, D), F32), pltpu.SemaphoreType.DMA(())],
    )
    return pl.pallas_call(
        functools.partial(_dispatch_kernel, zero_fill=not aliased),
        grid_spec=grid_spec,
        out_shape=jax.ShapeDtypeStruct((xb_rows, D), F32),
        input_output_aliases={3: 0} if aliased else {},
        compiler_params=pltpu.CompilerParams(dimension_semantics=("arbitrary",), has_side_effects=True),
        name="moe_dispatch",
    )(*args)


def _expert_kernel(be_ref, bi_ref, br_ref, bf_ref, x_ref, w1_ref, w2_ref, y_ref, w1b, w2b):
    b = pl.program_id(0)
    rows = br_ref[b]

    @pl.when(bf_ref[b] == 1)
    def _():
        w1b[...] = w1_ref[...].astype(BF16)
        w2b[...] = w2_ref[...].astype(BF16)

    @pl.when(rows > 0)
    def _():
        h = jnp.dot(x_ref[...].astype(BF16), w1b[...], preferred_element_type=F32)
        de = h.shape[1] // 2
        act = (jax.nn.silu(h[:, :de]) * h[:, de:]).astype(BF16)
        y_ref[...] = jnp.dot(act, w2b[...], preferred_element_type=F32)

    @pl.when(rows == 0)
    def _():
        y_ref[...] = jnp.zeros_like(y_ref)


def _experts(be, bi, br, bf, xb, w1, w2, layer):
    R, D = xb.shape
    nb = be.shape[0]
    d2 = w1.shape[-1]
    de = w2.shape[-2]
    grid_spec = pltpu.PrefetchScalarGridSpec(
        num_scalar_prefetch=4,
        grid=(nb,),
        in_specs=[pl.BlockSpec((EXP_BLOCK, D), lambda b, be, bi, br, bf: (bi[b], 0)),
                  pl.BlockSpec((None, None, D, d2), lambda b, be, bi, br, bf: (layer, be[b], 0, 0)),
                  pl.BlockSpec((None, None, de, D), lambda b, be, bi, br, bf: (layer, be[b], 0, 0))],
        out_specs=pl.BlockSpec((EXP_BLOCK, D), lambda b, be, bi, br, bf: (b, 0)),
        scratch_shapes=[pltpu.VMEM((D, d2), BF16), pltpu.VMEM((de, D), BF16)],
    )
    return pl.pallas_call(
        _expert_kernel,
        grid_spec=grid_spec,
        out_shape=jax.ShapeDtypeStruct((R, D), F32),
        compiler_params=_cparams(("arbitrary",)),
        name="moe_experts",
    )(be, bi, br, bf, xb, w1, w2)


def _combine_kernel(dest_ref, x1_ref, route_ref, gf_ref, yb_hbm, o_ref, ybuf, sem, *, final):
    T = x1_ref.shape[0]

    def body(k, c):
        r0 = pl.multiple_of(k * SUBLANES, SUBLANES)
        for s in range(TOP_K):
            rows = ybuf.at[s, pl.ds(r0, SUBLANES)]
            for j in range(SUBLANES):
                pltpu.make_async_copy(yb_hbm.at[pl.ds(dest_ref[s, r0 + j], 1)],
                                      rows.at[pl.ds(j, 1)], sem).start(priority=s)
        return c

    lax.fori_loop(0, T // SUBLANES, body, 0, unroll=2)
    for s in range(TOP_K):
        pltpu.make_async_copy(yb_hbm.at[pl.ds(0, T)], ybuf.at[s], sem).wait()
    route = route_ref[...]
    out = x1_ref[...] + ybuf[0] * route[:, 2:3] + ybuf[1] * route[:, 3:4]
    if final:
        ms = jnp.mean(out * out, axis=-1, keepdims=True)
        out = out * lax.rsqrt(ms + NORM_EPS) * gf_ref[...]
    o_ref[...] = out


def _combine(dest, x1, route, gf, yb, final):
    N, D = x1.shape
    T = min(COMBINE_TILE, N)
    dest3 = jnp.transpose(dest.reshape(TOP_K, N // T, T), (1, 0, 2))
    return pl.pallas_call(
        functools.partial(_combine_kernel, final=final),
        grid=(N // T,),
        in_specs=[pl.BlockSpec((None, TOP_K, T), lambda i: (i, 0, 0), memory_space=pltpu.SMEM),
                  pl.BlockSpec((T, D), lambda i: (i, 0)),
                  pl.BlockSpec((T, LANES), lambda i: (i, 0)),
                  pl.BlockSpec((1, D), lambda i: (0, 0)),
                  pl.BlockSpec(memory_space=pl.ANY)],
        out_specs=pl.BlockSpec((T, D), lambda i: (i, 0)),
        out_shape=jax.ShapeDtypeStruct((N, D), F32),
        scratch_shapes=[pltpu.VMEM((TOP_K, T, D), F32), pltpu.SemaphoreType.DMA(())],
        compiler_params=_cparams(("arbitrary",)),
        name="moe_combine",
    )(dest3, x1, route, gf, yb)


def _rope_tables(pos):
    inv = ROPE_THETA ** (-jnp.arange(HALF, dtype=F32) / HALF)
    ang = pos.astype(F32)[:, None] * inv[None, :]
    cos, sin = jnp.cos(ang), jnp.sin(ang)
    cq = jnp.tile(cos, (1, LANES // HALF))
    sq = jnp.concatenate([-sin, -sin, sin, sin], axis=1)
    return cq, sq, cos.T, sin.T


def _q_perm(d_att):
    idx = []
    for g in range(d_att // V_DIM):
        for part in range(2):
            for jj in range(2):
                base = HEAD_DIM * (2 * g + jj) + HALF * part
                idx.extend(range(base, base + HALF))
    return jnp.asarray(idx, jnp.int32)


def _gate_weights(w_ga, w_gx):
    nb, bw, _ = w_ga.shape
    half = nb // 2
    eye = jnp.eye(half, dtype=w_ga.dtype)

    def dense(w):
        return jnp.einsum('nij,nm->nimj', w, eye).reshape(half * bw, half * bw)

    return jnp.stack([jnp.concatenate([dense(w_ga[h * half:(h + 1) * half]),
                                       dense(w_gx[h * half:(h + 1) * half])], axis=1)
                      for h in range(2)]).astype(BF16)


def _block_tables(counts, nb):
    ne = counts.shape[0]
    nblk = (counts + EXP_BLOCK - 1) // EXP_BLOCK
    cum = jnp.sum(jnp.where(jnp.arange(ne)[None, :] <= jnp.arange(ne)[:, None], nblk[None, :], 0), axis=1)
    total = cum[-1]
    b = jnp.arange(nb, dtype=jnp.int32)
    bc = jnp.minimum(b, total - 1)
    e = jnp.minimum(jnp.sum(bc[:, None] >= cum[None, :], axis=1), ne - 1).astype(jnp.int32)
    pick = lambda tab: jnp.sum(jnp.where(e[:, None] == jnp.arange(ne)[None, :], tab[None, :], 0), axis=1)
    off = bc - (pick(cum) - pick(nblk))
    valid = b < total
    rows = jnp.where(valid, jnp.clip(pick(counts) - off * EXP_BLOCK, 0, EXP_BLOCK), 0)
    first = jnp.where(jnp.logical_and(valid, off == 0), 1, 0)
    last_blk = jnp.where(nblk > 0, cum - 1, -1)
    trail = total + jnp.arange(ne, dtype=jnp.int32)
    trail = jnp.where(trail < nb, trail, -1)
    zb = jnp.concatenate([last_blk, trail]).astype(jnp.int32)
    seg_start = ((cum - nblk) * EXP_BLOCK).astype(jnp.int32)
    return seg_start, e, bc.astype(jnp.int32), rows.astype(jnp.int32), first.astype(jnp.int32), zb


def kernel(x_prompt, x_sample, cache_k, cache_v, state_h, state_conv, page_table, ln1_g, w_in, conv_w, conv_b,
           w_ga, b_ga, w_gx, b_gx, lru_lambda, lam_q1, lam_k1, lam_q2, lam_k2, subln_g, w_out, ln2_g,
           w_route_group, b_route_group, w_route_expert, b_route_expert, w_exp_in, w_exp_out, ln_f):
    Bp, Sp, D = x_prompt.shape
    Bs = x_sample.shape[0]
    depth = w_in.shape[0]
    d_rnn = state_h.shape[-1]
    d_att = cache_k.shape[-2] * cache_k.shape[-1]
    n_groups = w_route_group.shape[-1]
    n_experts = w_route_expert.shape[-1]
    per_group = n_experts // n_groups
    n_pool, page = cache_k.shape[1], cache_k.shape[2]
    n_pages = page_table.shape[1]
    past_len = n_pages * page
    Np = Bp * Sp
    assert x_sample.shape[1] == 1 and n_groups + n_experts <= LANES

    cq, sq, ct, st = _rope_tables(jnp.arange(Sp))
    _, _, ct_s, st_s = _rope_tables(jnp.full((Bs,), past_len))
    perm = _q_perm(d_att)
    qscale = HEAD_DIM ** -0.5
    ck = jnp.transpose(cache_k, (0, 1, 3, 4, 2)).reshape(depth, n_pool, d_att, page)
    cv = cache_v.reshape(depth, n_pool, page * (d_att // V_DIM), V_DIM)
    pt_flat = page_table.reshape(-1).astype(jnp.int32)
    conv_s = jnp.transpose(state_conv, (0, 2, 1, 3))

    n_blocks = (TOP_K * (Np + Bs)) // EXP_BLOCK + n_experts

    xp = x_prompt
    xs = x_sample.reshape(Bs, D)
    kt_all = v_all = None
    hp_l, cp_l, ks_l, vs_l, hs_l, cs_l = [], [], [], [], [], []
    for l in range(depth):
        lambda_init = 0.8 - 0.6 * math.exp(-0.3 * l)
        out_scale = 1.0 - lambda_init
        lam = (jnp.exp(jnp.sum(lam_q1[l] * lam_k1[l])) - jnp.exp(jnp.sum(lam_q2[l] * lam_k2[l]))
               + lambda_init).reshape(1, 1).astype(F32)
        wl = w_in[l]
        w_xg = wl[:, :2 * d_rnn]
        w_q = wl[:, 2 * d_rnn:2 * d_rnn + d_att] * qscale
        w_k = wl[:, 2 * d_rnn + d_att:2 * d_rnn + 2 * d_att]
        w_v = wl[:, 2 * d_rnn + 2 * d_att:]
        wm_p = jnp.concatenate([w_xg, w_k[:, perm], w_v], axis=1).astype(BF16)
        wt_p = jnp.concatenate([w_q[:, perm], w_k, w_v], axis=1).T.astype(BF16)
        wm_s = jnp.concatenate([w_xg, w_v], axis=1).astype(BF16)
        wt_s = jnp.concatenate([w_k, w_q], axis=1).T.astype(BF16)
        g1 = ln1_g[l].reshape(1, D)
        wg = _gate_weights(w_ga[l], w_gx[l])
        bga = b_ga[l].reshape(1, d_rnn)
        bgx = b_gx[l].reshape(1, d_rnn)
        lru_l = lru_lambda[l].reshape(1, d_rnn)
        cw = conv_w[l]
        cb = conv_b[l].reshape(1, d_rnn)
        sg = subln_g[l].reshape(1, V_DIM)
        wo = w_out[l].astype(BF16)
        g2 = ln2_g[l].reshape(1, D)
        wr = jnp.zeros((D, LANES), F32).at[:, :n_groups].set(w_route_group[l]).at[
            :, n_groups:n_groups + n_experts].set(w_route_expert[l])
        wr = wr.astype(BF16)
        br =jnp.zeros((1, LANES), F32).at[0, :n_groups].set(b_route_group[l]).at[
            0, n_groups:n_groups + n_experts].set(b_route_expert[l])

        xr, gr, kb, v_all, qt, kt_all, vt = _inproj_prompt(xp, g1, wm_p, wt_p, cq, sq, ct, st, kt_all, v_all)
        y_rnn, h_last, c_tail = _lru_prompt(xr, gr, cw, cb, wg, bga, bgx, lru_l)
        xr_s, gr_s, v_s, kt_s, qt_s = _inproj_sample(xs, g1, wm_s, wt_s, ct_s, st_s)
        y_rnn_s, h_s, c_s = _lru_step(xr_s, gr_s, conv_s[l], state_h[l], cw, cb, wg, bga, bgx, lru_l)
        k_s = kt_s.T
        att, att_s = _attn_prompt_and_sample(qt, kb, vt, sg, lam, out_scale,
                                             pt_flat, qt_s.T, k_s, v_s, ck, cv, l)
        cnt0 = jnp.zeros((1, LANES), F32)
        x1, h2, route, route_t, cnt1 = _outproj_route(
            y_rnn.reshape(Np, d_rnn), att.reshape(Np, d_att), xp.reshape(Np, D), wo, g2, wr, br, cnt0,
            n_groups, per_group)
        x1_s, h2_s, route_s, route_ts, cnt2 = _outproj_route(y_rnn_s, att_s, xs, wo, g2, wr, br, cnt1,
                                                             n_groups, per_group)
        counts = cnt2[0, n_groups:n_groups + n_experts].astype(jnp.int32)
        seg_start, be, bi, brows, bfirst, zb = _block_tables(counts, n_blocks)

        dest_p, dest_s = _dest(seg_start, route_t), _dest(seg_start, route_ts)
        xb = _dispatch(zb, dest_p, h2, None, n_blocks * EXP_BLOCK)
        xb = _dispatch(zb, dest_s, h2_s, xb, n_blocks * EXP_BLOCK)
        yb = _experts(be, bi, brows, bfirst, xb, w_exp_in, w_exp_out, l)
        final = l == depth - 1
        gf = ln_f.reshape(1, D)
        xp_new = _combine(dest_p, x1, route, gf, yb, final)
        xs = _combine(dest_s, x1_s, route_s, gf, yb, final)
        xp = xp_new.reshape(Bp, Sp, D)

        n_qk = d_att // HEAD_DIM
        hp_l.append(h_last.reshape(Bp, d_rnn))
        cp_l.append(c_tail)
        ks_l.append(k_s.reshape(Bs, 1, n_qk, HEAD_DIM))
        vs_l.append(v_s.reshape(Bs, 1, d_att // V_DIM, V_DIM))
        hs_l.append(h_s)
        cs_l.append(jnp.transpose(c_s, (1, 0, 2)))
    k_prompt = jnp.transpose(kt_all.reshape(depth, Bp, n_qk, HEAD_DIM, Sp), (0, 1, 4, 2, 3))
    v_prompt = v_all.reshape(depth, Bp, Sp, d_att // V_DIM, V_DIM)
    return (xp, xs.reshape(Bs, 1, D),
            k_prompt, v_prompt, jnp.stack(hp_l), jnp.stack(cp_l),
            jnp.stack(ks_l), jnp.stack(vs_l), jnp.stack(hs_l), jnp.stack(cs_l))
```

```python
import functools
import math

import jax
import jax.numpy as jnp
from jax import lax
from jax.experimental import pallas as pl
from jax.experimental.pallas import tpu as pltpu

F32 = jnp.float32
BF16 = jnp.bfloat16

HEAD_DIM = 64
V_DIM = 2 * HEAD_DIM
HALF = HEAD_DIM // 2
ONES_ROWS = 16
CONV_W = 4
LRU_C = 8.0
ROPE_THETA = 10000.0
NORM_EPS = 1e-6
SUBLN_EPS = 1e-5
NEG_INF = -1e30
TOP_K = 2

LANES = 128
SUBLANES = 8
VMEM_LIMIT = 48 * 1024 * 1024
INPROJ_VMEM_LIMIT = 56 * 1024 * 1024

ROW_TILE = 512
LRU_TILE = 256
ATT_TILE = 512
EXP_BLOCK = 512
DISPATCH_TILE = 512
COMBINE_TILE = 256


def _cparams(sem, vmem_limit=VMEM_LIMIT):
    return pltpu.CompilerParams(dimension_semantics=sem, vmem_limit_bytes=vmem_limit)


def _inproj_kernel(x_ref, g_ref, wm_ref, wt_ref, cq_ref, sq_ref, ct_ref, st_ref, *rest, sample, n_prev=0):
    prev, outs = rest[:2 if n_prev else 0], rest[2 if n_prev else 0:]
    x = x_ref[...]
    ms = jnp.mean(x * x, axis=-1, keepdims=True)
    h = (x * lax.rsqrt(ms + NORM_EPS) * g_ref[...]).astype(BF16)
    main = jnp.dot(h, wm_ref[...], preferred_element_type=F32)
    d_rnn = outs[0].shape[1]
    ct = ct_ref[...]
    st = st_ref[...]
    tr = lax.dot_general(wt_ref[...], h, (((1,), (1,)), ((), ())), preferred_element_type=F32)

    def rope_t(blk):
        x1 = blk[:HALF]
        x2 = blk[HALF:]
        return x1 * ct - x2 * st, x2 * ct + x1 * st

    if sample:
        xr_ref, gr_ref, v_ref, kt_ref, qt_ref = outs
        xr_ref[...] = main[:, :d_rnn]
        gr_ref[...] = main[:, d_rnn:2 * d_rnn]
        v_ref[...] = main[:, 2 * d_rnn:]
        n_heads = kt_ref.shape[0] // HEAD_DIM
        for hh in range(n_heads):
            o1, o2 = rope_t(tr[hh * HEAD_DIM:(hh + 1) * HEAD_DIM])
            kt_ref[hh * HEAD_DIM:hh * HEAD_DIM + HALF, :] = o1
            kt_ref[hh * HEAD_DIM + HALF:(hh + 1) * HEAD_DIM, :] = o2
            base = (n_heads + hh) * HEAD_DIM
            p1, p2 = rope_t(tr[base:base + HEAD_DIM])
            qt_ref[hh * HEAD_DIM:hh * HEAD_DIM + HALF, :] = p1
            qt_ref[hh * HEAD_DIM + HALF:(hh + 1) * HEAD_DIM, :] = p2
    else:
        xr_ref, gr_ref, kb_ref, v_ref, qt_ref, kt_ref, vt_ref = outs
        xr_ref[...] = main[:, :d_rnn]
        gr_ref[...] = main[:, d_rnn:2 * d_rnn]
        d_att = kb_ref.shape[1]
        k = main[:, 2 * d_rnn:2 * d_rnn + d_att]
        cq = cq_ref[...]
        sq = sq_ref[...]
        for c in range(d_att // LANES):
            kc = k[:, c * LANES:(c + 1) * LANES]
            kb_ref[:, c * LANES:(c + 1) * LANES] = (kc * cq + pltpu.roll(kc, HEAD_DIM, 1) * sq).astype(BF16)
        if n_prev:
            kt_ref[:n_prev] = prev[0][...]
            v_ref[:n_prev] = prev[1][...]
        v = main[:, 2 * d_rnn + d_att:]
        T = v.shape[0]
        G = d_att // V_DIM
        for g in range(G):
            v_ref[n_prev, pl.ds(g, T, stride=G), :] = v[:, g * V_DIM:(g + 1) * V_DIM]
        n_heads = d_att // HEAD_DIM
        for hh in range(n_heads):
            g, jj = divmod(hh, 2)
            r1 = g * V_DIM + jj * HALF
            x1 = tr[r1:r1 + HALF]
            x2 = tr[r1 + HEAD_DIM:r1 + HEAD_DIM + HALF]
            qt_ref[r1:r1 + HALF, :] = (x1 * ct - x2 * st).astype(BF16)
            qt_ref[r1 + HEAD_DIM:r1 + HEAD_DIM + HALF, :] = (x2 * ct + x1 * st).astype(BF16)
            base = d_att + hh * HEAD_DIM
            o1, o2 = rope_t(tr[base:base + HEAD_DIM])
            kt_ref[n_prev, hh * HEAD_DIM:hh * HEAD_DIM + HALF, :] = o1
            kt_ref[n_prev, hh * HEAD_DIM + HALF:(hh + 1) * HEAD_DIM, :] = o2
        for g in range(G):
            r0 = g * (V_DIM + ONES_ROWS)
            vt_ref[r0:r0 + V_DIM, :] = tr[2 * d_att + g * V_DIM:2 * d_att + (g + 1) * V_DIM].astype(BF16)
            vt_ref[r0 + V_DIM:r0 + V_DIM + ONES_ROWS, :] = jnp.ones((ONES_ROWS, T), BF16)


def _inproj_prompt(x, g, wm, wt, cq, sq, ct, st, k_prev, v_prev):
    B, S, D = x.shape
    T = min(ROW_TILE, S)
    nt = S // T
    d_att = wt.shape[0] // 3
    d_rnn = (wm.shape[1] - 2 * d_att) // 2
    G = d_att // V_DIM
    n_prev = 0 if k_prev is None else k_prev.shape[0]
    row = lambda w: pl.BlockSpec((None, T, w), lambda b, t: (b, t, 0))
    full = lambda a: pl.BlockSpec(a.shape, lambda b, t: (0,) * a.ndim)
    once = lambda a: pl.BlockSpec(a.shape, lambda b, t: (0,) * a.ndim, pipeline_mode=pl.Buffered(1))
    tr_spec = pl.BlockSpec((None, d_att, T), lambda b, t: (b, 0, t))
    kstack = lambda n: pl.BlockSpec((n, None, d_att, T), lambda b, t: (0, b, 0, t))
    vstack = lambda n: pl.BlockSpec((n, None, T * G, V_DIM), lambda b, t: (0, b, t, 0))
    prev_specs = [kstack(n_prev), vstack(n_prev)] if n_prev else []
    prev_args = [k_prev, v_prev] if n_prev else []
    return pl.pallas_call(
        functools.partial(_inproj_kernel, sample=False, n_prev=n_prev),
        grid=(B, nt),
        in_specs=[row(D), full(g), once(wm), once(wt),
                  pl.BlockSpec((T, LANES), lambda b, t: (t, 0)),
                  pl.BlockSpec((T, LANES), lambda b, t: (t, 0)),
                  pl.BlockSpec((HALF, T), lambda b, t: (0, t)),
                  pl.BlockSpec((HALF, T), lambda b, t: (0, t))] + prev_specs,
        out_specs=[row(d_rnn), row(d_rnn), row(d_att), vstack(n_prev + 1), tr_spec, kstack(n_prev + 1),
                   pl.BlockSpec((None, G * (V_DIM + ONES_ROWS), T), lambda b, t: (b, 0, t))],
        out_shape=[jax.ShapeDtypeStruct((B, S, d_rnn), F32), jax.ShapeDtypeStruct((B, S, d_rnn), F32),
                   jax.ShapeDtypeStruct((B, S, d_att), BF16),
                   jax.ShapeDtypeStruct((n_prev + 1, B, S * G, V_DIM), F32),
                   jax.ShapeDtypeStruct((B, d_att, S), BF16),
                   jax.ShapeDtypeStruct((n_prev + 1, B, d_att, S), F32),
                   jax.ShapeDtypeStruct((B, G * (V_DIM + ONES_ROWS), S), BF16)],
        compiler_params=_cparams(("parallel", "arbitrary"), INPROJ_VMEM_LIMIT),
        name="inproj_prompt",
    )(x, g, wm, wt, cq, sq, ct, st, *prev_args)


def _inproj_sample(x, g, wm, wt, ct, st):
    n, D = x.shape
    d_att = wt.shape[0] // 2
    d_rnn = (wm.shape[1] - d_att) // 2
    full = lambda a: pl.BlockSpec(a.shape, lambda i: (0,) * a.ndim)
    dummy = jnp.zeros((SUBLANES, LANES), F32)
    out_shapes = [jax.ShapeDtypeStruct((n, d_rnn), F32), jax.ShapeDtypeStruct((n, d_rnn), F32),
                  jax.ShapeDtypeStruct((n, d_att), F32), jax.ShapeDtypeStruct((d_att, n), F32),
                  jax.ShapeDtypeStruct((d_att, n), F32)]
    return pl.pallas_call(
        functools.partial(_inproj_kernel, sample=True),
        grid=(1,),
        in_specs=[full(x), full(g), full(wm), full(wt), full(dummy), full(dummy), full(ct), full(st)],
        out_specs=[pl.BlockSpec(s.shape, lambda i: (0, 0)) for s in out_shapes],
        out_shape=out_shapes,
        compiler_params=_cparams(("arbitrary",)),
        name="inproj_sample",
    )(x, g, wm, wt, dummy, dummy, ct, st)


def _gates(xc, wg_ref, bga, bgx, lam):
    half = xc.shape[1] // 2
    xcb = xc.astype(BF16)
    g0 = jnp.dot(xcb[:, :half], wg_ref[0], preferred_element_type=F32)
    g1 = jnp.dot(xcb[:, half:], wg_ref[1], preferred_element_type=F32)
    r = jax.nn.sigmoid(jnp.concatenate([g0[:, :half], g1[:, :half]], axis=1) + bga)
    i = jax.nn.sigmoid(jnp.concatenate([g0[:, half:], g1[:, half:]], axis=1) + bgx)
    neg = -lam
    softplus = jnp.maximum(neg, 0.0) + jnp.log1p(jnp.exp(-jnp.abs(neg)))
    log_a = -LRU_C * r * softplus
    a = jnp.exp(log_a)
    mult = jnp.sqrt(1.0 - a * a)
    return a, i, mult


def _lru_kernel(xr_ref, gr_ref, cw_ref, cb_ref, wg_ref, bga_ref, bgx_ref, lam_ref,
                y_ref, hlast_ref, ctail_ref, xp_scr, a_scr, b_scr, h_scr):
    t = pl.program_id(1)
    T, C = xr_ref.shape

    @pl.when(t == 0)
    def _():
        xp_scr[0:SUBLANES, :] = jnp.zeros((SUBLANES, C), F32)
        h_scr[...] = jnp.zeros_like(h_scr)

    x = xr_ref[...]
    xp_scr[SUBLANES:SUBLANES + T, :] = x
    cw = cw_ref[...]
    xc = cb_ref[...]
    for j in range(CONV_W - 1):
        xc = xc + xp_scr[pl.ds(SUBLANES - (CONV_W - 1) + j, T), :] * cw[j:j + 1]
    xc = xc + x * cw[CONV_W - 1:CONV_W]
    tail = x[T - (CONV_W - 1):, :]
    xp_scr[SUBLANES - (CONV_W - 1):SUBLANES, :] = tail
    ctail_ref[...] = tail

    a, i, mult = _gates(xc, wg_ref, bga_ref[...], bgx_ref[...], lam_ref[...])
    row = lax.broadcasted_iota(jnp.int32, (T, 1), 0)
    mult = jnp.where(jnp.logical_and(row == 0, t == 0), 1.0, mult)
    a_scr[...] = a
    b_scr[...] = mult * i * xc

    rowi = lax.broadcasted_iota(jnp.int32, (SUBLANES, C), 0)

    def body(gi, h):
        off = pl.multiple_of(gi * SUBLANES, SUBLANES)
        A = a_scr[pl.ds(off, SUBLANES), :]
        Bv = b_scr[pl.ds(off, SUBLANES), :]
        for d in (1, 2, 4):
            keep = rowi >= d
            Bv = jnp.where(keep, Bv + A * pltpu.roll(Bv, d, 0), Bv)
            A = jnp.where(keep, A * pltpu.roll(A, d, 0), A)
        hrows = Bv + A * h
        b_scr[pl.ds(off, SUBLANES), :] = hrows
        return hrows[SUBLANES - 1:SUBLANES, :]

    h = lax.fori_loop(0, T // SUBLANES, body, h_scr[...], unroll=2)
    h_scr[...] = h
    hlast_ref[...] = h
    y_ref[...] = (b_scr[...] * jax.nn.gelu(gr_ref[...])).astype(BF16)


def _lru_prompt(xr, gr, cw, cb, wg, bga, bgx, lam):
    B, S, C = xr.shape
    T = min(LRU_TILE, S)
    nt = S // T
    row = pl.BlockSpec((None, T, C), lambda b, t: (b, t, 0))
    full = lambda a: pl.BlockSpec(a.shape, lambda b, t: (0,) * a.ndim)
    return pl.pallas_call(
        _lru_kernel,
        grid=(B, nt),
        in_specs=[row, row, full(cw), full(cb), full(wg), full(bga), full(bgx), full(lam)],
        out_specs=[row, pl.BlockSpec((None, 1, C), lambda b, t: (b, 0, 0)),
                   pl.BlockSpec((None, CONV_W - 1, C), lambda b, t: (b, 0, 0))],
        out_shape=[jax.ShapeDtypeStruct((B, S, C), BF16), jax.ShapeDtypeStruct((B, 1, C), F32),
                   jax.ShapeDtypeStruct((B, CONV_W - 1, C), F32)],
        scratch_shapes=[pltpu.VMEM((T + SUBLANES, C), F32), pltpu.VMEM((T, C), F32),
                        pltpu.VMEM((T, C), F32), pltpu.VMEM((1, C), F32)],
        compiler_params=_cparams(("parallel", "arbitrary")),
        name="rglru_prompt",
    )(xr, gr, cw, cb, wg, bga, bgx, lam)


def _lru_step_kernel(xr_ref, gr_ref, conv_ref, h0_ref, cw_ref, cb_ref, wg_ref, bga_ref, bgx_ref, lam_ref,
                     y_ref, hnew_ref, cnew_ref):
    x = xr_ref[...]
    cw = cw_ref[...]
    xc = cb_ref[...]
    for j in range(CONV_W - 1):
        xc = xc + conv_ref[j] * cw[j:j + 1]
    xc = xc + x * cw[CONV_W - 1:CONV_W]
    for j in range(CONV_W - 2):
        cnew_ref[j] = conv_ref[j + 1]
    cnew_ref[CONV_W - 2] = x
    a, i, mult = _gates(xc, wg_ref, bga_ref[...], bgx_ref[...], lam_ref[...])
    h = a * h0_ref[...] + mult * i * xc
    hnew_ref[...] = h
    y_ref[...] = (h * jax.nn.gelu(gr_ref[...])).astype(BF16)


def _lru_step(xr, gr, conv, h0, cw, cb, wg, bga, bgx, lam):
    n, C = xr.shape
    args = (xr, gr, conv, h0, cw, cb, wg, bga, bgx, lam)
    full = lambda a: pl.BlockSpec(a.shape, lambda i: (0,) * a.ndim)
    out_shapes = [jax.ShapeDtypeStruct((n, C), BF16), jax.ShapeDtypeStruct((n, C), F32),
                  jax.ShapeDtypeStruct((CONV_W - 1, n, C), F32)]
    return pl.pallas_call(
        _lru_step_kernel,
        grid=(1,),
        in_specs=[full(a) for a in args],
        out_specs=[pl.BlockSpec(s.shape, lambda i, nd=len(s.shape): (0,) * nd) for s in out_shapes],
        out_shape=out_shapes,
        compiler_params=_cparams(("arbitrary",)),
        name="rglru_step",
    )(*args)


def _subln(o, g, scale):
    ms = jnp.mean(o * o, axis=-1, keepdims=True)
    return o * lax.rsqrt(ms + SUBLN_EPS) * g * scale


def _attn_kernel(lam_ref, qt_ref, kb_ref, vt_ref, g_ref, o_ref, qcat, m, acc, s_even, s_odd, *, out_scale,
                 between=None):
    qi = pl.program_id(1)
    Tq = qt_ref.shape[1]
    Tk = Tq

    qt = qt_ref[...]
    row = lax.broadcasted_iota(jnp.int32, qt.shape, 0)
    second = (row & HALF) != 0
    zero = jnp.zeros_like(qt)
    qcat[:, :Tq] = jnp.where(second, zero, qt)
    qcat[:, Tq:] = jnp.where(second, qt, zero)
    m[...] = jnp.full_like(m, NEG_INF)
    acc[...] = jnp.zeros_like(acc)

    def scores(kj, sbuf):
        k0 = pl.multiple_of(kj * Tk, Tk)
        sbuf[...] = jnp.dot(kb_ref[pl.ds(k0, Tk), :], qcat[...], preferred_element_type=F32)

    def update(kj, sbuf, diagonal):
        k0 = pl.multiple_of(kj * Tk, Tk)
        s = sbuf[...]
        if diagonal:
            key = lax.broadcasted_iota(jnp.int32, s.shape, 0)
            qry = lax.broadcasted_iota(jnp.int32, s.shape, 1)
            qry = jnp.where(qry >= Tq, qry - Tq, qry)
            s = jnp.where(key <= qry, s, NEG_INF)
        m_old = m[...]
        m_new = jnp.maximum(m_old, jnp.max(s, axis=0, keepdims=True))
        alpha = jnp.exp(m_old - m_new)
        p = jnp.exp(s - m_new)
        acc[...] = alpha * acc[...] + jnp.dot(vt_ref[:, pl.ds(k0, Tk)], p.astype(BF16),
                                              preferred_element_type=F32)
        m[...] = m_new

    def finish():
        a = acc[:V_DIM, :] / acc[V_DIM:V_DIM + 1, :]
        o = a[:, :Tq] - lam_ref[0, 0] * a[:, Tq:]
        ms = jnp.mean(o * o, axis=0, keepdims=True)
        o = o * lax.rsqrt(ms + SUBLN_EPS) * (g_ref[...] * out_scale)
        o_ref[...] = o.T.astype(o_ref.dtype)

    scores(0, s_even)
    if between is not None:
        between()

    def pair(i, c):
        t = 2 * i
        scores(t + 1, s_odd)
        update(t, s_even, False)
        scores(t + 2, s_even)
        update(t + 1, s_odd, False)
        return c

    lax.fori_loop(0, qi // 2, pair, 0)

    @pl.when(qi % 2 == 0)
    def _():
        update(qi, s_even, True)
        finish()

    @pl.when(qi % 2 == 1)
    def _():
        scores(qi, s_odd)
        update(qi - 1, s_even, False)
        update(qi, s_odd, True)
        finish()


def _attn_prompt(qt, kb, vt, subln_g, lam, out_scale):
    B, S, d_att = kb.shape
    G = d_att // V_DIM
    T = min(ATT_TILE, S)
    return pl.pallas_call(
        functools.partial(_attn_kernel, out_scale=out_scale),
        grid=(B * G, S // T),
        in_specs=[
            pl.BlockSpec(memory_space=pltpu.SMEM),
            pl.BlockSpec((None, V_DIM, T), lambda bg, i: (bg // G, bg % G, i)),
            pl.BlockSpec((None, S, V_DIM), lambda bg, i: (bg // G, 0, bg % G)),
            pl.BlockSpec((None, V_DIM + ONES_ROWS, S), lambda bg, i: (bg // G, bg % G, 0)),
            pl.BlockSpec((V_DIM, 1), lambda bg, i: (0, 0)),
        ],
        out_specs=pl.BlockSpec((None, T, V_DIM), lambda bg, i: (bg // G, i, bg % G)),
        out_shape=jax.ShapeDtypeStruct((B, S, d_att), BF16),
        scratch_shapes=[pltpu.VMEM((V_DIM, 2 * T), BF16), pltpu.VMEM((1, 2 * T), F32),
                        pltpu.VMEM((V_DIM + ONES_ROWS, 2 * T), F32),
                        pltpu.VMEM((T, 2 * T), F32), pltpu.VMEM((T, 2 * T), F32)],
        compiler_params=_cparams(("parallel", "arbitrary")),
        name="diff_attn_prompt",
    )(lam, qt, kb, vt, subln_g.reshape(V_DIM, 1))


def _attn_sample_body(seq, lam_ref, qt_ref, kt_ref, vc_ref, g_ref, *rest, n_pages, out_scale):
    k_refs = rest[:n_pages]
    v_refs = rest[n_pages:2 * n_pages]
    o_ref = rest[2 * n_pages]
    lam = lam_ref[0, 0]
    d_att, n_lanes = qt_ref.shape
    n_heads = d_att // HEAD_DIM
    G = d_att // V_DIM
    R = 2 * SUBLANES
    mine = lax.broadcasted_iota(jnp.int32, (d_att, n_lanes), 1) == seq % n_lanes
    qcol = jnp.sum(jnp.where(mine, qt_ref[...], 0.0), axis=1, keepdims=True)
    kcol = jnp.sum(jnp.where(mine, kt_ref[...], 0.0), axis=1, keepdims=True)
    P = k_refs[0].shape[1]
    qcolb = jnp.broadcast_to(qcol, (d_att, P))
    per_head = lambda x: jnp.sum(x.reshape(n_heads, HEAD_DIM, x.shape[-1]), axis=1)
    s = jnp.concatenate([per_head(k_refs[j][...] * qcolb) for j in range(n_pages)], axis=1)
    s_cur = per_head(qcol * kcol)
    m = jnp.maximum(jnp.max(s, axis=-1, keepdims=True), s_cur)
    p = jnp.exp(s - m)
    p_cur = jnp.exp(s_cur - m)
    inv_l = 1.0 / (jnp.sum(p, axis=-1, keepdims=True) + p_cur)
    p = p * inv_l
    p_cur = p_cur * inv_l
    amap = p - lam * pltpu.roll(p, n_heads - 1, 0)
    amap = jnp.concatenate([amap, jnp.zeros_like(amap)], axis=0).astype(BF16)
    p_cur = jnp.broadcast_to(p_cur, (n_heads, LANES))
    amap_cur = p_cur - lam * pltpu.roll(p_cur, n_heads - 1, 0)
    outs = []
    for g in range(G):
        acc = jnp.zeros((R, V_DIM), F32)
        for j in range(n_pages):
            vjg = v_refs[j][pl.ds(g, P, stride=G), :].astype(BF16)
            acc = acc + jnp.dot(amap[:, j * P:(j + 1) * P], vjg, preferred_element_type=F32)
        o = acc[2 * g:2 * g + 1, :] + amap_cur[2 * g:2 * g + 1, :] * vc_ref[:, g * V_DIM:(g + 1) * V_DIM]
        outs.append(_subln(o, g_ref[...], out_scale))
    o_ref[...] = jnp.concatenate(outs, axis=1).astype(o_ref.dtype)


def _attn_sample_kernel(pt_ref, *refs, n_pages, out_scale):
    _attn_sample_body(pl.program_id(0), *refs, n_pages=n_pages, out_scale=out_scale)


def _attn_sample(pt_flat, lam, qt_s, kt_s, v_cur, subln_g, ck, cv, layer, out_scale):
    d_att, n = qt_s.shape
    n_pages = pt_flat.shape[0] // n
    P = ck.shape[-1]
    nl = min(n, LANES)
    vc3 = v_cur.reshape(n, 1, d_att)
    cols = lambda: pl.BlockSpec((d_att, nl), lambda b, pt: (0, b // nl))
    k_specs = [pl.BlockSpec((None, None, d_att, P), functools.partial(
        lambda b, pt, j: (layer, pt[b * n_pages + j], 0, 0), j=j)) for j in range(n_pages)]
    v_specs = [pl.BlockSpec((None, None, cv.shape[2], V_DIM), functools.partial(
        lambda b, pt, j: (layer, pt[b * n_pages + j], 0, 0), j=j)) for j in range(n_pages)]
    grid_spec = pltpu.PrefetchScalarGridSpec(
        num_scalar_prefetch=1,
        grid=(n,),
        in_specs=[pl.BlockSpec(memory_space=pltpu.SMEM), cols(), cols(),
                  pl.BlockSpec((None, 1, d_att), lambda b, pt: (b, 0, 0)),
                  pl.BlockSpec((1, V_DIM), lambda b, pt: (0, 0))] + k_specs + v_specs,
        out_specs=pl.BlockSpec((None, 1, d_att), lambda b, pt: (b, 0, 0)),
    )
    out = pl.pallas_call(
        functools.partial(_attn_sample_kernel, n_pages=n_pages, out_scale=out_scale),
        grid_spec=grid_spec,
        out_shape=jax.ShapeDtypeStruct((n, 1, d_att), BF16),
        compiler_params=_cparams(("arbitrary",)),
        name="diff_attn_sample",
    )(pt_flat, lam, qt_s, kt_s, vc3, subln_g, *([ck] * n_pages), *([cv] * n_pages))
    return out.reshape(n, d_att)


def _attn_fused_kernel(pt_ref, lam_ref, qt_ref, kb_ref, vt_ref, gcol_ref, qts_ref, kts_ref, vc_ref, g_ref, *rest,
                       n_pages, out_scale):
    pages = rest[:2 * n_pages]
    o_ref, os_ref, qcat, m, acc, s_even, s_odd = rest[2 * n_pages:]
    seq = pl.program_id(0) * pl.num_programs(1) + pl.program_id(1)
    sample = functools.partial(_attn_sample_body, seq, lam_ref, qts_ref, kts_ref, vc_ref, g_ref, *pages, os_ref,
                               n_pages=n_pages, out_scale=out_scale)
    _attn_kernel(lam_ref, qt_ref, kb_ref, vt_ref, gcol_ref, o_ref, qcat, m, acc, s_even, s_odd,
                 out_scale=out_scale, between=sample)


def _attn_prompt_and_sample(qt, kb, vt, subln_g, lam, out_scale, pt_flat, qt_s, kt_s, v_cur, ck, cv, layer):
    B, S, d_att = kb.shape
    G = d_att // V_DIM
    T = min(ATT_TILE, S)
    nq = S // T
    n = qt_s.shape[1]
    if n != B * G * nq:
        return (_attn_prompt(qt, kb, vt, subln_g, lam, out_scale),
                _attn_sample(pt_flat, lam, qt_s, kt_s, v_cur, subln_g, ck, cv, layer, out_scale))
    n_pages = pt_flat.shape[0] // n
    P = ck.shape[-1]
    nl = min(n, LANES)
    seq = lambda bg, i: bg * nq + i
    cols = lambda: pl.BlockSpec((d_att, nl), lambda bg, i, pt: (0, seq(bg, i) // nl))
    k_specs = [pl.BlockSpec((None, None, d_att, P), functools.partial(
        lambda bg, i, pt, j: (layer, pt[seq(bg, i) * n_pages + j], 0, 0), j=j)) for j in range(n_pages)]
    v_specs = [pl.BlockSpec((None, None, cv.shape[2], V_DIM), functools.partial(
        lambda bg, i, pt, j: (layer, pt[seq(bg, i) * n_pages + j], 0, 0), j=j)) for j in range(n_pages)]
    grid_spec = pltpu.PrefetchScalarGridSpec(
        num_scalar_prefetch=1,
        grid=(B * G, nq),
        in_specs=[
            pl.BlockSpec(memory_space=pltpu.SMEM),
            pl.BlockSpec((None, V_DIM, T), lambda bg, i, pt: (bg // G, bg % G, i)),
            pl.BlockSpec((None, S, V_DIM), lambda bg, i, pt: (bg // G, 0, bg % G)),
            pl.BlockSpec((None, V_DIM + ONES_ROWS, S), lambda bg, i, pt: (bg // G, bg % G, 0)),
            pl.BlockSpec((V_DIM, 1), lambda bg, i, pt: (0, 0)),
            cols(), cols(),
            pl.BlockSpec((None, 1, d_att), lambda bg, i, pt: (seq(bg, i), 0, 0)),
            pl.BlockSpec((1, V_DIM), lambda bg, i, pt: (0, 0)),
        ] + k_specs + v_specs,
        out_specs=[pl.BlockSpec((None, T, V_DIM), lambda bg, i, pt: (bg // G, i, bg % G)),
                   pl.BlockSpec((None, 1, d_att), lambda bg, i, pt: (seq(bg, i), 0, 0))],
        scratch_shapes=[pltpu.VMEM((V_DIM, 2 * T), BF16), pltpu.VMEM((1, 2 * T), F32),
                        pltpu.VMEM((V_DIM + ONES_ROWS, 2 * T), F32),
                        pltpu.VMEM((T, 2 * T), F32), pltpu.VMEM((T, 2 * T), F32)],
    )
    att, att_s = pl.pallas_call(
        functools.partial(_attn_fused_kernel, n_pages=n_pages, out_scale=out_scale),
        grid_spec=grid_spec,
        out_shape=[jax.ShapeDtypeStruct((B, S, d_att), BF16), jax.ShapeDtypeStruct((n, 1, d_att), BF16)],
        compiler_params=_cparams(("parallel", "arbitrary")),
        name="diff_attn_fused",
    )(pt_flat, lam, qt, kb, vt, subln_g.reshape(V_DIM, 1),
      qt_s, kt_s, v_cur.reshape(n, 1, d_att), subln_g,
      *([ck] * n_pages), *([cv] * n_pages))
    return att, att_s.reshape(n, d_att)


def _outproj_kernel(yr_ref, at_ref, x_ref, wo_ref, g2_ref, wr_ref, br_ref, cnt_in_ref,
                    x1_ref, h2_ref, route_ref, route_t_ref, cnt_out_ref, cnt_scr, *, n_groups, per_group):
    step = pl.program_id(0)

    @pl.when(step == 0)
    def _():
        cnt_scr[...] = cnt_in_ref[...]

    d_rnn = yr_ref.shape[1]
    y = (jnp.dot(yr_ref[...], wo_ref[:d_rnn, :], preferred_element_type=F32)
         + jnp.dot(at_ref[...], wo_ref[d_rnn:, :], preferred_element_type=F32))
    x1 = x_ref[...] + y
    x1_ref[...] = x1
    ms = jnp.mean(x1 * x1, axis=-1, keepdims=True)
    h2 = x1 * lax.rsqrt(ms + NORM_EPS) * g2_ref[...]
    h2_ref[...] = h2
    logits = jnp.dot(h2.astype(BF16), wr_ref[...], preferred_element_type=F32) + br_ref[...]
    T = logits.shape[0]
    lane = lax.broadcasted_iota(jnp.int32, (T, LANES), 1).astype(F32)
    first = lambda hit: jnp.min(jnp.where(hit, lane, float(LANES)), axis=-1, keepdims=True)
    gl = jnp.where(lane < n_groups, logits, NEG_INF)
    gmax = jnp.max(gl, axis=-1, keepdims=True)
    gidx = first(gl == gmax)
    pg = 1.0 / jnp.sum(jnp.exp(gl - gmax), axis=-1, keepdims=True)
    lo = n_groups + per_group * gidx
    el = jnp.where(jnp.logical_and(lane >= lo, lane < lo + per_group), logits, NEG_INF)
    ma = jnp.max(el, axis=-1, keepdims=True)
    ia = first(el == ma)
    el2 = jnp.where(lane == ia, NEG_INF, el)
    mb = jnp.max(el2, axis=-1, keepdims=True)
    ib = first(el2 == mb)
    eb = jnp.exp(mb - ma)
    gate_a = pg / (1.0 + eb)
    gate_b = pg * eb / (1.0 + eb)
    hit_a = lane == ia
    hit_b = lane == ib
    onehot = jnp.where(jnp.logical_or(hit_a, hit_b), 1.0, 0.0)
    r_i = lax.broadcasted_iota(jnp.int32, (T, T), 0)
    c_i = lax.broadcasted_iota(jnp.int32, (T, T), 1)
    lower = jnp.where(c_i < r_i, 1.0, 0.0).astype(BF16)
    before = jnp.dot(lower, onehot.astype(BF16), preferred_element_type=F32) + cnt_scr[...]
    rank_a = jnp.sum(jnp.where(hit_a, before, 0.0), axis=-1, keepdims=True)
    rank_b = jnp.sum(jnp.where(hit_b, before, 0.0), axis=-1, keepdims=True)
    cnt_scr[...] = cnt_scr[...] + jnp.sum(onehot, axis=0, keepdims=True)
    cnt_out_ref[...] = cnt_scr[...]
    cols = (ia - n_groups, ib - n_groups, gate_a, gate_b, rank_a, rank_b)
    route = jnp.zeros((T, LANES), F32)
    for c, val in enumerate(cols):
        route = jnp.where(lane == float(c), val, route)
    route_ref[...] = route
    route_t_ref[...] = route.T[:SUBLANES]


def _outproj_route(yr, at, x, wo, g2, wr, br, cnt_in, n_groups, per_group):
    N, D = x.shape
    T = min(ROW_TILE, N)
    d_rnn = yr.shape[1]
    row = lambda w: pl.BlockSpec((T, w), lambda i: (i, 0))
    full = lambda a: pl.BlockSpec(a.shape, lambda i: (0,) * a.ndim)
    return pl.pallas_call(
        functools.partial(_outproj_kernel, n_groups=n_groups, per_group=per_group),
        grid=(N // T,),
        in_specs=[row(d_rnn), row(at.shape[1]), row(D), full(wo), full(g2), full(wr), full(br), full(cnt_in)],
        out_specs=[row(D), row(D), row(LANES), pl.BlockSpec((SUBLANES, T), lambda i: (0, i)),
                   pl.BlockSpec((1, LANES), lambda i: (0, 0))],
        out_shape=[jax.ShapeDtypeStruct((N, D), F32), jax.ShapeDtypeStruct((N, D), F32),
                   jax.ShapeDtypeStruct((N, LANES), F32), jax.ShapeDtypeStruct((SUBLANES, N), F32),
                   jax.ShapeDtypeStruct((1, LANES), F32)],
        scratch_shapes=[pltpu.VMEM((1, LANES), F32)],
        compiler_params=_cparams(("arbitrary",)),
        name="outproj_route",
    )(yr, at, x, wo, g2, wr, br, cnt_in)


def _dest_kernel(seg_ref, rt_ref, o_ref):
    e = rt_ref[0:TOP_K, :].astype(jnp.int32)
    start = jnp.zeros_like(e)
    for ex in range(seg_ref.shape[0]):
        start = jnp.where(e == ex, seg_ref[ex], start)
    o_ref[...] = start + rt_ref[4:4 + TOP_K, :].astype(jnp.int32)


def _dest(seg_start, route_t):
    R, N = route_t.shape
    T = min(ROW_TILE, N)
    grid_spec = pltpu.PrefetchScalarGridSpec(
        num_scalar_prefetch=1,
        grid=(N // T,),
        in_specs=[pl.BlockSpec((R, T), lambda i, seg: (0, i))],
        out_specs=pl.BlockSpec((TOP_K, T), lambda i, seg: (0, i)),
    )
    return pl.pallas_call(
        _dest_kernel,
        grid_spec=grid_spec,
        out_shape=jax.ShapeDtypeStruct((TOP_K, N), jnp.int32),
        compiler_params=_cparams(("arbitrary",)),
        name="moe_dest",
    )(seg_start, route_t)


def _dispatch_kernel(zb_ref, dest_ref, h_ref, *rest, zero_fill):
    xb_hbm, zbuf, sem = rest[-3], rest[-2], rest[-1]
    T = h_ref.shape[0]

    if zero_fill:
        @pl.when(pl.program_id(0) == 0)
        def _():
            zbuf[...] = jnp.zeros_like(zbuf)

            def zcopy(i):
                row0 = pl.multiple_of(zb_ref[i] * EXP_BLOCK, EXP_BLOCK)
                return pltpu.make_async_copy(zbuf, xb_hbm.at[pl.ds(row0, EXP_BLOCK)], sem)

            def zstart(i, c):
                @pl.when(zb_ref[i] >= 0)
                def _():
                    zcopy(i).start()
                return c

            def zwait(i, c):
                @pl.when(zb_ref[i] >= 0)
                def _():
                    zcopy(i).wait()
                return c

            lax.fori_loop(0, zb_ref.shape[0], zstart, 0)
            lax.fori_loop(0, zb_ref.shape[0], zwait, 0)

    def body(k, c):
        r0 = pl.multiple_of(k * SUBLANES, SUBLANES)
        rows = h_ref.at[pl.ds(r0, SUBLANES)]
        for j in range(SUBLANES):
            for s in range(TOP_K):
                pltpu.make_async_copy(rows.at[pl.ds(j, 1)],
                                      xb_hbm.at[pl.ds(dest_ref[s, r0 + j], 1)], sem).start(priority=s)
        return c

    lax.fori_loop(0, T // SUBLANES, body, 0, unroll=2)
    for s in range(TOP_K):
        pltpu.make_async_copy(h_ref, xb_hbm.at[pl.ds(0, T)], sem).wait()


def _dispatch(zb, dest, h, xb, xb_rows):
    N, D = h.shape
    T = min(DISPATCH_TILE, N)
    dest3 = jnp.transpose(dest.reshape(TOP_K, N // T, T), (1, 0, 2))
    aliased = xb is not None
    in_specs = [pl.BlockSpec((None, TOP_K, T), lambda i, zb: (i, 0, 0), memory_space=pltpu.SMEM),
                pl.BlockSpec((T, D), lambda i, zb: (i, 0))]
    args = [zb, dest3, h]
    if aliased:
        in_specs.append(pl.BlockSpec(memory_space=pl.ANY))
        args.append(xb)
    grid_spec = pltpu.PrefetchScalarGridSpec(
        num_scalar_prefetch=1,
        grid=(N // T,),
        in_specs=in_specs,
        out_specs=pl.BlockSpec(memory_space=pl.ANY),
        scratch_shapes=[pltpu.VMEM((EXP_BLOCK, D), F32), pltpu.SemaphoreType.DMA(())],
    )
    return pl.pallas_call(
        functools.partial(_dispatch_kernel, zero_fill=not aliased),
        grid_spec=grid_spec,
        out_shape=jax.ShapeDtypeStruct((xb_rows, D), F32),
        input_output_aliases={3: 0} if aliased else {},
        compiler_params=pltpu.CompilerParams(dimension_semantics=("arbitrary",), has_side_effects=True),
        name="moe_dispatch",
    )(*args)


def _expert_kernel(be_ref, bi_ref, br_ref, bf_ref, x_ref, w1_ref, w2_ref, y_ref, w1b, w2b):
    b = pl.program_id(0)
    rows = br_ref[b]

    @pl.when(bf_ref[b] == 1)
    def _():
        w1b[...] = w1_ref[...].astype(BF16)
        w2b[...] = w2_ref[...].astype(BF16)

    @pl.when(rows > 0)
    def _():
        h = jnp.dot(x_ref[...].astype(BF16), w1b[...], preferred_element_type=F32)
        de = h.shape[1] // 2
        act = (jax.nn.silu(h[:, :de]) * h[:, de:]).astype(BF16)
        y_ref[...] = jnp.dot(act, w2b[...], preferred_element_type=F32)

    @pl.when(rows == 0)
    def _():
        y_ref[...] = jnp.zeros_like(y_ref)


def _experts(be, bi, br, bf, xb, w1, w2, layer):
    R, D = xb.shape
    nb = be.shape[0]
    d2 = w1.shape[-1]
    de = w2.shape[-2]
    grid_spec = pltpu.PrefetchScalarGridSpec(
        num_scalar_prefetch=4,
        grid=(nb,),
        in_specs=[pl.BlockSpec((EXP_BLOCK, D), lambda b, be, bi, br, bf: (bi[b], 0)),
                  pl.BlockSpec((None, None, D, d2), lambda b, be, bi, br, bf: (layer, be[b], 0, 0)),
                  pl.BlockSpec((None, None, de, D), lambda b, be, bi, br, bf: (layer, be[b], 0, 0))],
        out_specs=pl.BlockSpec((EXP_BLOCK, D), lambda b, be, bi, br, bf: (b, 0)),
        scratch_shapes=[pltpu.VMEM((D, d2), BF16), pltpu.VMEM((de, D), BF16)],
    )
    return pl.pallas_call(
        _expert_kernel,
        grid_spec=grid_spec,
        out_shape=jax.ShapeDtypeStruct((R, D), F32),
        compiler_params=_cparams(("arbitrary",)),
        name="moe_experts",
    )(be, bi, br, bf, xb, w1, w2)


def _combine_kernel(dest_ref, dest_next_ref, x1_ref, route_ref, gf_ref, yb_hbm, o_ref, ybuf, sem, *, final):
    T = x1_ref.shape[0]
    i = pl.program_id(0)
    slot = i % 2

    def issue(dref, sl):
        def body(k, c):
            r0 = pl.multiple_of(k * SUBLANES, SUBLANES)
            for s in range(TOP_K):
                rows = ybuf.at[sl, s, pl.ds(r0, SUBLANES)]
                for j in range(SUBLANES):
                    pltpu.make_async_copy(yb_hbm.at[pl.ds(dref[s, r0 + j], 1)],
                                          rows.at[pl.ds(j, 1)], sem.at[sl]).start(priority=s)
            return c

        lax.fori_loop(0, T // SUBLANES, body, 0, unroll=2)

    @pl.when(i == 0)
    def _():
        issue(dest_ref, 0)

    @pl.when(i + 1 < pl.num_programs(0))
    def _():
        issue(dest_next_ref, 1 - slot)

    for s in range(TOP_K):
        pltpu.make_async_copy(yb_hbm.at[pl.ds(0, T)], ybuf.at[slot, s], sem.at[slot]).wait()
    route = route_ref[...]
    out = x1_ref[...] + ybuf[slot, 0] * route[:, 2:3] + ybuf[slot, 1] * route[:, 3:4]
    if final:
        ms = jnp.mean(out * out, axis=-1, keepdims=True)
        out = out * lax.rsqrt(ms + NORM_EPS) * gf_ref[...]
    o_ref[...] = out


def _combine(dest, x1, route, gf, yb, final):
    N, D = x1.shape
    T = min(COMBINE_TILE, N)
    dest3 = jnp.transpose(dest.reshape(TOP_K, N // T, T), (1, 0, 2))
    last = N // T - 1
    return pl.pallas_call(
        functools.partial(_combine_kernel, final=final),
        grid=(N // T,),
        in_specs=[pl.BlockSpec((None, TOP_K, T), lambda i: (i, 0, 0), memory_space=pltpu.SMEM),
                  pl.BlockSpec((None, TOP_K, T), lambda i: (jnp.minimum(i + 1, last), 0, 0),
                               memory_space=pltpu.SMEM),
                  pl.BlockSpec((T, D), lambda i: (i, 0)),
                  pl.BlockSpec((T, LANES), lambda i: (i, 0)),
                  pl.BlockSpec((1, D), lambda i: (0, 0)),
                  pl.BlockSpec(memory_space=pl.ANY)],
        out_specs=pl.BlockSpec((T, D), lambda i: (i, 0)),
        out_shape=jax.ShapeDtypeStruct((N, D), F32),
        scratch_shapes=[pltpu.VMEM((2, TOP_K, T, D), F32), pltpu.SemaphoreType.DMA((2,))],
        compiler_params=_cparams(("arbitrary",)),
        name="moe_combine",
    )(dest3, dest3, x1, route, gf, yb)


def _rope_tables(pos):
    inv = ROPE_THETA ** (-jnp.arange(HALF, dtype=F32) / HALF)
    ang = pos.astype(F32)[:, None] * inv[None, :]
    cos, sin = jnp.cos(ang), jnp.sin(ang)
    cq = jnp.tile(cos, (1, LANES // HALF))
    sq = jnp.concatenate([-sin, -sin, sin, sin], axis=1)
    return cq, sq, cos.T, sin.T


def _q_perm(d_att):
    idx = []
    for g in range(d_att // V_DIM):
        for part in range(2):
            for jj in range(2):
                base = HEAD_DIM * (2 * g + jj) + HALF * part
                idx.extend(range(base, base + HALF))
    return jnp.asarray(idx, jnp.int32)


def _gate_weights(w_ga, w_gx):
    nb, bw, _ = w_ga.shape
    half = nb // 2
    eye = jnp.eye(half, dtype=w_ga.dtype)

    def dense(w):
        return jnp.einsum('nij,nm->nimj', w, eye).reshape(half * bw, half * bw)

    return jnp.stack([jnp.concatenate([dense(w_ga[h * half:(h + 1) * half]),
                                       dense(w_gx[h * half:(h + 1) * half])], axis=1)
                      for h in range(2)]).astype(BF16)


def _block_tables(counts, nb):
    ne = counts.shape[0]
    nblk = (counts + EXP_BLOCK - 1) // EXP_BLOCK
    cum = jnp.sum(jnp.where(jnp.arange(ne)[None, :] <= jnp.arange(ne)[:, None], nblk[None, :], 0), axis=1)
    total = cum[-1]
    b = jnp.arange(nb, dtype=jnp.int32)
    bc = jnp.minimum(b, total - 1)
    e = jnp.minimum(jnp.sum(bc[:, None] >= cum[None, :], axis=1), ne - 1).astype(jnp.int32)
    pick = lambda tab: jnp.sum(jnp.where(e[:, None] == jnp.arange(ne)[None, :], tab[None, :], 0), axis=1)
    off = bc - (pick(cum) - pick(nblk))
    valid = b < total
    rows = jnp.where(valid, jnp.clip(pick(counts) - off * EXP_BLOCK, 0, EXP_BLOCK), 0)
    first = jnp.where(jnp.logical_and(valid, off == 0), 1, 0)
    last_blk = jnp.where(nblk > 0, cum - 1, -1)
    trail = total + jnp.arange(ne, dtype=jnp.int32)
    trail = jnp.where(trail < nb, trail, -1)
    zb = jnp.concatenate([last_blk, trail]).astype(jnp.int32)
    seg_start = ((cum - nblk) * EXP_BLOCK).astype(jnp.int32)
    return seg_start, e, bc.astype(jnp.int32), rows.astype(jnp.int32), first.astype(jnp.int32), zb


def kernel(x_prompt, x_sample, cache_k, cache_v, state_h, state_conv, page_table, ln1_g, w_in, conv_w, conv_b,
           w_ga, b_ga, w_gx, b_gx, lru_lambda, lam_q1, lam_k1, lam_q2, lam_k2, subln_g, w_out, ln2_g,
           w_route_group, b_route_group, w_route_expert, b_route_expert, w_exp_in, w_exp_out, ln_f):
    Bp, Sp, D = x_prompt.shape
    Bs = x_sample.shape[0]
    depth = w_in.shape[0]
    d_rnn = state_h.shape[-1]
    d_att = cache_k.shape[-2] * cache_k.shape[-1]
    n_groups = w_route_group.shape[-1]
    n_experts = w_route_expert.shape[-1]
    per_group = n_experts // n_groups
    n_pool, page = cache_k.shape[1], cache_k.shape[2]
    n_pages = page_table.shape[1]
    past_len = n_pages * page
    Np = Bp * Sp
    assert x_sample.shape[1] == 1 and n_groups + n_experts <= LANES

    cq, sq, ct, st = _rope_tables(jnp.arange(Sp))
    _, _, ct_s, st_s = _rope_tables(jnp.full((Bs,), past_len))
    perm = _q_perm(d_att)
    qscale = HEAD_DIM ** -0.5
    ck = jnp.transpose(cache_k, (0, 1, 3, 4, 2)).reshape(depth, n_pool, d_att, page)
    cv = cache_v.reshape(depth, n_pool, page * (d_att // V_DIM), V_DIM)
    pt_flat = page_table.reshape(-1).astype(jnp.int32)
    conv_s = jnp.transpose(state_conv, (0, 2, 1, 3))

    n_blocks = (TOP_K * (Np + Bs)) // EXP_BLOCK + n_experts

    xp = x_prompt
    xs = x_sample.reshape(Bs, D)
    kt_all = v_all = None
    hp_l, cp_l, ks_l, vs_l, hs_l, cs_l = [], [], [], [], [], []
    for l in range(depth):
        lambda_init = 0.8 - 0.6 * math.exp(-0.3 * l)
        out_scale = 1.0 - lambda_init
        lam = (jnp.exp(jnp.sum(lam_q1[l] * lam_k1[l])) - jnp.exp(jnp.sum(lam_q2[l] * lam_k2[l]))
               + lambda_init).reshape(1, 1).astype(F32)
        wl = w_in[l]
        w_xg = wl[:, :2 * d_rnn]
        w_q = wl[:, 2 * d_rnn:2 * d_rnn + d_att] * qscale
        w_k = wl[:, 2 * d_rnn + d_att:2 * d_rnn + 2 * d_att]
        w_v = wl[:, 2 * d_rnn + 2 * d_att:]
        wm_p = jnp.concatenate([w_xg, w_k[:, perm], w_v], axis=1).astype(BF16)
        wt_p = jnp.concatenate([w_q[:, perm], w_k, w_v], axis=1).T.astype(BF16)
        wm_s = jnp.concatenate([w_xg, w_v], axis=1).astype(BF16)
        wt_s = jnp.concatenate([w_k, w_q], axis=1).T.astype(BF16)
        g1 = ln1_g[l].reshape(1, D)
        wg = _gate_weights(w_ga[l], w_gx[l])
        bga = b_ga[l].reshape(1, d_rnn)
        bgx = b_gx[l].reshape(1, d_rnn)
        lru_l = lru_lambda[l].reshape(1, d_rnn)
        cw = conv_w[l]
        cb = conv_b[l].reshape(1, d_rnn)
        sg = subln_g[l].reshape(1, V_DIM)
        wo = w_out[l].astype(BF16)
        g2 = ln2_g[l].reshape(1, D)
        wr = jnp.zeros((D, LANES), F32).at[:, :n_groups].set(w_route_group[l]).at[
            :, n_groups:n_groups + n_experts].set(w_route_expert[l])
        wr = wr.astype(BF16)
        br =jnp.zeros((1, LANES), F32).at[0, :n_groups].set(b_route_group[l]).at[
            0, n_groups:n_groups + n_experts].set(b_route_expert[l])

        xr, gr, kb, v_all, qt, kt_all, vt = _inproj_prompt(xp, g1, wm_p, wt_p, cq, sq, ct, st, kt_all, v_all)
        y_rnn, h_last, c_tail = _lru_prompt(xr, gr, cw, cb, wg, bga, bgx, lru_l)
        xr_s, gr_s, v_s, kt_s, qt_s = _inproj_sample(xs, g1, wm_s, wt_s, ct_s, st_s)
        y_rnn_s, h_s, c_s = _lru_step(xr_s, gr_s, conv_s[l], state_h[l], cw, cb, wg, bga, bgx, lru_l)
        k_s = kt_s.T
        att, att_s = _attn_prompt_and_sample(qt, kb, vt, sg, lam, out_scale,
                                             pt_flat, qt_s, kt_s, v_s, ck, cv, l)
        cnt0 = jnp.zeros((1, LANES), F32)
        x1, h2, route, route_t, cnt1 = _outproj_route(
            y_rnn.reshape(Np, d_rnn), att.reshape(Np, d_att), xp.reshape(Np, D), wo, g2, wr, br, cnt0,
            n_groups, per_group)
        x1_s, h2_s, route_s, route_ts, cnt2 = _outproj_route(y_rnn_s, att_s, xs, wo, g2, wr, br, cnt1,
                                                             n_groups, per_group)
        counts = cnt2[0, n_groups:n_groups + n_experts].astype(jnp.int32)
        seg_start, be, bi, brows, bfirst, zb = _block_tables(counts, n_blocks)

        dest_p, dest_s = _dest(seg_start, route_t), _dest(seg_start, route_ts)
        xb = _dispatch(zb, dest_p, h2, None, n_blocks * EXP_BLOCK)
        xb = _dispatch(zb, dest_s, h2_s, xb, n_blocks * EXP_BLOCK)
        yb = _experts(be, bi, brows, bfirst, xb, w_exp_in, w_exp_out, l)
        final = l == depth - 1
        gf = ln_f.reshape(1, D)
        xp_new = _combine(dest_p, x1, route, gf, yb, final)
        xs = _combine(dest_s, x1_s, route_s, gf, yb, final)
        xp = xp_new.reshape(Bp, Sp, D)

        n_qk = d_att // HEAD_DIM
        hp_l.append(h_last.reshape(Bp, d_rnn))
        cp_l.append(c_tail)
        ks_l.append(k_s.reshape(Bs, 1, n_qk, HEAD_DIM))
        vs_l.append(v_s.reshape(Bs, 1, d_att // V_DIM, V_DIM))
        hs_l.append(h_s)
        cs_l.append(jnp.transpose(c_s, (1, 0, 2)))
    k_prompt = jnp.transpose(kt_all.reshape(depth, Bp, n_qk, HEAD_DIM, Sp), (0, 1, 4, 2, 3))
    v_prompt = v_all.reshape(depth, Bp, Sp, d_att // V_DIM, V_DIM)
    return (xp, xs.reshape(Bs, 1, D),
            k_prompt, v_prompt, jnp.stack(hp_l), jnp.stack(cp_l),
            jnp.stack(ks_l), jnp.stack(vs_l), jnp.stack(hs_l), jnp.stack(cs_l))
```

```python
import functools
import math

import jax
import jax.numpy as jnp
from jax import lax
from jax.experimental import pallas as pl
from jax.experimental.pallas import tpu as pltpu

F32 = jnp.float32
BF16 = jnp.bfloat16

HEAD_DIM = 64
V_DIM = 2 * HEAD_DIM
HALF = HEAD_DIM // 2
ONES_ROWS = 16
CONV_W = 4
LRU_C = 8.0
ROPE_THETA = 10000.0
NORM_EPS = 1e-6
SUBLN_EPS = 1e-5
NEG_INF = -1e30
TOP_K = 2

LANES = 128
SUBLANES = 8
VMEM_LIMIT = 48 * 1024 * 1024
INPROJ_VMEM_LIMIT = 56 * 1024 * 1024

ROW_TILE = 512
LRU_TILE = 256
ATT_TILE = 512
EXP_BLOCK = 512
DISPATCH_TILE = 512
COMBINE_TILE = 256


def _cparams(sem, vmem_limit=VMEM_LIMIT):
    return pltpu.CompilerParams(dimension_semantics=sem, vmem_limit_bytes=vmem_limit)


def _inproj_kernel(x_ref, g_ref, wm_ref, wt_ref, cq_ref, sq_ref, ct_ref, st_ref, *rest, sample, n_prev=0):
    prev, outs = rest[:2 if n_prev else 0], rest[2 if n_prev else 0:]
    x = x_ref[...]
    ms = jnp.mean(x * x, axis=-1, keepdims=True)
    h = (x * lax.rsqrt(ms + NORM_EPS) * g_ref[...]).astype(BF16)
    main = jnp.dot(h, wm_ref[...], preferred_element_type=F32)
    d_rnn = outs[0].shape[1]
    ct = ct_ref[...]
    st = st_ref[...]
    tr = lax.dot_general(wt_ref[...], h, (((1,), (1,)), ((), ())), preferred_element_type=F32)

    def rope_t(blk):
        x1 = blk[:HALF]
        x2 = blk[HALF:]
        return x1 * ct - x2 * st, x2 * ct + x1 * st

    if sample:
        xr_ref, gr_ref, v_ref, kt_ref, qt_ref = outs
        xr_ref[...] = main[:, :d_rnn]
        gr_ref[...] = main[:, d_rnn:2 * d_rnn]
        v_ref[...] = main[:, 2 * d_rnn:]
        n_heads = kt_ref.shape[0] // HEAD_DIM
        for hh in range(n_heads):
            o1, o2 = rope_t(tr[hh * HEAD_DIM:(hh + 1) * HEAD_DIM])
            kt_ref[hh * HEAD_DIM:hh * HEAD_DIM + HALF, :] = o1
            kt_ref[hh * HEAD_DIM + HALF:(hh + 1) * HEAD_DIM, :] = o2
            base = (n_heads + hh) * HEAD_DIM
            p1, p2 = rope_t(tr[base:base + HEAD_DIM])
            qt_ref[hh * HEAD_DIM:hh * HEAD_DIM + HALF, :] = p1
            qt_ref[hh * HEAD_DIM + HALF:(hh + 1) * HEAD_DIM, :] = p2
    else:
        xr_ref, gr_ref, kb_ref, v_ref, qt_ref, kt_ref, vt_ref = outs
        xr_ref[...] = main[:, :d_rnn]
        gr_ref[...] = main[:, d_rnn:2 * d_rnn]
        d_att = kb_ref.shape[1]
        k = main[:, 2 * d_rnn:2 * d_rnn + d_att]
        cq = cq_ref[...]
        sq = sq_ref[...]
        for c in range(d_att // LANES):
            kc = k[:, c * LANES:(c + 1) * LANES]
            kb_ref[:, c * LANES:(c + 1) * LANES] = (kc * cq + pltpu.roll(kc, HEAD_DIM, 1) * sq).astype(BF16)
        if n_prev:
            kt_ref[:n_prev] = prev[0][...]
            v_ref[:n_prev] = prev[1][...]
        v = main[:, 2 * d_rnn + d_att:]
        T = v.shape[0]
        G = d_att // V_DIM
        for g in range(G):
            v_ref[n_prev, pl.ds(g, T, stride=G), :] = v[:, g * V_DIM:(g + 1) * V_DIM]
        n_heads = d_att // HEAD_DIM
        for hh in range(n_heads):
            g, jj = divmod(hh, 2)
            r1 = g * V_DIM + jj * HALF
            x1 = tr[r1:r1 + HALF]
            x2 = tr[r1 + HEAD_DIM:r1 + HEAD_DIM + HALF]
            qt_ref[r1:r1 + HALF, :] = (x1 * ct - x2 * st).astype(BF16)
            qt_ref[r1 + HEAD_DIM:r1 + HEAD_DIM + HALF, :] = (x2 * ct + x1 * st).astype(BF16)
            base = d_att + hh * HEAD_DIM
            o1, o2 = rope_t(tr[base:base + HEAD_DIM])
            kt_ref[n_prev, hh * HEAD_DIM:hh * HEAD_DIM + HALF, :] = o1
            kt_ref[n_prev, hh * HEAD_DIM + HALF:(hh + 1) * HEAD_DIM, :] = o2
        for g in range(G):
            r0 = g * (V_DIM + ONES_ROWS)
            vt_ref[r0:r0 + V_DIM, :] = tr[2 * d_att + g * V_DIM:2 * d_att + (g + 1) * V_DIM].astype(BF16)
            vt_ref[r0 + V_DIM:r0 + V_DIM + ONES_ROWS, :] = jnp.ones((ONES_ROWS, T), BF16)


def _inproj_prompt(x, g, wm, wt, cq, sq, ct, st, k_prev, v_prev):
    B, S, D = x.shape
    T = min(ROW_TILE, S)
    nt = S // T
    d_att = wt.shape[0] // 3
    d_rnn = (wm.shape[1] - 2 * d_att) // 2
    G = d_att // V_DIM
    n_prev = 0 if k_prev is None else k_prev.shape[0]
    row = lambda w: pl.BlockSpec((None, T, w), lambda b, t: (b, t, 0))
    full = lambda a: pl.BlockSpec(a.shape, lambda b, t: (0,) * a.ndim)
    once = lambda a: pl.BlockSpec(a.shape, lambda b, t: (0,) * a.ndim, pipeline_mode=pl.Buffered(1))
    tr_spec = pl.BlockSpec((None, d_att, T), lambda b, t: (b, 0, t))
    kstack = lambda n: pl.BlockSpec((n, None, d_att, T), lambda b, t: (0, b, 0, t))
    vstack = lambda n: pl.BlockSpec((n, None, T * G, V_DIM), lambda b, t: (0, b, t, 0))
    prev_specs = [kstack(n_prev), vstack(n_prev)] if n_prev else []
    prev_args = [k_prev, v_prev] if n_prev else []
    return pl.pallas_call(
        functools.partial(_inproj_kernel, sample=False, n_prev=n_prev),
        grid=(B, nt),
        in_specs=[row(D), full(g), once(wm), once(wt),
                  pl.BlockSpec((T, LANES), lambda b, t: (t, 0)),
                  pl.BlockSpec((T, LANES), lambda b, t: (t, 0)),
                  pl.BlockSpec((HALF, T), lambda b, t: (0, t)),
                  pl.BlockSpec((HALF, T), lambda b, t: (0, t))] + prev_specs,
        out_specs=[row(d_rnn), row(d_rnn), row(d_att), vstack(n_prev + 1), tr_spec, kstack(n_prev + 1),
                   pl.BlockSpec((None, G * (V_DIM + ONES_ROWS), T), lambda b, t: (b, 0, t))],
        out_shape=[jax.ShapeDtypeStruct((B, S, d_rnn), F32), jax.ShapeDtypeStruct((B, S, d_rnn), F32),
                   jax.ShapeDtypeStruct((B, S, d_att), BF16),
                   jax.ShapeDtypeStruct((n_prev + 1, B, S * G, V_DIM), F32),
                   jax.ShapeDtypeStruct((B, d_att, S), BF16),
                   jax.ShapeDtypeStruct((n_prev + 1, B, d_att, S), F32),
                   jax.ShapeDtypeStruct((B, G * (V_DIM + ONES_ROWS), S), BF16)],
        compiler_params=_cparams(("parallel", "arbitrary"), INPROJ_VMEM_LIMIT),
        name="inproj_prompt",
    )(x, g, wm, wt, cq, sq, ct, st, *prev_args)


def _inproj_sample(x, g, wm, wt, ct, st):
    n, D = x.shape
    d_att = wt.shape[0] // 2
    d_rnn = (wm.shape[1] - d_att) // 2
    full = lambda a: pl.BlockSpec(a.shape, lambda i: (0,) * a.ndim)
    dummy = jnp.zeros((SUBLANES, LANES), F32)
    out_shapes = [jax.ShapeDtypeStruct((n, d_rnn), F32), jax.ShapeDtypeStruct((n, d_rnn), F32),
                  jax.ShapeDtypeStruct((n, d_att), F32), jax.ShapeDtypeStruct((d_att, n), F32),
                  jax.ShapeDtypeStruct((d_att, n), F32)]
    return pl.pallas_call(
        functools.partial(_inproj_kernel, sample=True),
        grid=(1,),
        in_specs=[full(x), full(g), full(wm), full(wt), full(dummy), full(dummy), full(ct), full(st)],
        out_specs=[pl.BlockSpec(s.shape, lambda i: (0, 0)) for s in out_shapes],
        out_shape=out_shapes,
        compiler_params=_cparams(("arbitrary",)),
        name="inproj_sample",
    )(x, g, wm, wt, dummy, dummy, ct, st)


def _gates(xc, wg_ref, bga, bgx, lam):
    half = xc.shape[1] // 2
    xcb = xc.astype(BF16)
    g0 = jnp.dot(xcb[:, :half], wg_ref[0], preferred_element_type=F32)
    g1 = jnp.dot(xcb[:, half:], wg_ref[1], preferred_element_type=F32)
    r = jax.nn.sigmoid(jnp.concatenate([g0[:, :half], g1[:, :half]], axis=1) + bga)
    i = jax.nn.sigmoid(jnp.concatenate([g0[:, half:], g1[:, half:]], axis=1) + bgx)
    neg = -lam
    softplus = jnp.maximum(neg, 0.0) + jnp.log1p(jnp.exp(-jnp.abs(neg)))
    log_a = -LRU_C * r * softplus
    a = jnp.exp(log_a)
    mult = jnp.sqrt(1.0 - a * a)
    return a, i, mult


def _lru_kernel(xr_ref, gr_ref, cw_ref, cb_ref, wg_ref, bga_ref, bgx_ref, lam_ref,
                y_ref, hlast_ref, ctail_ref, xp_scr, a_scr, b_scr, h_scr):
    t = pl.program_id(1)
    T, C = xr_ref.shape

    @pl.when(t == 0)
    def _():
        xp_scr[0:SUBLANES, :] = jnp.zeros((SUBLANES, C), F32)
        h_scr[...] = jnp.zeros_like(h_scr)

    x = xr_ref[...]
    xp_scr[SUBLANES:SUBLANES + T, :] = x
    cw = cw_ref[...]
    xc = cb_ref[...]
    for j in range(CONV_W - 1):
        xc = xc + xp_scr[pl.ds(SUBLANES - (CONV_W - 1) + j, T), :] * cw[j:j + 1]
    xc = xc + x * cw[CONV_W - 1:CONV_W]
    tail = x[T - (CONV_W - 1):, :]
    xp_scr[SUBLANES - (CONV_W - 1):SUBLANES, :] = tail
    ctail_ref[...] = tail

    a, i, mult = _gates(xc, wg_ref, bga_ref[...], bgx_ref[...], lam_ref[...])
    row = lax.broadcasted_iota(jnp.int32, (T, 1), 0)
    mult = jnp.where(jnp.logical_and(row == 0, t == 0), 1.0, mult)
    a_scr[...] = a
    b_scr[...] = mult * i * xc

    rowi = lax.broadcasted_iota(jnp.int32, (SUBLANES, C), 0)

    def body(gi, h):
        off = pl.multiple_of(gi * SUBLANES, SUBLANES)
        A = a_scr[pl.ds(off, SUBLANES), :]
        Bv = b_scr[pl.ds(off, SUBLANES), :]
        for d in (1, 2, 4):
            keep = rowi >= d
            Bv = jnp.where(keep, Bv + A * pltpu.roll(Bv, d, 0), Bv)
            A = jnp.where(keep, A * pltpu.roll(A, d, 0), A)
        hrows = Bv + A * h
        b_scr[pl.ds(off, SUBLANES), :] = hrows
        return hrows[SUBLANES - 1:SUBLANES, :]

    h = lax.fori_loop(0, T // SUBLANES, body, h_scr[...], unroll=2)
    h_scr[...] = h
    hlast_ref[...] = h
    y_ref[...] = (b_scr[...] * jax.nn.gelu(gr_ref[...])).astype(BF16)


def _lru_prompt(xr, gr, cw, cb, wg, bga, bgx, lam):
    B, S, C = xr.shape
    T = min(LRU_TILE, S)
    nt = S // T
    row = pl.BlockSpec((None, T, C), lambda b, t: (b, t, 0))
    full = lambda a: pl.BlockSpec(a.shape, lambda b, t: (0,) * a.ndim)
    return pl.pallas_call(
        _lru_kernel,
        grid=(B, nt),
        in_specs=[row, row, full(cw), full(cb), full(wg), full(bga), full(bgx), full(lam)],
        out_specs=[row, pl.BlockSpec((None, 1, C), lambda b, t: (b, 0, 0)),
                   pl.BlockSpec((None, CONV_W - 1, C), lambda b, t: (b, 0, 0))],
        out_shape=[jax.ShapeDtypeStruct((B, S, C), BF16), jax.ShapeDtypeStruct((B, 1, C), F32),
                   jax.ShapeDtypeStruct((B, CONV_W - 1, C), F32)],
        scratch_shapes=[pltpu.VMEM((T + SUBLANES, C), F32), pltpu.VMEM((T, C), F32),
                        pltpu.VMEM((T, C), F32), pltpu.VMEM((1, C), F32)],
        compiler_params=_cparams(("parallel", "arbitrary")),
        name="rglru_prompt",
    )(xr, gr, cw, cb, wg, bga, bgx, lam)


def _lru_step_kernel(xr_ref, gr_ref, conv_ref, h0_ref, cw_ref, cb_ref, wg_ref, bga_ref, bgx_ref, lam_ref,
                     y_ref, hnew_ref, cnew_ref):
    x = xr_ref[...]
    cw = cw_ref[...]
    xc = cb_ref[...]
    for j in range(CONV_W - 1):
        xc = xc + conv_ref[j] * cw[j:j + 1]
    xc = xc + x * cw[CONV_W - 1:CONV_W]
    for j in range(CONV_W - 2):
        cnew_ref[j] = conv_ref[j + 1]
    cnew_ref[CONV_W - 2] = x
    a, i, mult = _gates(xc, wg_ref, bga_ref[...], bgx_ref[...], lam_ref[...])
    h = a * h0_ref[...] + mult * i * xc
    hnew_ref[...] = h
    y_ref[...] = (h * jax.nn.gelu(gr_ref[...])).astype(BF16)


def _lru_step(xr, gr, conv, h0, cw, cb, wg, bga, bgx, lam):
    n, C = xr.shape
    args = (xr, gr, conv, h0, cw, cb, wg, bga, bgx, lam)
    full = lambda a: pl.BlockSpec(a.shape, lambda i: (0,) * a.ndim)
    out_shapes = [jax.ShapeDtypeStruct((n, C), BF16), jax.ShapeDtypeStruct((n, C), F32),
                  jax.ShapeDtypeStruct((CONV_W - 1, n, C), F32)]
    return pl.pallas_call(
        _lru_step_kernel,
        grid=(1,),
        in_specs=[full(a) for a in args],
        out_specs=[pl.BlockSpec(s.shape, lambda i, nd=len(s.shape): (0,) * nd) for s in out_shapes],
        out_shape=out_shapes,
        compiler_params=_cparams(("arbitrary",)),
        name="rglru_step",
    )(*args)


def _subln(o, g, scale):
    ms = jnp.mean(o * o, axis=-1, keepdims=True)
    return o * lax.rsqrt(ms + SUBLN_EPS) * g * scale


def _attn_kernel(lam_ref, qt_ref, kb_ref, vt_ref, g_ref, o_ref, qcat, m, acc, s_even, s_odd, *, out_scale,
                 between=None):
    qi = pl.program_id(1)
    Tq = qt_ref.shape[1]
    Tk = Tq

    qt = qt_ref[...]
    row = lax.broadcasted_iota(jnp.int32, qt.shape, 0)
    second = (row & HALF) != 0
    zero = jnp.zeros_like(qt)
    qcat[:, :Tq] = jnp.where(second, zero, qt)
    qcat[:, Tq:] = jnp.where(second, qt, zero)
    m[...] = jnp.full_like(m, NEG_INF)
    acc[...] = jnp.zeros_like(acc)

    def scores(kj, sbuf):
        k0 = pl.multiple_of(kj * Tk, Tk)
        sbuf[...] = jnp.dot(kb_ref[pl.ds(k0, Tk), :], qcat[...], preferred_element_type=F32)

    def update(kj, sbuf, diagonal):
        k0 = pl.multiple_of(kj * Tk, Tk)
        s = sbuf[...]
        if diagonal:
            key = lax.broadcasted_iota(jnp.int32, s.shape, 0)
            qry = lax.broadcasted_iota(jnp.int32, s.shape, 1)
            qry = jnp.where(qry >= Tq, qry - Tq, qry)
            s = jnp.where(key <= qry, s, NEG_INF)
        m_old = m[...]
        m_new = jnp.maximum(m_old, jnp.max(s, axis=0, keepdims=True))
        alpha = jnp.exp(m_old - m_new)
        p = jnp.exp(s - m_new)
        acc[...] = alpha * acc[...] + jnp.dot(vt_ref[:, pl.ds(k0, Tk)], p.astype(BF16),
                                              preferred_element_type=F32)
        m[...] = m_new

    def finish():
        a = acc[:V_DIM, :] / acc[V_DIM:V_DIM + 1, :]
        o = a[:, :Tq] - lam_ref[0, 0] * a[:, Tq:]
        ms = jnp.mean(o * o, axis=0, keepdims=True)
        o = o * lax.rsqrt(ms + SUBLN_EPS) * (g_ref[...] * out_scale)
        o_ref[...] = o.T.astype(o_ref.dtype)

    scores(0, s_even)
    if between is not None:
        between()

    def pair(i, c):
        t = 2 * i
        scores(t + 1, s_odd)
        update(t, s_even, False)
        scores(t + 2, s_even)
        update(t + 1, s_odd, False)
        return c

    lax.fori_loop(0, qi // 2, pair, 0)

    @pl.when(qi % 2 == 0)
    def _():
        update(qi, s_even, True)
        finish()

    @pl.when(qi % 2 == 1)
    def _():
        scores(qi, s_odd)
        update(qi - 1, s_even, False)
        update(qi, s_odd, True)
        finish()


def _attn_prompt(qt, kb, vt, subln_g, lam, out_scale):
    B, S, d_att = kb.shape
    G = d_att // V_DIM
    T = min(ATT_TILE, S)
    return pl.pallas_call(
        functools.partial(_attn_kernel, out_scale=out_scale),
        grid=(B * G, S // T),
        in_specs=[
            pl.BlockSpec(memory_space=pltpu.SMEM),
            pl.BlockSpec((None, V_DIM, T), lambda bg, i: (bg // G, bg % G, i)),
            pl.BlockSpec((None, S, V_DIM), lambda bg, i: (bg // G, 0, bg % G)),
            pl.BlockSpec((None, V_DIM + ONES_ROWS, S), lambda bg, i: (bg // G, bg % G, 0)),
            pl.BlockSpec((V_DIM, 1), lambda bg, i: (0, 0)),
        ],
        out_specs=pl.BlockSpec((None, T, V_DIM), lambda bg, i: (bg // G, i, bg % G)),
        out_shape=jax.ShapeDtypeStruct((B, S, d_att), BF16),
        scratch_shapes=[pltpu.VMEM((V_DIM, 2 * T), BF16), pltpu.VMEM((1, 2 * T), F32),
                        pltpu.VMEM((V_DIM + ONES_ROWS, 2 * T), F32),
                        pltpu.VMEM((T, 2 * T), F32), pltpu.VMEM((T, 2 * T), F32)],
        compiler_params=_cparams(("parallel", "arbitrary")),
        name="diff_attn_prompt",
    )(lam, qt, kb, vt, subln_g.reshape(V_DIM, 1))


def _attn_sample_body(seq, lam_ref, qt_ref, kt_ref, vc_ref, g_ref, *rest, n_pages, out_scale):
    k_refs = rest[:n_pages]
    v_refs = rest[n_pages:2 * n_pages]
    o_ref = rest[2 * n_pages]
    lam = lam_ref[0, 0]
    d_att, n_lanes = qt_ref.shape
    n_heads = d_att // HEAD_DIM
    G = d_att // V_DIM
    R = 2 * SUBLANES
    mine = lax.broadcasted_iota(jnp.int32, (d_att, n_lanes), 1) == seq % n_lanes
    qcol = jnp.sum(jnp.where(mine, qt_ref[...], 0.0), axis=1, keepdims=True)
    kcol = jnp.sum(jnp.where(mine, kt_ref[...], 0.0), axis=1, keepdims=True)
    P = k_refs[0].shape[1]
    qcolb = jnp.broadcast_to(qcol, (d_att, P))
    per_head = lambda x: jnp.sum(x.reshape(n_heads, HEAD_DIM, x.shape[-1]), axis=1)
    s = jnp.concatenate([per_head(k_refs[j][...] * qcolb) for j in range(n_pages)], axis=1)
    s_cur = per_head(qcol * kcol)
    m = jnp.maximum(jnp.max(s, axis=-1, keepdims=True), s_cur)
    p = jnp.exp(s - m)
    p_cur = jnp.exp(s_cur - m)
    inv_l = 1.0 / (jnp.sum(p, axis=-1, keepdims=True) + p_cur)
    p = p * inv_l
    p_cur = p_cur * inv_l
    amap = p - lam * pltpu.roll(p, n_heads - 1, 0)
    amap = jnp.concatenate([amap, jnp.zeros_like(amap)], axis=0).astype(BF16)
    p_cur = jnp.broadcast_to(p_cur, (n_heads, LANES))
    amap_cur = p_cur - lam * pltpu.roll(p_cur, n_heads - 1, 0)
    outs = []
    for g in range(G):
        acc = jnp.zeros((R, V_DIM), F32)
        for j in range(n_pages):
            vjg = v_refs[j][pl.ds(g, P, stride=G), :].astype(BF16)
            acc = acc + jnp.dot(amap[:, j * P:(j + 1) * P], vjg, preferred_element_type=F32)
        o = acc[2 * g:2 * g + 1, :] + amap_cur[2 * g:2 * g + 1, :] * vc_ref[:, g * V_DIM:(g + 1) * V_DIM]
        outs.append(_subln(o, g_ref[...], out_scale))
    o_ref[...] = jnp.concatenate(outs, axis=1).astype(o_ref.dtype)


def _attn_sample_kernel(pt_ref, *refs, n_pages, out_scale):
    _attn_sample_body(pl.program_id(0), *refs, n_pages=n_pages, out_scale=out_scale)


def _attn_sample(pt_flat, lam, qt_s, kt_s, v_cur, subln_g, ck, cv, layer, out_scale):
    d_att, n = qt_s.shape
    n_pages = pt_flat.shape[0] // n
    P = ck.shape[-1]
    nl = min(n, LANES)
    vc3 = v_cur.reshape(n, 1, d_att)
    cols = lambda: pl.BlockSpec((d_att, nl), lambda b, pt: (0, b // nl))
    k_specs = [pl.BlockSpec((None, None, d_att, P), functools.partial(
        lambda b, pt, j: (layer, pt[b * n_pages + j], 0, 0), j=j)) for j in range(n_pages)]
    v_specs = [pl.BlockSpec((None, None, cv.shape[2], V_DIM), functools.partial(
        lambda b, pt, j: (layer, pt[b * n_pages + j], 0, 0), j=j)) for j in range(n_pages)]
    grid_spec = pltpu.PrefetchScalarGridSpec(
        num_scalar_prefetch=1,
        grid=(n,),
        in_specs=[pl.BlockSpec(memory_space=pltpu.SMEM), cols(), cols(),
                  pl.BlockSpec((None, 1, d_att), lambda b, pt: (b, 0, 0)),
                  pl.BlockSpec((1, V_DIM), lambda b, pt: (0, 0))] + k_specs + v_specs,
        out_specs=pl.BlockSpec((None, 1, d_att), lambda b, pt: (b, 0, 0)),
    )
    out = pl.pallas_call(
        functools.partial(_attn_sample_kernel, n_pages=n_pages, out_scale=out_scale),
        grid_spec=grid_spec,
        out_shape=jax.ShapeDtypeStruct((n, 1, d_att), BF16),
        compiler_params=_cparams(("arbitrary",)),
        name="diff_attn_sample",
    )(pt_flat, lam, qt_s, kt_s, vc3, subln_g, *([ck] * n_pages), *([cv] * n_pages))
    return out.reshape(n, d_att)


def _attn_fused_kernel(pt_ref, lam_ref, qt_ref, kb_ref, vt_ref, gcol_ref, qts_ref, kts_ref, vc_ref, g_ref, *rest,
                       n_pages, out_scale):
    pages = rest[:2 * n_pages]
    o_ref, os_ref, qcat, m, acc, s_even, s_odd = rest[2 * n_pages:]
    seq = pl.program_id(0) * pl.num_programs(1) + pl.program_id(1)
    sample = functools.partial(_attn_sample_body, seq, lam_ref, qts_ref, kts_ref, vc_ref, g_ref, *pages, os_ref,
                               n_pages=n_pages, out_scale=out_scale)
    _attn_kernel(lam_ref, qt_ref, kb_ref, vt_ref, gcol_ref, o_ref, qcat, m, acc, s_even, s_odd,
                 out_scale=out_scale, between=sample)


def _attn_prompt_and_sample(qt, kb, vt, subln_g, lam, out_scale, pt_flat, qt_s, kt_s, v_cur, ck, cv, layer):
    B, S, d_att = kb.shape
    G = d_att // V_DIM
    T = min(ATT_TILE, S)
    nq = S // T
    n = qt_s.shape[1]
    if n != B * G * nq:
        return (_attn_prompt(qt, kb, vt, subln_g, lam, out_scale),
                _attn_sample(pt_flat, lam, qt_s, kt_s, v_cur, subln_g, ck, cv, layer, out_scale))
    n_pages = pt_flat.shape[0] // n
    P = ck.shape[-1]
    nl = min(n, LANES)
    seq = lambda bg, i: bg * nq + i
    cols = lambda: pl.BlockSpec((d_att, nl), lambda bg, i, pt: (0, seq(bg, i) // nl))
    k_specs = [pl.BlockSpec((None, None, d_att, P), functools.partial(
        lambda bg, i, pt, j: (layer, pt[seq(bg, i) * n_pages + j], 0, 0), j=j)) for j in range(n_pages)]
    v_specs = [pl.BlockSpec((None, None, cv.shape[2], V_DIM), functools.partial(
        lambda bg, i, pt, j: (layer, pt[seq(bg, i) * n_pages + j], 0, 0), j=j)) for j in range(n_pages)]
    grid_spec = pltpu.PrefetchScalarGridSpec(
        num_scalar_prefetch=1,
        grid=(B * G, nq),
        in_specs=[
            pl.BlockSpec(memory_space=pltpu.SMEM),
            pl.BlockSpec((None, V_DIM, T), lambda bg, i, pt: (bg // G, bg % G, i)),
            pl.BlockSpec((None, S, V_DIM), lambda bg, i, pt: (bg // G, 0, bg % G)),
            pl.BlockSpec((None, V_DIM + ONES_ROWS, S), lambda bg, i, pt: (bg // G, bg % G, 0)),
            pl.BlockSpec((V_DIM, 1), lambda bg, i, pt: (0, 0)),
            cols(), cols(),
            pl.BlockSpec((None, 1, d_att), lambda bg, i, pt: (seq(bg, i), 0, 0)),
            pl.BlockSpec((1, V_DIM), lambda bg, i, pt: (0, 0)),
        ] + k_specs + v_specs,
        out_specs=[pl.BlockSpec((None, T, V_DIM), lambda bg, i, pt: (bg // G, i, bg % G)),
                   pl.BlockSpec((None, 1, d_att), lambda bg, i, pt: (seq(bg, i), 0, 0))],
        scratch_shapes=[pltpu.VMEM((V_DIM, 2 * T), BF16), pltpu.VMEM((1, 2 * T), F32),
                        pltpu.VMEM((V_DIM + ONES_ROWS, 2 * T), F32),
                        pltpu.VMEM((T, 2 * T), F32), pltpu.VMEM((T, 2 * T), F32)],
    )
    att, att_s = pl.pallas_call(
        functools.partial(_attn_fused_kernel, n_pages=n_pages, out_scale=out_scale),
        grid_spec=grid_spec,
        out_shape=[jax.ShapeDtypeStruct((B, S, d_att), BF16), jax.ShapeDtypeStruct((n, 1, d_att), BF16)],
        compiler_params=_cparams(("parallel", "arbitrary")),
        name="diff_attn_fused",
    )(pt_flat, lam, qt, kb, vt, subln_g.reshape(V_DIM, 1),
      qt_s, kt_s, v_cur.reshape(n, 1, d_att), subln_g,
      *([ck] * n_pages), *([cv] * n_pages))
    return att, att_s.reshape(n, d_att)


def _outproj_kernel(yr_ref, at_ref, x_ref, wo_ref, g2_ref, wr_ref, br_ref, cnt_in_ref,
                    x1_ref, h2_ref, route_ref, route_t_ref, cnt_out_ref, cnt_scr, *, n_groups, per_group):
    step = pl.program_id(0)

    @pl.when(step == 0)
    def _():
        cnt_scr[...] = cnt_in_ref[...]

    d_rnn = yr_ref.shape[1]
    y = (jnp.dot(yr_ref[...], wo_ref[:d_rnn, :], preferred_element_type=F32)
         + jnp.dot(at_ref[...], wo_ref[d_rnn:, :], preferred_element_type=F32))
    x1 = x_ref[...] + y
    x1_ref[...] = x1
    ms = jnp.mean(x1 * x1, axis=-1, keepdims=True)
    h2 = x1 * lax.rsqrt(ms + NORM_EPS) * g2_ref[...]
    bits = pltpu.bitcast(h2.astype(BF16).astype(F32), jnp.uint32)
    half = bits.shape[1] // 2
    h2_ref[...] = bits[:, :half] | (bits[:, half:] >> 16)
    logits = jnp.dot(h2.astype(BF16), wr_ref[...], preferred_element_type=F32) + br_ref[...]
    T = logits.shape[0]
    lane = lax.broadcasted_iota(jnp.int32, (T, LANES), 1).astype(F32)
    first = lambda hit: jnp.min(jnp.where(hit, lane, float(LANES)), axis=-1, keepdims=True)
    gl = jnp.where(lane < n_groups, logits, NEG_INF)
    gmax = jnp.max(gl, axis=-1, keepdims=True)
    gidx = first(gl == gmax)
    pg = 1.0 / jnp.sum(jnp.exp(gl - gmax), axis=-1, keepdims=True)
    lo = n_groups + per_group * gidx
    el = jnp.where(jnp.logical_and(lane >= lo, lane < lo + per_group), logits, NEG_INF)
    ma = jnp.max(el, axis=-1, keepdims=True)
    ia = first(el == ma)
    el2 = jnp.where(lane == ia, NEG_INF, el)
    mb = jnp.max(el2, axis=-1, keepdims=True)
    ib = first(el2 == mb)
    eb = jnp.exp(mb - ma)
    gate_a = pg / (1.0 + eb)
    gate_b = pg * eb / (1.0 + eb)
    hit_a = lane == ia
    hit_b = lane == ib
    onehot = jnp.where(jnp.logical_or(hit_a, hit_b), 1.0, 0.0)
    r_i = lax.broadcasted_iota(jnp.int32, (T, T), 0)
    c_i = lax.broadcasted_iota(jnp.int32, (T, T), 1)
    lower = jnp.where(c_i < r_i, 1.0, 0.0).astype(BF16)
    before = jnp.dot(lower, onehot.astype(BF16), preferred_element_type=F32) + cnt_scr[...]
    rank_a = jnp.sum(jnp.where(hit_a, before, 0.0), axis=-1, keepdims=True)
    rank_b = jnp.sum(jnp.where(hit_b, before, 0.0), axis=-1, keepdims=True)
    cnt_scr[...] = cnt_scr[...] + jnp.sum(onehot, axis=0, keepdims=True)
    cnt_out_ref[...] = cnt_scr[...]
    cols = (ia - n_groups, ib - n_groups, gate_a, gate_b, rank_a, rank_b)
    route = jnp.zeros((T, LANES), F32)
    for c, val in enumerate(cols):
        route = jnp.where(lane == float(c), val, route)
    route_ref[...] = route
    route_t_ref[...] = route.T[:SUBLANES]


def _outproj_route(yr, at, x, wo, g2, wr, br, cnt_in, n_groups, per_group):
    N, D = x.shape
    T = min(ROW_TILE, N)
    d_rnn = yr.shape[1]
    row = lambda w: pl.BlockSpec((T, w), lambda i: (i, 0))
    full = lambda a: pl.BlockSpec(a.shape, lambda i: (0,) * a.ndim)
    return pl.pallas_call(
        functools.partial(_outproj_kernel, n_groups=n_groups, per_group=per_group),
        grid=(N // T,),
        in_specs=[row(d_rnn), row(at.shape[1]), row(D), full(wo), full(g2), full(wr), full(br), full(cnt_in)],
        out_specs=[row(D), row(D // 2), row(LANES), pl.BlockSpec((SUBLANES, T), lambda i: (0, i)),
                   pl.BlockSpec((1, LANES), lambda i: (0, 0))],
        out_shape=[jax.ShapeDtypeStruct((N, D), F32), jax.ShapeDtypeStruct((N, D // 2), jnp.uint32),
                   jax.ShapeDtypeStruct((N, LANES), F32), jax.ShapeDtypeStruct((SUBLANES, N), F32),
                   jax.ShapeDtypeStruct((1, LANES), F32)],
        scratch_shapes=[pltpu.VMEM((1, LANES), F32)],
        compiler_params=_cparams(("arbitrary",)),
        name="outproj_route",
    )(yr, at, x, wo, g2, wr, br, cnt_in)


def _dest_kernel(seg_ref, rt_ref, o_ref):
    e = rt_ref[0:TOP_K, :].astype(jnp.int32)
    start = jnp.zeros_like(e)
    for ex in range(seg_ref.shape[0]):
        start = jnp.where(e == ex, seg_ref[ex], start)
    o_ref[...] = start + rt_ref[4:4 + TOP_K, :].astype(jnp.int32)


def _dest(seg_start, route_t):
    R, N = route_t.shape
    T = min(ROW_TILE, N)
    grid_spec = pltpu.PrefetchScalarGridSpec(
        num_scalar_prefetch=1,
        grid=(N // T,),
        in_specs=[pl.BlockSpec((R, T), lambda i, seg: (0, i))],
        out_specs=pl.BlockSpec((TOP_K, T), lambda i, seg: (0, i)),
    )
    return pl.pallas_call(
        _dest_kernel,
        grid_spec=grid_spec,
        out_shape=jax.ShapeDtypeStruct((TOP_K, N), jnp.int32),
        compiler_params=_cparams(("arbitrary",)),
        name="moe_dest",
    )(seg_start, route_t)


def _dispatch_kernel(zb_ref, dest_ref, h_ref, *rest, zero_fill):
    xb_hbm, zbuf, sem = rest[-3], rest[-2], rest[-1]
    T = h_ref.shape[0]

    if zero_fill:
        @pl.when(pl.program_id(0) == 0)
        def _():
            zbuf[...] = jnp.zeros_like(zbuf)

            def zcopy(i):
                row0 = pl.multiple_of(zb_ref[i] * EXP_BLOCK, EXP_BLOCK)
                return pltpu.make_async_copy(zbuf, xb_hbm.at[pl.ds(row0, EXP_BLOCK)], sem)

            def zstart(i, c):
                @pl.when(zb_ref[i] >= 0)
                def _():
                    zcopy(i).start()
                return c

            def zwait(i, c):
                @pl.when(zb_ref[i] >= 0)
                def _():
                    zcopy(i).wait()
                return c

            lax.fori_loop(0, zb_ref.shape[0], zstart, 0)
            lax.fori_loop(0, zb_ref.shape[0], zwait, 0)

    def body(k, c):
        r0 = pl.multiple_of(k * SUBLANES, SUBLANES)
        rows = h_ref.at[pl.ds(r0, SUBLANES)]
        for j in range(SUBLANES):
            for s in range(TOP_K):
                pltpu.make_async_copy(rows.at[pl.ds(j, 1)],
                                      xb_hbm.at[pl.ds(dest_ref[s, r0 + j], 1)], sem).start(priority=s)
        return c

    lax.fori_loop(0, T // SUBLANES, body, 0, unroll=2)
    for s in range(TOP_K):
        pltpu.make_async_copy(h_ref, xb_hbm.at[pl.ds(0, T)], sem).wait()


def _dispatch(zb, dest, h, xb, xb_rows):
    N, D = h.shape
    T = min(DISPATCH_TILE, N)
    dest3 = jnp.transpose(dest.reshape(TOP_K, N // T, T), (1, 0, 2))
    aliased = xb is not None
    in_specs = [pl.BlockSpec((None, TOP_K, T), lambda i, zb: (i, 0, 0), memory_space=pltpu.SMEM),
                pl.BlockSpec((T, D), lambda i, zb: (i, 0))]
    args = [zb, dest3, h]
    if aliased:
        in_specs.append(pl.BlockSpec(memory_space=pl.ANY))
        args.append(xb)
    grid_spec = pltpu.PrefetchScalarGridSpec(
        num_scalar_prefetch=1,
        grid=(N // T,),
        in_specs=in_specs,
        out_specs=pl.BlockSpec(memory_space=pl.ANY),
        scratch_shapes=[pltpu.VMEM((EXP_BLOCK, D), h.dtype), pltpu.SemaphoreType.DMA(())],
    )
    return pl.pallas_call(
        functools.partial(_dispatch_kernel, zero_fill=not aliased),
        grid_spec=grid_spec,
        out_shape=jax.ShapeDtypeStruct((xb_rows, D), h.dtype),
        input_output_aliases={3: 0} if aliased else {},
        compiler_params=pltpu.CompilerParams(dimension_semantics=("arbitrary",), has_side_effects=True),
        name="moe_dispatch",
    )(*args)


def _expert_kernel(be_ref, bi_ref, br_ref, bf_ref, x_ref, w1_ref, w2_ref, y_ref, w1b, w2b):
    b = pl.program_id(0)
    rows = br_ref[b]

    @pl.when(bf_ref[b] == 1)
    def _():
        w1b[...] = w1_ref[...].astype(BF16)
        w2b[...] = w2_ref[...].astype(BF16)

    @pl.when(rows > 0)
    def _():
        words = x_ref[...]
        hi = pltpu.bitcast(words & jnp.uint32(0xFFFF0000), F32)
        lo = pltpu.bitcast(words << 16, F32)
        x = jnp.concatenate([hi, lo], axis=1).astype(BF16)
        h = jnp.dot(x, w1b[...], preferred_element_type=F32)
        de = h.shape[1] // 2
        act = (jax.nn.silu(h[:, :de]) * h[:, de:]).astype(BF16)
        y_ref[...] = jnp.dot(act, w2b[...], preferred_element_type=F32)

    @pl.when(rows == 0)
    def _():
        y_ref[...] = jnp.zeros_like(y_ref)


def _experts(be, bi, br, bf, xb, w1, w2, layer):
    R = xb.shape[0]
    nb = be.shape[0]
    D, d2 = w1.shape[-2:]
    de = w2.shape[-2]
    grid_spec = pltpu.PrefetchScalarGridSpec(
        num_scalar_prefetch=4,
        grid=(nb,),
        in_specs=[pl.BlockSpec((EXP_BLOCK, xb.shape[1]), lambda b, be, bi, br, bf: (bi[b], 0)),
                  pl.BlockSpec((None, None, D, d2), lambda b, be, bi, br, bf: (layer, be[b], 0, 0)),
                  pl.BlockSpec((None, None, de, D), lambda b, be, bi, br, bf: (layer, be[b], 0, 0))],
        out_specs=pl.BlockSpec((EXP_BLOCK, D), lambda b, be, bi, br, bf: (b, 0)),
        scratch_shapes=[pltpu.VMEM((D, d2), BF16), pltpu.VMEM((de, D), BF16)],
    )
    return pl.pallas_call(
        _expert_kernel,
        grid_spec=grid_spec,
        out_shape=jax.ShapeDtypeStruct((R, D), F32),
        compiler_params=_cparams(("arbitrary",)),
        name="moe_experts",
    )(be, bi, br, bf, xb, w1, w2)


def _combine_kernel(dest_ref, dest_next_ref, x1_ref, route_ref, gf_ref, yb_hbm, o_ref, ybuf, sem, *, final):
    T = x1_ref.shape[0]
    i = pl.program_id(0)
    slot = i % 2

    def issue(dref, sl):
        def body(k, c):
            r0 = pl.multiple_of(k * SUBLANES, SUBLANES)
            for s in range(TOP_K):
                rows = ybuf.at[sl, s, pl.ds(r0, SUBLANES)]
                for j in range(SUBLANES):
                    pltpu.make_async_copy(yb_hbm.at[pl.ds(dref[s, r0 + j], 1)],
                                          rows.at[pl.ds(j, 1)], sem.at[sl]).start(priority=s)
            return c

        lax.fori_loop(0, T // SUBLANES, body, 0, unroll=2)

    @pl.when(i == 0)
    def _():
        issue(dest_ref, 0)

    @pl.when(i + 1 < pl.num_programs(0))
    def _():
        issue(dest_next_ref, 1 - slot)

    for s in range(TOP_K):
        pltpu.make_async_copy(yb_hbm.at[pl.ds(0, T)], ybuf.at[slot, s], sem.at[slot]).wait()
    route = route_ref[...]
    out = x1_ref[...] + ybuf[slot, 0] * route[:, 2:3] + ybuf[slot, 1] * route[:, 3:4]
    if final:
        ms = jnp.mean(out * out, axis=-1, keepdims=True)
        out = out * lax.rsqrt(ms + NORM_EPS) * gf_ref[...]
    o_ref[...] = out


def _combine(dest, x1, route, gf, yb, final):
    N, D = x1.shape
    T = min(COMBINE_TILE, N)
    dest3 = jnp.transpose(dest.reshape(TOP_K, N // T, T), (1, 0, 2))
    last = N // T - 1
    return pl.pallas_call(
        functools.partial(_combine_kernel, final=final),
        grid=(N // T,),
        in_specs=[pl.BlockSpec((None, TOP_K, T), lambda i: (i, 0, 0), memory_space=pltpu.SMEM),
                  pl.BlockSpec((None, TOP_K, T), lambda i: (jnp.minimum(i + 1, last), 0, 0),
                               memory_space=pltpu.SMEM),
                  pl.BlockSpec((T, D), lambda i: (i, 0)),
                  pl.BlockSpec((T, LANES), lambda i: (i, 0)),
                  pl.BlockSpec((1, D), lambda i: (0, 0)),
                  pl.BlockSpec(memory_space=pl.ANY)],
        out_specs=pl.BlockSpec((T, D), lambda i: (i, 0)),
        out_shape=jax.ShapeDtypeStruct((N, D), F32),
        scratch_shapes=[pltpu.VMEM((2, TOP_K, T, D), F32), pltpu.SemaphoreType.DMA((2,))],
        compiler_params=_cparams(("arbitrary",)),
        name="moe_combine",
    )(dest3, dest3, x1, route, gf, yb)


def _rope_tables(pos):
    inv = ROPE_THETA ** (-jnp.arange(HALF, dtype=F32) / HALF)
    ang = pos.astype(F32)[:, None] * inv[None, :]
    cos, sin = jnp.cos(ang), jnp.sin(ang)
    cq = jnp.tile(cos, (1, LANES // HALF))
    sq = jnp.concatenate([-sin, -sin, sin, sin], axis=1)
    return cq, sq, cos.T, sin.T


def _q_perm(d_att):
    idx = []
    for g in range(d_att // V_DIM):
        for part in range(2):
            for jj in range(2):
                base = HEAD_DIM * (2 * g + jj) + HALF * part
                idx.extend(range(base, base + HALF))
    return jnp.asarray(idx, jnp.int32)


def _gate_weights(w_ga, w_gx):
    nb, bw, _ = w_ga.shape
    half = nb // 2
    eye = jnp.eye(half, dtype=w_ga.dtype)

    def dense(w):
        return jnp.einsum('nij,nm->nimj', w, eye).reshape(half * bw, half * bw)

    return jnp.stack([jnp.concatenate([dense(w_ga[h * half:(h + 1) * half]),
                                       dense(w_gx[h * half:(h + 1) * half])], axis=1)
                      for h in range(2)]).astype(BF16)


def _block_tables(counts, nb):
    ne = counts.shape[0]
    nblk = (counts + EXP_BLOCK - 1) // EXP_BLOCK
    cum = jnp.sum(jnp.where(jnp.arange(ne)[None, :] <= jnp.arange(ne)[:, None], nblk[None, :], 0), axis=1)
    total = cum[-1]
    b = jnp.arange(nb, dtype=jnp.int32)
    bc = jnp.minimum(b, total - 1)
    e = jnp.minimum(jnp.sum(bc[:, None] >= cum[None, :], axis=1), ne - 1).astype(jnp.int32)
    pick = lambda tab: jnp.sum(jnp.where(e[:, None] == jnp.arange(ne)[None, :], tab[None, :], 0), axis=1)
    off = bc - (pick(cum) - pick(nblk))
    valid = b < total
    rows = jnp.where(valid, jnp.clip(pick(counts) - off * EXP_BLOCK, 0, EXP_BLOCK), 0)
    first = jnp.where(jnp.logical_and(valid, off == 0), 1, 0)
    last_blk = jnp.where(nblk > 0, cum - 1, -1)
    trail = total + jnp.arange(ne, dtype=jnp.int32)
    trail = jnp.where(trail < nb, trail, -1)
    zb = jnp.concatenate([last_blk, trail]).astype(jnp.int32)
    seg_start = ((cum - nblk) * EXP_BLOCK).astype(jnp.int32)
    return seg_start, e, bc.astype(jnp.int32), rows.astype(jnp.int32), first.astype(jnp.int32), zb


def kernel(x_prompt, x_sample, cache_k, cache_v, state_h, state_conv, page_table, ln1_g, w_in, conv_w, conv_b,
           w_ga, b_ga, w_gx, b_gx, lru_lambda, lam_q1, lam_k1, lam_q2, lam_k2, subln_g, w_out, ln2_g,
           w_route_group, b_route_group, w_route_expert, b_route_expert, w_exp_in, w_exp_out, ln_f):
    Bp, Sp, D = x_prompt.shape
    Bs = x_sample.shape[0]
    depth = w_in.shape[0]
    d_rnn = state_h.shape[-1]
    d_att = cache_k.shape[-2] * cache_k.shape[-1]
    n_groups = w_route_group.shape[-1]
    n_experts = w_route_expert.shape[-1]
    per_group = n_experts // n_groups
    n_pool, page = cache_k.shape[1], cache_k.shape[2]
    n_pages = page_table.shape[1]
    past_len = n_pages * page
    Np = Bp * Sp
    assert x_sample.shape[1] == 1 and n_groups + n_experts <= LANES

    cq, sq, ct, st = _rope_tables(jnp.arange(Sp))
    _, _, ct_s, st_s = _rope_tables(jnp.full((Bs,), past_len))
    perm = _q_perm(d_att)
    qscale = HEAD_DIM ** -0.5
    ck = jnp.transpose(cache_k, (0, 1, 3, 4, 2)).reshape(depth, n_pool, d_att, page)
    cv = cache_v.reshape(depth, n_pool, page * (d_att // V_DIM), V_DIM)
    pt_flat = page_table.reshape(-1).astype(jnp.int32)
    conv_s = jnp.transpose(state_conv, (0, 2, 1, 3))

    n_blocks = (TOP_K * (Np + Bs)) // EXP_BLOCK + n_experts

    xp = x_prompt
    xs = x_sample.reshape(Bs, D)
    kt_all = v_all = None
    hp_l, cp_l, ks_l, vs_l, hs_l, cs_l = [], [], [], [], [], []
    for l in range(depth):
        lambda_init = 0.8 - 0.6 * math.exp(-0.3 * l)
        out_scale = 1.0 - lambda_init
        lam = (jnp.exp(jnp.sum(lam_q1[l] * lam_k1[l])) - jnp.exp(jnp.sum(lam_q2[l] * lam_k2[l]))
               + lambda_init).reshape(1, 1).astype(F32)
        wl = w_in[l]
        w_xg = wl[:, :2 * d_rnn]
        w_q = wl[:, 2 * d_rnn:2 * d_rnn + d_att] * qscale
        w_k = wl[:, 2 * d_rnn + d_att:2 * d_rnn + 2 * d_att]
        w_v = wl[:, 2 * d_rnn + 2 * d_att:]
        wm_p = jnp.concatenate([w_xg, w_k[:, perm], w_v], axis=1).astype(BF16)
        wt_p = jnp.concatenate([w_q[:, perm], w_k, w_v], axis=1).T.astype(BF16)
        wm_s = jnp.concatenate([w_xg, w_v], axis=1).astype(BF16)
        wt_s = jnp.concatenate([w_k, w_q], axis=1).T.astype(BF16)
        g1 = ln1_g[l].reshape(1, D)
        wg = _gate_weights(w_ga[l], w_gx[l])
        bga = b_ga[l].reshape(1, d_rnn)
        bgx = b_gx[l].reshape(1, d_rnn)
        lru_l = lru_lambda[l].reshape(1, d_rnn)
        cw = conv_w[l]
        cb = conv_b[l].reshape(1, d_rnn)
        sg = subln_g[l].reshape(1, V_DIM)
        wo = w_out[l].astype(BF16)
        g2 = ln2_g[l].reshape(1, D)
        wr = jnp.zeros((D, LANES), F32).at[:, :n_groups].set(w_route_group[l]).at[
            :, n_groups:n_groups + n_experts].set(w_route_expert[l])
        wr = wr.astype(BF16)
        br =jnp.zeros((1, LANES), F32).at[0, :n_groups].set(b_route_group[l]).at[
            0, n_groups:n_groups + n_experts].set(b_route_expert[l])

        xr, gr, kb, v_all, qt, kt_all, vt = _inproj_prompt(xp, g1, wm_p, wt_p, cq, sq, ct, st, kt_all, v_all)
        y_rnn, h_last, c_tail = _lru_prompt(xr, gr, cw, cb, wg, bga, bgx, lru_l)
        xr_s, gr_s, v_s, kt_s, qt_s = _inproj_sample(xs, g1, wm_s, wt_s, ct_s, st_s)
        y_rnn_s, h_s, c_s = _lru_step(xr_s, gr_s, conv_s[l], state_h[l], cw, cb, wg, bga, bgx, lru_l)
        k_s = kt_s.T
        att, att_s = _attn_prompt_and_sample(qt, kb, vt, sg, lam, out_scale,
                                             pt_flat, qt_s, kt_s, v_s, ck, cv, l)
        cnt0 = jnp.zeros((1, LANES), F32)
        x1, h2, route, route_t, cnt1 = _outproj_route(
            y_rnn.reshape(Np, d_rnn), att.reshape(Np, d_att), xp.reshape(Np, D), wo, g2, wr, br, cnt0,
            n_groups, per_group)
        x1_s, h2_s, route_s, route_ts, cnt2 = _outproj_route(y_rnn_s, att_s, xs, wo, g2, wr, br, cnt1,
                                                             n_groups, per_group)
        counts = cnt2[0, n_groups:n_groups + n_experts].astype(jnp.int32)
        seg_start, be, bi, brows, bfirst, zb = _block_tables(counts, n_blocks)

        dest_p, dest_s = _dest(seg_start, route_t), _dest(seg_start, route_ts)
        xb = _dispatch(zb, dest_p, h2, None, n_blocks * EXP_BLOCK)
        xb = _dispatch(zb, dest_s, h2_s, xb, n_blocks * EXP_BLOCK)
        yb = _experts(be, bi, brows, bfirst, xb, w_exp_in, w_exp_out, l)
        final = l == depth - 1
        gf = ln_f.reshape(1, D)
        xp_new = _combine(dest_p, x1, route, gf, yb, final)
        xs = _combine(dest_s, x1_s, route_s, gf, yb, final)
        xp = xp_new.reshape(Bp, Sp, D)

        n_qk = d_att // HEAD_DIM
        hp_l.append(h_last.reshape(Bp, d_rnn))
        cp_l.append(c_tail)
        ks_l.append(k_s.reshape(Bs, 1, n_qk, HEAD_DIM))
        vs_l.append(v_s.reshape(Bs, 1, d_att // V_DIM, V_DIM))
        hs_l.append(h_s)
        cs_l.append(jnp.transpose(c_s, (1, 0, 2)))
    k_prompt = jnp.transpose(kt_all.reshape(depth, Bp, n_qk, HEAD_DIM, Sp), (0, 1, 4, 2, 3))
    v_prompt = v_all.reshape(depth, Bp, Sp, d_att // V_DIM, V_DIM)
    return (xp, xs.reshape(Bs, 1, D),
            k_prompt, v_prompt, jnp.stack(hp_l), jnp.stack(cp_l),
            jnp.stack(ks_l), jnp.stack(vs_l), jnp.stack(hs_l), jnp.stack(cs_l))
```

```python
import functools
import math

import jax
import jax.numpy as jnp
from jax import lax
from jax.experimental import pallas as pl
from jax.experimental.pallas import tpu as pltpu

F32 = jnp.float32
BF16 = jnp.bfloat16

HEAD_DIM = 64
V_DIM = 2 * HEAD_DIM
HALF = HEAD_DIM // 2
ONES_ROWS = 16
CONV_W = 4
LRU_C = 8.0
ROPE_THETA = 10000.0
NORM_EPS = 1e-6
SUBLN_EPS = 1e-5
NEG_INF = -1e30
TOP_K = 2

LANES = 128
SUBLANES = 8
VMEM_LIMIT = 48 * 1024 * 1024
INPROJ_VMEM_LIMIT = 56 * 1024 * 1024

ROW_TILE = 512
LRU_TILE = 256
ATT_TILE = 512
EXP_BLOCK = 512
DISPATCH_TILE = 512
COMBINE_TILE = 256


def _cparams(sem, vmem_limit=VMEM_LIMIT):
    return pltpu.CompilerParams(dimension_semantics=sem, vmem_limit_bytes=vmem_limit)


def _inproj_kernel(x_ref, g_ref, wm_ref, wt_ref, cq_ref, sq_ref, ct_ref, st_ref, *rest, sample, n_prev=0):
    prev, outs = rest[:2 if n_prev else 0], rest[2 if n_prev else 0:]
    x = x_ref[...]
    ms = jnp.mean(x * x, axis=-1, keepdims=True)
    h = (x * lax.rsqrt(ms + NORM_EPS) * g_ref[...]).astype(BF16)
    main = jnp.dot(h, wm_ref[...], preferred_element_type=F32)
    d_rnn = outs[0].shape[1]
    ct = ct_ref[...]
    st = st_ref[...]
    tr = lax.dot_general(wt_ref[...], h, (((1,), (1,)), ((), ())), preferred_element_type=F32)

    def rope_t(blk):
        x1 = blk[:HALF]
        x2 = blk[HALF:]
        return x1 * ct - x2 * st, x2 * ct + x1 * st

    if sample:
        xr_ref, gr_ref, v_ref, kt_ref, qt_ref = outs
        xr_ref[...] = main[:, :d_rnn]
        gr_ref[...] = main[:, d_rnn:2 * d_rnn]
        v_ref[...] = main[:, 2 * d_rnn:]
        n_heads = kt_ref.shape[0] // HEAD_DIM
        for hh in range(n_heads):
            o1, o2 = rope_t(tr[hh * HEAD_DIM:(hh + 1) * HEAD_DIM])
            kt_ref[hh * HEAD_DIM:hh * HEAD_DIM + HALF, :] = o1
            kt_ref[hh * HEAD_DIM + HALF:(hh + 1) * HEAD_DIM, :] = o2
            base = (n_heads + hh) * HEAD_DIM
            p1, p2 = rope_t(tr[base:base + HEAD_DIM])
            qt_ref[hh * HEAD_DIM:hh * HEAD_DIM + HALF, :] = p1
            qt_ref[hh * HEAD_DIM + HALF:(hh + 1) * HEAD_DIM, :] = p2
    else:
        xr_ref, gr_ref, kb_ref, v_ref, qt_ref, kt_ref, vt_ref = outs
        xr_ref[...] = main[:, :d_rnn]
        gr_ref[...] = main[:, d_rnn:2 * d_rnn]
        d_att = kb_ref.shape[1]
        k = main[:, 2 * d_rnn:2 * d_rnn + d_att]
        cq = cq_ref[...]
        sq = sq_ref[...]
        for c in range(d_att // LANES):
            kc = k[:, c * LANES:(c + 1) * LANES]
            kb_ref[:, c * LANES:(c + 1) * LANES] = (kc * cq + pltpu.roll(kc, HEAD_DIM, 1) * sq).astype(BF16)
        if n_prev:
            kt_ref[:n_prev] = prev[0][...]
            v_ref[:n_prev] = prev[1][...]
        v = main[:, 2 * d_rnn + d_att:]
        T = v.shape[0]
        G = d_att // V_DIM
        for g in range(G):
            v_ref[n_prev, pl.ds(g, T, stride=G), :] = v[:, g * V_DIM:(g + 1) * V_DIM]
        n_heads = d_att // HEAD_DIM
        for hh in range(n_heads):
            g, jj = divmod(hh, 2)
            r1 = g * V_DIM + jj * HALF
            x1 = tr[r1:r1 + HALF]
            x2 = tr[r1 + HEAD_DIM:r1 + HEAD_DIM + HALF]
            qt_ref[r1:r1 + HALF, :] = (x1 * ct - x2 * st).astype(BF16)
            qt_ref[r1 + HEAD_DIM:r1 + HEAD_DIM + HALF, :] = (x2 * ct + x1 * st).astype(BF16)
            base = d_att + hh * HEAD_DIM
            o1, o2 = rope_t(tr[base:base + HEAD_DIM])
            kt_ref[n_prev, hh * HEAD_DIM:hh * HEAD_DIM + HALF, :] = o1
            kt_ref[n_prev, hh * HEAD_DIM + HALF:(hh + 1) * HEAD_DIM, :] = o2
        for g in range(G):
            r0 = g * (V_DIM + ONES_ROWS)
            vt_ref[r0:r0 + V_DIM, :] = tr[2 * d_att + g * V_DIM:2 * d_att + (g + 1) * V_DIM].astype(BF16)
            vt_ref[r0 + V_DIM:r0 + V_DIM + ONES_ROWS, :] = jnp.ones((ONES_ROWS, T), BF16)


def _inproj_prompt(x, g, wm, wt, cq, sq, ct, st, k_prev, v_prev):
    B, S, D = x.shape
    T = min(ROW_TILE, S)
    nt = S // T
    d_att = wt.shape[0] // 3
    d_rnn = (wm.shape[1] - 2 * d_att) // 2
    G = d_att // V_DIM
    n_prev = 0 if k_prev is None else k_prev.shape[0]
    row = lambda w: pl.BlockSpec((None, T, w), lambda b, t: (b, t, 0))
    full = lambda a: pl.BlockSpec(a.shape, lambda b, t: (0,) * a.ndim)
    once = lambda a: pl.BlockSpec(a.shape, lambda b, t: (0,) * a.ndim, pipeline_mode=pl.Buffered(1))
    tr_spec = pl.BlockSpec((None, d_att, T), lambda b, t: (b, 0, t))
    kstack = lambda n: pl.BlockSpec((n, None, d_att, T), lambda b, t: (0, b, 0, t))
    vstack = lambda n: pl.BlockSpec((n, None, T * G, V_DIM), lambda b, t: (0, b, t, 0))
    prev_specs = [kstack(n_prev), vstack(n_prev)] if n_prev else []
    prev_args = [k_prev, v_prev] if n_prev else []
    return pl.pallas_call(
        functools.partial(_inproj_kernel, sample=False, n_prev=n_prev),
        grid=(B, nt),
        in_specs=[row(D), full(g), once(wm), once(wt),
                  pl.BlockSpec((T, LANES), lambda b, t: (t, 0)),
                  pl.BlockSpec((T, LANES), lambda b, t: (t, 0)),
                  pl.BlockSpec((HALF, T), lambda b, t: (0, t)),
                  pl.BlockSpec((HALF, T), lambda b, t: (0, t))] + prev_specs,
        out_specs=[row(d_rnn), row(d_rnn), row(d_att), vstack(n_prev + 1), tr_spec, kstack(n_prev + 1),
                   pl.BlockSpec((None, G * (V_DIM + ONES_ROWS), T), lambda b, t: (b, 0, t))],
        out_shape=[jax.ShapeDtypeStruct((B, S, d_rnn), F32), jax.ShapeDtypeStruct((B, S, d_rnn), F32),
                   jax.ShapeDtypeStruct((B, S, d_att), BF16),
                   jax.ShapeDtypeStruct((n_prev + 1, B, S * G, V_DIM), F32),
                   jax.ShapeDtypeStruct((B, d_att, S), BF16),
                   jax.ShapeDtypeStruct((n_prev + 1, B, d_att, S), F32),
                   jax.ShapeDtypeStruct((B, G * (V_DIM + ONES_ROWS), S), BF16)],
        compiler_params=_cparams(("parallel", "arbitrary"), INPROJ_VMEM_LIMIT),
        name="inproj_prompt",
    )(x, g, wm, wt, cq, sq, ct, st, *prev_args)


def _inproj_sample(x, g, wm, wt, ct, st):
    n, D = x.shape
    d_att = wt.shape[0] // 2
    d_rnn = (wm.shape[1] - d_att) // 2
    full = lambda a: pl.BlockSpec(a.shape, lambda i: (0,) * a.ndim)
    dummy = jnp.zeros((SUBLANES, LANES), F32)
    out_shapes = [jax.ShapeDtypeStruct((n, d_rnn), F32), jax.ShapeDtypeStruct((n, d_rnn), F32),
                  jax.ShapeDtypeStruct((n, d_att), F32), jax.ShapeDtypeStruct((d_att, n), F32),
                  jax.ShapeDtypeStruct((d_att, n), F32)]
    return pl.pallas_call(
        functools.partial(_inproj_kernel, sample=True),
        grid=(1,),
        in_specs=[full(x), full(g), full(wm), full(wt), full(dummy), full(dummy), full(ct), full(st)],
        out_specs=[pl.BlockSpec(s.shape, lambda i: (0, 0)) for s in out_shapes],
        out_shape=out_shapes,
        compiler_params=_cparams(("arbitrary",)),
        name="inproj_sample",
    )(x, g, wm, wt, dummy, dummy, ct, st)


def _gates(xc, wg_ref, bga, bgx, lam):
    half = xc.shape[1] // 2
    xcb = xc.astype(BF16)
    g0 = jnp.dot(xcb[:, :half], wg_ref[0], preferred_element_type=F32)
    g1 = jnp.dot(xcb[:, half:], wg_ref[1], preferred_element_type=F32)
    r = jax.nn.sigmoid(jnp.concatenate([g0[:, :half], g1[:, :half]], axis=1) + bga)
    i = jax.nn.sigmoid(jnp.concatenate([g0[:, half:], g1[:, half:]], axis=1) + bgx)
    neg = -lam
    softplus = jnp.maximum(neg, 0.0) + jnp.log1p(jnp.exp(-jnp.abs(neg)))
    log_a = -LRU_C * r * softplus
    a = jnp.exp(log_a)
    mult = jnp.sqrt(1.0 - a * a)
    return a, i, mult


def _lru_kernel(xr_ref, gr_ref, cw_ref, cb_ref, wg_ref, bga_ref, bgx_ref, lam_ref,
                y_ref, hlast_ref, ctail_ref, xp_scr, a_scr, b_scr, h_scr):
    t = pl.program_id(1)
    T, C = xr_ref.shape

    @pl.when(t == 0)
    def _():
        xp_scr[0:SUBLANES, :] = jnp.zeros((SUBLANES, C), F32)
        h_scr[...] = jnp.zeros_like(h_scr)

    x = xr_ref[...]
    xp_scr[SUBLANES:SUBLANES + T, :] = x
    cw = cw_ref[...]
    xc = cb_ref[...]
    for j in range(CONV_W - 1):
        xc = xc + xp_scr[pl.ds(SUBLANES - (CONV_W - 1) + j, T), :] * cw[j:j + 1]
    xc = xc + x * cw[CONV_W - 1:CONV_W]
    tail = x[T - (CONV_W - 1):, :]
    xp_scr[SUBLANES - (CONV_W - 1):SUBLANES, :] = tail
    ctail_ref[...] = tail

    a, i, mult = _gates(xc, wg_ref, bga_ref[...], bgx_ref[...], lam_ref[...])
    row = lax.broadcasted_iota(jnp.int32, (T, 1), 0)
    mult = jnp.where(jnp.logical_and(row == 0, t == 0), 1.0, mult)
    a_scr[...] = a
    b_scr[...] = mult * i * xc

    rowi = lax.broadcasted_iota(jnp.int32, (SUBLANES, C), 0)

    def body(gi, h):
        off = pl.multiple_of(gi * SUBLANES, SUBLANES)
        A = a_scr[pl.ds(off, SUBLANES), :]
        Bv = b_scr[pl.ds(off, SUBLANES), :]
        for d in (1, 2, 4):
            keep = rowi >= d
            Bv = jnp.where(keep, Bv + A * pltpu.roll(Bv, d, 0), Bv)
            A = jnp.where(keep, A * pltpu.roll(A, d, 0), A)
        hrows = Bv + A * h
        b_scr[pl.ds(off, SUBLANES), :] = hrows
        return hrows[SUBLANES - 1:SUBLANES, :]

    h = lax.fori_loop(0, T // SUBLANES, body, h_scr[...], unroll=2)
    h_scr[...] = h
    hlast_ref[...] = h
    y_ref[...] = (b_scr[...] * jax.nn.gelu(gr_ref[...])).astype(BF16)


def _lru_prompt(xr, gr, cw, cb, wg, bga, bgx, lam):
    B, S, C = xr.shape
    T = min(LRU_TILE, S)
    nt = S // T
    row = pl.BlockSpec((None, T, C), lambda b, t: (b, t, 0))
    full = lambda a: pl.BlockSpec(a.shape, lambda b, t: (0,) * a.ndim)
    return pl.pallas_call(
        _lru_kernel,
        grid=(B, nt),
        in_specs=[row, row, full(cw), full(cb), full(wg), full(bga), full(bgx), full(lam)],
        out_specs=[row, pl.BlockSpec((None, 1, C), lambda b, t: (b, 0, 0)),
                   pl.BlockSpec((None, CONV_W - 1, C), lambda b, t: (b, 0, 0))],
        out_shape=[jax.ShapeDtypeStruct((B, S, C), BF16), jax.ShapeDtypeStruct((B, 1, C), F32),
                   jax.ShapeDtypeStruct((B, CONV_W - 1, C), F32)],
        scratch_shapes=[pltpu.VMEM((T + SUBLANES, C), F32), pltpu.VMEM((T, C), F32),
                        pltpu.VMEM((T, C), F32), pltpu.VMEM((1, C), F32)],
        compiler_params=_cparams(("parallel", "arbitrary")),
        name="rglru_prompt",
    )(xr, gr, cw, cb, wg, bga, bgx, lam)


def _lru_step_kernel(xr_ref, gr_ref, conv_ref, h0_ref, cw_ref, cb_ref, wg_ref, bga_ref, bgx_ref, lam_ref,
                     y_ref, hnew_ref, cnew_ref):
    x = xr_ref[...]
    cw = cw_ref[...]
    xc = cb_ref[...]
    for j in range(CONV_W - 1):
        xc = xc + conv_ref[j] * cw[j:j + 1]
    xc = xc + x * cw[CONV_W - 1:CONV_W]
    for j in range(CONV_W - 2):
        cnew_ref[j] = conv_ref[j + 1]
    cnew_ref[CONV_W - 2] = x
    a, i, mult = _gates(xc, wg_ref, bga_ref[...], bgx_ref[...], lam_ref[...])
    h = a * h0_ref[...] + mult * i * xc
    hnew_ref[...] = h
    y_ref[...] = (h * jax.nn.gelu(gr_ref[...])).astype(BF16)


def _lru_step(xr, gr, conv, h0, cw, cb, wg, bga, bgx, lam):
    n, C = xr.shape
    args = (xr, gr, conv, h0, cw, cb, wg, bga, bgx, lam)
    full = lambda a: pl.BlockSpec(a.shape, lambda i: (0,) * a.ndim)
    out_shapes = [jax.ShapeDtypeStruct((n, C), BF16), jax.ShapeDtypeStruct((n, C), F32),
                  jax.ShapeDtypeStruct((CONV_W - 1, n, C), F32)]
    return pl.pallas_call(
        _lru_step_kernel,
        grid=(1,),
        in_specs=[full(a) for a in args],
        out_specs=[pl.BlockSpec(s.shape, lambda i, nd=len(s.shape): (0,) * nd) for s in out_shapes],
        out_shape=out_shapes,
        compiler_params=_cparams(("arbitrary",)),
        name="rglru_step",
    )(*args)


def _subln(o, g, scale):
    ms = jnp.mean(o * o, axis=-1, keepdims=True)
    return o * lax.rsqrt(ms + SUBLN_EPS) * g * scale


def _attn_kernel(lam_ref, qt_ref, kb_ref, vt_ref, g_ref, o_ref, qcat, m, acc, s_even, s_odd, *, out_scale,
                 between=None):
    qi = pl.program_id(1)
    Tq = qt_ref.shape[1]
    Tk = Tq

    qt = qt_ref[...]
    row = lax.broadcasted_iota(jnp.int32, qt.shape, 0)
    second = (row & HALF) != 0
    zero = jnp.zeros_like(qt)
    qcat[:, :Tq] = jnp.where(second, zero, qt)
    qcat[:, Tq:] = jnp.where(second, qt, zero)
    m[...] = jnp.full_like(m, NEG_INF)
    acc[...] = jnp.zeros_like(acc)

    def scores(kj, sbuf):
        k0 = pl.multiple_of(kj * Tk, Tk)
        sbuf[...] = jnp.dot(kb_ref[pl.ds(k0, Tk), :], qcat[...], preferred_element_type=F32)

    def update(kj, sbuf, diagonal):
        k0 = pl.multiple_of(kj * Tk, Tk)
        s = sbuf[...]
        if diagonal:
            key = lax.broadcasted_iota(jnp.int32, s.shape, 0)
            qry = lax.broadcasted_iota(jnp.int32, s.shape, 1)
            qry = jnp.where(qry >= Tq, qry - Tq, qry)
            s = jnp.where(key <= qry, s, NEG_INF)
        m_old = m[...]
        m_new = jnp.maximum(m_old, jnp.max(s, axis=0, keepdims=True))
        alpha = jnp.exp(m_old - m_new)
        p = jnp.exp(s - m_new)
        acc[...] = alpha * acc[...] + jnp.dot(vt_ref[:, pl.ds(k0, Tk)], p.astype(BF16),
                                              preferred_element_type=F32)
        m[...] = m_new

    def finish():
        a = acc[:V_DIM, :] / acc[V_DIM:V_DIM + 1, :]
        o = a[:, :Tq] - lam_ref[0, 0] * a[:, Tq:]
        ms = jnp.mean(o * o, axis=0, keepdims=True)
        o = o * lax.rsqrt(ms + SUBLN_EPS) * (g_ref[...] * out_scale)
        o_ref[...] = o.T.astype(o_ref.dtype)

    scores(0, s_even)
    if between is not None:
        between()

    def pair(i, c):
        t = 2 * i
        scores(t + 1, s_odd)
        update(t, s_even, False)
        scores(t + 2, s_even)
        update(t + 1, s_odd, False)
        return c

    lax.fori_loop(0, qi // 2, pair, 0)

    @pl.when(qi % 2 == 0)
    def _():
        update(qi, s_even, True)
        finish()

    @pl.when(qi % 2 == 1)
    def _():
        scores(qi, s_odd)
        update(qi - 1, s_even, False)
        update(qi, s_odd, True)
        finish()


def _attn_prompt(qt, kb, vt, subln_g, lam, out_scale):
    B, S, d_att = kb.shape
    G = d_att // V_DIM
    T = min(ATT_TILE, S)
    return pl.pallas_call(
        functools.partial(_attn_kernel, out_scale=out_scale),
        grid=(B * G, S // T),
        in_specs=[
            pl.BlockSpec(memory_space=pltpu.SMEM),
            pl.BlockSpec((None, V_DIM, T), lambda bg, i: (bg // G, bg % G, i)),
            pl.BlockSpec((None, S, V_DIM), lambda bg, i: (bg // G, 0, bg % G)),
            pl.BlockSpec((None, V_DIM + ONES_ROWS, S), lambda bg, i: (bg // G, bg % G, 0)),
            pl.BlockSpec((V_DIM, 1), lambda bg, i: (0, 0)),
        ],
        out_specs=pl.BlockSpec((None, T, V_DIM), lambda bg, i: (bg // G, i, bg % G)),
        out_shape=jax.ShapeDtypeStruct((B, S, d_att), BF16),
        scratch_shapes=[pltpu.VMEM((V_DIM, 2 * T), BF16), pltpu.VMEM((1, 2 * T), F32),
                        pltpu.VMEM((V_DIM + ONES_ROWS, 2 * T), F32),
                        pltpu.VMEM((T, 2 * T), F32), pltpu.VMEM((T, 2 * T), F32)],
        compiler_params=_cparams(("parallel", "arbitrary")),
        name="diff_attn_prompt",
    )(lam, qt, kb, vt, subln_g.reshape(V_DIM, 1))


def _attn_sample_body(seq, lam_ref, qt_ref, kt_ref, vc_ref, g_ref, *rest, n_pages, out_scale):
    k_refs = rest[:n_pages]
    v_refs = rest[n_pages:2 * n_pages]
    o_ref = rest[2 * n_pages]
    lam = lam_ref[0, 0]
    d_att, n_lanes = qt_ref.shape
    n_heads = d_att // HEAD_DIM
    G = d_att // V_DIM
    R = 2 * SUBLANES
    mine = lax.broadcasted_iota(jnp.int32, (d_att, n_lanes), 1) == seq % n_lanes
    qcol = jnp.sum(jnp.where(mine, qt_ref[...], 0.0), axis=1, keepdims=True)
    kcol = jnp.sum(jnp.where(mine, kt_ref[...], 0.0), axis=1, keepdims=True)
    P = k_refs[0].shape[1]
    qcolb = jnp.broadcast_to(qcol, (d_att, P))
    per_head = lambda x: jnp.sum(x.reshape(n_heads, HEAD_DIM, x.shape[-1]), axis=1)
    s = jnp.concatenate([per_head(k_refs[j][...] * qcolb) for j in range(n_pages)], axis=1)
    s_cur = per_head(qcol * kcol)
    m = jnp.maximum(jnp.max(s, axis=-1, keepdims=True), s_cur)
    p = jnp.exp(s - m)
    p_cur = jnp.exp(s_cur - m)
    inv_l = 1.0 / (jnp.sum(p, axis=-1, keepdims=True) + p_cur)
    p = p * inv_l
    p_cur = p_cur * inv_l
    amap = p - lam * pltpu.roll(p, n_heads - 1, 0)
    amap = jnp.concatenate([amap, jnp.zeros_like(amap)], axis=0).astype(BF16)
    p_cur = jnp.broadcast_to(p_cur, (n_heads, LANES))
    amap_cur = p_cur - lam * pltpu.roll(p_cur, n_heads - 1, 0)
    outs = []
    for g in range(G):
        acc = jnp.zeros((R, V_DIM), F32)
        for j in range(n_pages):
            vjg = v_refs[j][pl.ds(g, P, stride=G), :].astype(BF16)
            acc = acc + jnp.dot(amap[:, j * P:(j + 1) * P], vjg, preferred_element_type=F32)
        o = acc[2 * g:2 * g + 1, :] + amap_cur[2 * g:2 * g + 1, :] * vc_ref[:, g * V_DIM:(g + 1) * V_DIM]
        outs.append(_subln(o, g_ref[...], out_scale))
    o_ref[...] = jnp.concatenate(outs, axis=1).astype(o_ref.dtype)


def _attn_sample_kernel(pt_ref, *refs, n_pages, out_scale):
    _attn_sample_body(pl.program_id(0), *refs, n_pages=n_pages, out_scale=out_scale)


def _attn_sample(pt_flat, lam, qt_s, kt_s, v_cur, subln_g, ck, cv, layer, out_scale):
    d_att, n = qt_s.shape
    n_pages = pt_flat.shape[0] // n
    P = ck.shape[-1]
    nl = min(n, LANES)
    vc3 = v_cur.reshape(n, 1, d_att)
    cols = lambda: pl.BlockSpec((d_att, nl), lambda b, pt: (0, b // nl))
    k_specs = [pl.BlockSpec((None, None, d_att, P), functools.partial(
        lambda b, pt, j: (layer, pt[b * n_pages + j], 0, 0), j=j)) for j in range(n_pages)]
    v_specs = [pl.BlockSpec((None, None, cv.shape[2], V_DIM), functools.partial(
        lambda b, pt, j: (layer, pt[b * n_pages + j], 0, 0), j=j)) for j in range(n_pages)]
    grid_spec = pltpu.PrefetchScalarGridSpec(
        num_scalar_prefetch=1,
        grid=(n,),
        in_specs=[pl.BlockSpec(memory_space=pltpu.SMEM), cols(), cols(),
                  pl.BlockSpec((None, 1, d_att), lambda b, pt: (b, 0, 0)),
                  pl.BlockSpec((1, V_DIM), lambda b, pt: (0, 0))] + k_specs + v_specs,
        out_specs=pl.BlockSpec((None, 1, d_att), lambda b, pt: (b, 0, 0)),
    )
    out = pl.pallas_call(
        functools.partial(_attn_sample_kernel, n_pages=n_pages, out_scale=out_scale),
        grid_spec=grid_spec,
        out_shape=jax.ShapeDtypeStruct((n, 1, d_att), BF16),
        compiler_params=_cparams(("arbitrary",)),
        name="diff_attn_sample",
    )(pt_flat, lam, qt_s, kt_s, vc3, subln_g, *([ck] * n_pages), *([cv] * n_pages))
    return out.reshape(n, d_att)


def _attn_fused_kernel(pt_ref, lam_ref, qt_ref, kb_ref, vt_ref, gcol_ref, qts_ref, kts_ref, vc_ref, g_ref, *rest,
                       n_pages, out_scale):
    pages = rest[:2 * n_pages]
    o_ref, os_ref, qcat, m, acc, s_even, s_odd = rest[2 * n_pages:]
    seq = pl.program_id(0) * pl.num_programs(1) + pl.program_id(1)
    sample = functools.partial(_attn_sample_body, seq, lam_ref, qts_ref, kts_ref, vc_ref, g_ref, *pages, os_ref,
                               n_pages=n_pages, out_scale=out_scale)
    _attn_kernel(lam_ref, qt_ref, kb_ref, vt_ref, gcol_ref, o_ref, qcat, m, acc, s_even, s_odd,
                 out_scale=out_scale, between=sample)


def _attn_prompt_and_sample(qt, kb, vt, subln_g, lam, out_scale, pt_flat, qt_s, kt_s, v_cur, ck, cv, layer):
    B, S, d_att = kb.shape
    G = d_att // V_DIM
    T = min(ATT_TILE, S)
    nq = S // T
    n = qt_s.shape[1]
    if n != B * G * nq:
        return (_attn_prompt(qt, kb, vt, subln_g, lam, out_scale),
                _attn_sample(pt_flat, lam, qt_s, kt_s, v_cur, subln_g, ck, cv, layer, out_scale))
    n_pages = pt_flat.shape[0] // n
    P = ck.shape[-1]
    nl = min(n, LANES)
    seq = lambda bg, i: bg * nq + i
    cols = lambda: pl.BlockSpec((d_att, nl), lambda bg, i, pt: (0, seq(bg, i) // nl))
    k_specs = [pl.BlockSpec((None, None, d_att, P), functools.partial(
        lambda bg, i, pt, j: (layer, pt[seq(bg, i) * n_pages + j], 0, 0), j=j)) for j in range(n_pages)]
    v_specs = [pl.BlockSpec((None, None, cv.shape[2], V_DIM), functools.partial(
        lambda bg, i, pt, j: (layer, pt[seq(bg, i) * n_pages + j], 0, 0), j=j)) for j in range(n_pages)]
    grid_spec = pltpu.PrefetchScalarGridSpec(
        num_scalar_prefetch=1,
        grid=(B * G, nq),
        in_specs=[
            pl.BlockSpec(memory_space=pltpu.SMEM),
            pl.BlockSpec((None, V_DIM, T), lambda bg, i, pt: (bg // G, bg % G, i)),
            pl.BlockSpec((None, S, V_DIM), lambda bg, i, pt: (bg // G, 0, bg % G)),
            pl.BlockSpec((None, V_DIM + ONES_ROWS, S), lambda bg, i, pt: (bg // G, bg % G, 0)),
            pl.BlockSpec((V_DIM, 1), lambda bg, i, pt: (0, 0)),
            cols(), cols(),
            pl.BlockSpec((None, 1, d_att), lambda bg, i, pt: (seq(bg, i), 0, 0)),
            pl.BlockSpec((1, V_DIM), lambda bg, i, pt: (0, 0)),
        ] + k_specs + v_specs,
        out_specs=[pl.BlockSpec((None, T, V_DIM), lambda bg, i, pt: (bg // G, i, bg % G)),
                   pl.BlockSpec((None, 1, d_att), lambda bg, i, pt: (seq(bg, i), 0, 0))],
        scratch_shapes=[pltpu.VMEM((V_DIM, 2 * T), BF16), pltpu.VMEM((1, 2 * T), F32),
                        pltpu.VMEM((V_DIM + ONES_ROWS, 2 * T), F32),
                        pltpu.VMEM((T, 2 * T), F32), pltpu.VMEM((T, 2 * T), F32)],
    )
    att, att_s = pl.pallas_call(
        functools.partial(_attn_fused_kernel, n_pages=n_pages, out_scale=out_scale),
        grid_spec=grid_spec,
        out_shape=[jax.ShapeDtypeStruct((B, S, d_att), BF16), jax.ShapeDtypeStruct((n, 1, d_att), BF16)],
        compiler_params=_cparams(("parallel", "arbitrary")),
        name="diff_attn_fused",
    )(pt_flat, lam, qt, kb, vt, subln_g.reshape(V_DIM, 1),
      qt_s, kt_s, v_cur.reshape(n, 1, d_att), subln_g,
      *([ck] * n_pages), *([cv] * n_pages))
    return att, att_s.reshape(n, d_att)


def _store_token_tiles(ref, x):
    T, D = x.shape
    n = D // LANES
    for c in range(n):
        ref[pl.ds(c, T, stride=n), :] = x[:, c * LANES:(c + 1) * LANES]


def _load_token_tiles(ref, T):
    n = ref.shape[0] // T
    return jnp.concatenate([ref[pl.ds(c, T, stride=n), :] for c in range(n)], axis=1)


def _outproj_kernel(yr_ref, at_ref, x_ref, wo_ref, g2_ref, wr_ref, br_ref, cnt_in_ref,
                    x1_ref, h2_ref, route_ref, route_t_ref, cnt_out_ref, cnt_scr, *, n_groups, per_group):
    step = pl.program_id(0)

    @pl.when(step == 0)
    def _():
        cnt_scr[...] = cnt_in_ref[...]

    d_rnn = yr_ref.shape[1]
    y = (jnp.dot(yr_ref[...], wo_ref[:d_rnn, :], preferred_element_type=F32)
         + jnp.dot(at_ref[...], wo_ref[d_rnn:, :], preferred_element_type=F32))
    x1 = x_ref[...] + y
    x1_ref[...] = x1
    ms = jnp.mean(x1 * x1, axis=-1, keepdims=True)
    h2 = x1 * lax.rsqrt(ms + NORM_EPS) * g2_ref[...]
    _store_token_tiles(h2_ref, h2)
    logits = jnp.dot(h2.astype(BF16), wr_ref[...], preferred_element_type=F32) + br_ref[...]
    T = logits.shape[0]
    lane = lax.broadcasted_iota(jnp.int32, (T, LANES), 1).astype(F32)
    first = lambda hit: jnp.min(jnp.where(hit, lane, float(LANES)), axis=-1, keepdims=True)
    gl = jnp.where(lane < n_groups, logits, NEG_INF)
    gmax = jnp.max(gl, axis=-1, keepdims=True)
    gidx = first(gl == gmax)
    pg = 1.0 / jnp.sum(jnp.exp(gl - gmax), axis=-1, keepdims=True)
    lo = n_groups + per_group * gidx
    el = jnp.where(jnp.logical_and(lane >= lo, lane < lo + per_group), logits, NEG_INF)
    ma = jnp.max(el, axis=-1, keepdims=True)
    ia = first(el == ma)
    el2 = jnp.where(lane == ia, NEG_INF, el)
    mb = jnp.max(el2, axis=-1, keepdims=True)
    ib = first(el2 == mb)
    eb = jnp.exp(mb - ma)
    gate_a = pg / (1.0 + eb)
    gate_b = pg * eb / (1.0 + eb)
    hit_a = lane == ia
    hit_b = lane == ib
    onehot = jnp.where(jnp.logical_or(hit_a, hit_b), 1.0, 0.0)
    r_i = lax.broadcasted_iota(jnp.int32, (T, T), 0)
    c_i = lax.broadcasted_iota(jnp.int32, (T, T), 1)
    lower = jnp.where(c_i < r_i, 1.0, 0.0).astype(BF16)
    before = jnp.dot(lower, onehot.astype(BF16), preferred_element_type=F32) + cnt_scr[...]
    rank_a = jnp.sum(jnp.where(hit_a, before, 0.0), axis=-1, keepdims=True)
    rank_b = jnp.sum(jnp.where(hit_b, before, 0.0), axis=-1, keepdims=True)
    cnt_scr[...] = cnt_scr[...] + jnp.sum(onehot, axis=0, keepdims=True)
    cnt_out_ref[...] = cnt_scr[...]
    cols = (ia - n_groups, ib - n_groups, gate_a, gate_b, rank_a, rank_b)
    route = jnp.zeros((T, LANES), F32)
    for c, val in enumerate(cols):
        route = jnp.where(lane == float(c), val, route)
    route_ref[...] = route
    route_t_ref[...] = route.T[:SUBLANES]


def _outproj_route(yr, at, x, wo, g2, wr, br, cnt_in, n_groups, per_group):
    N, D = x.shape
    T = min(ROW_TILE, N)
    d_rnn = yr.shape[1]
    row = lambda w: pl.BlockSpec((T, w), lambda i: (i, 0))
    full = lambda a: pl.BlockSpec(a.shape, lambda i: (0,) * a.ndim)
    return pl.pallas_call(
        functools.partial(_outproj_kernel, n_groups=n_groups, per_group=per_group),
        grid=(N // T,),
        in_specs=[row(d_rnn), row(at.shape[1]), row(D), full(wo), full(g2), full(wr), full(br), full(cnt_in)],
        out_specs=[row(D), pl.BlockSpec((T * (D // LANES), LANES), lambda i: (i, 0)), row(LANES),
                   pl.BlockSpec((SUBLANES, T), lambda i: (0, i)),
                   pl.BlockSpec((1, LANES), lambda i: (0, 0))],
        out_shape=[jax.ShapeDtypeStruct((N, D), F32), jax.ShapeDtypeStruct((N * (D // LANES), LANES), F32),
                   jax.ShapeDtypeStruct((N, LANES), F32), jax.ShapeDtypeStruct((SUBLANES, N), F32),
                   jax.ShapeDtypeStruct((1, LANES), F32)],
        scratch_shapes=[pltpu.VMEM((1, LANES), F32)],
        compiler_params=_cparams(("arbitrary",)),
        name="outproj_route",
    )(yr, at, x, wo, g2, wr, br, cnt_in)


def _dest_kernel(seg_ref, rt_ref, o_ref):
    e = rt_ref[0:TOP_K, :].astype(jnp.int32)
    start = jnp.zeros_like(e)
    for ex in range(seg_ref.shape[0]):
        start = jnp.where(e == ex, seg_ref[ex], start)
    o_ref[...] = start + rt_ref[4:4 + TOP_K, :].astype(jnp.int32)


def _dest(seg_start, route_t):
    R, N = route_t.shape
    T = min(ROW_TILE, N)
    grid_spec = pltpu.PrefetchScalarGridSpec(
        num_scalar_prefetch=1,
        grid=(N // T,),
        in_specs=[pl.BlockSpec((R, T), lambda i, seg: (0, i))],
        out_specs=pl.BlockSpec((TOP_K, T), lambda i, seg: (0, i)),
    )
    return pl.pallas_call(
        _dest_kernel,
        grid_spec=grid_spec,
        out_shape=jax.ShapeDtypeStruct((TOP_K, N), jnp.int32),
        compiler_params=_cparams(("arbitrary",)),
        name="moe_dest",
    )(seg_start, route_t)


def _dispatch_kernel(zb_ref, dest_ref, h_ref, *rest, zero_fill):
    xb_hbm, zbuf, sem = rest[-3], rest[-2], rest[-1]
    T = dest_ref.shape[1]
    n = h_ref.shape[0] // T

    if zero_fill:
        @pl.when(pl.program_id(0) == 0)
        def _():
            zbuf[...] = jnp.zeros_like(zbuf)

            def zcopy(i):
                row0 = pl.multiple_of(zb_ref[i] * (EXP_BLOCK * n), EXP_BLOCK * n)
                return pltpu.make_async_copy(zbuf, xb_hbm.at[pl.ds(row0, EXP_BLOCK * n)], sem)

            def zstart(i, c):
                @pl.when(zb_ref[i] >= 0)
                def _():
                    zcopy(i).start()
                return c

            def zwait(i, c):
                @pl.when(zb_ref[i] >= 0)
                def _():
                    zcopy(i).wait()
                return c

            lax.fori_loop(0, zb_ref.shape[0], zstart, 0)
            lax.fori_loop(0, zb_ref.shape[0], zwait, 0)

    def body(k, c):
        t0 = pl.multiple_of(k * SUBLANES, SUBLANES)
        group = h_ref.at[pl.ds(pl.multiple_of(t0 * n, SUBLANES * n), SUBLANES * n)]
        for j in range(SUBLANES):
            for s in range(TOP_K):
                row0 = pl.multiple_of(dest_ref[s, t0 + j] * n, n)
                pltpu.make_async_copy(group.at[pl.ds(j * n, n)],
                                      xb_hbm.at[pl.ds(row0, n)], sem).start(priority=s)
        return c

    lax.fori_loop(0, T // SUBLANES, body, 0, unroll=2)
    for s in range(TOP_K):
        pltpu.make_async_copy(h_ref, xb_hbm.at[pl.ds(0, T * n)], sem).wait()


def _dispatch(zb, dest, h, xb, xb_rows):
    N = dest.shape[1]
    n = h.shape[0] // N
    T = min(DISPATCH_TILE, N)
    dest3 = jnp.transpose(dest.reshape(TOP_K, N // T, T), (1, 0, 2))
    aliased = xb is not None
    in_specs = [pl.BlockSpec((None, TOP_K, T), lambda i, zb: (i, 0, 0), memory_space=pltpu.SMEM),
                pl.BlockSpec((T * n, LANES), lambda i, zb: (i, 0))]
    args = [zb, dest3, h]
    if aliased:
        in_specs.append(pl.BlockSpec(memory_space=pl.ANY))
        args.append(xb)
    grid_spec = pltpu.PrefetchScalarGridSpec(
        num_scalar_prefetch=1,
        grid=(N // T,),
        in_specs=in_specs,
        out_specs=pl.BlockSpec(memory_space=pl.ANY),
        scratch_shapes=[pltpu.VMEM((EXP_BLOCK * n, LANES), h.dtype), pltpu.SemaphoreType.DMA(())],
    )
    return pl.pallas_call(
        functools.partial(_dispatch_kernel, zero_fill=not aliased),
        grid_spec=grid_spec,
        out_shape=jax.ShapeDtypeStruct((xb_rows * n, LANES), h.dtype),
        input_output_aliases={3: 0} if aliased else {},
        compiler_params=pltpu.CompilerParams(dimension_semantics=("arbitrary",), has_side_effects=True),
        name="moe_dispatch",
    )(*args)


def _expert_kernel(be_ref, bi_ref, br_ref, bf_ref, x_ref, w1_ref, w2_ref, y_ref, w1b, w2b):
    b = pl.program_id(0)
    rows = br_ref[b]

    @pl.when(bf_ref[b] == 1)
    def _():
        w1b[...] = w1_ref[...].astype(BF16)
        w2b[...] = w2_ref[...].astype(BF16)

    @pl.when(rows > 0)
    def _():
        x = _load_token_tiles(x_ref, EXP_BLOCK).astype(BF16)
        h = jnp.dot(x, w1b[...], preferred_element_type=F32)
        de = h.shape[1] // 2
        act = (jax.nn.silu(h[:, :de]) * h[:, de:]).astype(BF16)
        _store_token_tiles(y_ref, jnp.dot(act, w2b[...], preferred_element_type=F32))

    @pl.when(rows == 0)
    def _():
        y_ref[...] = jnp.zeros_like(y_ref)


def _experts(be, bi, br, bf, xb, w1, w2, layer):
    nb = be.shape[0]
    D, d2 = w1.shape[-2:]
    de = w2.shape[-2]
    blk = EXP_BLOCK * (D // LANES)
    grid_spec = pltpu.PrefetchScalarGridSpec(
        num_scalar_prefetch=4,
        grid=(nb,),
        in_specs=[pl.BlockSpec((blk, LANES), lambda b, be, bi, br, bf: (bi[b], 0)),
                  pl.BlockSpec((None, None, D, d2), lambda b, be, bi, br, bf: (layer, be[b], 0, 0)),
                  pl.BlockSpec((None, None, de, D), lambda b, be, bi, br, bf: (layer, be[b], 0, 0))],
        out_specs=pl.BlockSpec((blk, LANES), lambda b, be, bi, br, bf: (b, 0)),
        scratch_shapes=[pltpu.VMEM((D, d2), BF16), pltpu.VMEM((de, D), BF16)],
    )
    return pl.pallas_call(
        _expert_kernel,
        grid_spec=grid_spec,
        out_shape=jax.ShapeDtypeStruct(xb.shape, F32),
        compiler_params=_cparams(("arbitrary",)),
        name="moe_experts",
    )(be, bi, br, bf, xb, w1, w2)


def _combine_kernel(dest_ref, dest_next_ref, x1_ref, route_ref, gf_ref, yb_hbm, o_ref, ybuf, sem, *, final):
    T = x1_ref.shape[0]
    n = ybuf.shape[2] // T
    i = pl.program_id(0)
    slot = i % 2

    def issue(dref, sl):
        def body(k, c):
            t0 = pl.multiple_of(k * SUBLANES, SUBLANES)
            for s in range(TOP_K):
                group = ybuf.at[sl, s, pl.ds(pl.multiple_of(t0 * n, SUBLANES * n), SUBLANES * n)]
                for j in range(SUBLANES):
                    row0 = pl.multiple_of(dref[s, t0 + j] * n, n)
                    pltpu.make_async_copy(yb_hbm.at[pl.ds(row0, n)],
                                          group.at[pl.ds(j * n, n)], sem.at[sl]).start(priority=s)
            return c

        lax.fori_loop(0, T // SUBLANES, body, 0, unroll=2)

    @pl.when(i == 0)
    def _():
        issue(dest_ref, 0)

    @pl.when(i + 1 < pl.num_programs(0))
    def _():
        issue(dest_next_ref, 1 - slot)

    for s in range(TOP_K):
        pltpu.make_async_copy(yb_hbm.at[pl.ds(0, T * n)], ybuf.at[slot, s], sem.at[slot]).wait()
    route = route_ref[...]
    out = (x1_ref[...] + _load_token_tiles(ybuf.at[slot, 0], T) * route[:, 2:3]
           + _load_token_tiles(ybuf.at[slot, 1], T) * route[:, 3:4])
    if final:
        ms = jnp.mean(out * out, axis=-1, keepdims=True)
        out = out * lax.rsqrt(ms + NORM_EPS) * gf_ref[...]
    o_ref[...] = out


def _combine(dest, x1, route, gf, yb, final):
    N, D = x1.shape
    T = min(COMBINE_TILE, N)
    dest3 = jnp.transpose(dest.reshape(TOP_K, N // T, T), (1, 0, 2))
    last = N // T - 1
    return pl.pallas_call(
        functools.partial(_combine_kernel, final=final),
        grid=(N // T,),
        in_specs=[pl.BlockSpec((None, TOP_K, T), lambda i: (i, 0, 0), memory_space=pltpu.SMEM),
                  pl.BlockSpec((None, TOP_K, T), lambda i: (jnp.minimum(i + 1, last), 0, 0),
                               memory_space=pltpu.SMEM),
                  pl.BlockSpec((T, D), lambda i: (i, 0)),
                  pl.BlockSpec((T, LANES), lambda i: (i, 0)),
                  pl.BlockSpec((1, D), lambda i: (0, 0)),
                  pl.BlockSpec(memory_space=pl.ANY)],
        out_specs=pl.BlockSpec((T, D), lambda i: (i, 0)),
        out_shape=jax.ShapeDtypeStruct((N, D), F32),
        scratch_shapes=[pltpu.VMEM((2, TOP_K, T * (D // LANES), LANES), F32), pltpu.SemaphoreType.DMA((2,))],
        compiler_params=_cparams(("arbitrary",)),
        name="moe_combine",
    )(dest3, dest3, x1, route, gf, yb)


def _rope_tables(pos):
    inv = ROPE_THETA ** (-jnp.arange(HALF, dtype=F32) / HALF)
    ang = pos.astype(F32)[:, None] * inv[None, :]
    cos, sin = jnp.cos(ang), jnp.sin(ang)
    cq = jnp.tile(cos, (1, LANES // HALF))
    sq = jnp.concatenate([-sin, -sin, sin, sin], axis=1)
    return cq, sq, cos.T, sin.T


def _q_perm(d_att):
    idx = []
    for g in range(d_att // V_DIM):
        for part in range(2):
            for jj in range(2):
                base = HEAD_DIM * (2 * g + jj) + HALF * part
                idx.extend(range(base, base + HALF))
    return jnp.asarray(idx, jnp.int32)


def _gate_weights(w_ga, w_gx):
    nb, bw, _ = w_ga.shape
    half = nb // 2
    eye = jnp.eye(half, dtype=w_ga.dtype)

    def dense(w):
        return jnp.einsum('nij,nm->nimj', w, eye).reshape(half * bw, half * bw)

    return jnp.stack([jnp.concatenate([dense(w_ga[h * half:(h + 1) * half]),
                                       dense(w_gx[h * half:(h + 1) * half])], axis=1)
                      for h in range(2)]).astype(BF16)


def _block_tables(counts, nb):
    ne = counts.shape[0]
    nblk = (counts + EXP_BLOCK - 1) // EXP_BLOCK
    cum = jnp.sum(jnp.where(jnp.arange(ne)[None, :] <= jnp.arange(ne)[:, None], nblk[None, :], 0), axis=1)
    total = cum[-1]
    b = jnp.arange(nb, dtype=jnp.int32)
    bc = jnp.minimum(b, total - 1)
    e = jnp.minimum(jnp.sum(bc[:, None] >= cum[None, :], axis=1), ne - 1).astype(jnp.int32)
    pick = lambda tab: jnp.sum(jnp.where(e[:, None] == jnp.arange(ne)[None, :], tab[None, :], 0), axis=1)
    off = bc - (pick(cum) - pick(nblk))
    valid = b < total
    rows = jnp.where(valid, jnp.clip(pick(counts) - off * EXP_BLOCK, 0, EXP_BLOCK), 0)
    first = jnp.where(jnp.logical_and(valid, off == 0), 1, 0)
    last_blk = jnp.where(nblk > 0, cum - 1, -1)
    trail = total + jnp.arange(ne, dtype=jnp.int32)
    trail = jnp.where(trail < nb, trail, -1)
    zb = jnp.concatenate([last_blk, trail]).astype(jnp.int32)
    seg_start = ((cum - nblk) * EXP_BLOCK).astype(jnp.int32)
    return seg_start, e, bc.astype(jnp.int32), rows.astype(jnp.int32), first.astype(jnp.int32), zb


def kernel(x_prompt, x_sample, cache_k, cache_v, state_h, state_conv, page_table, ln1_g, w_in, conv_w, conv_b,
           w_ga, b_ga, w_gx, b_gx, lru_lambda, lam_q1, lam_k1, lam_q2, lam_k2, subln_g, w_out, ln2_g,
           w_route_group, b_route_group, w_route_expert, b_route_expert, w_exp_in, w_exp_out, ln_f):
    Bp, Sp, D = x_prompt.shape
    Bs = x_sample.shape[0]
    depth = w_in.shape[0]
    d_rnn = state_h.shape[-1]
    d_att = cache_k.shape[-2] * cache_k.shape[-1]
    n_groups = w_route_group.shape[-1]
    n_experts = w_route_expert.shape[-1]
    per_group = n_experts // n_groups
    n_pool, page = cache_k.shape[1], cache_k.shape[2]
    n_pages = page_table.shape[1]
    past_len = n_pages * page
    Np = Bp * Sp
    assert x_sample.shape[1] == 1 and n_groups + n_experts <= LANES

    cq, sq, ct, st = _rope_tables(jnp.arange(Sp))
    _, _, ct_s, st_s = _rope_tables(jnp.full((Bs,), past_len))
    perm = _q_perm(d_att)
    qscale = HEAD_DIM ** -0.5
    ck = jnp.transpose(cache_k, (0, 1, 3, 4, 2)).reshape(depth, n_pool, d_att, page)
    cv = cache_v.reshape(depth, n_pool, page * (d_att // V_DIM), V_DIM)
    pt_flat = page_table.reshape(-1).astype(jnp.int32)
    conv_s = jnp.transpose(state_conv, (0, 2, 1, 3))

    n_blocks = (TOP_K * (Np + Bs)) // EXP_BLOCK + n_experts

    xp = x_prompt
    xs = x_sample.reshape(Bs, D)
    kt_all = v_all = None
    hp_l, cp_l, ks_l, vs_l, hs_l, cs_l = [], [], [], [], [], []
    for l in range(depth):
        lambda_init = 0.8 - 0.6 * math.exp(-0.3 * l)
        out_scale = 1.0 - lambda_init
        lam = (jnp.exp(jnp.sum(lam_q1[l] * lam_k1[l])) - jnp.exp(jnp.sum(lam_q2[l] * lam_k2[l]))
               + lambda_init).reshape(1, 1).astype(F32)
        wl = w_in[l]
        w_xg = wl[:, :2 * d_rnn]
        w_q = wl[:, 2 * d_rnn:2 * d_rnn + d_att] * qscale
        w_k = wl[:, 2 * d_rnn + d_att:2 * d_rnn + 2 * d_att]
        w_v = wl[:, 2 * d_rnn + 2 * d_att:]
        wm_p = jnp.concatenate([w_xg, w_k[:, perm], w_v], axis=1).astype(BF16)
        wt_p = jnp.concatenate([w_q[:, perm], w_k, w_v], axis=1).T.astype(BF16)
        wm_s = jnp.concatenate([w_xg, w_v], axis=1).astype(BF16)
        wt_s = jnp.concatenate([w_k, w_q], axis=1).T.astype(BF16)
        g1 = ln1_g[l].reshape(1, D)
        wg = _gate_weights(w_ga[l], w_gx[l])
        bga = b_ga[l].reshape(1, d_rnn)
        bgx = b_gx[l].reshape(1, d_rnn)
        lru_l = lru_lambda[l].reshape(1, d_rnn)
        cw = conv_w[l]
        cb = conv_b[l].reshape(1, d_rnn)
        sg = subln_g[l].reshape(1, V_DIM)
        wo = w_out[l].astype(BF16)
        g2 = ln2_g[l].reshape(1, D)
        wr = jnp.zeros((D, LANES), F32).at[:, :n_groups].set(w_route_group[l]).at[
            :, n_groups:n_groups + n_experts].set(w_route_expert[l])
        wr = wr.astype(BF16)
        br =jnp.zeros((1, LANES), F32).at[0, :n_groups].set(b_route_group[l]).at[
            0, n_groups:n_groups + n_experts].set(b_route_expert[l])

        xr, gr, kb, v_all, qt, kt_all, vt = _inproj_prompt(xp, g1, wm_p, wt_p, cq, sq, ct, st, kt_all, v_all)
        y_rnn, h_last, c_tail = _lru_prompt(xr, gr, cw, cb, wg, bga, bgx, lru_l)
        xr_s, gr_s, v_s, kt_s, qt_s = _inproj_sample(xs, g1, wm_s, wt_s, ct_s, st_s)
        y_rnn_s, h_s, c_s = _lru_step(xr_s, gr_s, conv_s[l], state_h[l], cw, cb, wg, bga, bgx, lru_l)
        k_s = kt_s.T
        att, att_s = _attn_prompt_and_sample(qt, kb, vt, sg, lam, out_scale,
                                             pt_flat, qt_s, kt_s, v_s, ck, cv, l)
        cnt0 = jnp.zeros((1, LANES), F32)
        x1, h2, route, route_t, cnt1 = _outproj_route(
            y_rnn.reshape(Np, d_rnn), att.reshape(Np, d_att), xp.reshape(Np, D), wo, g2, wr, br, cnt0,
            n_groups, per_group)
        x1_s, h2_s, route_s, route_ts, cnt2 = _outproj_route(y_rnn_s, att_s, xs, wo, g2, wr, br, cnt1,
                                                             n_groups, per_group)
        counts = cnt2[0, n_groups:n_groups + n_experts].astype(jnp.int32)
        seg_start, be, bi, brows, bfirst, zb = _block_tables(counts, n_blocks)

        dest_p, dest_s = _dest(seg_start, route_t), _dest(seg_start, route_ts)
        xb = _dispatch(zb, dest_p, h2, None, n_blocks * EXP_BLOCK)
        xb = _dispatch(zb, dest_s, h2_s, xb, n_blocks * EXP_BLOCK)
        yb = _experts(be, bi, brows, bfirst, xb, w_exp_in, w_exp_out, l)
        final = l == depth - 1
        gf = ln_f.reshape(1, D)
        xp_new = _combine(dest_p, x1, route, gf, yb, final)
        xs = _combine(dest_s, x1_s, route_s, gf, yb, final)
        xp = xp_new.reshape(Bp, Sp, D)

        n_qk = d_att // HEAD_DIM
        hp_l.append(h_last.reshape(Bp, d_rnn))
        cp_l.append(c_tail)
        ks_l.append(k_s.reshape(Bs, 1, n_qk, HEAD_DIM))
        vs_l.append(v_s.reshape(Bs, 1, d_att // V_DIM, V_DIM))
        hs_l.append(h_s)
        cs_l.append(jnp.transpose(c_s, (1, 0, 2)))
    k_prompt = jnp.transpose(kt_all.reshape(depth, Bp, n_qk, HEAD_DIM, Sp), (0, 1, 4, 2, 3))
    v_prompt = v_all.reshape(depth, Bp, Sp, d_att // V_DIM, V_DIM)
    return (xp, xs.reshape(Bs, 1, D),
            k_prompt, v_prompt, jnp.stack(hp_l), jnp.stack(cp_l),
            jnp.stack(ks_l), jnp.stack(vs_l), jnp.stack(hs_l), jnp.stack(cs_l))
```

```python
import functools
import math

import jax
import jax.numpy as jnp
from jax import lax
from jax.experimental import pallas as pl
from jax.experimental.pallas import tpu as pltpu

F32 = jnp.float32
BF16 = jnp.bfloat16

HEAD_DIM = 64
V_DIM = 2 * HEAD_DIM
HALF = HEAD_DIM // 2
ONES_ROWS = 16
CONV_W = 4
LRU_C = 8.0
ROPE_THETA = 10000.0
NORM_EPS = 1e-6
SUBLN_EPS = 1e-5
NEG_INF = -1e30
TOP_K = 2

LANES = 128
SUBLANES = 8
VMEM_LIMIT = 48 * 1024 * 1024
INPROJ_VMEM_LIMIT = 56 * 1024 * 1024

ROW_TILE = 512
LRU_TILE = 256
ATT_TILE = 512
EXP_BLOCK = 512
DISPATCH_TILE = 512
COMBINE_TILE = 256


def _cparams(sem, vmem_limit=VMEM_LIMIT):
    return pltpu.CompilerParams(dimension_semantics=sem, vmem_limit_bytes=vmem_limit)


def _inproj_kernel(x_ref, g_ref, wm_ref, wt_ref, cq_ref, sq_ref, ct_ref, st_ref, *rest, sample, n_prev=0):
    prev, outs = rest[:2 if n_prev else 0], rest[2 if n_prev else 0:]
    x = x_ref[...]
    ms = jnp.mean(x * x, axis=-1, keepdims=True)
    h = (x * lax.rsqrt(ms + NORM_EPS) * g_ref[...]).astype(BF16)
    main = jnp.dot(h, wm_ref[...], preferred_element_type=F32)
    d_rnn = outs[0].shape[1]
    ct = ct_ref[...]
    st = st_ref[...]
    tr = lax.dot_general(wt_ref[...], h, (((1,), (1,)), ((), ())), preferred_element_type=F32)

    def rope_t(blk):
        x1 = blk[:HALF]
        x2 = blk[HALF:]
        return x1 * ct - x2 * st, x2 * ct + x1 * st

    if sample:
        xr_ref, gr_ref, v_ref, kt_ref, qt_ref = outs
        d_att = kt_ref.shape[0]
        xr_ref[...] = main[:, :d_rnn]
        gr_ref[...] = main[:, d_rnn:2 * d_rnn]
        v_ref[...] = main[:, 2 * d_rnn + d_att:]
        for hh in range(d_att // HEAD_DIM):
            g, jj = divmod(hh, 2)
            r1 = g * V_DIM + jj * HALF
            x1 = tr[r1:r1 + HALF]
            x2 = tr[r1 + HEAD_DIM:r1 + HEAD_DIM + HALF]
            qt_ref[hh * HEAD_DIM:hh * HEAD_DIM + HALF, :] = x1 * ct - x2 * st
            qt_ref[hh * HEAD_DIM + HALF:(hh + 1) * HEAD_DIM, :] = x2 * ct + x1 * st
            base = d_att + hh * HEAD_DIM
            o1, o2 = rope_t(tr[base:base + HEAD_DIM])
            kt_ref[hh * HEAD_DIM:hh * HEAD_DIM + HALF, :] = o1
            kt_ref[hh * HEAD_DIM + HALF:(hh + 1) * HEAD_DIM, :] = o2
    else:
        xr_ref, gr_ref, kb_ref, v_ref, qt_ref, kt_ref, vt_ref = outs
        xr_ref[...] = main[:, :d_rnn]
        gr_ref[...] = main[:, d_rnn:2 * d_rnn]
        d_att = kb_ref.shape[1]
        k = main[:, 2 * d_rnn:2 * d_rnn + d_att]
        cq = cq_ref[...]
        sq = sq_ref[...]
        for c in range(d_att // LANES):
            kc = k[:, c * LANES:(c + 1) * LANES]
            kb_ref[:, c * LANES:(c + 1) * LANES] = (kc * cq + pltpu.roll(kc, HEAD_DIM, 1) * sq).astype(BF16)
        if n_prev:
            kt_ref[:n_prev] = prev[0][...]
            v_ref[:n_prev] = prev[1][...]
        v = main[:, 2 * d_rnn + d_att:]
        T = v.shape[0]
        G = d_att // V_DIM
        for g in range(G):
            v_ref[n_prev, pl.ds(g, T, stride=G), :] = v[:, g * V_DIM:(g + 1) * V_DIM]
        n_heads = d_att // HEAD_DIM
        for hh in range(n_heads):
            g, jj = divmod(hh, 2)
            r1 = g * V_DIM + jj * HALF
            x1 = tr[r1:r1 + HALF]
            x2 = tr[r1 + HEAD_DIM:r1 + HEAD_DIM + HALF]
            qt_ref[r1:r1 + HALF, :] = (x1 * ct - x2 * st).astype(BF16)
            qt_ref[r1 + HEAD_DIM:r1 + HEAD_DIM + HALF, :] = (x2 * ct + x1 * st).astype(BF16)
            base = d_att + hh * HEAD_DIM
            o1, o2 = rope_t(tr[base:base + HEAD_DIM])
            kt_ref[n_prev, hh * HEAD_DIM:hh * HEAD_DIM + HALF, :] = o1
            kt_ref[n_prev, hh * HEAD_DIM + HALF:(hh + 1) * HEAD_DIM, :] = o2
        for g in range(G):
            r0 = g * (V_DIM + ONES_ROWS)
            vt_ref[r0:r0 + V_DIM, :] = tr[2 * d_att + g * V_DIM:2 * d_att + (g + 1) * V_DIM].astype(BF16)
            vt_ref[r0 + V_DIM:r0 + V_DIM + ONES_ROWS, :] = jnp.ones((ONES_ROWS, T), BF16)


def _inproj_prompt(x, g, wm, wt, cq, sq, ct, st, k_prev, v_prev):
    B, S, D = x.shape
    T = min(ROW_TILE, S)
    nt = S // T
    d_att = wt.shape[0] // 3
    d_rnn = (wm.shape[1] - 2 * d_att) // 2
    G = d_att // V_DIM
    n_prev = 0 if k_prev is None else k_prev.shape[0]
    row = lambda w: pl.BlockSpec((None, T, w), lambda b, t: (b, t, 0))
    full = lambda a: pl.BlockSpec(a.shape, lambda b, t: (0,) * a.ndim)
    once = lambda a: pl.BlockSpec(a.shape, lambda b, t: (0,) * a.ndim, pipeline_mode=pl.Buffered(1))
    tr_spec = pl.BlockSpec((None, d_att, T), lambda b, t: (b, 0, t))
    kstack = lambda n: pl.BlockSpec((n, None, d_att, T), lambda b, t: (0, b, 0, t))
    vstack = lambda n: pl.BlockSpec((n, None, T * G, V_DIM), lambda b, t: (0, b, t, 0))
    prev_specs = [kstack(n_prev), vstack(n_prev)] if n_prev else []
    prev_args = [k_prev, v_prev] if n_prev else []
    return pl.pallas_call(
        functools.partial(_inproj_kernel, sample=False, n_prev=n_prev),
        grid=(B, nt),
        in_specs=[row(D), full(g), once(wm), once(wt),
                  pl.BlockSpec((T, LANES), lambda b, t: (t, 0)),
                  pl.BlockSpec((T, LANES), lambda b, t: (t, 0)),
                  pl.BlockSpec((HALF, T), lambda b, t: (0, t)),
                  pl.BlockSpec((HALF, T), lambda b, t: (0, t))] + prev_specs,
        out_specs=[row(d_rnn), row(d_rnn), row(d_att), vstack(n_prev + 1), tr_spec, kstack(n_prev + 1),
                   pl.BlockSpec((None, G * (V_DIM + ONES_ROWS), T), lambda b, t: (b, 0, t))],
        out_shape=[jax.ShapeDtypeStruct((B, S, d_rnn), F32), jax.ShapeDtypeStruct((B, S, d_rnn), F32),
                   jax.ShapeDtypeStruct((B, S, d_att), BF16),
                   jax.ShapeDtypeStruct((n_prev + 1, B, S * G, V_DIM), F32),
                   jax.ShapeDtypeStruct((B, d_att, S), BF16),
                   jax.ShapeDtypeStruct((n_prev + 1, B, d_att, S), F32),
                   jax.ShapeDtypeStruct((B, G * (V_DIM + ONES_ROWS), S), BF16)],
        compiler_params=_cparams(("parallel", "arbitrary"), INPROJ_VMEM_LIMIT),
        name="inproj_prompt",
    )(x, g, wm, wt, cq, sq, ct, st, *prev_args)


def _inproj_sample(x, g, wm, wt, ct, st):
    n, D = x.shape
    d_att = wt.shape[0] // 3
    d_rnn = (wm.shape[1] - 2 * d_att) // 2
    full = lambda a: pl.BlockSpec(a.shape, lambda i: (0,) * a.ndim)
    dummy = jnp.zeros((SUBLANES, LANES), F32)
    out_shapes = [jax.ShapeDtypeStruct((n, d_rnn), F32), jax.ShapeDtypeStruct((n, d_rnn), F32),
                  jax.ShapeDtypeStruct((n, d_att), F32), jax.ShapeDtypeStruct((d_att, n), F32),
                  jax.ShapeDtypeStruct((d_att, n), F32)]
    return pl.pallas_call(
        functools.partial(_inproj_kernel, sample=True),
        grid=(1,),
        in_specs=[full(x), full(g), full(wm), full(wt), full(dummy), full(dummy), full(ct), full(st)],
        out_specs=[pl.BlockSpec(s.shape, lambda i: (0, 0)) for s in out_shapes],
        out_shape=out_shapes,
        compiler_params=_cparams(("arbitrary",)),
        name="inproj_sample",
    )(x, g, wm, wt, dummy, dummy, ct, st)


def _gates(xc, wg_ref, bga, bgx, lam):
    half = xc.shape[1] // 2
    xcb = xc.astype(BF16)
    g0 = jnp.dot(xcb[:, :half], wg_ref[0], preferred_element_type=F32)
    g1 = jnp.dot(xcb[:, half:], wg_ref[1], preferred_element_type=F32)
    r = jax.nn.sigmoid(jnp.concatenate([g0[:, :half], g1[:, :half]], axis=1) + bga)
    i = jax.nn.sigmoid(jnp.concatenate([g0[:, half:], g1[:, half:]], axis=1) + bgx)
    neg = -lam
    softplus = jnp.maximum(neg, 0.0) + jnp.log1p(jnp.exp(-jnp.abs(neg)))
    log_a = -LRU_C * r * softplus
    a = jnp.exp(log_a)
    mult = jnp.sqrt(1.0 - a * a)
    return a, i, mult


def _lru_kernel(xr_ref, gr_ref, cw_ref, cb_ref, wg_ref, bga_ref, bgx_ref, lam_ref,
                y_ref, hlast_ref, ctail_ref, xp_scr, a_scr, b_scr, h_scr):
    t = pl.program_id(1)
    T, C = xr_ref.shape

    @pl.when(t == 0)
    def _():
        xp_scr[0:SUBLANES, :] = jnp.zeros((SUBLANES, C), F32)
        h_scr[...] = jnp.zeros_like(h_scr)

    x = xr_ref[...]
    xp_scr[SUBLANES:SUBLANES + T, :] = x
    cw = cw_ref[...]
    xc = cb_ref[...]
    for j in range(CONV_W - 1):
        xc = xc + xp_scr[pl.ds(SUBLANES - (CONV_W - 1) + j, T), :] * cw[j:j + 1]
    xc = xc + x * cw[CONV_W - 1:CONV_W]
    tail = x[T - (CONV_W - 1):, :]
    xp_scr[SUBLANES - (CONV_W - 1):SUBLANES, :] = tail
    ctail_ref[...] = tail

    a, i, mult = _gates(xc, wg_ref, bga_ref[...], bgx_ref[...], lam_ref[...])
    row = lax.broadcasted_iota(jnp.int32, (T, 1), 0)
    mult = jnp.where(jnp.logical_and(row == 0, t == 0), 1.0, mult)
    a_scr[...] = a
    b_scr[...] = mult * i * xc

    rowi = lax.broadcasted_iota(jnp.int32, (SUBLANES, C), 0)

    def body(gi, h):
        off = pl.multiple_of(gi * SUBLANES, SUBLANES)
        A = a_scr[pl.ds(off, SUBLANES), :]
        Bv = b_scr[pl.ds(off, SUBLANES), :]
        for d in (1, 2, 4):
            keep = rowi >= d
            Bv = jnp.where(keep, Bv + A * pltpu.roll(Bv, d, 0), Bv)
            A = jnp.where(keep, A * pltpu.roll(A, d, 0), A)
        hrows = Bv + A * h
        b_scr[pl.ds(off, SUBLANES), :] = hrows
        return hrows[SUBLANES - 1:SUBLANES, :]

    h = lax.fori_loop(0, T // SUBLANES, body, h_scr[...], unroll=2)
    h_scr[...] = h
    hlast_ref[...] = h
    y_ref[...] = (b_scr[...] * jax.nn.gelu(gr_ref[...])).astype(BF16)


def _lru_prompt(xr, gr, cw, cb, wg, bga, bgx, lam):
    B, S, C = xr.shape
    T = min(LRU_TILE, S)
    nt = S // T
    row = pl.BlockSpec((None, T, C), lambda b, t: (b, t, 0))
    full = lambda a: pl.BlockSpec(a.shape, lambda b, t: (0,) * a.ndim)
    return pl.pallas_call(
        _lru_kernel,
        grid=(B, nt),
        in_specs=[row, row, full(cw), full(cb), full(wg), full(bga), full(bgx), full(lam)],
        out_specs=[row, pl.BlockSpec((None, 1, C), lambda b, t: (b, 0, 0)),
                   pl.BlockSpec((None, CONV_W - 1, C), lambda b, t: (b, 0, 0))],
        out_shape=[jax.ShapeDtypeStruct((B, S, C), BF16), jax.ShapeDtypeStruct((B, 1, C), F32),
                   jax.ShapeDtypeStruct((B, CONV_W - 1, C), F32)],
        scratch_shapes=[pltpu.VMEM((T + SUBLANES, C), F32), pltpu.VMEM((T, C), F32),
                        pltpu.VMEM((T, C), F32), pltpu.VMEM((1, C), F32)],
        compiler_params=_cparams(("parallel", "arbitrary")),
        name="rglru_prompt",
    )(xr, gr, cw, cb, wg, bga, bgx, lam)


def _lru_step_kernel(xr_ref, gr_ref, conv_ref, h0_ref, cw_ref, cb_ref, wg_ref, bga_ref, bgx_ref, lam_ref,
                     y_ref, hnew_ref, cnew_ref):
    x = xr_ref[...]
    cw = cw_ref[...]
    xc = cb_ref[...]
    for j in range(CONV_W - 1):
        xc = xc + conv_ref[j] * cw[j:j + 1]
    xc = xc + x * cw[CONV_W - 1:CONV_W]
    for j in range(CONV_W - 2):
        cnew_ref[j] = conv_ref[j + 1]
    cnew_ref[CONV_W - 2] = x
    a, i, mult = _gates(xc, wg_ref, bga_ref[...], bgx_ref[...], lam_ref[...])
    h = a * h0_ref[...] + mult * i * xc
    hnew_ref[...] = h
    y_ref[...] = (h * jax.nn.gelu(gr_ref[...])).astype(BF16)


def _lru_step(xr, gr, conv, h0, cw, cb, wg, bga, bgx, lam):
    n, C = xr.shape
    args = (xr, gr, conv, h0, cw, cb, wg, bga, bgx, lam)
    full = lambda a: pl.BlockSpec(a.shape, lambda i: (0,) * a.ndim)
    out_shapes = [jax.ShapeDtypeStruct((n, C), BF16), jax.ShapeDtypeStruct((n, C), F32),
                  jax.ShapeDtypeStruct((CONV_W - 1, n, C), F32)]
    return pl.pallas_call(
        _lru_step_kernel,
        grid=(1,),
        in_specs=[full(a) for a in args],
        out_specs=[pl.BlockSpec(s.shape, lambda i, nd=len(s.shape): (0,) * nd) for s in out_shapes],
        out_shape=out_shapes,
        compiler_params=_cparams(("arbitrary",)),
        name="rglru_step",
    )(*args)


def _subln(o, g, scale):
    ms = jnp.mean(o * o, axis=-1, keepdims=True)
    return o * lax.rsqrt(ms + SUBLN_EPS) * g * scale


def _attn_kernel(lam_ref, qt_ref, kb_ref, vt_ref, g_ref, o_ref, qcat, m, acc, s_even, s_odd, *, out_scale,
                 between=None):
    qi = pl.program_id(1)
    Tq = qt_ref.shape[1]
    Tk = Tq

    qt = qt_ref[...]
    row = lax.broadcasted_iota(jnp.int32, qt.shape, 0)
    second = (row & HALF) != 0
    zero = jnp.zeros_like(qt)
    qcat[:, :Tq] = jnp.where(second, zero, qt)
    qcat[:, Tq:] = jnp.where(second, qt, zero)
    m[...] = jnp.full_like(m, NEG_INF)
    acc[...] = jnp.zeros_like(acc)

    def scores(kj, sbuf):
        k0 = pl.multiple_of(kj * Tk, Tk)
        sbuf[...] = jnp.dot(kb_ref[pl.ds(k0, Tk), :], qcat[...], preferred_element_type=F32)

    def update(kj, sbuf, diagonal):
        k0 = pl.multiple_of(kj * Tk, Tk)
        s = sbuf[...]
        if diagonal:
            key = lax.broadcasted_iota(jnp.int32, s.shape, 0)
            qry = lax.broadcasted_iota(jnp.int32, s.shape, 1)
            qry = jnp.where(qry >= Tq, qry - Tq, qry)
            s = jnp.where(key <= qry, s, NEG_INF)
        m_old = m[...]
        m_new = jnp.maximum(m_old, jnp.max(s, axis=0, keepdims=True))
        alpha = jnp.exp(m_old - m_new)
        p = jnp.exp(s - m_new)
        acc[...] = alpha * acc[...] + jnp.dot(vt_ref[:, pl.ds(k0, Tk)], p.astype(BF16),
                                              preferred_element_type=F32)
        m[...] = m_new

    def finish():
        a = acc[:V_DIM, :] / acc[V_DIM:V_DIM + 1, :]
        o = a[:, :Tq] - lam_ref[0, 0] * a[:, Tq:]
        ms = jnp.mean(o * o, axis=0, keepdims=True)
        o = o * lax.rsqrt(ms + SUBLN_EPS) * (g_ref[...] * out_scale)
        o_ref[...] = o.T.astype(o_ref.dtype)

    scores(0, s_even)
    if between is not None:
        between()

    def pair(i, c):
        t = 2 * i
        scores(t + 1, s_odd)
        update(t, s_even, False)
        scores(t + 2, s_even)
        update(t + 1, s_odd, False)
        return c

    lax.fori_loop(0, qi // 2, pair, 0)

    @pl.when(qi % 2 == 0)
    def _():
        update(qi, s_even, True)
        finish()

    @pl.when(qi % 2 == 1)
    def _():
        scores(qi, s_odd)
        update(qi - 1, s_even, False)
        update(qi, s_odd, True)
        finish()


def _attn_prompt(qt, kb, vt, subln_g, lam, out_scale):
    B, S, d_att = kb.shape
    G = d_att // V_DIM
    T = min(ATT_TILE, S)
    return pl.pallas_call(
        functools.partial(_attn_kernel, out_scale=out_scale),
        grid=(B * G, S // T),
        in_specs=[
            pl.BlockSpec(memory_space=pltpu.SMEM),
            pl.BlockSpec((None, V_DIM, T), lambda bg, i: (bg // G, bg % G, i)),
            pl.BlockSpec((None, S, V_DIM), lambda bg, i: (bg // G, 0, bg % G)),
            pl.BlockSpec((None, V_DIM + ONES_ROWS, S), lambda bg, i: (bg // G, bg % G, 0)),
            pl.BlockSpec((V_DIM, 1), lambda bg, i: (0, 0)),
        ],
        out_specs=pl.BlockSpec((None, T, V_DIM), lambda bg, i: (bg // G, i, bg % G)),
        out_shape=jax.ShapeDtypeStruct((B, S, d_att), BF16),
        scratch_shapes=[pltpu.VMEM((V_DIM, 2 * T), BF16), pltpu.VMEM((1, 2 * T), F32),
                        pltpu.VMEM((V_DIM + ONES_ROWS, 2 * T), F32),
                        pltpu.VMEM((T, 2 * T), F32), pltpu.VMEM((T, 2 * T), F32)],
        compiler_params=_cparams(("parallel", "arbitrary")),
        name="diff_attn_prompt",
    )(lam, qt, kb, vt, subln_g.reshape(V_DIM, 1))


def _attn_sample_body(seq, lam_ref, qt_ref, kt_ref, vc_ref, g_ref, *rest, n_pages, out_scale):
    k_refs = rest[:n_pages]
    v_refs = rest[n_pages:2 * n_pages]
    o_ref = rest[2 * n_pages]
    lam = lam_ref[0, 0]
    d_att, n_lanes = qt_ref.shape
    n_heads = d_att // HEAD_DIM
    G = d_att // V_DIM
    R = 2 * SUBLANES
    mine = lax.broadcasted_iota(jnp.int32, (d_att, n_lanes), 1) == seq % n_lanes
    qcol = jnp.sum(jnp.where(mine, qt_ref[...], 0.0), axis=1, keepdims=True)
    kcol = jnp.sum(jnp.where(mine, kt_ref[...], 0.0), axis=1, keepdims=True)
    P = k_refs[0].shape[1]
    qcolb = jnp.broadcast_to(qcol, (d_att, P))
    per_head = lambda x: jnp.sum(x.reshape(n_heads, HEAD_DIM, x.shape[-1]), axis=1)
    s = jnp.concatenate([per_head(k_refs[j][...] * qcolb) for j in range(n_pages)], axis=1)
    s_cur = per_head(qcol * kcol)
    m = jnp.maximum(jnp.max(s, axis=-1, keepdims=True), s_cur)
    p = jnp.exp(s - m)
    p_cur = jnp.exp(s_cur - m)
    inv_l = 1.0 / (jnp.sum(p, axis=-1, keepdims=True) + p_cur)
    p = p * inv_l
    p_cur = p_cur * inv_l
    amap = p - lam * pltpu.roll(p, n_heads - 1, 0)
    amap = jnp.concatenate([amap, jnp.zeros_like(amap)], axis=0).astype(BF16)
    p_cur = jnp.broadcast_to(p_cur, (n_heads, LANES))
    amap_cur = p_cur - lam * pltpu.roll(p_cur, n_heads - 1, 0)
    outs = []
    for g in range(G):
        acc = jnp.zeros((R, V_DIM), F32)
        for j in range(n_pages):
            vjg = v_refs[j][pl.ds(g, P, stride=G), :].astype(BF16)
            acc = acc + jnp.dot(amap[:, j * P:(j + 1) * P], vjg, preferred_element_type=F32)
        o = acc[2 * g:2 * g + 1, :] + amap_cur[2 * g:2 * g + 1, :] * vc_ref[:, g * V_DIM:(g + 1) * V_DIM]
        outs.append(_subln(o, g_ref[...], out_scale))
    o_ref[...] = jnp.concatenate(outs, axis=1).astype(o_ref.dtype)


def _attn_sample_kernel(pt_ref, *refs, n_pages, out_scale):
    _attn_sample_body(pl.program_id(0), *refs, n_pages=n_pages, out_scale=out_scale)


def _attn_sample(pt_flat, lam, qt_s, kt_s, v_cur, subln_g, ck, cv, layer, out_scale):
    d_att, n = qt_s.shape
    n_pages = pt_flat.shape[0] // n
    P = ck.shape[-1]
    nl = min(n, LANES)
    vc3 = v_cur.reshape(n, 1, d_att)
    cols = lambda: pl.BlockSpec((d_att, nl), lambda b, pt: (0, b // nl))
    k_specs = [pl.BlockSpec((None, None, d_att, P), functools.partial(
        lambda b, pt, j: (layer, pt[b * n_pages + j], 0, 0), j=j)) for j in range(n_pages)]
    v_specs = [pl.BlockSpec((None, None, cv.shape[2], V_DIM), functools.partial(
        lambda b, pt, j: (layer, pt[b * n_pages + j], 0, 0), j=j)) for j in range(n_pages)]
    grid_spec = pltpu.PrefetchScalarGridSpec(
        num_scalar_prefetch=1,
        grid=(n,),
        in_specs=[pl.BlockSpec(memory_space=pltpu.SMEM), cols(), cols(),
                  pl.BlockSpec((None, 1, d_att), lambda b, pt: (b, 0, 0)),
                  pl.BlockSpec((1, V_DIM), lambda b, pt: (0, 0))] + k_specs + v_specs,
        out_specs=pl.BlockSpec((None, 1, d_att), lambda b, pt: (b, 0, 0)),
    )
    out = pl.pallas_call(
        functools.partial(_attn_sample_kernel, n_pages=n_pages, out_scale=out_scale),
        grid_spec=grid_spec,
        out_shape=jax.ShapeDtypeStruct((n, 1, d_att), BF16),
        compiler_params=_cparams(("arbitrary",)),
        name="diff_attn_sample",
    )(pt_flat, lam, qt_s, kt_s, vc3, subln_g, *([ck] * n_pages), *([cv] * n_pages))
    return out.reshape(n, d_att)


def _attn_fused_kernel(pt_ref, lam_ref, qt_ref, kb_ref, vt_ref, gcol_ref, qts_ref, kts_ref, vc_ref, g_ref, *rest,
                       n_pages, out_scale):
    pages = rest[:2 * n_pages]
    o_ref, os_ref, qcat, m, acc, s_even, s_odd = rest[2 * n_pages:]
    seq = pl.program_id(0) * pl.num_programs(1) + pl.program_id(1)
    sample = functools.partial(_attn_sample_body, seq, lam_ref, qts_ref, kts_ref, vc_ref, g_ref, *pages, os_ref,
                               n_pages=n_pages, out_scale=out_scale)
    _attn_kernel(lam_ref, qt_ref, kb_ref, vt_ref, gcol_ref, o_ref, qcat, m, acc, s_even, s_odd,
                 out_scale=out_scale, between=sample)


def _attn_prompt_and_sample(qt, kb, vt, subln_g, lam, out_scale, pt_flat, qt_s, kt_s, v_cur, ck, cv, layer):
    B, S, d_att = kb.shape
    G = d_att // V_DIM
    T = min(ATT_TILE, S)
    nq = S // T
    n = qt_s.shape[1]
    if n != B * G * nq:
        return (_attn_prompt(qt, kb, vt, subln_g, lam, out_scale),
                _attn_sample(pt_flat, lam, qt_s, kt_s, v_cur, subln_g, ck, cv, layer, out_scale))
    n_pages = pt_flat.shape[0] // n
    P = ck.shape[-1]
    nl = min(n, LANES)
    seq = lambda bg, i: bg * nq + i
    cols = lambda: pl.BlockSpec((d_att, nl), lambda bg, i, pt: (0, seq(bg, i) // nl))
    k_specs = [pl.BlockSpec((None, None, d_att, P), functools.partial(
        lambda bg, i, pt, j: (layer, pt[seq(bg, i) * n_pages + j], 0, 0), j=j)) for j in range(n_pages)]
    v_specs = [pl.BlockSpec((None, None, cv.shape[2], V_DIM), functools.partial(
        lambda bg, i, pt, j: (layer, pt[seq(bg, i) * n_pages + j], 0, 0), j=j)) for j in range(n_pages)]
    grid_spec = pltpu.PrefetchScalarGridSpec(
        num_scalar_prefetch=1,
        grid=(B * G, nq),
        in_specs=[
            pl.BlockSpec(memory_space=pltpu.SMEM),
            pl.BlockSpec((None, V_DIM, T), lambda bg, i, pt: (bg // G, bg % G, i)),
            pl.BlockSpec((None, S, V_DIM), lambda bg, i, pt: (bg // G, 0, bg % G)),
            pl.BlockSpec((None, V_DIM + ONES_ROWS, S), lambda bg, i, pt: (bg // G, bg % G, 0)),
            pl.BlockSpec((V_DIM, 1), lambda bg, i, pt: (0, 0)),
            cols(), cols(),
            pl.BlockSpec((None, 1, d_att), lambda bg, i, pt: (seq(bg, i), 0, 0)),
            pl.BlockSpec((1, V_DIM), lambda bg, i, pt: (0, 0)),
        ] + k_specs + v_specs,
        out_specs=[pl.BlockSpec((None, T, V_DIM), lambda bg, i, pt: (bg // G, i, bg % G)),
                   pl.BlockSpec((None, 1, d_att), lambda bg, i, pt: (seq(bg, i), 0, 0))],
        scratch_shapes=[pltpu.VMEM((V_DIM, 2 * T), BF16), pltpu.VMEM((1, 2 * T), F32),
                        pltpu.VMEM((V_DIM + ONES_ROWS, 2 * T), F32),
                        pltpu.VMEM((T, 2 * T), F32), pltpu.VMEM((T, 2 * T), F32)],
    )
    att, att_s = pl.pallas_call(
        functools.partial(_attn_fused_kernel, n_pages=n_pages, out_scale=out_scale),
        grid_spec=grid_spec,
        out_shape=[jax.ShapeDtypeStruct((B, S, d_att), BF16), jax.ShapeDtypeStruct((n, 1, d_att), BF16)],
        compiler_params=_cparams(("parallel", "arbitrary")),
        name="diff_attn_fused",
    )(pt_flat, lam, qt, kb, vt, subln_g.reshape(V_DIM, 1),
      qt_s, kt_s, v_cur.reshape(n, 1, d_att), subln_g,
      *([ck] * n_pages), *([cv] * n_pages))
    return att, att_s.reshape(n, d_att)


def _store_token_tiles(ref, x):
    T, D = x.shape
    n = D // LANES
    for c in range(n):
        ref[pl.ds(c, T, stride=n), :] = x[:, c * LANES:(c + 1) * LANES]


def _load_token_tiles(ref, T):
    n = ref.shape[0] // T
    return jnp.concatenate([ref[pl.ds(c, T, stride=n), :] for c in range(n)], axis=1)


def _outproj_kernel(yr_ref, at_ref, x_ref, wo_ref, g2_ref, wr_ref, br_ref, cnt_in_ref,
                    x1_ref, h2_ref, route_ref, route_t_ref, cnt_out_ref, cnt_scr, *, n_groups, per_group):
    step = pl.program_id(0)

    @pl.when(step == 0)
    def _():
        cnt_scr[...] = cnt_in_ref[...]

    cnt = _outproj_rows(pl.ds(0, x_ref.shape[0]), cnt_scr[...], yr_ref, at_ref, x_ref, wo_ref,
                        g2_ref, wr_ref, br_ref, x1_ref, h2_ref, route_ref, route_t_ref,
                        n_groups=n_groups, per_group=per_group)
    cnt_scr[...] = cnt
    cnt_out_ref[...] = cnt


def _outproj_rows(rows, cnt, yr_ref, at_ref, x_ref, wo_ref, g2_ref, wr_ref, br_ref,
                  x1_ref, h2_ref, route_ref, route_t_ref, *, n_groups, per_group):
    d_rnn = yr_ref.shape[1]
    y = (jnp.dot(yr_ref[rows, :], wo_ref[:d_rnn, :], preferred_element_type=F32)
         + jnp.dot(at_ref[rows, :], wo_ref[d_rnn:, :], preferred_element_type=F32))
    x1 = x_ref[rows, :] + y
    x1_ref[rows, :] = x1
    ms = jnp.mean(x1 * x1, axis=-1, keepdims=True)
    h2 = x1 * lax.rsqrt(ms + NORM_EPS) * g2_ref[...]
    T = x1.shape[0]
    n_tile = h2.shape[1] // LANES
    _store_token_tiles(h2_ref.at[pl.ds(rows.start * n_tile, T * n_tile)], h2)
    logits = jnp.dot(h2.astype(BF16), wr_ref[...], preferred_element_type=F32) + br_ref[...]
    lane = lax.broadcasted_iota(jnp.int32, (T, LANES), 1).astype(F32)
    first = lambda hit: jnp.min(jnp.where(hit, lane, float(LANES)), axis=-1, keepdims=True)
    gl = jnp.where(lane < n_groups, logits, NEG_INF)
    gmax = jnp.max(gl, axis=-1, keepdims=True)
    gidx = first(gl == gmax)
    pg = 1.0 / jnp.sum(jnp.exp(gl - gmax), axis=-1, keepdims=True)
    lo = n_groups + per_group * gidx
    el = jnp.where(jnp.logical_and(lane >= lo, lane < lo + per_group), logits, NEG_INF)
    ma = jnp.max(el, axis=-1, keepdims=True)
    ia = first(el == ma)
    el2 = jnp.where(lane == ia, NEG_INF, el)
    mb = jnp.max(el2, axis=-1, keepdims=True)
    ib = first(el2 == mb)
    eb = jnp.exp(mb - ma)
    gate_a = pg / (1.0 + eb)
    gate_b = pg * eb / (1.0 + eb)
    hit_a = lane == ia
    hit_b = lane == ib
    onehot = jnp.where(jnp.logical_or(hit_a, hit_b), 1.0, 0.0)
    r_i = lax.broadcasted_iota(jnp.int32, (T, T), 0)
    c_i = lax.broadcasted_iota(jnp.int32, (T, T), 1)
    lower = jnp.where(c_i < r_i, 1.0, 0.0).astype(BF16)
    before = jnp.dot(lower, onehot.astype(BF16), preferred_element_type=F32) + cnt
    rank_a = jnp.sum(jnp.where(hit_a, before, 0.0), axis=-1, keepdims=True)
    rank_b = jnp.sum(jnp.where(hit_b, before, 0.0), axis=-1, keepdims=True)
    cols = (ia - n_groups, ib - n_groups, gate_a, gate_b, rank_a, rank_b)
    route = jnp.zeros((T, LANES), F32)
    for c, val in enumerate(cols):
        route = jnp.where(lane == float(c), val, route)
    route_ref[rows, :] = route
    route_t_ref[:, rows] = route.T[:SUBLANES]
    return cnt + jnp.sum(onehot, axis=0, keepdims=True)


def _outproj_route(yr, at, x, wo, g2, wr, br, cnt_in, n_groups, per_group):
    N, D = x.shape
    T = min(ROW_TILE, N)
    d_rnn = yr.shape[1]
    row = lambda w: pl.BlockSpec((T, w), lambda i: (i, 0))
    full = lambda a: pl.BlockSpec(a.shape, lambda i: (0,) * a.ndim)
    return pl.pallas_call(
        functools.partial(_outproj_kernel, n_groups=n_groups, per_group=per_group),
        grid=(N // T,),
        in_specs=[row(d_rnn), row(at.shape[1]), row(D), full(wo), full(g2), full(wr), full(br), full(cnt_in)],
        out_specs=[row(D), pl.BlockSpec((T * (D // LANES), LANES), lambda i: (i, 0)), row(LANES),
                   pl.BlockSpec((SUBLANES, T), lambda i: (0, i)),
                   pl.BlockSpec((1, LANES), lambda i: (0, 0))],
        out_shape=[jax.ShapeDtypeStruct((N, D), F32), jax.ShapeDtypeStruct((N * (D // LANES), LANES), F32),
                   jax.ShapeDtypeStruct((N, LANES), F32), jax.ShapeDtypeStruct((SUBLANES, N), F32),
                   jax.ShapeDtypeStruct((1, LANES), F32)],
        scratch_shapes=[pltpu.VMEM((1, LANES), F32)],
        compiler_params=_cparams(("arbitrary",)),
        name="outproj_route",
    )(yr, at, x, wo, g2, wr, br, cnt_in)


def _dest_kernel(seg_ref, rt_ref, o_ref):
    e = rt_ref[0:TOP_K, :].astype(jnp.int32)
    start = jnp.zeros_like(e)
    for ex in range(seg_ref.shape[0]):
        start = jnp.where(e == ex, seg_ref[ex], start)
    o_ref[...] = start + rt_ref[4:4 + TOP_K, :].astype(jnp.int32)


def _dest(seg_start, route_t):
    R, N = route_t.shape
    T = min(ROW_TILE, N)
    grid_spec = pltpu.PrefetchScalarGridSpec(
        num_scalar_prefetch=1,
        grid=(N // T,),
        in_specs=[pl.BlockSpec((R, T), lambda i, seg: (0, i))],
        out_specs=pl.BlockSpec((TOP_K, T), lambda i, seg: (0, i)),
    )
    return pl.pallas_call(
        _dest_kernel,
        grid_spec=grid_spec,
        out_shape=jax.ShapeDtypeStruct((TOP_K, N), jnp.int32),
        compiler_params=_cparams(("arbitrary",)),
        name="moe_dest",
    )(seg_start, route_t)


def _dispatch_kernel(zb_ref, dest_ref, h_ref, *rest, zero_fill):
    xb_hbm, zbuf, sem = rest[-3], rest[-2], rest[-1]
    T = dest_ref.shape[1]
    n = h_ref.shape[0] // T

    if zero_fill:
        @pl.when(pl.program_id(0) == 0)
        def _():
            zbuf[...] = jnp.zeros_like(zbuf)

            def zcopy(i):
                row0 = pl.multiple_of(zb_ref[i] * (EXP_BLOCK * n), EXP_BLOCK * n)
                return pltpu.make_async_copy(zbuf, xb_hbm.at[pl.ds(row0, EXP_BLOCK * n)], sem)

            def zstart(i, c):
                @pl.when(zb_ref[i] >= 0)
                def _():
                    zcopy(i).start()
                return c

            def zwait(i, c):
                @pl.when(zb_ref[i] >= 0)
                def _():
                    zcopy(i).wait()
                return c

            lax.fori_loop(0, zb_ref.shape[0], zstart, 0)
            lax.fori_loop(0, zb_ref.shape[0], zwait, 0)

    def body(k, c):
        t0 = pl.multiple_of(k * SUBLANES, SUBLANES)
        group = h_ref.at[pl.ds(pl.multiple_of(t0 * n, SUBLANES * n), SUBLANES * n)]
        for j in range(SUBLANES):
            for s in range(TOP_K):
                row0 = pl.multiple_of(dest_ref[s, t0 + j] * n, n)
                pltpu.make_async_copy(group.at[pl.ds(j * n, n)],
                                      xb_hbm.at[pl.ds(row0, n)], sem).start(priority=s)
        return c

    lax.fori_loop(0, T // SUBLANES, body, 0, unroll=2)
    for s in range(TOP_K):
        pltpu.make_async_copy(h_ref, xb_hbm.at[pl.ds(0, T * n)], sem).wait()


def _dispatch(zb, dest, h, xb, xb_rows):
    N = dest.shape[1]
    n = h.shape[0] // N
    T = min(DISPATCH_TILE, N)
    dest3 = jnp.transpose(dest.reshape(TOP_K, N // T, T), (1, 0, 2))
    aliased = xb is not None
    in_specs = [pl.BlockSpec((None, TOP_K, T), lambda i, zb: (i, 0, 0), memory_space=pltpu.SMEM),
                pl.BlockSpec((T * n, LANES), lambda i, zb: (i, 0))]
    args = [zb, dest3, h]
    if aliased:
        in_specs.append(pl.BlockSpec(memory_space=pl.ANY))
        args.append(xb)
    grid_spec = pltpu.PrefetchScalarGridSpec(
        num_scalar_prefetch=1,
        grid=(N // T,),
        in_specs=in_specs,
        out_specs=pl.BlockSpec(memory_space=pl.ANY),
        scratch_shapes=[pltpu.VMEM((EXP_BLOCK * n, LANES), h.dtype), pltpu.SemaphoreType.DMA(())],
    )
    return pl.pallas_call(
        functools.partial(_dispatch_kernel, zero_fill=not aliased),
        grid_spec=grid_spec,
        out_shape=jax.ShapeDtypeStruct((xb_rows * n, LANES), h.dtype),
        input_output_aliases={3: 0} if aliased else {},
        compiler_params=pltpu.CompilerParams(dimension_semantics=("arbitrary",), has_side_effects=True),
        name="moe_dispatch",
    )(*args)


def _expert_kernel(be_ref, bi_ref, br_ref, bf_ref, x_ref, w1_ref, w2_ref, y_ref, w1b, w2b):
    b = pl.program_id(0)
    rows = br_ref[b]

    @pl.when(bf_ref[b] == 1)
    def _():
        w1b[...] = w1_ref[...].astype(BF16)
        w2b[...] = w2_ref[...].astype(BF16)

    @pl.when(rows > 0)
    def _():
        x = _load_token_tiles(x_ref, EXP_BLOCK).astype(BF16)
        h = jnp.dot(x, w1b[...], preferred_element_type=F32)
        de = h.shape[1] // 2
        act = (jax.nn.silu(h[:, :de]) * h[:, de:]).astype(BF16)
        _store_token_tiles(y_ref, jnp.dot(act, w2b[...], preferred_element_type=F32))

    @pl.when(rows == 0)
    def _():
        y_ref[...] = jnp.zeros_like(y_ref)


def _experts(be, bi, br, bf, xb, w1, w2, layer):
    nb = be.shape[0]
    D, d2 = w1.shape[-2:]
    de = w2.shape[-2]
    blk = EXP_BLOCK * (D // LANES)
    grid_spec = pltpu.PrefetchScalarGridSpec(
        num_scalar_prefetch=4,
        grid=(nb,),
        in_specs=[pl.BlockSpec((blk, LANES), lambda b, be, bi, br, bf: (bi[b], 0)),
                  pl.BlockSpec((None, None, D, d2), lambda b, be, bi, br, bf: (layer, be[b], 0, 0)),
                  pl.BlockSpec((None, None, de, D), lambda b, be, bi, br, bf: (layer, be[b], 0, 0))],
        out_specs=pl.BlockSpec((blk, LANES), lambda b, be, bi, br, bf: (b, 0)),
        scratch_shapes=[pltpu.VMEM((D, d2), BF16), pltpu.VMEM((de, D), BF16)],
    )
    return pl.pallas_call(
        _expert_kernel,
        grid_spec=grid_spec,
        out_shape=jax.ShapeDtypeStruct(xb.shape, F32),
        compiler_params=_cparams(("arbitrary",)),
        name="moe_experts",
    )(be, bi, br, bf, xb, w1, w2)


def _combine_kernel(dest_ref, dest_next_ref, x1_ref, route_ref, gf_ref, yb_hbm, o_ref, ybuf, sem, *, final):
    T = x1_ref.shape[0]
    n = ybuf.shape[2] // T
    i = pl.program_id(0)
    slot = i % 2

    def issue(dref, sl):
        def body(k, c):
            t0 = pl.multiple_of(k * SUBLANES, SUBLANES)
            for s in range(TOP_K):
                group = ybuf.at[sl, s, pl.ds(pl.multiple_of(t0 * n, SUBLANES * n), SUBLANES * n)]
                for j in range(SUBLANES):
                    row0 = pl.multiple_of(dref[s, t0 + j] * n, n)
                    pltpu.make_async_copy(yb_hbm.at[pl.ds(row0, n)],
                                          group.at[pl.ds(j * n, n)], sem.at[sl]).start(priority=s)
            return c

        lax.fori_loop(0, T // SUBLANES, body, 0, unroll=2)

    @pl.when(i == 0)
    def _():
        issue(dest_ref, 0)

    @pl.when(i + 1 < pl.num_programs(0))
    def _():
        issue(dest_next_ref, 1 - slot)

    for s in range(TOP_K):
        pltpu.make_async_copy(yb_hbm.at[pl.ds(0, T * n)], ybuf.at[slot, s], sem.at[slot]).wait()
    route = route_ref[...]
    out = (x1_ref[...] + _load_token_tiles(ybuf.at[slot, 0], T) * route[:, 2:3]
           + _load_token_tiles(ybuf.at[slot, 1], T) * route[:, 3:4])
    if final:
        ms = jnp.mean(out * out, axis=-1, keepdims=True)
        out = out * lax.rsqrt(ms + NORM_EPS) * gf_ref[...]
    o_ref[...] = out


def _combine(dest, x1, route, gf, yb, final):
    N, D = x1.shape
    T = min(COMBINE_TILE, N)
    dest3 = jnp.transpose(dest.reshape(TOP_K, N // T, T), (1, 0, 2))
    last = N // T - 1
    return pl.pallas_call(
        functools.partial(_combine_kernel, final=final),
        grid=(N // T,),
        in_specs=[pl.BlockSpec((None, TOP_K, T), lambda i: (i, 0, 0), memory_space=pltpu.SMEM),
                  pl.BlockSpec((None, TOP_K, T), lambda i: (jnp.minimum(i + 1, last), 0, 0),
                               memory_space=pltpu.SMEM),
                  pl.BlockSpec((T, D), lambda i: (i, 0)),
                  pl.BlockSpec((T, LANES), lambda i: (i, 0)),
                  pl.BlockSpec((1, D), lambda i: (0, 0)),
                  pl.BlockSpec(memory_space=pl.ANY)],
        out_specs=pl.BlockSpec((T, D), lambda i: (i, 0)),
        out_shape=jax.ShapeDtypeStruct((N, D), F32),
        scratch_shapes=[pltpu.VMEM((2, TOP_K, T * (D // LANES), LANES), F32), pltpu.SemaphoreType.DMA((2,))],
        compiler_params=_cparams(("arbitrary",)),
        name="moe_combine",
    )(dest3, dest3, x1, route, gf, yb)


def _rope_tables(pos):
    inv = ROPE_THETA ** (-jnp.arange(HALF, dtype=F32) / HALF)
    ang = pos.astype(F32)[:, None] * inv[None, :]
    cos, sin = jnp.cos(ang), jnp.sin(ang)
    cq = jnp.tile(cos, (1, LANES // HALF))
    sq = jnp.concatenate([-sin, -sin, sin, sin], axis=1)
    return cq, sq, cos.T, sin.T


def _attention_order(w):
    D, d_att = w.shape
    w = w.reshape(D, d_att // V_DIM, 2, 2, HALF)
    return jnp.swapaxes(w, 2, 3).reshape(D, d_att)


def _gate_weights(w_ga, w_gx):
    nb, bw, _ = w_ga.shape
    half = nb // 2
    eye = jnp.eye(half, dtype=w_ga.dtype)

    def dense(w):
        return jnp.einsum('nij,nm->nimj', w, eye).reshape(half * bw, half * bw)

    return jnp.stack([jnp.concatenate([dense(w_ga[h * half:(h + 1) * half]),
                                       dense(w_gx[h * half:(h + 1) * half])], axis=1)
                      for h in range(2)]).astype(BF16)


def _block_tables(counts, nb):
    ne = counts.shape[0]
    nblk = (counts + EXP_BLOCK - 1) // EXP_BLOCK
    cum = jnp.sum(jnp.where(jnp.arange(ne)[None, :] <= jnp.arange(ne)[:, None], nblk[None, :], 0), axis=1)
    total = cum[-1]
    b = jnp.arange(nb, dtype=jnp.int32)
    bc = jnp.minimum(b, total - 1)
    e = jnp.minimum(jnp.sum(bc[:, None] >= cum[None, :], axis=1), ne - 1).astype(jnp.int32)
    pick = lambda tab: jnp.sum(jnp.where(e[:, None] == jnp.arange(ne)[None, :], tab[None, :], 0), axis=1)
    off = bc - (pick(cum) - pick(nblk))
    valid = b < total
    rows = jnp.where(valid, jnp.clip(pick(counts) - off * EXP_BLOCK, 0, EXP_BLOCK), 0)
    first = jnp.where(jnp.logical_and(valid, off == 0), 1, 0)
    last_blk = jnp.where(nblk > 0, cum - 1, -1)
    trail = total + jnp.arange(ne, dtype=jnp.int32)
    trail = jnp.where(trail < nb, trail, -1)
    zb = jnp.concatenate([last_blk, trail]).astype(jnp.int32)
    seg_start = ((cum - nblk) * EXP_BLOCK).astype(jnp.int32)
    return seg_start, e, bc.astype(jnp.int32), rows.astype(jnp.int32), first.astype(jnp.int32), zb


def kernel(x_prompt, x_sample, cache_k, cache_v, state_h, state_conv, page_table, ln1_g, w_in, conv_w, conv_b,
           w_ga, b_ga, w_gx, b_gx, lru_lambda, lam_q1, lam_k1, lam_q2, lam_k2, subln_g, w_out, ln2_g,
           w_route_group, b_route_group, w_route_expert, b_route_expert, w_exp_in, w_exp_out, ln_f):
    Bp, Sp, D = x_prompt.shape
    Bs = x_sample.shape[0]
    depth = w_in.shape[0]
    d_rnn = state_h.shape[-1]
    d_att = cache_k.shape[-2] * cache_k.shape[-1]
    n_groups = w_route_group.shape[-1]
    n_experts = w_route_expert.shape[-1]
    per_group = n_experts // n_groups
    n_pool, page = cache_k.shape[1], cache_k.shape[2]
    n_pages = page_table.shape[1]
    past_len = n_pages * page
    Np = Bp * Sp
    assert x_sample.shape[1] == 1 and n_groups + n_experts <= LANES

    cq, sq, ct, st = _rope_tables(jnp.arange(Sp))
    _, _, ct_s, st_s = _rope_tables(jnp.full((Bs,), past_len))
    qscale = HEAD_DIM ** -0.5
    ck = jnp.transpose(cache_k, (0, 1, 3, 4, 2)).reshape(depth, n_pool, d_att, page)
    cv = cache_v.reshape(depth, n_pool, page * (d_att // V_DIM), V_DIM)
    pt_flat = page_table.reshape(-1).astype(jnp.int32)
    conv_s = jnp.transpose(state_conv, (0, 2, 1, 3))

    n_blocks = (TOP_K * (Np + Bs)) // EXP_BLOCK + n_experts

    xp = x_prompt
    xs = x_sample.reshape(Bs, D)
    kt_all = v_all = None
    hp_l, cp_l, ks_l, vs_l, hs_l, cs_l = [], [], [], [], [], []
    for l in range(depth):
        lambda_init = 0.8 - 0.6 * math.exp(-0.3 * l)
        out_scale = 1.0 - lambda_init
        lam = (jnp.exp(jnp.sum(lam_q1[l] * lam_k1[l])) - jnp.exp(jnp.sum(lam_q2[l] * lam_k2[l]))
               + lambda_init).reshape(1, 1).astype(F32)
        wl = w_in[l]
        w_xg = wl[:, :2 * d_rnn]
        w_q = wl[:, 2 * d_rnn:2 * d_rnn + d_att] * qscale
        w_k = wl[:, 2 * d_rnn + d_att:2 * d_rnn + 2 * d_att]
        w_v = wl[:, 2 * d_rnn + 2 * d_att:]
        wm_p = jnp.concatenate([w_xg, _attention_order(w_k), w_v], axis=1).astype(BF16)
        wt_p = jnp.concatenate([_attention_order(w_q), w_k, w_v], axis=1).T.astype(BF16)
        g1 = ln1_g[l].reshape(1, D)
        wg = _gate_weights(w_ga[l], w_gx[l])
        bga = b_ga[l].reshape(1, d_rnn)
        bgx = b_gx[l].reshape(1, d_rnn)
        lru_l = lru_lambda[l].reshape(1, d_rnn)
        cw = conv_w[l]
        cb = conv_b[l].reshape(1, d_rnn)
        sg = subln_g[l].reshape(1, V_DIM)
        wo = w_out[l].astype(BF16)
        g2 = ln2_g[l].reshape(1, D)
        n_route = n_groups + n_experts
        wr = jnp.concatenate([w_route_group[l], w_route_expert[l], jnp.zeros((D, LANES - n_route), F32)],
                             axis=1).astype(BF16)
        br = jnp.concatenate([b_route_group[l], b_route_expert[l], jnp.zeros((LANES - n_route,), F32)]
                             ).reshape(1, LANES)

        xr, gr, kb, v_all, qt, kt_all, vt = _inproj_prompt(xp, g1, wm_p, wt_p, cq, sq, ct, st, kt_all, v_all)
        y_rnn, h_last, c_tail = _lru_prompt(xr, gr, cw, cb, wg, bga, bgx, lru_l)
        xr_s, gr_s, v_s, kt_s, qt_s = _inproj_sample(xs, g1, wm_p, wt_p, ct_s, st_s)
        y_rnn_s, h_s, c_s = _lru_step(xr_s, gr_s, conv_s[l], state_h[l], cw, cb, wg, bga, bgx, lru_l)
        k_s = kt_s.T
        att, att_s = _attn_prompt_and_sample(qt, kb, vt, sg, lam, out_scale,
                                             pt_flat, qt_s, kt_s, v_s, ck, cv, l)
        cnt0 = jnp.zeros((1, LANES), F32)
        x1, h2, route, route_t, cnt1 = _outproj_route(
            y_rnn.reshape(Np, d_rnn), att.reshape(Np, d_att), xp.reshape(Np, D), wo, g2, wr, br, cnt0,
            n_groups, per_group)
        x1_s, h2_s, route_s, route_ts, cnt2 = _outproj_route(y_rnn_s, att_s, xs, wo, g2, wr, br, cnt1,
                                                             n_groups, per_group)
        counts = cnt2[0, n_groups:n_groups + n_experts].astype(jnp.int32)
        seg_start, be, bi, brows, bfirst, zb = _block_tables(counts, n_blocks)

        dest_p, dest_s = _dest(seg_start, route_t), _dest(seg_start, route_ts)
        xb = _dispatch(zb, dest_p, h2, None, n_blocks * EXP_BLOCK)
        xb = _dispatch(zb, dest_s, h2_s, xb, n_blocks * EXP_BLOCK)
        yb = _experts(be, bi, brows, bfirst, xb, w_exp_in, w_exp_out, l)
        final = l == depth - 1
        gf = ln_f.reshape(1, D)
        xp_new = _combine(dest_p, x1, route, gf, yb, final)
        xs = _combine(dest_s, x1_s, route_s, gf, yb, final)
        xp = xp_new.reshape(Bp, Sp, D)

        n_qk = d_att // HEAD_DIM
        hp_l.append(h_last.reshape(Bp, d_rnn))
        cp_l.append(c_tail)
        ks_l.append(k_s.reshape(Bs, 1, n_qk, HEAD_DIM))
        vs_l.append(v_s.reshape(Bs, 1, d_att // V_DIM, V_DIM))
        hs_l.append(h_s)
        cs_l.append(jnp.transpose(c_s, (1, 0, 2)))
    k_prompt = jnp.transpose(kt_all.reshape(depth, Bp, n_qk, HEAD_DIM, Sp), (0, 1, 4, 2, 3))
    v_prompt = v_all.reshape(depth, Bp, Sp, d_att // V_DIM, V_DIM)
    return (xp, xs.reshape(Bs, 1, D),
            k_prompt, v_prompt, jnp.stack(hp_l), jnp.stack(cp_l),
            jnp.stack(ks_l), jnp.stack(vs_l), jnp.stack(hs_l), jnp.stack(cs_l))
```

```python
import functools
import math

import jax
import jax.numpy as jnp
from jax import lax
from jax.experimental import pallas as pl
from jax.experimental.pallas import tpu as pltpu

F32 = jnp.float32
BF16 = jnp.bfloat16

HEAD_DIM = 64
V_DIM = 2 * HEAD_DIM
HALF = HEAD_DIM // 2
ONES_ROWS = 16
CONV_W = 4
LRU_C = 8.0
ROPE_THETA = 10000.0
NORM_EPS = 1e-6
SUBLN_EPS = 1e-5
NEG_INF = -1e30
TOP_K = 2

LANES = 128
SUBLANES = 8
VMEM_LIMIT = 48 * 1024 * 1024
INPROJ_VMEM_LIMIT = 56 * 1024 * 1024

ROW_TILE = 512
LRU_TILE = 512
ATT_TILE = 512
EXP_BLOCK = 512
DISPATCH_TILE = 512
COMBINE_TILE = 256


def _cparams(sem, vmem_limit=VMEM_LIMIT):
    return pltpu.CompilerParams(dimension_semantics=sem, vmem_limit_bytes=vmem_limit)


def _inproj_kernel(x_ref, g_ref, wm_ref, wt_ref, cq_ref, sq_ref, ct_ref, st_ref, *rest, sample, n_prev=0):
    prev, outs = rest[:2 if n_prev else 0], rest[2 if n_prev else 0:]
    x = x_ref[...]
    ms = jnp.mean(x * x, axis=-1, keepdims=True)
    h = (x * lax.rsqrt(ms + NORM_EPS) * g_ref[...]).astype(BF16)
    main = jnp.dot(h, wm_ref[...], preferred_element_type=F32)
    d_rnn = outs[0].shape[1]
    ct = ct_ref[...]
    st = st_ref[...]
    tr = lax.dot_general(wt_ref[...], h, (((1,), (1,)), ((), ())), preferred_element_type=F32)

    d_att = wt_ref.shape[0] // 2
    cq = cq_ref[...]
    sq = sq_ref[...]

    def roped_k(c):
        kc = main[:, 2 * d_rnn + c * LANES:2 * d_rnn + (c + 1) * LANES]
        return kc * cq + pltpu.roll(kc, HEAD_DIM, 1) * sq

    def store_kt(dst, c, kr):
        kt = kr.T
        for jj in range(2):
            r0 = (2 * c + jj) * HEAD_DIM
            dst[r0:r0 + HALF, :] = kt[jj * HALF:(jj + 1) * HALF]
            dst[r0 + HALF:r0 + HEAD_DIM, :] = kt[HEAD_DIM + jj * HALF:HEAD_DIM + (jj + 1) * HALF]

    if sample:
        xr_ref, gr_ref, v_ref, kt_ref, qt_ref = outs
        xr_ref[...] = main[:, :d_rnn]
        gr_ref[...] = main[:, d_rnn:2 * d_rnn]
        v_ref[...] = main[:, 2 * d_rnn + d_att:]
        for c in range(d_att // LANES):
            store_kt(kt_ref, c, roped_k(c))
        for hh in range(d_att // HEAD_DIM):
            g, jj = divmod(hh, 2)
            r1 = g * V_DIM + jj * HALF
            x1 = tr[r1:r1 + HALF]
            x2 = tr[r1 + HEAD_DIM:r1 + HEAD_DIM + HALF]
            qt_ref[hh * HEAD_DIM:hh * HEAD_DIM + HALF, :] = x1 * ct - x2 * st
            qt_ref[hh * HEAD_DIM + HALF:(hh + 1) * HEAD_DIM, :] = x2 * ct + x1 * st
    else:
        xr_ref, gr_ref, kb_ref, v_ref, qt_ref, kt_ref, vt_ref = outs
        xr_ref[...] = main[:, :d_rnn]
        gr_ref[...] = main[:, d_rnn:2 * d_rnn]
        if n_prev:
            kt_ref[:n_prev] = prev[0][...]
            v_ref[:n_prev] = prev[1][...]
        v = main[:, 2 * d_rnn + d_att:]
        T = v.shape[0]
        G = d_att // V_DIM
        for g in range(G):
            v_ref[n_prev, pl.ds(g, T, stride=G), :] = v[:, g * V_DIM:(g + 1) * V_DIM]
        for c in range(d_att // LANES):
            kr = roped_k(c)
            kb_ref[:, c * LANES:(c + 1) * LANES] = kr.astype(BF16)
            store_kt(kt_ref.at[n_prev], c, kr)
        for hh in range(d_att // HEAD_DIM):
            g, jj = divmod(hh, 2)
            r1 = g * V_DIM + jj * HALF
            x1 = tr[r1:r1 + HALF]
            x2 = tr[r1 + HEAD_DIM:r1 + HEAD_DIM + HALF]
            qt_ref[r1:r1 + HALF, :] = (x1 * ct - x2 * st).astype(BF16)
            qt_ref[r1 + HEAD_DIM:r1 + HEAD_DIM + HALF, :] = (x2 * ct + x1 * st).astype(BF16)
        for g in range(G):
            r0 = g * (V_DIM + ONES_ROWS)
            vt_ref[r0:r0 + V_DIM, :] = tr[d_att + g * V_DIM:d_att + (g + 1) * V_DIM].astype(BF16)
            vt_ref[r0 + V_DIM:r0 + V_DIM + ONES_ROWS, :] = jnp.ones((ONES_ROWS, T), BF16)


def _inproj_prompt(x, g, wm, wt, cq, sq, ct, st, k_prev, v_prev):
    B, S, D = x.shape
    T = min(ROW_TILE, S)
    nt = S // T
    d_att = wt.shape[0] // 2
    d_rnn = (wm.shape[1] - 2 * d_att) // 2
    G = d_att // V_DIM
    n_prev = 0 if k_prev is None else k_prev.shape[0]
    row = lambda w: pl.BlockSpec((None, T, w), lambda b, t: (b, t, 0))
    full = lambda a: pl.BlockSpec(a.shape, lambda b, t: (0,) * a.ndim)
    once = lambda a: pl.BlockSpec(a.shape, lambda b, t: (0,) * a.ndim, pipeline_mode=pl.Buffered(1))
    tr_spec = pl.BlockSpec((None, d_att, T), lambda b, t: (b, 0, t))
    kstack = lambda n: pl.BlockSpec((n, None, d_att, T), lambda b, t: (0, b, 0, t))
    vstack = lambda n: pl.BlockSpec((n, None, T * G, V_DIM), lambda b, t: (0, b, t, 0))
    prev_specs = [kstack(n_prev), vstack(n_prev)] if n_prev else []
    prev_args = [k_prev, v_prev] if n_prev else []
    return pl.pallas_call(
        functools.partial(_inproj_kernel, sample=False, n_prev=n_prev),
        grid=(B, nt),
        in_specs=[row(D), full(g), once(wm), once(wt),
                  pl.BlockSpec((T, LANES), lambda b, t: (t, 0)),
                  pl.BlockSpec((T, LANES), lambda b, t: (t, 0)),
                  pl.BlockSpec((HALF, T), lambda b, t: (0, t)),
                  pl.BlockSpec((HALF, T), lambda b, t: (0, t))] + prev_specs,
        out_specs=[row(d_rnn), row(d_rnn), row(d_att), vstack(n_prev + 1), tr_spec, kstack(n_prev + 1),
                   pl.BlockSpec((None, G * (V_DIM + ONES_ROWS), T), lambda b, t: (b, 0, t))],
        out_shape=[jax.ShapeDtypeStruct((B, S, d_rnn), F32), jax.ShapeDtypeStruct((B, S, d_rnn), F32),
                   jax.ShapeDtypeStruct((B, S, d_att), BF16),
                   jax.ShapeDtypeStruct((n_prev + 1, B, S * G, V_DIM), F32),
                   jax.ShapeDtypeStruct((B, d_att, S), BF16),
                   jax.ShapeDtypeStruct((n_prev + 1, B, d_att, S), F32),
                   jax.ShapeDtypeStruct((B, G * (V_DIM + ONES_ROWS), S), BF16)],
        compiler_params=_cparams(("parallel", "arbitrary"), INPROJ_VMEM_LIMIT),
        name="inproj_prompt",
    )(x, g, wm, wt, cq, sq, ct, st, *prev_args)


def _inproj_sample(x, g, wm, wt, cq, sq, ct, st):
    n, D = x.shape
    d_att = wt.shape[0] // 2
    d_rnn = (wm.shape[1] - 2 * d_att) // 2
    full = lambda a: pl.BlockSpec(a.shape, lambda i: (0,) * a.ndim)
    out_shapes = [jax.ShapeDtypeStruct((n, d_rnn), F32), jax.ShapeDtypeStruct((n, d_rnn), F32),
                  jax.ShapeDtypeStruct((n, d_att), F32), jax.ShapeDtypeStruct((d_att, n), F32),
                  jax.ShapeDtypeStruct((d_att, n), F32)]
    return pl.pallas_call(
        functools.partial(_inproj_kernel, sample=True),
        grid=(1,),
        in_specs=[full(x), full(g), full(wm), full(wt), full(cq), full(sq), full(ct), full(st)],
        out_specs=[pl.BlockSpec(s.shape, lambda i: (0, 0)) for s in out_shapes],
        out_shape=out_shapes,
        compiler_params=_cparams(("arbitrary",)),
        name="inproj_sample",
    )(x, g, wm, wt, cq, sq, ct, st)


def _gates(xc, wg_ref, bga, bgx, lam):
    half = xc.shape[1] // 2
    xcb = xc.astype(BF16)
    g0 = jnp.dot(xcb[:, :half], wg_ref[0], preferred_element_type=F32)
    g1 = jnp.dot(xcb[:, half:], wg_ref[1], preferred_element_type=F32)
    r = jax.nn.sigmoid(jnp.concatenate([g0[:, :half], g1[:, :half]], axis=1) + bga)
    i = jax.nn.sigmoid(jnp.concatenate([g0[:, half:], g1[:, half:]], axis=1) + bgx)
    neg = -lam
    softplus = jnp.maximum(neg, 0.0) + jnp.log1p(jnp.exp(-jnp.abs(neg)))
    log_a = -LRU_C * r * softplus
    a = jnp.exp(log_a)
    mult = jnp.sqrt(1.0 - a * a)
    return a, i, mult


def _lru_kernel(xr_ref, gr_ref, cw_ref, cb_ref, wg_ref, bga_ref, bgx_ref, lam_ref,
                y_ref, hlast_ref, ctail_ref, xp_scr, a_scr, b_scr, h_scr):
    t = pl.program_id(1)
    T, C = xr_ref.shape

    @pl.when(t == 0)
    def _():
        xp_scr[0:SUBLANES, :] = jnp.zeros((SUBLANES, C), F32)
        h_scr[...] = jnp.zeros_like(h_scr)

    x = xr_ref[...]
    xp_scr[SUBLANES:SUBLANES + T, :] = x
    cw = cw_ref[...]
    xc = cb_ref[...]
    for j in range(CONV_W - 1):
        xc = xc + xp_scr[pl.ds(SUBLANES - (CONV_W - 1) + j, T), :] * cw[j:j + 1]
    xc = xc + x * cw[CONV_W - 1:CONV_W]
    tail = x[T - (CONV_W - 1):, :]
    xp_scr[SUBLANES - (CONV_W - 1):SUBLANES, :] = tail
    ctail_ref[...] = tail

    a, i, mult = _gates(xc, wg_ref, bga_ref[...], bgx_ref[...], lam_ref[...])
    row = lax.broadcasted_iota(jnp.int32, (T, 1), 0)
    mult = jnp.where(jnp.logical_and(row == 0, t == 0), 1.0, mult)
    a_scr[...] = a
    b_scr[...] = mult * i * xc

    rowi = lax.broadcasted_iota(jnp.int32, (SUBLANES, C), 0)

    def body(gi, h):
        off = pl.multiple_of(gi * SUBLANES, SUBLANES)
        A = a_scr[pl.ds(off, SUBLANES), :]
        Bv = b_scr[pl.ds(off, SUBLANES), :]
        for d in (1, 2, 4):
            keep = rowi >= d
            Bv = jnp.where(keep, Bv + A * pltpu.roll(Bv, d, 0), Bv)
            A = jnp.where(keep, A * pltpu.roll(A, d, 0), A)
        hrows = Bv + A * h
        b_scr[pl.ds(off, SUBLANES), :] = hrows
        return hrows[SUBLANES - 1:SUBLANES, :]

    h = lax.fori_loop(0, T // SUBLANES, body, h_scr[...], unroll=2)
    h_scr[...] = h
    hlast_ref[...] = h
    y_ref[...] = (b_scr[...] * jax.nn.gelu(gr_ref[...])).astype(BF16)


def _lru_prompt(xr, gr, cw, cb, wg, bga, bgx, lam):
    B, S, C = xr.shape
    T = min(LRU_TILE, S)
    nt = S // T
    row = pl.BlockSpec((None, T, C), lambda b, t: (b, t, 0))
    full = lambda a: pl.BlockSpec(a.shape, lambda b, t: (0,) * a.ndim)
    return pl.pallas_call(
        _lru_kernel,
        grid=(B, nt),
        in_specs=[row, row, full(cw), full(cb), full(wg), full(bga), full(bgx), full(lam)],
        out_specs=[row, pl.BlockSpec((None, 1, C), lambda b, t: (b, 0, 0)),
                   pl.BlockSpec((None, CONV_W - 1, C), lambda b, t: (b, 0, 0))],
        out_shape=[jax.ShapeDtypeStruct((B, S, C), BF16), jax.ShapeDtypeStruct((B, 1, C), F32),
                   jax.ShapeDtypeStruct((B, CONV_W - 1, C), F32)],
        scratch_shapes=[pltpu.VMEM((T + SUBLANES, C), F32), pltpu.VMEM((T, C), F32),
                        pltpu.VMEM((T, C), F32), pltpu.VMEM((1, C), F32)],
        compiler_params=_cparams(("parallel", "arbitrary")),
        name="rglru_prompt",
    )(xr, gr, cw, cb, wg, bga, bgx, lam)


def _lru_step_kernel(xr_ref, gr_ref, conv_ref, h0_ref, cw_ref, cb_ref, wg_ref, bga_ref, bgx_ref, lam_ref,
                     y_ref, hnew_ref, cnew_ref):
    x = xr_ref[...]
    cw = cw_ref[...]
    xc = cb_ref[...]
    for j in range(CONV_W - 1):
        xc = xc + conv_ref[j] * cw[j:j + 1]
    xc = xc + x * cw[CONV_W - 1:CONV_W]
    for j in range(CONV_W - 2):
        cnew_ref[j] = conv_ref[j + 1]
    cnew_ref[CONV_W - 2] = x
    a, i, mult = _gates(xc, wg_ref, bga_ref[...], bgx_ref[...], lam_ref[...])
    h = a * h0_ref[...] + mult * i * xc
    hnew_ref[...] = h
    y_ref[...] = (h * jax.nn.gelu(gr_ref[...])).astype(BF16)


def _lru_step(xr, gr, conv, h0, cw, cb, wg, bga, bgx, lam):
    n, C = xr.shape
    args = (xr, gr, conv, h0, cw, cb, wg, bga, bgx, lam)
    full = lambda a: pl.BlockSpec(a.shape, lambda i: (0,) * a.ndim)
    out_shapes = [jax.ShapeDtypeStruct((n, C), BF16), jax.ShapeDtypeStruct((n, C), F32),
                  jax.ShapeDtypeStruct((CONV_W - 1, n, C), F32)]
    return pl.pallas_call(
        _lru_step_kernel,
        grid=(1,),
        in_specs=[full(a) for a in args],
        out_specs=[pl.BlockSpec(s.shape, lambda i, nd=len(s.shape): (0,) * nd) for s in out_shapes],
        out_shape=out_shapes,
        compiler_params=_cparams(("arbitrary",)),
        name="rglru_step",
    )(*args)


def _subln(o, g, scale):
    ms = jnp.mean(o * o, axis=-1, keepdims=True)
    return o * lax.rsqrt(ms + SUBLN_EPS) * g * scale


def _attn_kernel(lam_ref, qt_ref, kb_ref, vt_ref, g_ref, o_ref, qcat, m, acc, s_even, s_odd, *, out_scale,
                 between=None):
    qi = pl.program_id(1)
    Tq = qt_ref.shape[1]
    Tk = Tq

    qt = qt_ref[...]
    row = lax.broadcasted_iota(jnp.int32, qt.shape, 0)
    second = (row & HALF) != 0
    zero = jnp.zeros_like(qt)
    qcat[:, :Tq] = jnp.where(second, zero, qt)
    qcat[:, Tq:] = jnp.where(second, qt, zero)
    m[...] = jnp.full_like(m, NEG_INF)
    acc[...] = jnp.zeros_like(acc)

    def scores(kj, sbuf):
        k0 = pl.multiple_of(kj * Tk, Tk)
        sbuf[...] = jnp.dot(kb_ref[pl.ds(k0, Tk), :], qcat[...], preferred_element_type=F32)

    def update(kj, sbuf, diagonal):
        k0 = pl.multiple_of(kj * Tk, Tk)
        s = sbuf[...]
        if diagonal:
            key = lax.broadcasted_iota(jnp.int32, s.shape, 0)
            qry = lax.broadcasted_iota(jnp.int32, s.shape, 1)
            qry = jnp.where(qry >= Tq, qry - Tq, qry)
            s = jnp.where(key <= qry, s, NEG_INF)
        m_old = m[...]
        m_new = jnp.maximum(m_old, jnp.max(s, axis=0, keepdims=True))
        alpha = jnp.exp(m_old - m_new)
        p = jnp.exp(s - m_new)
        acc[...] = alpha * acc[...] + jnp.dot(vt_ref[:, pl.ds(k0, Tk)], p.astype(BF16),
                                              preferred_element_type=F32)
        m[...] = m_new

    def finish():
        a = acc[:V_DIM, :] / acc[V_DIM:V_DIM + 1, :]
        o = a[:, :Tq] - lam_ref[0, 0] * a[:, Tq:]
        ms = jnp.mean(o * o, axis=0, keepdims=True)
        o = o * lax.rsqrt(ms + SUBLN_EPS) * (g_ref[...] * out_scale)
        o_ref[...] = o.T.astype(o_ref.dtype)

    scores(0, s_even)
    if between is not None:
        between()

    def pair(i, c):
        t = 2 * i
        scores(t + 1, s_odd)
        update(t, s_even, False)
        scores(t + 2, s_even)
        update(t + 1, s_odd, False)
        return c

    lax.fori_loop(0, qi // 2, pair, 0)

    @pl.when(qi % 2 == 0)
    def _():
        update(qi, s_even, True)
        finish()

    @pl.when(qi % 2 == 1)
    def _():
        scores(qi, s_odd)
        update(qi - 1, s_even, False)
        update(qi, s_odd, True)
        finish()


def _attn_prompt(qt, kb, vt, subln_g, lam, out_scale):
    B, S, d_att = kb.shape
    G = d_att // V_DIM
    T = min(ATT_TILE, S)
    return pl.pallas_call(
        functools.partial(_attn_kernel, out_scale=out_scale),
        grid=(B * G, S // T),
        in_specs=[
            pl.BlockSpec(memory_space=pltpu.SMEM),
            pl.BlockSpec((None, V_DIM, T), lambda bg, i: (bg // G, bg % G, i)),
            pl.BlockSpec((None, S, V_DIM), lambda bg, i: (bg // G, 0, bg % G)),
            pl.BlockSpec((None, V_DIM + ONES_ROWS, S), lambda bg, i: (bg // G, bg % G, 0)),
            pl.BlockSpec((V_DIM, 1), lambda bg, i: (0, 0)),
        ],
        out_specs=pl.BlockSpec((None, T, V_DIM), lambda bg, i: (bg // G, i, bg % G)),
        out_shape=jax.ShapeDtypeStruct((B, S, d_att), BF16),
        scratch_shapes=[pltpu.VMEM((V_DIM, 2 * T), BF16), pltpu.VMEM((1, 2 * T), F32),
                        pltpu.VMEM((V_DIM + ONES_ROWS, 2 * T), F32),
                        pltpu.VMEM((T, 2 * T), F32), pltpu.VMEM((T, 2 * T), F32)],
        compiler_params=_cparams(("parallel", "arbitrary")),
        name="diff_attn_prompt",
    )(lam, qt, kb, vt, subln_g.reshape(V_DIM, 1))


def _attn_sample_body(seq, lam_ref, qt_ref, kt_ref, vc_ref, g_ref, *rest, n_pages, out_scale):
    k_refs = rest[:n_pages]
    v_refs = rest[n_pages:2 * n_pages]
    o_ref = rest[2 * n_pages]
    lam = lam_ref[0, 0]
    d_att, n_lanes = qt_ref.shape
    n_heads = d_att // HEAD_DIM
    G = d_att // V_DIM
    R = 2 * SUBLANES
    mine = lax.broadcasted_iota(jnp.int32, (d_att, n_lanes), 1) == seq % n_lanes
    qcol = jnp.sum(jnp.where(mine, qt_ref[...], 0.0), axis=1, keepdims=True)
    kcol = jnp.sum(jnp.where(mine, kt_ref[...], 0.0), axis=1, keepdims=True)
    P = k_refs[0].shape[1]
    qcolb = jnp.broadcast_to(qcol, (d_att, P))
    per_head = lambda x: jnp.sum(x.reshape(n_heads, HEAD_DIM, x.shape[-1]), axis=1)
    s = jnp.concatenate([per_head(k_refs[j][...] * qcolb) for j in range(n_pages)], axis=1)
    s_cur = per_head(qcol * kcol)
    m = jnp.maximum(jnp.max(s, axis=-1, keepdims=True), s_cur)
    p = jnp.exp(s - m)
    p_cur = jnp.exp(s_cur - m)
    inv_l = 1.0 / (jnp.sum(p, axis=-1, keepdims=True) + p_cur)
    p = p * inv_l
    p_cur = p_cur * inv_l
    amap = p - lam * pltpu.roll(p, n_heads - 1, 0)
    amap = jnp.concatenate([amap, jnp.zeros_like(amap)], axis=0).astype(BF16)
    p_cur = jnp.broadcast_to(p_cur, (n_heads, LANES))
    amap_cur = p_cur - lam * pltpu.roll(p_cur, n_heads - 1, 0)
    outs = []
    for g in range(G):
        acc = jnp.zeros((R, V_DIM), F32)
        for j in range(n_pages):
            vjg = v_refs[j][pl.ds(g, P, stride=G), :].astype(BF16)
            acc = acc + jnp.dot(amap[:, j * P:(j + 1) * P], vjg, preferred_element_type=F32)
        o = acc[2 * g:2 * g + 1, :] + amap_cur[2 * g:2 * g + 1, :] * vc_ref[:, g * V_DIM:(g + 1) * V_DIM]
        outs.append(_subln(o, g_ref[...], out_scale))
    o_ref[...] = jnp.concatenate(outs, axis=1).astype(o_ref.dtype)


def _attn_sample_kernel(pt_ref, *refs, n_pages, out_scale):
    _attn_sample_body(pl.program_id(0), *refs, n_pages=n_pages, out_scale=out_scale)


def _attn_sample(pt_flat, lam, qt_s, kt_s, v_cur, subln_g, ck, cv, layer, out_scale):
    d_att, n = qt_s.shape
    n_pages = pt_flat.shape[0] // n
    P = ck.shape[-1]
    nl = min(n, LANES)
    vc3 = v_cur.reshape(n, 1, d_att)
    cols = lambda: pl.BlockSpec((d_att, nl), lambda b, pt: (0, b // nl))
    k_specs = [pl.BlockSpec((None, None, d_att, P), functools.partial(
        lambda b, pt, j: (layer, pt[b * n_pages + j], 0, 0), j=j)) for j in range(n_pages)]
    v_specs = [pl.BlockSpec((None, None, cv.shape[2], V_DIM), functools.partial(
        lambda b, pt, j: (layer, pt[b * n_pages + j], 0, 0), j=j)) for j in range(n_pages)]
    grid_spec = pltpu.PrefetchScalarGridSpec(
        num_scalar_prefetch=1,
        grid=(n,),
        in_specs=[pl.BlockSpec(memory_space=pltpu.SMEM), cols(), cols(),
                  pl.BlockSpec((None, 1, d_att), lambda b, pt: (b, 0, 0)),
                  pl.BlockSpec((1, V_DIM), lambda b, pt: (0, 0))] + k_specs + v_specs,
        out_specs=pl.BlockSpec((None, 1, d_att), lambda b, pt: (b, 0, 0)),
    )
    out = pl.pallas_call(
        functools.partial(_attn_sample_kernel, n_pages=n_pages, out_scale=out_scale),
        grid_spec=grid_spec,
        out_shape=jax.ShapeDtypeStruct((n, 1, d_att), BF16),
        compiler_params=_cparams(("arbitrary",)),
        name="diff_attn_sample",
    )(pt_flat, lam, qt_s, kt_s, vc3, subln_g, *([ck] * n_pages), *([cv] * n_pages))
    return out.reshape(n, d_att)


def _attn_fused_kernel(pt_ref, lam_ref, qt_ref, kb_ref, vt_ref, gcol_ref, qts_ref, kts_ref, vc_ref, g_ref, *rest,
                       n_pages, out_scale):
    pages = rest[:2 * n_pages]
    o_ref, os_ref, qcat, m, acc, s_even, s_odd = rest[2 * n_pages:]
    seq = pl.program_id(0) * pl.num_programs(1) + pl.program_id(1)
    sample = functools.partial(_attn_sample_body, seq, lam_ref, qts_ref, kts_ref, vc_ref, g_ref, *pages, os_ref,
                               n_pages=n_pages, out_scale=out_scale)
    _attn_kernel(lam_ref, qt_ref, kb_ref, vt_ref, gcol_ref, o_ref, qcat, m, acc, s_even, s_odd,
                 out_scale=out_scale, between=sample)


def _attn_prompt_and_sample(qt, kb, vt, subln_g, lam, out_scale, pt_flat, qt_s, kt_s, v_cur, ck, cv, layer):
    B, S, d_att = kb.shape
    G = d_att // V_DIM
    T = min(ATT_TILE, S)
    nq = S // T
    n = qt_s.shape[1]
    if n != B * G * nq:
        return (_attn_prompt(qt, kb, vt, subln_g, lam, out_scale),
                _attn_sample(pt_flat, lam, qt_s, kt_s, v_cur, subln_g, ck, cv, layer, out_scale))
    n_pages = pt_flat.shape[0] // n
    P = ck.shape[-1]
    nl = min(n, LANES)
    seq = lambda bg, i: bg * nq + i
    cols = lambda: pl.BlockSpec((d_att, nl), lambda bg, i, pt: (0, seq(bg, i) // nl))
    k_specs = [pl.BlockSpec((None, None, d_att, P), functools.partial(
        lambda bg, i, pt, j: (layer, pt[seq(bg, i) * n_pages + j], 0, 0), j=j)) for j in range(n_pages)]
    v_specs = [pl.BlockSpec((None, None, cv.shape[2], V_DIM), functools.partial(
        lambda bg, i, pt, j: (layer, pt[seq(bg, i) * n_pages + j], 0, 0), j=j)) for j in range(n_pages)]
    grid_spec = pltpu.PrefetchScalarGridSpec(
        num_scalar_prefetch=1,
        grid=(B * G, nq),
        in_specs=[
            pl.BlockSpec(memory_space=pltpu.SMEM),
            pl.BlockSpec((None, V_DIM, T), lambda bg, i, pt: (bg // G, bg % G, i)),
            pl.BlockSpec((None, S, V_DIM), lambda bg, i, pt: (bg // G, 0, bg % G)),
            pl.BlockSpec((None, V_DIM + ONES_ROWS, S), lambda bg, i, pt: (bg // G, bg % G, 0)),
            pl.BlockSpec((V_DIM, 1), lambda bg, i, pt: (0, 0)),
            cols(), cols(),
            pl.BlockSpec((None, 1, d_att), lambda bg, i, pt: (seq(bg, i), 0, 0)),
            pl.BlockSpec((1, V_DIM), lambda bg, i, pt: (0, 0)),
        ] + k_specs + v_specs,
        out_specs=[pl.BlockSpec((None, T, V_DIM), lambda bg, i, pt: (bg // G, i, bg % G)),
                   pl.BlockSpec((None, 1, d_att), lambda bg, i, pt: (seq(bg, i), 0, 0))],
        scratch_shapes=[pltpu.VMEM((V_DIM, 2 * T), BF16), pltpu.VMEM((1, 2 * T), F32),
                        pltpu.VMEM((V_DIM + ONES_ROWS, 2 * T), F32),
                        pltpu.VMEM((T, 2 * T), F32), pltpu.VMEM((T, 2 * T), F32)],
    )
    att, att_s = pl.pallas_call(
        functools.partial(_attn_fused_kernel, n_pages=n_pages, out_scale=out_scale),
        grid_spec=grid_spec,
        out_shape=[jax.ShapeDtypeStruct((B, S, d_att), BF16), jax.ShapeDtypeStruct((n, 1, d_att), BF16)],
        compiler_params=_cparams(("parallel", "arbitrary")),
        name="diff_attn_fused",
    )(pt_flat, lam, qt, kb, vt, subln_g.reshape(V_DIM, 1),
      qt_s, kt_s, v_cur.reshape(n, 1, d_att), subln_g,
      *([ck] * n_pages), *([cv] * n_pages))
    return att, att_s.reshape(n, d_att)


def _store_token_tiles(ref, x):
    T, D = x.shape
    n = D // LANES
    for c in range(n):
        ref[pl.ds(c, T, stride=n), :] = x[:, c * LANES:(c + 1) * LANES]


def _load_token_tiles(ref, T):
    n = ref.shape[0] // T
    return jnp.concatenate([ref[pl.ds(c, T, stride=n), :] for c in range(n)], axis=1)


def _outproj_kernel(yr_ref, at_ref, x_ref, wo_ref, g2_ref, wr_ref, br_ref, cnt_in_ref,
                    x1_ref, h2_ref, route_ref, route_t_ref, cnt_out_ref, cnt_scr, *, n_groups, per_group):
    step = pl.program_id(0)

    @pl.when(step == 0)
    def _():
        cnt_scr[...] = cnt_in_ref[...]

    cnt = _outproj_rows(pl.ds(0, x_ref.shape[0]), cnt_scr[...], yr_ref, at_ref, x_ref, wo_ref,
                        g2_ref, wr_ref, br_ref, x1_ref, h2_ref, route_ref, route_t_ref,
                        n_groups=n_groups, per_group=per_group)
    cnt_scr[...] = cnt
    cnt_out_ref[...] = cnt


def _outproj_rows(rows, cnt, yr_ref, at_ref, x_ref, wo_ref, g2_ref, wr_ref, br_ref,
                  x1_ref, h2_ref, route_ref, route_t_ref, *, n_groups, per_group):
    d_rnn = yr_ref.shape[1]
    y = (jnp.dot(yr_ref[rows, :], wo_ref[:d_rnn, :], preferred_element_type=F32)
         + jnp.dot(at_ref[rows, :], wo_ref[d_rnn:, :], preferred_element_type=F32))
    x1 = x_ref[rows, :] + y
    x1_ref[rows, :] = x1
    ms = jnp.mean(x1 * x1, axis=-1, keepdims=True)
    h2 = x1 * lax.rsqrt(ms + NORM_EPS) * g2_ref[...]
    T = x1.shape[0]
    n_tile = h2.shape[1] // LANES
    _store_token_tiles(h2_ref.at[pl.ds(rows.start * n_tile, T * n_tile)], h2)
    logits = jnp.dot(h2.astype(BF16), wr_ref[...], preferred_element_type=F32) + br_ref[...]
    lane = lax.broadcasted_iota(jnp.int32, (T, LANES), 1).astype(F32)
    first = lambda hit: jnp.min(jnp.where(hit, lane, float(LANES)), axis=-1, keepdims=True)
    gl = jnp.where(lane < n_groups, logits, NEG_INF)
    gmax = jnp.max(gl, axis=-1, keepdims=True)
    gidx = first(gl == gmax)
    pg = 1.0 / jnp.sum(jnp.exp(gl - gmax), axis=-1, keepdims=True)
    lo = n_groups + per_group * gidx
    el = jnp.where(jnp.logical_and(lane >= lo, lane < lo + per_group), logits, NEG_INF)
    ma = jnp.max(el, axis=-1, keepdims=True)
    ia = first(el == ma)
    el2 = jnp.where(lane == ia, NEG_INF, el)
    mb = jnp.max(el2, axis=-1, keepdims=True)
    ib = first(el2 == mb)
    eb = jnp.exp(mb - ma)
    gate_a = pg / (1.0 + eb)
    gate_b = pg * eb / (1.0 + eb)
    hit_a = lane == ia
    hit_b = lane == ib
    onehot = jnp.where(jnp.logical_or(hit_a, hit_b), 1.0, 0.0)
    r_i = lax.broadcasted_iota(jnp.int32, (T, T), 0)
    c_i = lax.broadcasted_iota(jnp.int32, (T, T), 1)
    lower = jnp.where(c_i < r_i, 1.0, 0.0).astype(BF16)
    before = jnp.dot(lower, onehot.astype(BF16), preferred_element_type=F32) + cnt
    rank_a = jnp.sum(jnp.where(hit_a, before, 0.0), axis=-1, keepdims=True)
    rank_b = jnp.sum(jnp.where(hit_b, before, 0.0), axis=-1, keepdims=True)
    cols = (ia - n_groups, ib - n_groups, gate_a, gate_b, rank_a, rank_b)
    route = jnp.zeros((T, LANES), F32)
    for c, val in enumerate(cols):
        route = jnp.where(lane == float(c), val, route)
    route_ref[rows, :] = route
    route_t_ref[:, rows] = route.T[:SUBLANES]
    return cnt + jnp.sum(onehot, axis=0, keepdims=True)


def _outproj_route(yr, at, x, wo, g2, wr, br, cnt_in, n_groups, per_group):
    N, D = x.shape
    T = min(ROW_TILE, N)
    d_rnn = yr.shape[1]
    row = lambda w: pl.BlockSpec((T, w), lambda i: (i, 0))
    full = lambda a: pl.BlockSpec(a.shape, lambda i: (0,) * a.ndim)
    return pl.pallas_call(
        functools.partial(_outproj_kernel, n_groups=n_groups, per_group=per_group),
        grid=(N // T,),
        in_specs=[row(d_rnn), row(at.shape[1]), row(D), full(wo), full(g2), full(wr), full(br), full(cnt_in)],
        out_specs=[row(D), pl.BlockSpec((T * (D // LANES), LANES), lambda i: (i, 0)), row(LANES),
                   pl.BlockSpec((SUBLANES, T), lambda i: (0, i)),
                   pl.BlockSpec((1, LANES), lambda i: (0, 0))],
        out_shape=[jax.ShapeDtypeStruct((N, D), F32), jax.ShapeDtypeStruct((N * (D // LANES), LANES), F32),
                   jax.ShapeDtypeStruct((N, LANES), F32), jax.ShapeDtypeStruct((SUBLANES, N), F32),
                   jax.ShapeDtypeStruct((1, LANES), F32)],
        scratch_shapes=[pltpu.VMEM((1, LANES), F32)],
        compiler_params=_cparams(("arbitrary",)),
        name="outproj_route",
    )(yr, at, x, wo, g2, wr, br, cnt_in)


def _dest_kernel(seg_ref, rt_ref, o_ref):
    e = rt_ref[0:TOP_K, :].astype(jnp.int32)
    start = jnp.zeros_like(e)
    for ex in range(seg_ref.shape[0]):
        start = jnp.where(e == ex, seg_ref[ex], start)
    o_ref[...] = start + rt_ref[4:4 + TOP_K, :].astype(jnp.int32)


def _dest(seg_start, route_t):
    R, N = route_t.shape
    T = min(ROW_TILE, N)
    grid_spec = pltpu.PrefetchScalarGridSpec(
        num_scalar_prefetch=1,
        grid=(N // T,),
        in_specs=[pl.BlockSpec((R, T), lambda i, seg: (0, i))],
        out_specs=pl.BlockSpec((TOP_K, T), lambda i, seg: (0, i)),
    )
    return pl.pallas_call(
        _dest_kernel,
        grid_spec=grid_spec,
        out_shape=jax.ShapeDtypeStruct((TOP_K, N), jnp.int32),
        compiler_params=_cparams(("arbitrary",)),
        name="moe_dest",
    )(seg_start, route_t)


def _dispatch_kernel(zb_ref, dest_ref, h_ref, *rest, zero_fill):
    xb_hbm, zbuf, sem = rest[-3], rest[-2], rest[-1]
    T = dest_ref.shape[1]
    n = h_ref.shape[0] // T

    if zero_fill:
        @pl.when(pl.program_id(0) == 0)
        def _():
            zbuf[...] = jnp.zeros_like(zbuf)

            def zcopy(i):
                row0 = pl.multiple_of(zb_ref[i] * (EXP_BLOCK * n), EXP_BLOCK * n)
                return pltpu.make_async_copy(zbuf, xb_hbm.at[pl.ds(row0, EXP_BLOCK * n)], sem)

            def zstart(i, c):
                @pl.when(zb_ref[i] >= 0)
                def _():
                    zcopy(i).start()
                return c

            def zwait(i, c):
                @pl.when(zb_ref[i] >= 0)
                def _():
                    zcopy(i).wait()
                return c

            lax.fori_loop(0, zb_ref.shape[0], zstart, 0)
            lax.fori_loop(0, zb_ref.shape[0], zwait, 0)

    def body(k, c):
        t0 = pl.multiple_of(k * SUBLANES, SUBLANES)
        group = h_ref.at[pl.ds(pl.multiple_of(t0 * n, SUBLANES * n), SUBLANES * n)]
        for j in range(SUBLANES):
            for s in range(TOP_K):
                row0 = pl.multiple_of(dest_ref[s, t0 + j] * n, n)
                pltpu.make_async_copy(group.at[pl.ds(j * n, n)],
                                      xb_hbm.at[pl.ds(row0, n)], sem).start(priority=s)
        return c

    lax.fori_loop(0, T // SUBLANES, body, 0, unroll=2)
    for s in range(TOP_K):
        pltpu.make_async_copy(h_ref, xb_hbm.at[pl.ds(0, T * n)], sem).wait()


def _dispatch(zb, dest, h, xb, xb_rows):
    N = dest.shape[1]
    n = h.shape[0] // N
    T = min(DISPATCH_TILE, N)
    dest3 = jnp.transpose(dest.reshape(TOP_K, N // T, T), (1, 0, 2))
    aliased = xb is not None
    in_specs = [pl.BlockSpec((None, TOP_K, T), lambda i, zb: (i, 0, 0), memory_space=pltpu.SMEM),
                pl.BlockSpec((T * n, LANES), lambda i, zb: (i, 0))]
    args = [zb, dest3, h]
    if aliased:
        in_specs.append(pl.BlockSpec(memory_space=pl.ANY))
        args.append(xb)
    grid_spec = pltpu.PrefetchScalarGridSpec(
        num_scalar_prefetch=1,
        grid=(N // T,),
        in_specs=in_specs,
        out_specs=pl.BlockSpec(memory_space=pl.ANY),
        scratch_shapes=[pltpu.VMEM((EXP_BLOCK * n, LANES), h.dtype), pltpu.SemaphoreType.DMA(())],
    )
    return pl.pallas_call(
        functools.partial(_dispatch_kernel, zero_fill=not aliased),
        grid_spec=grid_spec,
        out_shape=jax.ShapeDtypeStruct((xb_rows * n, LANES), h.dtype),
        input_output_aliases={3: 0} if aliased else {},
        compiler_params=pltpu.CompilerParams(dimension_semantics=("arbitrary",), has_side_effects=True),
        name="moe_dispatch",
    )(*args)


def _expert_kernel(be_ref, bi_ref, br_ref, bf_ref, x_ref, w1_ref, w2_ref, y_ref, w1b, w2b):
    b = pl.program_id(0)
    rows = br_ref[b]

    @pl.when(bf_ref[b] == 1)
    def _():
        w1b[...] = w1_ref[...].astype(BF16)
        w2b[...] = w2_ref[...].astype(BF16)

    @pl.when(rows > 0)
    def _():
        x = _load_token_tiles(x_ref, EXP_BLOCK).astype(BF16)
        h = jnp.dot(x, w1b[...], preferred_element_type=F32)
        de = h.shape[1] // 2
        act = (jax.nn.silu(h[:, :de]) * h[:, de:]).astype(BF16)
        _store_token_tiles(y_ref, jnp.dot(act, w2b[...], preferred_element_type=F32))

    @pl.when(rows == 0)
    def _():
        y_ref[...] = jnp.zeros_like(y_ref)


def _experts(be, bi, br, bf, xb, w1, w2, layer):
    nb = be.shape[0]
    D, d2 = w1.shape[-2:]
    de = w2.shape[-2]
    blk = EXP_BLOCK * (D // LANES)
    grid_spec = pltpu.PrefetchScalarGridSpec(
        num_scalar_prefetch=4,
        grid=(nb,),
        in_specs=[pl.BlockSpec((blk, LANES), lambda b, be, bi, br, bf: (bi[b], 0)),
                  pl.BlockSpec((None, None, D, d2), lambda b, be, bi, br, bf: (layer, be[b], 0, 0)),
                  pl.BlockSpec((None, None, de, D), lambda b, be, bi, br, bf: (layer, be[b], 0, 0))],
        out_specs=pl.BlockSpec((blk, LANES), lambda b, be, bi, br, bf: (b, 0)),
        scratch_shapes=[pltpu.VMEM((D, d2), BF16), pltpu.VMEM((de, D), BF16)],
    )
    return pl.pallas_call(
        _expert_kernel,
        grid_spec=grid_spec,
        out_shape=jax.ShapeDtypeStruct(xb.shape, F32),
        compiler_params=_cparams(("arbitrary",)),
        name="moe_experts",
    )(be, bi, br, bf, xb, w1, w2)


def _combine_kernel(dest_ref, dest_next_ref, x1_ref, route_ref, gf_ref, yb_hbm, o_ref, ybuf, sem, *, final):
    T = x1_ref.shape[0]
    n = ybuf.shape[2] // T
    i = pl.program_id(0)
    slot = i % 2

    def issue(dref, sl):
        def body(k, c):
            t0 = pl.multiple_of(k * SUBLANES, SUBLANES)
            for s in range(TOP_K):
                group = ybuf.at[sl, s, pl.ds(pl.multiple_of(t0 * n, SUBLANES * n), SUBLANES * n)]
                for j in range(SUBLANES):
                    row0 = pl.multiple_of(dref[s, t0 + j] * n, n)
                    pltpu.make_async_copy(yb_hbm.at[pl.ds(row0, n)],
                                          group.at[pl.ds(j * n, n)], sem.at[sl]).start(priority=s)
            return c

        lax.fori_loop(0, T // SUBLANES, body, 0, unroll=2)

    @pl.when(i == 0)
    def _():
        issue(dest_ref, 0)

    @pl.when(i + 1 < pl.num_programs(0))
    def _():
        issue(dest_next_ref, 1 - slot)

    for s in range(TOP_K):
        pltpu.make_async_copy(yb_hbm.at[pl.ds(0, T * n)], ybuf.at[slot, s], sem.at[slot]).wait()
    route = route_ref[...]
    out = (x1_ref[...] + _load_token_tiles(ybuf.at[slot, 0], T) * route[:, 2:3]
           + _load_token_tiles(ybuf.at[slot, 1], T) * route[:, 3:4])
    if final:
        ms = jnp.mean(out * out, axis=-1, keepdims=True)
        out = out * lax.rsqrt(ms + NORM_EPS) * gf_ref[...]
    o_ref[...] = out


def _combine(dest, x1, route, gf, yb, final):
    N, D = x1.shape
    T = min(COMBINE_TILE, N)
    dest3 = jnp.transpose(dest.reshape(TOP_K, N // T, T), (1, 0, 2))
    last = N // T - 1
    return pl.pallas_call(
        functools.partial(_combine_kernel, final=final),
        grid=(N // T,),
        in_specs=[pl.BlockSpec((None, TOP_K, T), lambda i: (i, 0, 0), memory_space=pltpu.SMEM),
                  pl.BlockSpec((None, TOP_K, T), lambda i: (jnp.minimum(i + 1, last), 0, 0),
                               memory_space=pltpu.SMEM),
                  pl.BlockSpec((T, D), lambda i: (i, 0)),
                  pl.BlockSpec((T, LANES), lambda i: (i, 0)),
                  pl.BlockSpec((1, D), lambda i: (0, 0)),
                  pl.BlockSpec(memory_space=pl.ANY)],
        out_specs=pl.BlockSpec((T, D), lambda i: (i, 0)),
        out_shape=jax.ShapeDtypeStruct((N, D), F32),
        scratch_shapes=[pltpu.VMEM((2, TOP_K, T * (D // LANES), LANES), F32), pltpu.SemaphoreType.DMA((2,))],
        compiler_params=_cparams(("arbitrary",)),
        name="moe_combine",
    )(dest3, dest3, x1, route, gf, yb)


def _rope_tables(pos):
    inv = ROPE_THETA ** (-jnp.arange(HALF, dtype=F32) / HALF)
    ang = pos.astype(F32)[:, None] * inv[None, :]
    cos, sin = jnp.cos(ang), jnp.sin(ang)
    cq = jnp.tile(cos, (1, LANES // HALF))
    sq = jnp.concatenate([-sin, -sin, sin, sin], axis=1)
    return cq, sq, cos.T, sin.T


def _attention_order(w):
    D, d_att = w.shape
    w = w.reshape(D, d_att // V_DIM, 2, 2, HALF)
    return jnp.swapaxes(w, 2, 3).reshape(D, d_att)


def _gate_weights(w_ga, w_gx):
    nb, bw, _ = w_ga.shape
    half = nb // 2
    eye = jnp.eye(half, dtype=w_ga.dtype)

    def dense(w):
        return jnp.einsum('nij,nm->nimj', w, eye).reshape(half * bw, half * bw)

    return jnp.stack([jnp.concatenate([dense(w_ga[h * half:(h + 1) * half]),
                                       dense(w_gx[h * half:(h + 1) * half])], axis=1)
                      for h in range(2)]).astype(BF16)


def _block_tables(counts, nb):
    ne = counts.shape[0]
    nblk = (counts + EXP_BLOCK - 1) // EXP_BLOCK
    cum = jnp.sum(jnp.where(jnp.arange(ne)[None, :] <= jnp.arange(ne)[:, None], nblk[None, :], 0), axis=1)
    total = cum[-1]
    b = jnp.arange(nb, dtype=jnp.int32)
    bc = jnp.minimum(b, total - 1)
    e = jnp.minimum(jnp.sum(bc[:, None] >= cum[None, :], axis=1), ne - 1).astype(jnp.int32)
    pick = lambda tab: jnp.sum(jnp.where(e[:, None] == jnp.arange(ne)[None, :], tab[None, :], 0), axis=1)
    off = bc - (pick(cum) - pick(nblk))
    valid = b < total
    rows = jnp.where(valid, jnp.clip(pick(counts) - off * EXP_BLOCK, 0, EXP_BLOCK), 0)
    first = jnp.where(jnp.logical_and(valid, off == 0), 1, 0)
    last_blk = jnp.where(nblk > 0, cum - 1, -1)
    trail = total + jnp.arange(ne, dtype=jnp.int32)
    trail = jnp.where(trail < nb, trail, -1)
    zb = jnp.concatenate([last_blk, trail]).astype(jnp.int32)
    seg_start = ((cum - nblk) * EXP_BLOCK).astype(jnp.int32)
    return seg_start, e, bc.astype(jnp.int32), rows.astype(jnp.int32), first.astype(jnp.int32), zb


def kernel(x_prompt, x_sample, cache_k, cache_v, state_h, state_conv, page_table, ln1_g, w_in, conv_w, conv_b,
           w_ga, b_ga, w_gx, b_gx, lru_lambda, lam_q1, lam_k1, lam_q2, lam_k2, subln_g, w_out, ln2_g,
           w_route_group, b_route_group, w_route_expert, b_route_expert, w_exp_in, w_exp_out, ln_f):
    Bp, Sp, D = x_prompt.shape
    Bs = x_sample.shape[0]
    depth = w_in.shape[0]
    d_rnn = state_h.shape[-1]
    d_att = cache_k.shape[-2] * cache_k.shape[-1]
    n_groups = w_route_group.shape[-1]
    n_experts = w_route_expert.shape[-1]
    per_group = n_experts // n_groups
    n_pool, page = cache_k.shape[1], cache_k.shape[2]
    n_pages = page_table.shape[1]
    past_len = n_pages * page
    Np = Bp * Sp
    assert x_sample.shape[1] == 1 and n_groups + n_experts <= LANES

    cq, sq, ct, st = _rope_tables(jnp.arange(Sp))
    cq_s, sq_s, ct_s, st_s = _rope_tables(jnp.full((Bs,), past_len))
    qscale = HEAD_DIM ** -0.5
    ck = jnp.transpose(cache_k, (0, 1, 3, 4, 2)).reshape(depth, n_pool, d_att, page)
    cv = cache_v.reshape(depth, n_pool, page * (d_att // V_DIM), V_DIM)
    pt_flat = page_table.reshape(-1).astype(jnp.int32)
    conv_s = jnp.transpose(state_conv, (0, 2, 1, 3))

    n_blocks = (TOP_K * (Np + Bs)) // EXP_BLOCK + n_experts

    xp = x_prompt
    xs = x_sample.reshape(Bs, D)
    kt_all = v_all = None
    hp_l, cp_l, ks_l, vs_l, hs_l, cs_l = [], [], [], [], [], []
    for l in range(depth):
        lambda_init = 0.8 - 0.6 * math.exp(-0.3 * l)
        out_scale = 1.0 - lambda_init
        lam = (jnp.exp(jnp.sum(lam_q1[l] * lam_k1[l])) - jnp.exp(jnp.sum(lam_q2[l] * lam_k2[l]))
               + lambda_init).reshape(1, 1).astype(F32)
        wl = w_in[l]
        w_xg = wl[:, :2 * d_rnn]
        w_q = wl[:, 2 * d_rnn:2 * d_rnn + d_att] * qscale
        w_k = wl[:, 2 * d_rnn + d_att:2 * d_rnn + 2 * d_att]
        w_v = wl[:, 2 * d_rnn + 2 * d_att:]
        wm_p = jnp.concatenate([w_xg, _attention_order(w_k), w_v], axis=1).astype(BF16)
        wt_p = jnp.concatenate([_attention_order(w_q), w_v], axis=1).T.astype(BF16)
        g1 = ln1_g[l].reshape(1, D)
        wg = _gate_weights(w_ga[l], w_gx[l])
        bga = b_ga[l].reshape(1, d_rnn)
        bgx = b_gx[l].reshape(1, d_rnn)
        lru_l = lru_lambda[l].reshape(1, d_rnn)
        cw = conv_w[l]
        cb = conv_b[l].reshape(1, d_rnn)
        sg = subln_g[l].reshape(1, V_DIM)
        wo = w_out[l].astype(BF16)
        g2 = ln2_g[l].reshape(1, D)
        n_route = n_groups + n_experts
        wr = jnp.concatenate([w_route_group[l], w_route_expert[l], jnp.zeros((D, LANES - n_route), F32)],
                             axis=1).astype(BF16)
        br = jnp.concatenate([b_route_group[l], b_route_expert[l], jnp.zeros((LANES - n_route,), F32)]
                             ).reshape(1, LANES)

        xr, gr, kb, v_all, qt, kt_all, vt = _inproj_prompt(xp, g1, wm_p, wt_p, cq, sq, ct, st, kt_all, v_all)
        y_rnn, h_last, c_tail = _lru_prompt(xr, gr, cw, cb, wg, bga, bgx, lru_l)
        xr_s, gr_s, v_s, kt_s, qt_s = _inproj_sample(xs, g1, wm_p, wt_p, cq_s, sq_s, ct_s, st_s)
        y_rnn_s, h_s, c_s = _lru_step(xr_s, gr_s, conv_s[l], state_h[l], cw, cb, wg, bga, bgx, lru_l)
        k_s = kt_s.T
        att, att_s = _attn_prompt_and_sample(qt, kb, vt, sg, lam, out_scale,
                                             pt_flat, qt_s, kt_s, v_s, ck, cv, l)
        cnt0 = jnp.zeros((1, LANES), F32)
        x1, h2, route, route_t, cnt1 = _outproj_route(
            y_rnn.reshape(Np, d_rnn), att.reshape(Np, d_att), xp.reshape(Np, D), wo, g2, wr, br, cnt0,
            n_groups, per_group)
        x1_s, h2_s, route_s, route_ts, cnt2 = _outproj_route(y_rnn_s, att_s, xs, wo, g2, wr, br, cnt1,
                                                             n_groups, per_group)
        counts = cnt2[0, n_groups:n_groups + n_experts].astype(jnp.int32)
        seg_start, be, bi, brows, bfirst, zb = _block_tables(counts, n_blocks)

        dest_p, dest_s = _dest(seg_start, route_t), _dest(seg_start, route_ts)
        xb = _dispatch(zb, dest_p, h2, None, n_blocks * EXP_BLOCK)
        xb = _dispatch(zb, dest_s, h2_s, xb, n_blocks * EXP_BLOCK)
        yb = _experts(be, bi, brows, bfirst, xb, w_exp_in, w_exp_out, l)
        final = l == depth - 1
        gf = ln_f.reshape(1, D)
        xp_new = _combine(dest_p, x1, route, gf, yb, final)
        xs = _combine(dest_s, x1_s, route_s, gf, yb, final)
        xp = xp_new.reshape(Bp, Sp, D)

        n_qk = d_att // HEAD_DIM
        hp_l.append(h_last.reshape(Bp, d_rnn))
        cp_l.append(c_tail)
        ks_l.append(k_s.reshape(Bs, 1, n_qk, HEAD_DIM))
        vs_l.append(v_s.reshape(Bs, 1, d_att // V_DIM, V_DIM))
        hs_l.append(h_s)
        cs_l.append(jnp.transpose(c_s, (1, 0, 2)))
    k_prompt = jnp.transpose(kt_all.reshape(depth, Bp, n_qk, HEAD_DIM, Sp), (0, 1, 4, 2, 3))
    v_prompt = v_all.reshape(depth, Bp, Sp, d_att // V_DIM, V_DIM)
    return (xp, xs.reshape(Bs, 1, D),
            k_prompt, v_prompt, jnp.stack(hp_l), jnp.stack(cp_l),
            jnp.stack(ks_l), jnp.stack(vs_l), jnp.stack(hs_l), jnp.stack(cs_l))
```

```python
import functools
import math

import jax
import jax.numpy as jnp
from jax import lax
from jax.experimental import pallas as pl
from jax.experimental.pallas import tpu as pltpu

F32 = jnp.float32
BF16 = jnp.bfloat16

HEAD_DIM = 64
V_DIM = 2 * HEAD_DIM
HALF = HEAD_DIM // 2
ONES_ROWS = 16
CONV_W = 4
LRU_C = 8.0
ROPE_THETA = 10000.0
NORM_EPS = 1e-6
SUBLN_EPS = 1e-5
NEG_INF = -1e30
TOP_K = 2

LANES = 128
SUBLANES = 8
VMEM_LIMIT = 48 * 1024 * 1024
INPROJ_VMEM_LIMIT = 56 * 1024 * 1024

ROW_TILE = 512
LRU_TILE = 512
ATT_TILE = 512
EXP_BLOCK = 512
DISPATCH_TILE = 512
COMBINE_TILE = 512


def _cparams(sem, vmem_limit=VMEM_LIMIT):
    return pltpu.CompilerParams(dimension_semantics=sem, vmem_limit_bytes=vmem_limit)


def _inproj_kernel(x_ref, g_ref, wm_ref, wt_ref, cq_ref, sq_ref, ct_ref, st_ref, *rest, sample, n_prev=0):
    prev, outs = rest[:2 if n_prev else 0], rest[2 if n_prev else 0:]
    x = x_ref[...]
    ms = jnp.mean(x * x, axis=-1, keepdims=True)
    h = (x * lax.rsqrt(ms + NORM_EPS) * g_ref[...]).astype(BF16)
    main = jnp.dot(h, wm_ref[...], preferred_element_type=F32)
    d_rnn = outs[0].shape[1]
    ct = ct_ref[...]
    st = st_ref[...]
    tr = lax.dot_general(wt_ref[...], h, (((1,), (1,)), ((), ())), preferred_element_type=F32)

    d_att = wt_ref.shape[0] // 2
    cq = cq_ref[...]
    sq = sq_ref[...]

    def roped_k(c):
        kc = main[:, 2 * d_rnn + c * LANES:2 * d_rnn + (c + 1) * LANES]
        return kc * cq + pltpu.roll(kc, HEAD_DIM, 1) * sq

    def store_kt(dst, c, kr):
        kt = kr.T
        for jj in range(2):
            r0 = (2 * c + jj) * HEAD_DIM
            dst[r0:r0 + HALF, :] = kt[jj * HALF:(jj + 1) * HALF]
            dst[r0 + HALF:r0 + HEAD_DIM, :] = kt[HEAD_DIM + jj * HALF:HEAD_DIM + (jj + 1) * HALF]

    if sample:
        xr_ref, gr_ref, v_ref, kt_ref, qt_ref = outs
        xr_ref[...] = main[:, :d_rnn]
        gr_ref[...] = main[:, d_rnn:2 * d_rnn]
        v_ref[...] = main[:, 2 * d_rnn + d_att:]
        for c in range(d_att // LANES):
            store_kt(kt_ref, c, roped_k(c))
        for hh in range(d_att // HEAD_DIM):
            g, jj = divmod(hh, 2)
            r1 = g * V_DIM + jj * HALF
            x1 = tr[r1:r1 + HALF]
            x2 = tr[r1 + HEAD_DIM:r1 + HEAD_DIM + HALF]
            qt_ref[hh * HEAD_DIM:hh * HEAD_DIM + HALF, :] = x1 * ct - x2 * st
            qt_ref[hh * HEAD_DIM + HALF:(hh + 1) * HEAD_DIM, :] = x2 * ct + x1 * st
    else:
        xr_ref, gr_ref, kb_ref, v_ref, qt_ref, kt_ref, vt_ref = outs
        xr_ref[...] = main[:, :d_rnn]
        gr_ref[...] = main[:, d_rnn:2 * d_rnn]
        if n_prev:
            kt_ref[:n_prev] = prev[0][...]
            v_ref[:n_prev] = prev[1][...]
        v = main[:, 2 * d_rnn + d_att:]
        T = v.shape[0]
        G = d_att // V_DIM
        for g in range(G):
            v_ref[n_prev, pl.ds(g, T, stride=G), :] = v[:, g * V_DIM:(g + 1) * V_DIM]
        for c in range(d_att // LANES):
            kr = roped_k(c)
            kb_ref[:, c * LANES:(c + 1) * LANES] = kr.astype(BF16)
            store_kt(kt_ref.at[n_prev], c, kr)
        for hh in range(d_att // HEAD_DIM):
            g, jj = divmod(hh, 2)
            r1 = g * V_DIM + jj * HALF
            x1 = tr[r1:r1 + HALF]
            x2 = tr[r1 + HEAD_DIM:r1 + HEAD_DIM + HALF]
            qt_ref[r1:r1 + HALF, :] = (x1 * ct - x2 * st).astype(BF16)
            qt_ref[r1 + HEAD_DIM:r1 + HEAD_DIM + HALF, :] = (x2 * ct + x1 * st).astype(BF16)
        for g in range(G):
            r0 = g * (V_DIM + ONES_ROWS)
            vt_ref[r0:r0 + V_DIM, :] = tr[d_att + g * V_DIM:d_att + (g + 1) * V_DIM].astype(BF16)
            vt_ref[r0 + V_DIM:r0 + V_DIM + ONES_ROWS, :] = jnp.ones((ONES_ROWS, T), BF16)


def _inproj_prompt(x, g, wm, wt, cq, sq, ct, st, k_prev, v_prev):
    B, S, D = x.shape
    T = min(ROW_TILE, S)
    nt = S // T
    d_att = wt.shape[0] // 2
    d_rnn = (wm.shape[1] - 2 * d_att) // 2
    G = d_att // V_DIM
    n_prev = 0 if k_prev is None else k_prev.shape[0]
    row = lambda w: pl.BlockSpec((None, T, w), lambda b, t: (b, t, 0))
    full = lambda a: pl.BlockSpec(a.shape, lambda b, t: (0,) * a.ndim)
    once = lambda a: pl.BlockSpec(a.shape, lambda b, t: (0,) * a.ndim, pipeline_mode=pl.Buffered(1))
    tr_spec = pl.BlockSpec((None, d_att, T), lambda b, t: (b, 0, t))
    kstack = lambda n: pl.BlockSpec((n, None, d_att, T), lambda b, t: (0, b, 0, t))
    vstack = lambda n: pl.BlockSpec((n, None, T * G, V_DIM), lambda b, t: (0, b, t, 0))
    prev_specs = [kstack(n_prev), vstack(n_prev)] if n_prev else []
    prev_args = [k_prev, v_prev] if n_prev else []
    return pl.pallas_call(
        functools.partial(_inproj_kernel, sample=False, n_prev=n_prev),
        grid=(B, nt),
        in_specs=[row(D), full(g), once(wm), once(wt),
                  pl.BlockSpec((T, LANES), lambda b, t: (t, 0)),
                  pl.BlockSpec((T, LANES), lambda b, t: (t, 0)),
                  pl.BlockSpec((HALF, T), lambda b, t: (0, t)),
                  pl.BlockSpec((HALF, T), lambda b, t: (0, t))] + prev_specs,
        out_specs=[row(d_rnn), row(d_rnn), row(d_att), vstack(n_prev + 1), tr_spec, kstack(n_prev + 1),
                   pl.BlockSpec((None, G * (V_DIM + ONES_ROWS), T), lambda b, t: (b, 0, t))],
        out_shape=[jax.ShapeDtypeStruct((B, S, d_rnn), F32), jax.ShapeDtypeStruct((B, S, d_rnn), F32),
                   jax.ShapeDtypeStruct((B, S, d_att), BF16),
                   jax.ShapeDtypeStruct((n_prev + 1, B, S * G, V_DIM), F32),
                   jax.ShapeDtypeStruct((B, d_att, S), BF16),
                   jax.ShapeDtypeStruct((n_prev + 1, B, d_att, S), F32),
                   jax.ShapeDtypeStruct((B, G * (V_DIM + ONES_ROWS), S), BF16)],
        compiler_params=_cparams(("parallel", "arbitrary"), INPROJ_VMEM_LIMIT),
        name="inproj_prompt",
    )(x, g, wm, wt, cq, sq, ct, st, *prev_args)


def _inproj_sample(x, g, wm, wt, cq, sq, ct, st):
    n, D = x.shape
    d_att = wt.shape[0] // 2
    d_rnn = (wm.shape[1] - 2 * d_att) // 2
    full = lambda a: pl.BlockSpec(a.shape, lambda i: (0,) * a.ndim)
    out_shapes = [jax.ShapeDtypeStruct((n, d_rnn), F32), jax.ShapeDtypeStruct((n, d_rnn), F32),
                  jax.ShapeDtypeStruct((n, d_att), F32), jax.ShapeDtypeStruct((d_att, n), F32),
                  jax.ShapeDtypeStruct((d_att, n), F32)]
    return pl.pallas_call(
        functools.partial(_inproj_kernel, sample=True),
        grid=(1,),
        in_specs=[full(x), full(g), full(wm), full(wt), full(cq), full(sq), full(ct), full(st)],
        out_specs=[pl.BlockSpec(s.shape, lambda i: (0, 0)) for s in out_shapes],
        out_shape=out_shapes,
        compiler_params=_cparams(("arbitrary",)),
        name="inproj_sample",
    )(x, g, wm, wt, cq, sq, ct, st)


def _gates(xc, wg_ref, bga, bgx, lam):
    half = xc.shape[1] // 2
    xcb = xc.astype(BF16)
    g0 = jnp.dot(xcb[:, :half], wg_ref[0], preferred_element_type=F32)
    g1 = jnp.dot(xcb[:, half:], wg_ref[1], preferred_element_type=F32)
    r = jax.nn.sigmoid(jnp.concatenate([g0[:, :half], g1[:, :half]], axis=1) + bga)
    i = jax.nn.sigmoid(jnp.concatenate([g0[:, half:], g1[:, half:]], axis=1) + bgx)
    neg = -lam
    softplus = jnp.maximum(neg, 0.0) + jnp.log1p(jnp.exp(-jnp.abs(neg)))
    log_a = -LRU_C * r * softplus
    a = jnp.exp(log_a)
    mult = jnp.sqrt(1.0 - a * a)
    return a, i, mult


def _lru_kernel(xr_ref, gr_ref, cw_ref, cb_ref, wg_ref, bga_ref, bgx_ref, lam_ref,
                y_ref, hlast_ref, ctail_ref, xp_scr, a_scr, b_scr, h_scr):
    t = pl.program_id(1)
    T, C = xr_ref.shape

    @pl.when(t == 0)
    def _():
        xp_scr[0:SUBLANES, :] = jnp.zeros((SUBLANES, C), F32)
        h_scr[...] = jnp.zeros_like(h_scr)

    x = xr_ref[...]
    xp_scr[SUBLANES:SUBLANES + T, :] = x
    cw = cw_ref[...]
    xc = cb_ref[...]
    for j in range(CONV_W - 1):
        xc = xc + xp_scr[pl.ds(SUBLANES - (CONV_W - 1) + j, T), :] * cw[j:j + 1]
    xc = xc + x * cw[CONV_W - 1:CONV_W]
    tail = x[T - (CONV_W - 1):, :]
    xp_scr[SUBLANES - (CONV_W - 1):SUBLANES, :] = tail
    ctail_ref[...] = tail

    a, i, mult = _gates(xc, wg_ref, bga_ref[...], bgx_ref[...], lam_ref[...])
    row = lax.broadcasted_iota(jnp.int32, (T, 1), 0)
    mult = jnp.where(jnp.logical_and(row == 0, t == 0), 1.0, mult)
    a_scr[...] = a
    b_scr[...] = mult * i * xc

    rowi = lax.broadcasted_iota(jnp.int32, (SUBLANES, C), 0)

    def body(gi, h):
        off = pl.multiple_of(gi * SUBLANES, SUBLANES)
        A = a_scr[pl.ds(off, SUBLANES), :]
        Bv = b_scr[pl.ds(off, SUBLANES), :]
        for d in (1, 2, 4):
            keep = rowi >= d
            Bv = jnp.where(keep, Bv + A * pltpu.roll(Bv, d, 0), Bv)
            A = jnp.where(keep, A * pltpu.roll(A, d, 0), A)
        hrows = Bv + A * h
        b_scr[pl.ds(off, SUBLANES), :] = hrows
        return hrows[SUBLANES - 1:SUBLANES, :]

    h = lax.fori_loop(0, T // SUBLANES, body, h_scr[...], unroll=2)
    h_scr[...] = h
    hlast_ref[...] = h
    y_ref[...] = (b_scr[...] * jax.nn.gelu(gr_ref[...])).astype(BF16)


def _lru_prompt(xr, gr, cw, cb, wg, bga, bgx, lam):
    B, S, C = xr.shape
    T = min(LRU_TILE, S)
    nt = S // T
    row = pl.BlockSpec((None, T, C), lambda b, t: (b, t, 0))
    full = lambda a: pl.BlockSpec(a.shape, lambda b, t: (0,) * a.ndim)
    return pl.pallas_call(
        _lru_kernel,
        grid=(B, nt),
        in_specs=[row, row, full(cw), full(cb), full(wg), full(bga), full(bgx), full(lam)],
        out_specs=[row, pl.BlockSpec((None, 1, C), lambda b, t: (b, 0, 0)),
                   pl.BlockSpec((None, CONV_W - 1, C), lambda b, t: (b, 0, 0))],
        out_shape=[jax.ShapeDtypeStruct((B, S, C), BF16), jax.ShapeDtypeStruct((B, 1, C), F32),
                   jax.ShapeDtypeStruct((B, CONV_W - 1, C), F32)],
        scratch_shapes=[pltpu.VMEM((T + SUBLANES, C), F32), pltpu.VMEM((T, C), F32),
                        pltpu.VMEM((T, C), F32), pltpu.VMEM((1, C), F32)],
        compiler_params=_cparams(("parallel", "arbitrary")),
        name="rglru_prompt",
    )(xr, gr, cw, cb, wg, bga, bgx, lam)


def _lru_step_kernel(xr_ref, gr_ref, conv_ref, h0_ref, cw_ref, cb_ref, wg_ref, bga_ref, bgx_ref, lam_ref,
                     y_ref, hnew_ref, cnew_ref):
    x = xr_ref[...]
    cw = cw_ref[...]
    xc = cb_ref[...]
    for j in range(CONV_W - 1):
        xc = xc + conv_ref[j] * cw[j:j + 1]
    xc = xc + x * cw[CONV_W - 1:CONV_W]
    for j in range(CONV_W - 2):
        cnew_ref[j] = conv_ref[j + 1]
    cnew_ref[CONV_W - 2] = x
    a, i, mult = _gates(xc, wg_ref, bga_ref[...], bgx_ref[...], lam_ref[...])
    h = a * h0_ref[...] + mult * i * xc
    hnew_ref[...] = h
    y_ref[...] = (h * jax.nn.gelu(gr_ref[...])).astype(BF16)


def _lru_step(xr, gr, conv, h0, cw, cb, wg, bga, bgx, lam):
    n, C = xr.shape
    args = (xr, gr, conv, h0, cw, cb, wg, bga, bgx, lam)
    full = lambda a: pl.BlockSpec(a.shape, lambda i: (0,) * a.ndim)
    out_shapes = [jax.ShapeDtypeStruct((n, C), BF16), jax.ShapeDtypeStruct((n, C), F32),
                  jax.ShapeDtypeStruct((CONV_W - 1, n, C), F32)]
    return pl.pallas_call(
        _lru_step_kernel,
        grid=(1,),
        in_specs=[full(a) for a in args],
        out_specs=[pl.BlockSpec(s.shape, lambda i, nd=len(s.shape): (0,) * nd) for s in out_shapes],
        out_shape=out_shapes,
        compiler_params=_cparams(("arbitrary",)),
        name="rglru_step",
    )(*args)


def _subln(o, g, scale):
    ms = jnp.mean(o * o, axis=-1, keepdims=True)
    return o * lax.rsqrt(ms + SUBLN_EPS) * g * scale


def _attn_kernel(lam_ref, qt_ref, kb_ref, vt_ref, g_ref, o_ref, qcat, m, acc, s_even, s_odd, *, out_scale,
                 between=None):
    qi = pl.program_id(1)
    Tq = qt_ref.shape[1]
    Tk = Tq

    qt = qt_ref[...]
    row = lax.broadcasted_iota(jnp.int32, qt.shape, 0)
    second = (row & HALF) != 0
    zero = jnp.zeros_like(qt)
    qcat[:, :Tq] = jnp.where(second, zero, qt)
    qcat[:, Tq:] = jnp.where(second, qt, zero)
    m[...] = jnp.full_like(m, NEG_INF)
    acc[...] = jnp.zeros_like(acc)

    def scores(kj, sbuf):
        k0 = pl.multiple_of(kj * Tk, Tk)
        sbuf[...] = jnp.dot(kb_ref[pl.ds(k0, Tk), :], qcat[...], preferred_element_type=F32)

    def update(kj, sbuf, diagonal):
        k0 = pl.multiple_of(kj * Tk, Tk)
        s = sbuf[...]
        if diagonal:
            key = lax.broadcasted_iota(jnp.int32, s.shape, 0)
            qry = lax.broadcasted_iota(jnp.int32, s.shape, 1)
            qry = jnp.where(qry >= Tq, qry - Tq, qry)
            s = jnp.where(key <= qry, s, NEG_INF)
        m_old = m[...]
        m_new = jnp.maximum(m_old, jnp.max(s, axis=0, keepdims=True))
        alpha = jnp.exp(m_old - m_new)
        p = jnp.exp(s - m_new)
        acc[...] = alpha * acc[...] + jnp.dot(vt_ref[:, pl.ds(k0, Tk)], p.astype(BF16),
                                              preferred_element_type=F32)
        m[...] = m_new

    def finish():
        a = acc[:V_DIM, :] / acc[V_DIM:V_DIM + 1, :]
        o = a[:, :Tq] - lam_ref[0, 0] * a[:, Tq:]
        ms = jnp.mean(o * o, axis=0, keepdims=True)
        o = o * lax.rsqrt(ms + SUBLN_EPS) * (g_ref[...] * out_scale)
        o_ref[...] = o.T.astype(o_ref.dtype)

    scores(0, s_even)
    if between is not None:
        between()

    def pair(i, c):
        t = 2 * i
        scores(t + 1, s_odd)
        update(t, s_even, False)
        scores(t + 2, s_even)
        update(t + 1, s_odd, False)
        return c

    lax.fori_loop(0, qi // 2, pair, 0)

    @pl.when(qi % 2 == 0)
    def _():
        update(qi, s_even, True)
        finish()

    @pl.when(qi % 2 == 1)
    def _():
        scores(qi, s_odd)
        update(qi - 1, s_even, False)
        update(qi, s_odd, True)
        finish()


def _attn_prompt(qt, kb, vt, subln_g, lam, out_scale):
    B, S, d_att = kb.shape
    G = d_att // V_DIM
    T = min(ATT_TILE, S)
    return pl.pallas_call(
        functools.partial(_attn_kernel, out_scale=out_scale),
        grid=(B * G, S // T),
        in_specs=[
            pl.BlockSpec(memory_space=pltpu.SMEM),
            pl.BlockSpec((None, V_DIM, T), lambda bg, i: (bg // G, bg % G, i)),
            pl.BlockSpec((None, S, V_DIM), lambda bg, i: (bg // G, 0, bg % G)),
            pl.BlockSpec((None, V_DIM + ONES_ROWS, S), lambda bg, i: (bg // G, bg % G, 0)),
            pl.BlockSpec((V_DIM, 1), lambda bg, i: (0, 0)),
        ],
        out_specs=pl.BlockSpec((None, T, V_DIM), lambda bg, i: (bg // G, i, bg % G)),
        out_shape=jax.ShapeDtypeStruct((B, S, d_att), BF16),
        scratch_shapes=[pltpu.VMEM((V_DIM, 2 * T), BF16), pltpu.VMEM((1, 2 * T), F32),
                        pltpu.VMEM((V_DIM + ONES_ROWS, 2 * T), F32),
                        pltpu.VMEM((T, 2 * T), F32), pltpu.VMEM((T, 2 * T), F32)],
        compiler_params=_cparams(("parallel", "arbitrary")),
        name="diff_attn_prompt",
    )(lam, qt, kb, vt, subln_g.reshape(V_DIM, 1))


def _attn_sample_body(seq, lam_ref, qt_ref, kt_ref, vc_ref, g_ref, *rest, n_pages, out_scale):
    k_refs = rest[:n_pages]
    v_refs = rest[n_pages:2 * n_pages]
    o_ref = rest[2 * n_pages]
    lam = lam_ref[0, 0]
    d_att, n_lanes = qt_ref.shape
    n_heads = d_att // HEAD_DIM
    G = d_att // V_DIM
    R = 2 * SUBLANES
    mine = lax.broadcasted_iota(jnp.int32, (d_att, n_lanes), 1) == seq % n_lanes
    qcol = jnp.sum(jnp.where(mine, qt_ref[...], 0.0), axis=1, keepdims=True)
    kcol = jnp.sum(jnp.where(mine, kt_ref[...], 0.0), axis=1, keepdims=True)
    P = k_refs[0].shape[1]
    qcolb = jnp.broadcast_to(qcol, (d_att, P))
    per_head = lambda x: jnp.sum(x.reshape(n_heads, HEAD_DIM, x.shape[-1]), axis=1)
    s = jnp.concatenate([per_head(k_refs[j][...] * qcolb) for j in range(n_pages)], axis=1)
    s_cur = per_head(qcol * kcol)
    m = jnp.maximum(jnp.max(s, axis=-1, keepdims=True), s_cur)
    p = jnp.exp(s - m)
    p_cur = jnp.exp(s_cur - m)
    inv_l = 1.0 / (jnp.sum(p, axis=-1, keepdims=True) + p_cur)
    p = p * inv_l
    p_cur = p_cur * inv_l
    amap = p - lam * pltpu.roll(p, n_heads - 1, 0)
    amap = jnp.concatenate([amap, jnp.zeros_like(amap)], axis=0).astype(BF16)
    p_cur = jnp.broadcast_to(p_cur, (n_heads, LANES))
    amap_cur = p_cur - lam * pltpu.roll(p_cur, n_heads - 1, 0)
    outs = []
    for g in range(G):
        acc = jnp.zeros((R, V_DIM), F32)
        for j in range(n_pages):
            vjg = v_refs[j][pl.ds(g, P, stride=G), :].astype(BF16)
            acc = acc + jnp.dot(amap[:, j * P:(j + 1) * P], vjg, preferred_element_type=F32)
        o = acc[2 * g:2 * g + 1, :] + amap_cur[2 * g:2 * g + 1, :] * vc_ref[:, g * V_DIM:(g + 1) * V_DIM]
        outs.append(_subln(o, g_ref[...], out_scale))
    o_ref[...] = jnp.concatenate(outs, axis=1).astype(o_ref.dtype)


def _attn_sample_kernel(pt_ref, *refs, n_pages, out_scale):
    _attn_sample_body(pl.program_id(0), *refs, n_pages=n_pages, out_scale=out_scale)


def _attn_sample(pt_flat, lam, qt_s, kt_s, v_cur, subln_g, ck, cv, layer, out_scale):
    d_att, n = qt_s.shape
    n_pages = pt_flat.shape[0] // n
    P = ck.shape[-1]
    nl = min(n, LANES)
    vc3 = v_cur.reshape(n, 1, d_att)
    cols = lambda: pl.BlockSpec((d_att, nl), lambda b, pt: (0, b // nl))
    k_specs = [pl.BlockSpec((None, None, d_att, P), functools.partial(
        lambda b, pt, j: (layer, pt[b * n_pages + j], 0, 0), j=j)) for j in range(n_pages)]
    v_specs = [pl.BlockSpec((None, None, cv.shape[2], V_DIM), functools.partial(
        lambda b, pt, j: (layer, pt[b * n_pages + j], 0, 0), j=j)) for j in range(n_pages)]
    grid_spec = pltpu.PrefetchScalarGridSpec(
        num_scalar_prefetch=1,
        grid=(n,),
        in_specs=[pl.BlockSpec(memory_space=pltpu.SMEM), cols(), cols(),
                  pl.BlockSpec((None, 1, d_att), lambda b, pt: (b, 0, 0)),
                  pl.BlockSpec((1, V_DIM), lambda b, pt: (0, 0))] + k_specs + v_specs,
        out_specs=pl.BlockSpec((None, 1, d_att), lambda b, pt: (b, 0, 0)),
    )
    out = pl.pallas_call(
        functools.partial(_attn_sample_kernel, n_pages=n_pages, out_scale=out_scale),
        grid_spec=grid_spec,
        out_shape=jax.ShapeDtypeStruct((n, 1, d_att), BF16),
        compiler_params=_cparams(("arbitrary",)),
        name="diff_attn_sample",
    )(pt_flat, lam, qt_s, kt_s, vc3, subln_g, *([ck] * n_pages), *([cv] * n_pages))
    return out.reshape(n, d_att)


def _attn_fused_kernel(pt_ref, lam_ref, qt_ref, kb_ref, vt_ref, gcol_ref, qts_ref, kts_ref, vc_ref, g_ref, *rest,
                       n_pages, out_scale):
    pages = rest[:2 * n_pages]
    o_ref, os_ref, qcat, m, acc, s_even, s_odd = rest[2 * n_pages:]
    seq = pl.program_id(0) * pl.num_programs(1) + pl.program_id(1)
    sample = functools.partial(_attn_sample_body, seq, lam_ref, qts_ref, kts_ref, vc_ref, g_ref, *pages, os_ref,
                               n_pages=n_pages, out_scale=out_scale)
    _attn_kernel(lam_ref, qt_ref, kb_ref, vt_ref, gcol_ref, o_ref, qcat, m, acc, s_even, s_odd,
                 out_scale=out_scale, between=sample)


def _attn_prompt_and_sample(qt, kb, vt, subln_g, lam, out_scale, pt_flat, qt_s, kt_s, v_cur, ck, cv, layer):
    B, S, d_att = kb.shape
    G = d_att // V_DIM
    T = min(ATT_TILE, S)
    nq = S // T
    n = qt_s.shape[1]
    if n != B * G * nq:
        return (_attn_prompt(qt, kb, vt, subln_g, lam, out_scale),
                _attn_sample(pt_flat, lam, qt_s, kt_s, v_cur, subln_g, ck, cv, layer, out_scale))
    n_pages = pt_flat.shape[0] // n
    P = ck.shape[-1]
    nl = min(n, LANES)
    seq = lambda bg, i: bg * nq + i
    cols = lambda: pl.BlockSpec((d_att, nl), lambda bg, i, pt: (0, seq(bg, i) // nl))
    k_specs = [pl.BlockSpec((None, None, d_att, P), functools.partial(
        lambda bg, i, pt, j: (layer, pt[seq(bg, i) * n_pages + j], 0, 0), j=j)) for j in range(n_pages)]
    v_specs = [pl.BlockSpec((None, None, cv.shape[2], V_DIM), functools.partial(
        lambda bg, i, pt, j: (layer, pt[seq(bg, i) * n_pages + j], 0, 0), j=j)) for j in range(n_pages)]
    grid_spec = pltpu.PrefetchScalarGridSpec(
        num_scalar_prefetch=1,
        grid=(B * G, nq),
        in_specs=[
            pl.BlockSpec(memory_space=pltpu.SMEM),
            pl.BlockSpec((None, V_DIM, T), lambda bg, i, pt: (bg // G, bg % G, i)),
            pl.BlockSpec((None, S, V_DIM), lambda bg, i, pt: (bg // G, 0, bg % G)),
            pl.BlockSpec((None, V_DIM + ONES_ROWS, S), lambda bg, i, pt: (bg // G, bg % G, 0)),
            pl.BlockSpec((V_DIM, 1), lambda bg, i, pt: (0, 0)),
            cols(), cols(),
            pl.BlockSpec((None, 1, d_att), lambda bg, i, pt: (seq(bg, i), 0, 0)),
            pl.BlockSpec((1, V_DIM), lambda bg, i, pt: (0, 0)),
        ] + k_specs + v_specs,
        out_specs=[pl.BlockSpec((None, T, V_DIM), lambda bg, i, pt: (bg // G, i, bg % G)),
                   pl.BlockSpec((None, 1, d_att), lambda bg, i, pt: (seq(bg, i), 0, 0))],
        scratch_shapes=[pltpu.VMEM((V_DIM, 2 * T), BF16), pltpu.VMEM((1, 2 * T), F32),
                        pltpu.VMEM((V_DIM + ONES_ROWS, 2 * T), F32),
                        pltpu.VMEM((T, 2 * T), F32), pltpu.VMEM((T, 2 * T), F32)],
    )
    att, att_s = pl.pallas_call(
        functools.partial(_attn_fused_kernel, n_pages=n_pages, out_scale=out_scale),
        grid_spec=grid_spec,
        out_shape=[jax.ShapeDtypeStruct((B, S, d_att), BF16), jax.ShapeDtypeStruct((n, 1, d_att), BF16)],
        compiler_params=_cparams(("parallel", "arbitrary")),
        name="diff_attn_fused",
    )(pt_flat, lam, qt, kb, vt, subln_g.reshape(V_DIM, 1),
      qt_s, kt_s, v_cur.reshape(n, 1, d_att), subln_g,
      *([ck] * n_pages), *([cv] * n_pages))
    return att, att_s.reshape(n, d_att)


def _store_token_tiles(ref, x):
    T, D = x.shape
    n = D // LANES
    for c in range(n):
        ref[pl.ds(c, T, stride=n), :] = x[:, c * LANES:(c + 1) * LANES]


def _load_token_tiles(ref, T):
    n = ref.shape[0] // T
    return jnp.concatenate([ref[pl.ds(c, T, stride=n), :] for c in range(n)], axis=1)


def _outproj_kernel(yr_ref, at_ref, x_ref, wo_ref, g2_ref, wr_ref, br_ref, cnt_in_ref,
                    x1_ref, h2_ref, route_ref, route_t_ref, cnt_out_ref, cnt_scr, *, n_groups, per_group):
    step = pl.program_id(0)

    @pl.when(step == 0)
    def _():
        cnt_scr[...] = cnt_in_ref[...]

    cnt = _outproj_rows(pl.ds(0, x_ref.shape[0]), cnt_scr[...], yr_ref, at_ref, x_ref, wo_ref,
                        g2_ref, wr_ref, br_ref, x1_ref, h2_ref, route_ref, route_t_ref,
                        n_groups=n_groups, per_group=per_group)
    cnt_scr[...] = cnt
    cnt_out_ref[...] = cnt


def _outproj_rows(rows, cnt, yr_ref, at_ref, x_ref, wo_ref, g2_ref, wr_ref, br_ref,
                  x1_ref, h2_ref, route_ref, route_t_ref, *, n_groups, per_group):
    d_rnn = yr_ref.shape[1]
    y = (jnp.dot(yr_ref[rows, :], wo_ref[:d_rnn, :], preferred_element_type=F32)
         + jnp.dot(at_ref[rows, :], wo_ref[d_rnn:, :], preferred_element_type=F32))
    x1 = x_ref[rows, :] + y
    x1_ref[rows, :] = x1
    ms = jnp.mean(x1 * x1, axis=-1, keepdims=True)
    h2 = x1 * lax.rsqrt(ms + NORM_EPS) * g2_ref[...]
    T = x1.shape[0]
    n_tile = h2.shape[1] // LANES
    _store_token_tiles(h2_ref.at[pl.ds(rows.start * n_tile, T * n_tile)], h2)
    logits = jnp.dot(h2.astype(BF16), wr_ref[...], preferred_element_type=F32) + br_ref[...]
    lane = lax.broadcasted_iota(jnp.int32, (T, LANES), 1).astype(F32)
    first = lambda hit: jnp.min(jnp.where(hit, lane, float(LANES)), axis=-1, keepdims=True)
    gl = jnp.where(lane < n_groups, logits, NEG_INF)
    gmax = jnp.max(gl, axis=-1, keepdims=True)
    gidx = first(gl == gmax)
    pg = 1.0 / jnp.sum(jnp.exp(gl - gmax), axis=-1, keepdims=True)
    lo = n_groups + per_group * gidx
    el = jnp.where(jnp.logical_and(lane >= lo, lane < lo + per_group), logits, NEG_INF)
    ma = jnp.max(el, axis=-1, keepdims=True)
    ia = first(el == ma)
    el2 = jnp.where(lane == ia, NEG_INF, el)
    mb = jnp.max(el2, axis=-1, keepdims=True)
    ib = first(el2 == mb)
    eb = jnp.exp(mb - ma)
    gate_a = pg / (1.0 + eb)
    gate_b = pg * eb / (1.0 + eb)
    hit_a = lane == ia
    hit_b = lane == ib
    onehot = jnp.where(jnp.logical_or(hit_a, hit_b), 1.0, 0.0)
    r_i = lax.broadcasted_iota(jnp.int32, (T, T), 0)
    c_i = lax.broadcasted_iota(jnp.int32, (T, T), 1)
    lower = jnp.where(c_i < r_i, 1.0, 0.0).astype(BF16)
    before = jnp.dot(lower, onehot.astype(BF16), preferred_element_type=F32) + cnt
    rank_a = jnp.sum(jnp.where(hit_a, before, 0.0), axis=-1, keepdims=True)
    rank_b = jnp.sum(jnp.where(hit_b, before, 0.0), axis=-1, keepdims=True)
    cols = (ia - n_groups, ib - n_groups, gate_a, gate_b, rank_a, rank_b)
    route = jnp.zeros((T, LANES), F32)
    for c, val in enumerate(cols):
        route = jnp.where(lane == float(c), val, route)
    route_ref[rows, :] = route
    route_t_ref[:, rows] = route.T[:SUBLANES]
    return cnt + jnp.sum(onehot, axis=0, keepdims=True)


def _outproj_route(yr, at, x, wo, g2, wr, br, cnt_in, n_groups, per_group):
    N, D = x.shape
    T = min(ROW_TILE, N)
    d_rnn = yr.shape[1]
    row = lambda w: pl.BlockSpec((T, w), lambda i: (i, 0))
    full = lambda a: pl.BlockSpec(a.shape, lambda i: (0,) * a.ndim)
    return pl.pallas_call(
        functools.partial(_outproj_kernel, n_groups=n_groups, per_group=per_group),
        grid=(N // T,),
        in_specs=[row(d_rnn), row(at.shape[1]), row(D), full(wo), full(g2), full(wr), full(br), full(cnt_in)],
        out_specs=[row(D), pl.BlockSpec((T * (D // LANES), LANES), lambda i: (i, 0)), row(LANES),
                   pl.BlockSpec((SUBLANES, T), lambda i: (0, i)),
                   pl.BlockSpec((1, LANES), lambda i: (0, 0))],
        out_shape=[jax.ShapeDtypeStruct((N, D), F32), jax.ShapeDtypeStruct((N * (D // LANES), LANES), F32),
                   jax.ShapeDtypeStruct((N, LANES), F32), jax.ShapeDtypeStruct((SUBLANES, N), F32),
                   jax.ShapeDtypeStruct((1, LANES), F32)],
        scratch_shapes=[pltpu.VMEM((1, LANES), F32)],
        compiler_params=_cparams(("arbitrary",)),
        name="outproj_route",
    )(yr, at, x, wo, g2, wr, br, cnt_in)


def _dest_kernel(seg_ref, rt_ref, o_ref):
    e = rt_ref[0:TOP_K, :].astype(jnp.int32)
    start = jnp.zeros_like(e)
    for ex in range(seg_ref.shape[0]):
        start = jnp.where(e == ex, seg_ref[ex], start)
    o_ref[...] = start + rt_ref[4:4 + TOP_K, :].astype(jnp.int32)


def _dest(seg_start, route_t):
    R, N = route_t.shape
    T = min(ROW_TILE, N)
    grid_spec = pltpu.PrefetchScalarGridSpec(
        num_scalar_prefetch=1,
        grid=(N // T,),
        in_specs=[pl.BlockSpec((R, T), lambda i, seg: (0, i))],
        out_specs=pl.BlockSpec((TOP_K, T), lambda i, seg: (0, i)),
    )
    return pl.pallas_call(
        _dest_kernel,
        grid_spec=grid_spec,
        out_shape=jax.ShapeDtypeStruct((TOP_K, N), jnp.int32),
        compiler_params=_cparams(("arbitrary",)),
        name="moe_dest",
    )(seg_start, route_t)


def _dispatch_kernel(zb_ref, dest_ref, h_ref, *rest, zero_fill):
    xb_hbm, zbuf, sem = rest[-3], rest[-2], rest[-1]
    T = dest_ref.shape[1]
    n = h_ref.shape[0] // T

    if zero_fill:
        @pl.when(pl.program_id(0) == 0)
        def _():
            zbuf[...] = jnp.zeros_like(zbuf)

            def zcopy(i):
                row0 = pl.multiple_of(zb_ref[i] * (EXP_BLOCK * n), EXP_BLOCK * n)
                return pltpu.make_async_copy(zbuf, xb_hbm.at[pl.ds(row0, EXP_BLOCK * n)], sem)

            def zstart(i, c):
                @pl.when(zb_ref[i] >= 0)
                def _():
                    zcopy(i).start()
                return c

            def zwait(i, c):
                @pl.when(zb_ref[i] >= 0)
                def _():
                    zcopy(i).wait()
                return c

            lax.fori_loop(0, zb_ref.shape[0], zstart, 0)
            lax.fori_loop(0, zb_ref.shape[0], zwait, 0)

    def body(k, c):
        t0 = pl.multiple_of(k * SUBLANES, SUBLANES)
        group = h_ref.at[pl.ds(pl.multiple_of(t0 * n, SUBLANES * n), SUBLANES * n)]
        for j in range(SUBLANES):
            for s in range(TOP_K):
                row0 = pl.multiple_of(dest_ref[s, t0 + j] * n, n)
                pltpu.make_async_copy(group.at[pl.ds(j * n, n)],
                                      xb_hbm.at[pl.ds(row0, n)], sem).start(priority=s)
        return c

    lax.fori_loop(0, T // SUBLANES, body, 0, unroll=2)
    for s in range(TOP_K):
        pltpu.make_async_copy(h_ref, xb_hbm.at[pl.ds(0, T * n)], sem).wait()


def _dispatch(zb, dest, h, xb, xb_rows):
    N = dest.shape[1]
    n = h.shape[0] // N
    T = min(DISPATCH_TILE, N)
    dest3 = jnp.transpose(dest.reshape(TOP_K, N // T, T), (1, 0, 2))
    aliased = xb is not None
    in_specs = [pl.BlockSpec((None, TOP_K, T), lambda i, zb: (i, 0, 0), memory_space=pltpu.SMEM),
                pl.BlockSpec((T * n, LANES), lambda i, zb: (i, 0))]
    args = [zb, dest3, h]
    if aliased:
        in_specs.append(pl.BlockSpec(memory_space=pl.ANY))
        args.append(xb)
    grid_spec = pltpu.PrefetchScalarGridSpec(
        num_scalar_prefetch=1,
        grid=(N // T,),
        in_specs=in_specs,
        out_specs=pl.BlockSpec(memory_space=pl.ANY),
        scratch_shapes=[pltpu.VMEM((EXP_BLOCK * n, LANES), h.dtype), pltpu.SemaphoreType.DMA(())],
    )
    return pl.pallas_call(
        functools.partial(_dispatch_kernel, zero_fill=not aliased),
        grid_spec=grid_spec,
        out_shape=jax.ShapeDtypeStruct((xb_rows * n, LANES), h.dtype),
        input_output_aliases={3: 0} if aliased else {},
        compiler_params=pltpu.CompilerParams(dimension_semantics=("arbitrary",), has_side_effects=True),
        name="moe_dispatch",
    )(*args)


def _expert_kernel(be_ref, bi_ref, br_ref, bf_ref, x_ref, w1_ref, w2_ref, y_ref, w1b, w2b):
    b = pl.program_id(0)
    rows = br_ref[b]

    @pl.when(bf_ref[b] == 1)
    def _():
        w1b[...] = w1_ref[...].astype(BF16)
        w2b[...] = w2_ref[...].astype(BF16)

    @pl.when(rows > 0)
    def _():
        x = _load_token_tiles(x_ref, EXP_BLOCK).astype(BF16)
        h = jnp.dot(x, w1b[...], preferred_element_type=F32)
        de = h.shape[1] // 2
        act = (jax.nn.silu(h[:, :de]) * h[:, de:]).astype(BF16)
        _store_token_tiles(y_ref, jnp.dot(act, w2b[...], preferred_element_type=F32))

    @pl.when(rows == 0)
    def _():
        y_ref[...] = jnp.zeros_like(y_ref)


def _experts(be, bi, br, bf, xb, w1, w2, layer):
    nb = be.shape[0]
    D, d2 = w1.shape[-2:]
    de = w2.shape[-2]
    blk = EXP_BLOCK * (D // LANES)
    grid_spec = pltpu.PrefetchScalarGridSpec(
        num_scalar_prefetch=4,
        grid=(nb,),
        in_specs=[pl.BlockSpec((blk, LANES), lambda b, be, bi, br, bf: (bi[b], 0)),
                  pl.BlockSpec((None, None, D, d2), lambda b, be, bi, br, bf: (layer, be[b], 0, 0)),
                  pl.BlockSpec((None, None, de, D), lambda b, be, bi, br, bf: (layer, be[b], 0, 0))],
        out_specs=pl.BlockSpec((blk, LANES), lambda b, be, bi, br, bf: (b, 0)),
        scratch_shapes=[pltpu.VMEM((D, d2), BF16), pltpu.VMEM((de, D), BF16)],
    )
    return pl.pallas_call(
        _expert_kernel,
        grid_spec=grid_spec,
        out_shape=jax.ShapeDtypeStruct(xb.shape, F32),
        compiler_params=_cparams(("arbitrary",)),
        name="moe_experts",
    )(be, bi, br, bf, xb, w1, w2)


def _combine_kernel(dest_ref, dest_next_ref, x1_ref, route_ref, gf_ref, yb_hbm, o_ref, ybuf, sem, *, final):
    T = x1_ref.shape[0]
    n = ybuf.shape[2] // T
    i = pl.program_id(0)
    slot = i % 2

    def issue(dref, sl):
        def body(k, c):
            t0 = pl.multiple_of(k * SUBLANES, SUBLANES)
            for s in range(TOP_K):
                group = ybuf.at[sl, s, pl.ds(pl.multiple_of(t0 * n, SUBLANES * n), SUBLANES * n)]
                for j in range(SUBLANES):
                    row0 = pl.multiple_of(dref[s, t0 + j] * n, n)
                    pltpu.make_async_copy(yb_hbm.at[pl.ds(row0, n)],
                                          group.at[pl.ds(j * n, n)], sem.at[sl]).start(priority=s)
            return c

        lax.fori_loop(0, T // SUBLANES, body, 0, unroll=2)

    @pl.when(i == 0)
    def _():
        issue(dest_ref, 0)

    @pl.when(i + 1 < pl.num_programs(0))
    def _():
        issue(dest_next_ref, 1 - slot)

    for s in range(TOP_K):
        pltpu.make_async_copy(yb_hbm.at[pl.ds(0, T * n)], ybuf.at[slot, s], sem.at[slot]).wait()
    route = route_ref[...]
    out = (x1_ref[...] + _load_token_tiles(ybuf.at[slot, 0], T) * route[:, 2:3]
           + _load_token_tiles(ybuf.at[slot, 1], T) * route[:, 3:4])
    if final:
        ms = jnp.mean(out * out, axis=-1, keepdims=True)
        out = out * lax.rsqrt(ms + NORM_EPS) * gf_ref[...]
    o_ref[...] = out


def _combine(dest, x1, route, gf, yb, final):
    N, D = x1.shape
    T = min(COMBINE_TILE, N)
    dest3 = jnp.transpose(dest.reshape(TOP_K, N // T, T), (1, 0, 2))
    last = N // T - 1
    return pl.pallas_call(
        functools.partial(_combine_kernel, final=final),
        grid=(N // T,),
        in_specs=[pl.BlockSpec((None, TOP_K, T), lambda i: (i, 0, 0), memory_space=pltpu.SMEM),
                  pl.BlockSpec((None, TOP_K, T), lambda i: (jnp.minimum(i + 1, last), 0, 0),
                               memory_space=pltpu.SMEM),
                  pl.BlockSpec((T, D), lambda i: (i, 0)),
                  pl.BlockSpec((T, LANES), lambda i: (i, 0)),
                  pl.BlockSpec((1, D), lambda i: (0, 0)),
                  pl.BlockSpec(memory_space=pl.ANY)],
        out_specs=pl.BlockSpec((T, D), lambda i: (i, 0)),
        out_shape=jax.ShapeDtypeStruct((N, D), F32),
        scratch_shapes=[pltpu.VMEM((2, TOP_K, T * (D // LANES), LANES), F32), pltpu.SemaphoreType.DMA((2,))],
        compiler_params=_cparams(("arbitrary",)),
        name="moe_combine",
    )(dest3, dest3, x1, route, gf, yb)


def _rope_tables(pos):
    inv = ROPE_THETA ** (-jnp.arange(HALF, dtype=F32) / HALF)
    ang = pos.astype(F32)[:, None] * inv[None, :]
    cos, sin = jnp.cos(ang), jnp.sin(ang)
    cq = jnp.tile(cos, (1, LANES // HALF))
    sq = jnp.concatenate([-sin, -sin, sin, sin], axis=1)
    return cq, sq, cos.T, sin.T


def _attention_order(w):
    D, d_att = w.shape
    w = w.reshape(D, d_att // V_DIM, 2, 2, HALF)
    return jnp.swapaxes(w, 2, 3).reshape(D, d_att)


def _gate_weights(w_ga, w_gx):
    nb, bw, _ = w_ga.shape
    half = nb // 2
    eye = jnp.eye(half, dtype=w_ga.dtype)

    def dense(w):
        return jnp.einsum('nij,nm->nimj', w, eye).reshape(half * bw, half * bw)

    return jnp.stack([jnp.concatenate([dense(w_ga[h * half:(h + 1) * half]),
                                       dense(w_gx[h * half:(h + 1) * half])], axis=1)
                      for h in range(2)]).astype(BF16)


def _block_tables(counts, nb):
    ne = counts.shape[0]
    nblk = (counts + EXP_BLOCK - 1) // EXP_BLOCK
    cum = jnp.sum(jnp.where(jnp.arange(ne)[None, :] <= jnp.arange(ne)[:, None], nblk[None, :], 0), axis=1)
    total = cum[-1]
    b = jnp.arange(nb, dtype=jnp.int32)
    bc = jnp.minimum(b, total - 1)
    e = jnp.minimum(jnp.sum(bc[:, None] >= cum[None, :], axis=1), ne - 1).astype(jnp.int32)
    pick = lambda tab: jnp.sum(jnp.where(e[:, None] == jnp.arange(ne)[None, :], tab[None, :], 0), axis=1)
    off = bc - (pick(cum) - pick(nblk))
    valid = b < total
    rows = jnp.where(valid, jnp.clip(pick(counts) - off * EXP_BLOCK, 0, EXP_BLOCK), 0)
    first = jnp.where(jnp.logical_and(valid, off == 0), 1, 0)
    last_blk = jnp.where(nblk > 0, cum - 1, -1)
    trail = total + jnp.arange(ne, dtype=jnp.int32)
    trail = jnp.where(trail < nb, trail, -1)
    zb = jnp.concatenate([last_blk, trail]).astype(jnp.int32)
    seg_start = ((cum - nblk) * EXP_BLOCK).astype(jnp.int32)
    return seg_start, e, bc.astype(jnp.int32), rows.astype(jnp.int32), first.astype(jnp.int32), zb


def kernel(x_prompt, x_sample, cache_k, cache_v, state_h, state_conv, page_table, ln1_g, w_in, conv_w, conv_b,
           w_ga, b_ga, w_gx, b_gx, lru_lambda, lam_q1, lam_k1, lam_q2, lam_k2, subln_g, w_out, ln2_g,
           w_route_group, b_route_group, w_route_expert, b_route_expert, w_exp_in, w_exp_out, ln_f):
    Bp, Sp, D = x_prompt.shape
    Bs = x_sample.shape[0]
    depth = w_in.shape[0]
    d_rnn = state_h.shape[-1]
    d_att = cache_k.shape[-2] * cache_k.shape[-1]
    n_groups = w_route_group.shape[-1]
    n_experts = w_route_expert.shape[-1]
    per_group = n_experts // n_groups
    n_pool, page = cache_k.shape[1], cache_k.shape[2]
    n_pages = page_table.shape[1]
    past_len = n_pages * page
    Np = Bp * Sp
    assert x_sample.shape[1] == 1 and n_groups + n_experts <= LANES

    cq, sq, ct, st = _rope_tables(jnp.arange(Sp))
    cq_s, sq_s, ct_s, st_s = _rope_tables(jnp.full((Bs,), past_len))
    qscale = HEAD_DIM ** -0.5
    ck = jnp.transpose(cache_k, (0, 1, 3, 4, 2)).reshape(depth, n_pool, d_att, page)
    cv = cache_v.reshape(depth, n_pool, page * (d_att // V_DIM), V_DIM)
    pt_flat = page_table.reshape(-1).astype(jnp.int32)
    conv_s = jnp.transpose(state_conv, (0, 2, 1, 3))

    n_blocks = (TOP_K * (Np + Bs)) // EXP_BLOCK + n_experts

    xp = x_prompt
    xs = x_sample.reshape(Bs, D)
    kt_all = v_all = None
    hp_l, cp_l, ks_l, vs_l, hs_l, cs_l = [], [], [], [], [], []
    for l in range(depth):
        lambda_init = 0.8 - 0.6 * math.exp(-0.3 * l)
        out_scale = 1.0 - lambda_init
        lam = (jnp.exp(jnp.sum(lam_q1[l] * lam_k1[l])) - jnp.exp(jnp.sum(lam_q2[l] * lam_k2[l]))
               + lambda_init).reshape(1, 1).astype(F32)
        wl = w_in[l]
        w_xg = wl[:, :2 * d_rnn]
        w_q = wl[:, 2 * d_rnn:2 * d_rnn + d_att] * qscale
        w_k = wl[:, 2 * d_rnn + d_att:2 * d_rnn + 2 * d_att]
        w_v = wl[:, 2 * d_rnn + 2 * d_att:]
        wm_p = jnp.concatenate([w_xg, _attention_order(w_k), w_v], axis=1).astype(BF16)
        wt_p = jnp.concatenate([_attention_order(w_q), w_v], axis=1).T.astype(BF16)
        g1 = ln1_g[l].reshape(1, D)
        wg = _gate_weights(w_ga[l], w_gx[l])
        bga = b_ga[l].reshape(1, d_rnn)
        bgx = b_gx[l].reshape(1, d_rnn)
        lru_l = lru_lambda[l].reshape(1, d_rnn)
        cw = conv_w[l]
        cb = conv_b[l].reshape(1, d_rnn)
        sg = subln_g[l].reshape(1, V_DIM)
        wo = w_out[l].astype(BF16)
        g2 = ln2_g[l].reshape(1, D)
        n_route = n_groups + n_experts
        wr = jnp.concatenate([w_route_group[l], w_route_expert[l], jnp.zeros((D, LANES - n_route), F32)],
                             axis=1).astype(BF16)
        br = jnp.concatenate([b_route_group[l], b_route_expert[l], jnp.zeros((LANES - n_route,), F32)]
                             ).reshape(1, LANES)

        xr, gr, kb, v_all, qt, kt_all, vt = _inproj_prompt(xp, g1, wm_p, wt_p, cq, sq, ct, st, kt_all, v_all)
        y_rnn, h_last, c_tail = _lru_prompt(xr, gr, cw, cb, wg, bga, bgx, lru_l)
        xr_s, gr_s, v_s, kt_s, qt_s = _inproj_sample(xs, g1, wm_p, wt_p, cq_s, sq_s, ct_s, st_s)
        y_rnn_s, h_s, c_s = _lru_step(xr_s, gr_s, conv_s[l], state_h[l], cw, cb, wg, bga, bgx, lru_l)
        k_s = kt_s.T
        att, att_s = _attn_prompt_and_sample(qt, kb, vt, sg, lam, out_scale,
                                             pt_flat, qt_s, kt_s, v_s, ck, cv, l)
        cnt0 = jnp.zeros((1, LANES), F32)
        x1, h2, route, route_t, cnt1 = _outproj_route(
            y_rnn.reshape(Np, d_rnn), att.reshape(Np, d_att), xp.reshape(Np, D), wo, g2, wr, br, cnt0,
            n_groups, per_group)
        x1_s, h2_s, route_s, route_ts, cnt2 = _outproj_route(y_rnn_s, att_s, xs, wo, g2, wr, br, cnt1,
                                                             n_groups, per_group)
        counts = cnt2[0, n_groups:n_groups + n_experts].astype(jnp.int32)
        seg_start, be, bi, brows, bfirst, zb = _block_tables(counts, n_blocks)

        dest_p, dest_s = _dest(seg_start, route_t), _dest(seg_start, route_ts)
        xb = _dispatch(zb, dest_p, h2, None, n_blocks * EXP_BLOCK)
        xb = _dispatch(zb, dest_s, h2_s, xb, n_blocks * EXP_BLOCK)
        yb = _experts(be, bi, brows, bfirst, xb, w_exp_in, w_exp_out, l)
        final = l == depth - 1
        gf = ln_f.reshape(1, D)
        xp_new = _combine(dest_p, x1, route, gf, yb, final)
        xs = _combine(dest_s, x1_s, route_s, gf, yb, final)
        xp = xp_new.reshape(Bp, Sp, D)

        n_qk = d_att // HEAD_DIM
        hp_l.append(h_last.reshape(Bp, d_rnn))
        cp_l.append(c_tail)
        ks_l.append(k_s.reshape(Bs, 1, n_qk, HEAD_DIM))
        vs_l.append(v_s.reshape(Bs, 1, d_att // V_DIM, V_DIM))
        hs_l.append(h_s)
        cs_l.append(jnp.transpose(c_s, (1, 0, 2)))
    k_prompt = jnp.transpose(kt_all.reshape(depth, Bp, n_qk, HEAD_DIM, Sp), (0, 1, 4, 2, 3))
    v_prompt = v_all.reshape(depth, Bp, Sp, d_att // V_DIM, V_DIM)
    return (xp, xs.reshape(Bs, 1, D),
            k_prompt, v_prompt, jnp.stack(hp_l), jnp.stack(cp_l),
            jnp.stack(ks_l), jnp.stack(vs_l), jnp.stack(hs_l), jnp.stack(cs_l))
```

```python
import functools
import math

import jax
import jax.numpy as jnp
from jax import lax
from jax.experimental import pallas as pl
from jax.experimental.pallas import tpu as pltpu

F32 = jnp.float32
BF16 = jnp.bfloat16

HEAD_DIM = 64
V_DIM = 2 * HEAD_DIM
HALF = HEAD_DIM // 2
ONES_ROWS = 16
CONV_W = 4
LRU_C = 8.0
ROPE_THETA = 10000.0
NORM_EPS = 1e-6
SUBLN_EPS = 1e-5
NEG_INF = -1e30
TOP_K = 2

LANES = 128
SUBLANES = 8
VMEM_LIMIT = 48 * 1024 * 1024
INPROJ_VMEM_LIMIT = 56 * 1024 * 1024

ROW_TILE = 512
LRU_TILE = 512
ATT_TILE = 512
EXP_BLOCK = 512
DISPATCH_TILE = 512
COMBINE_TILE = 256


def _cparams(sem, vmem_limit=VMEM_LIMIT):
    return pltpu.CompilerParams(dimension_semantics=sem, vmem_limit_bytes=vmem_limit)


def _inproj_kernel(x_ref, g_ref, wm_ref, wt_ref, cq_ref, sq_ref, ct_ref, st_ref, *rest, sample, n_prev=0):
    prev, outs = rest[:2 if n_prev else 0], rest[2 if n_prev else 0:]
    x = x_ref[...]
    ms = jnp.mean(x * x, axis=-1, keepdims=True)
    h = (x * lax.rsqrt(ms + NORM_EPS) * g_ref[...]).astype(BF16)
    main = jnp.dot(h, wm_ref[...], preferred_element_type=F32)
    d_rnn = outs[0].shape[1]
    ct = ct_ref[...]
    st = st_ref[...]
    tr = lax.dot_general(wt_ref[...], h, (((1,), (1,)), ((), ())), preferred_element_type=F32)

    d_att = wt_ref.shape[0] // 2
    cq = cq_ref[...]
    sq = sq_ref[...]

    def roped_k(c):
        kc = main[:, 2 * d_rnn + c * LANES:2 * d_rnn + (c + 1) * LANES]
        return kc * cq + pltpu.roll(kc, HEAD_DIM, 1) * sq

    def store_kt(dst, c, kr):
        kt = kr.T
        for jj in range(2):
            r0 = (2 * c + jj) * HEAD_DIM
            dst[r0:r0 + HALF, :] = kt[jj * HALF:(jj + 1) * HALF]
            dst[r0 + HALF:r0 + HEAD_DIM, :] = kt[HEAD_DIM + jj * HALF:HEAD_DIM + (jj + 1) * HALF]

    if sample:
        xr_ref, gr_ref, v_ref, kt_ref, qt_ref = outs
        xr_ref[...] = main[:, :d_rnn]
        gr_ref[...] = main[:, d_rnn:2 * d_rnn]
        v_ref[...] = main[:, 2 * d_rnn + d_att:]
        for c in range(d_att // LANES):
            store_kt(kt_ref, c, roped_k(c))
        for hh in range(d_att // HEAD_DIM):
            g, jj = divmod(hh, 2)
            r1 = g * V_DIM + jj * HALF
            x1 = tr[r1:r1 + HALF]
            x2 = tr[r1 + HEAD_DIM:r1 + HEAD_DIM + HALF]
            qt_ref[hh * HEAD_DIM:hh * HEAD_DIM + HALF, :] = x1 * ct - x2 * st
            qt_ref[hh * HEAD_DIM + HALF:(hh + 1) * HEAD_DIM, :] = x2 * ct + x1 * st
    else:
        xr_ref, gr_ref, kb_ref, v_ref, qt_ref, kt_ref, vt_ref = outs
        xr_ref[...] = main[:, :d_rnn]
        gr_ref[...] = main[:, d_rnn:2 * d_rnn]
        if n_prev:
            kt_ref[:n_prev] = prev[0][...]
            v_ref[:n_prev] = prev[1][...]
        v = main[:, 2 * d_rnn + d_att:]
        T = v.shape[0]
        G = d_att // V_DIM
        for g in range(G):
            v_ref[n_prev, pl.ds(g, T, stride=G), :] = v[:, g * V_DIM:(g + 1) * V_DIM]
        for c in range(d_att // LANES):
            kr = roped_k(c)
            kb_ref[:, c * LANES:(c + 1) * LANES] = kr.astype(BF16)
            store_kt(kt_ref.at[n_prev], c, kr)
        for hh in range(d_att // HEAD_DIM):
            g, jj = divmod(hh, 2)
            r1 = g * V_DIM + jj * HALF
            x1 = tr[r1:r1 + HALF]
            x2 = tr[r1 + HEAD_DIM:r1 + HEAD_DIM + HALF]
            qt_ref[r1:r1 + HALF, :] = (x1 * ct - x2 * st).astype(BF16)
            qt_ref[r1 + HEAD_DIM:r1 + HEAD_DIM + HALF, :] = (x2 * ct + x1 * st).astype(BF16)
        for g in range(G):
            r0 = g * (V_DIM + ONES_ROWS)
            vt_ref[r0:r0 + V_DIM, :] = tr[d_att + g * V_DIM:d_att + (g + 1) * V_DIM].astype(BF16)
            vt_ref[r0 + V_DIM:r0 + V_DIM + ONES_ROWS, :] = jnp.ones((ONES_ROWS, T), BF16)


def _inproj_prompt(x, g, wm, wt, cq, sq, ct, st, k_prev, v_prev):
    B, S, D = x.shape
    T = min(ROW_TILE, S)
    nt = S // T
    d_att = wt.shape[0] // 2
    d_rnn = (wm.shape[1] - 2 * d_att) // 2
    G = d_att // V_DIM
    n_prev = 0 if k_prev is None else k_prev.shape[0]
    row = lambda w: pl.BlockSpec((None, T, w), lambda b, t: (b, t, 0))
    full = lambda a: pl.BlockSpec(a.shape, lambda b, t: (0,) * a.ndim)
    once = lambda a: pl.BlockSpec(a.shape, lambda b, t: (0,) * a.ndim, pipeline_mode=pl.Buffered(1))
    tr_spec = pl.BlockSpec((None, d_att, T), lambda b, t: (b, 0, t))
    kstack = lambda n: pl.BlockSpec((n, None, d_att, T), lambda b, t: (0, b, 0, t))
    vstack = lambda n: pl.BlockSpec((n, None, T * G, V_DIM), lambda b, t: (0, b, t, 0))
    prev_specs = [kstack(n_prev), vstack(n_prev)] if n_prev else []
    prev_args = [k_prev, v_prev] if n_prev else []
    return pl.pallas_call(
        functools.partial(_inproj_kernel, sample=False, n_prev=n_prev),
        grid=(B, nt),
        in_specs=[row(D), full(g), once(wm), once(wt),
                  pl.BlockSpec((T, LANES), lambda b, t: (t, 0)),
                  pl.BlockSpec((T, LANES), lambda b, t: (t, 0)),
                  pl.BlockSpec((HALF, T), lambda b, t: (0, t)),
                  pl.BlockSpec((HALF, T), lambda b, t: (0, t))] + prev_specs,
        out_specs=[row(d_rnn), row(d_rnn), row(d_att), vstack(n_prev + 1), tr_spec, kstack(n_prev + 1),
                   pl.BlockSpec((None, G * (V_DIM + ONES_ROWS), T), lambda b, t: (b, 0, t))],
        out_shape=[jax.ShapeDtypeStruct((B, S, d_rnn), F32), jax.ShapeDtypeStruct((B, S, d_rnn), F32),
                   jax.ShapeDtypeStruct((B, S, d_att), BF16),
                   jax.ShapeDtypeStruct((n_prev + 1, B, S * G, V_DIM), F32),
                   jax.ShapeDtypeStruct((B, d_att, S), BF16),
                   jax.ShapeDtypeStruct((n_prev + 1, B, d_att, S), F32),
                   jax.ShapeDtypeStruct((B, G * (V_DIM + ONES_ROWS), S), BF16)],
        compiler_params=_cparams(("parallel", "arbitrary"), INPROJ_VMEM_LIMIT),
        name="inproj_prompt",
    )(x, g, wm, wt, cq, sq, ct, st, *prev_args)


def _inproj_sample(x, g, wm, wt, cq, sq, ct, st):
    n, D = x.shape
    d_att = wt.shape[0] // 2
    d_rnn = (wm.shape[1] - 2 * d_att) // 2
    full = lambda a: pl.BlockSpec(a.shape, lambda i: (0,) * a.ndim)
    out_shapes = [jax.ShapeDtypeStruct((n, d_rnn), F32), jax.ShapeDtypeStruct((n, d_rnn), F32),
                  jax.ShapeDtypeStruct((n, d_att), F32), jax.ShapeDtypeStruct((d_att, n), F32),
                  jax.ShapeDtypeStruct((d_att, n), F32)]
    return pl.pallas_call(
        functools.partial(_inproj_kernel, sample=True),
        grid=(1,),
        in_specs=[full(x), full(g), full(wm), full(wt), full(cq), full(sq), full(ct), full(st)],
        out_specs=[pl.BlockSpec(s.shape, lambda i: (0, 0)) for s in out_shapes],
        out_shape=out_shapes,
        compiler_params=_cparams(("arbitrary",)),
        name="inproj_sample",
    )(x, g, wm, wt, cq, sq, ct, st)


def _gates(xc, wg_ref, bga, bgx, lam):
    half = xc.shape[1] // 2
    xcb = xc.astype(BF16)
    g0 = jnp.dot(xcb[:, :half], wg_ref[0], preferred_element_type=F32)
    g1 = jnp.dot(xcb[:, half:], wg_ref[1], preferred_element_type=F32)
    r = jax.nn.sigmoid(jnp.concatenate([g0[:, :half], g1[:, :half]], axis=1) + bga)
    i = jax.nn.sigmoid(jnp.concatenate([g0[:, half:], g1[:, half:]], axis=1) + bgx)
    neg = -lam
    softplus = jnp.maximum(neg, 0.0) + jnp.log1p(jnp.exp(-jnp.abs(neg)))
    log_a = -LRU_C * r * softplus
    a = jnp.exp(log_a)
    mult = jnp.sqrt(1.0 - a * a)
    return a, i, mult


def _lru_kernel(xr_ref, gr_ref, cw_ref, cb_ref, wg_ref, bga_ref, bgx_ref, lam_ref, *rest, n_prev=0):
    if n_prev:
        kp_ref, vp_ref, kn_ref, vn_ref = rest[:4]
        y_ref, hlast_ref, ctail_ref, ko_ref, vo_ref, xp_scr, a_scr, b_scr, h_scr = rest[4:]
        ko_ref[:n_prev] = kp_ref[...]
        ko_ref[n_prev] = kn_ref[0]
        vo_ref[:n_prev] = vp_ref[...]
        vo_ref[n_prev] = vn_ref[0]
    else:
        y_ref, hlast_ref, ctail_ref, xp_scr, a_scr, b_scr, h_scr = rest
    t = pl.program_id(1)
    T, C = xr_ref.shape

    @pl.when(t == 0)
    def _():
        xp_scr[0:SUBLANES, :] = jnp.zeros((SUBLANES, C), F32)
        h_scr[...] = jnp.zeros_like(h_scr)

    x = xr_ref[...]
    xp_scr[SUBLANES:SUBLANES + T, :] = x
    cw = cw_ref[...]
    xc = cb_ref[...]
    for j in range(CONV_W - 1):
        xc = xc + xp_scr[pl.ds(SUBLANES - (CONV_W - 1) + j, T), :] * cw[j:j + 1]
    xc = xc + x * cw[CONV_W - 1:CONV_W]
    tail = x[T - (CONV_W - 1):, :]
    xp_scr[SUBLANES - (CONV_W - 1):SUBLANES, :] = tail
    ctail_ref[...] = tail

    a, i, mult = _gates(xc, wg_ref, bga_ref[...], bgx_ref[...], lam_ref[...])
    row = lax.broadcasted_iota(jnp.int32, (T, 1), 0)
    mult = jnp.where(jnp.logical_and(row == 0, t == 0), 1.0, mult)
    a_scr[...] = a
    b_scr[...] = mult * i * xc

    rowi = lax.broadcasted_iota(jnp.int32, (SUBLANES, C), 0)

    def body(gi, h):
        off = pl.multiple_of(gi * SUBLANES, SUBLANES)
        A = a_scr[pl.ds(off, SUBLANES), :]
        Bv = b_scr[pl.ds(off, SUBLANES), :]
        for d in (1, 2, 4):
            keep = rowi >= d
            Bv = jnp.where(keep, Bv + A * pltpu.roll(Bv, d, 0), Bv)
            A = jnp.where(keep, A * pltpu.roll(A, d, 0), A)
        hrows = Bv + A * h
        b_scr[pl.ds(off, SUBLANES), :] = hrows
        return hrows[SUBLANES - 1:SUBLANES, :]

    h = lax.fori_loop(0, T // SUBLANES, body, h_scr[...], unroll=2)
    h_scr[...] = h
    hlast_ref[...] = h
    y_ref[...] = (b_scr[...] * jax.nn.gelu(gr_ref[...])).astype(BF16)


def _lru_prompt(xr, gr, cw, cb, wg, bga, bgx, lam, stack=None):
    B, S, C = xr.shape
    T = min(LRU_TILE, S)
    nt = S // T
    row = pl.BlockSpec((None, T, C), lambda b, t: (b, t, 0))
    full = lambda a: pl.BlockSpec(a.shape, lambda b, t: (0,) * a.ndim)
    in_specs = [row, row, full(cw), full(cb), full(wg), full(bga), full(bgx), full(lam)]
    out_specs = [row, pl.BlockSpec((None, 1, C), lambda b, t: (b, 0, 0)),
                 pl.BlockSpec((None, CONV_W - 1, C), lambda b, t: (b, 0, 0))]
    out_shape = [jax.ShapeDtypeStruct((B, S, C), BF16), jax.ShapeDtypeStruct((B, 1, C), F32),
                 jax.ShapeDtypeStruct((B, CONV_W - 1, C), F32)]
    n_prev = 0
    args = [xr, gr, cw, cb, wg, bga, bgx, lam]
    if stack is not None:
        k_prev, v_prev, k_new, v_new = stack
        n_prev = k_prev.shape[0]
        d_att = k_prev.shape[2]
        rows_v = v_prev.shape[2] // nt
        kspec = lambda n: pl.BlockSpec((n, None, d_att, T), lambda b, t: (0, b, 0, t))
        vspec = lambda n: pl.BlockSpec((n, None, rows_v, V_DIM), lambda b, t: (0, b, t, 0))
        in_specs += [kspec(n_prev), vspec(n_prev), kspec(1), vspec(1)]
        out_specs += [kspec(n_prev + 1), vspec(n_prev + 1)]
        out_shape += [jax.ShapeDtypeStruct((n_prev + 1,) + k_prev.shape[1:], F32),
                      jax.ShapeDtypeStruct((n_prev + 1,) + v_prev.shape[1:], F32)]
        args += [k_prev, v_prev, k_new, v_new]
    return pl.pallas_call(
        functools.partial(_lru_kernel, n_prev=n_prev),
        grid=(B, nt),
        in_specs=in_specs,
        out_specs=out_specs,
        out_shape=out_shape,
        scratch_shapes=[pltpu.VMEM((T + SUBLANES, C), F32), pltpu.VMEM((T, C), F32),
                        pltpu.VMEM((T, C), F32), pltpu.VMEM((1, C), F32)],
        compiler_params=_cparams(("parallel", "arbitrary")),
        name="rglru_prompt",
    )(*args)


def _lru_step_kernel(xr_ref, gr_ref, conv_ref, h0_ref, cw_ref, cb_ref, wg_ref, bga_ref, bgx_ref, lam_ref,
                     y_ref, hnew_ref, cnew_ref):
    x = xr_ref[...]
    cw = cw_ref[...]
    xc = cb_ref[...]
    for j in range(CONV_W - 1):
        xc = xc + conv_ref[j] * cw[j:j + 1]
    xc = xc + x * cw[CONV_W - 1:CONV_W]
    for j in range(CONV_W - 2):
        cnew_ref[j] = conv_ref[j + 1]
    cnew_ref[CONV_W - 2] = x
    a, i, mult = _gates(xc, wg_ref, bga_ref[...], bgx_ref[...], lam_ref[...])
    h = a * h0_ref[...] + mult * i * xc
    hnew_ref[...] = h
    y_ref[...] = (h * jax.nn.gelu(gr_ref[...])).astype(BF16)


def _lru_step(xr, gr, conv, h0, cw, cb, wg, bga, bgx, lam):
    n, C = xr.shape
    args = (xr, gr, conv, h0, cw, cb, wg, bga, bgx, lam)
    full = lambda a: pl.BlockSpec(a.shape, lambda i: (0,) * a.ndim)
    out_shapes = [jax.ShapeDtypeStruct((n, C), BF16), jax.ShapeDtypeStruct((n, C), F32),
                  jax.ShapeDtypeStruct((CONV_W - 1, n, C), F32)]
    return pl.pallas_call(
        _lru_step_kernel,
        grid=(1,),
        in_specs=[full(a) for a in args],
        out_specs=[pl.BlockSpec(s.shape, lambda i, nd=len(s.shape): (0,) * nd) for s in out_shapes],
        out_shape=out_shapes,
        compiler_params=_cparams(("arbitrary",)),
        name="rglru_step",
    )(*args)


def _subln(o, g, scale):
    ms = jnp.mean(o * o, axis=-1, keepdims=True)
    return o * lax.rsqrt(ms + SUBLN_EPS) * g * scale


def _attn_kernel(lam_ref, qt_ref, kb_ref, vt_ref, g_ref, o_ref, qcat, m, acc, s_even, s_odd, *, out_scale,
                 between=None):
    qi = pl.program_id(1)
    Tq = qt_ref.shape[1]
    Tk = Tq

    qt = qt_ref[...]
    row = lax.broadcasted_iota(jnp.int32, qt.shape, 0)
    second = (row & HALF) != 0
    zero = jnp.zeros_like(qt)
    qcat[:, :Tq] = jnp.where(second, zero, qt)
    qcat[:, Tq:] = jnp.where(second, qt, zero)
    m[...] = jnp.full_like(m, NEG_INF)
    acc[...] = jnp.zeros_like(acc)

    def scores(kj, sbuf):
        k0 = pl.multiple_of(kj * Tk, Tk)
        sbuf[...] = jnp.dot(kb_ref[pl.ds(k0, Tk), :], qcat[...], preferred_element_type=F32)

    def update(kj, sbuf, diagonal):
        k0 = pl.multiple_of(kj * Tk, Tk)
        s = sbuf[...]
        if diagonal:
            key = lax.broadcasted_iota(jnp.int32, s.shape, 0)
            qry = lax.broadcasted_iota(jnp.int32, s.shape, 1)
            qry = jnp.where(qry >= Tq, qry - Tq, qry)
            s = jnp.where(key <= qry, s, NEG_INF)
        m_old = m[...]
        m_new = jnp.maximum(m_old, jnp.max(s, axis=0, keepdims=True))
        alpha = jnp.exp(m_old - m_new)
        p = jnp.exp(s - m_new)
        acc[...] = alpha * acc[...] + jnp.dot(vt_ref[:, pl.ds(k0, Tk)], p.astype(BF16),
                                              preferred_element_type=F32)
        m[...] = m_new

    def finish():
        a = acc[:V_DIM, :] / acc[V_DIM:V_DIM + 1, :]
        o = a[:, :Tq] - lam_ref[0, 0] * a[:, Tq:]
        ms = jnp.mean(o * o, axis=0, keepdims=True)
        o = o * lax.rsqrt(ms + SUBLN_EPS) * (g_ref[...] * out_scale)
        o_ref[...] = o.T.astype(o_ref.dtype)

    scores(0, s_even)
    if between is not None:
        between()

    def pair(i, c):
        t = 2 * i
        scores(t + 1, s_odd)
        update(t, s_even, False)
        scores(t + 2, s_even)
        update(t + 1, s_odd, False)
        return c

    lax.fori_loop(0, qi // 2, pair, 0)

    @pl.when(qi % 2 == 0)
    def _():
        update(qi, s_even, True)
        finish()

    @pl.when(qi % 2 == 1)
    def _():
        scores(qi, s_odd)
        update(qi - 1, s_even, False)
        update(qi, s_odd, True)
        finish()


def _attn_prompt(qt, kb, vt, subln_g, lam, out_scale):
    B, S, d_att = kb.shape
    G = d_att // V_DIM
    T = min(ATT_TILE, S)
    return pl.pallas_call(
        functools.partial(_attn_kernel, out_scale=out_scale),
        grid=(B * G, S // T),
        in_specs=[
            pl.BlockSpec(memory_space=pltpu.SMEM),
            pl.BlockSpec((None, V_DIM, T), lambda bg, i: (bg // G, bg % G, i)),
            pl.BlockSpec((None, S, V_DIM), lambda bg, i: (bg // G, 0, bg % G)),
            pl.BlockSpec((None, V_DIM + ONES_ROWS, S), lambda bg, i: (bg // G, bg % G, 0)),
            pl.BlockSpec((V_DIM, 1), lambda bg, i: (0, 0)),
        ],
        out_specs=pl.BlockSpec((None, T, V_DIM), lambda bg, i: (bg // G, i, bg % G)),
        out_shape=jax.ShapeDtypeStruct((B, S, d_att), BF16),
        scratch_shapes=[pltpu.VMEM((V_DIM, 2 * T), BF16), pltpu.VMEM((1, 2 * T), F32),
                        pltpu.VMEM((V_DIM + ONES_ROWS, 2 * T), F32),
                        pltpu.VMEM((T, 2 * T), F32), pltpu.VMEM((T, 2 * T), F32)],
        compiler_params=_cparams(("parallel", "arbitrary")),
        name="diff_attn_prompt",
    )(lam, qt, kb, vt, subln_g.reshape(V_DIM, 1))


def _attn_sample_body(seq, lam_ref, qt_ref, kt_ref, vc_ref, g_ref, *rest, n_pages, out_scale):
    k_refs = rest[:n_pages]
    v_refs = rest[n_pages:2 * n_pages]
    o_ref = rest[2 * n_pages]
    lam = lam_ref[0, 0]
    d_att, n_lanes = qt_ref.shape
    n_heads = d_att // HEAD_DIM
    G = d_att // V_DIM
    R = 2 * SUBLANES
    mine = lax.broadcasted_iota(jnp.int32, (d_att, n_lanes), 1) == seq % n_lanes
    qcol = jnp.sum(jnp.where(mine, qt_ref[...], 0.0), axis=1, keepdims=True)
    kcol = jnp.sum(jnp.where(mine, kt_ref[...], 0.0), axis=1, keepdims=True)
    P = k_refs[0].shape[1]
    qcolb = jnp.broadcast_to(qcol, (d_att, P))
    per_head = lambda x: jnp.sum(x.reshape(n_heads, HEAD_DIM, x.shape[-1]), axis=1)
    s = jnp.concatenate([per_head(k_refs[j][...] * qcolb) for j in range(n_pages)], axis=1)
    s_cur = per_head(qcol * kcol)
    m = jnp.maximum(jnp.max(s, axis=-1, keepdims=True), s_cur)
    p = jnp.exp(s - m)
    p_cur = jnp.exp(s_cur - m)
    inv_l = 1.0 / (jnp.sum(p, axis=-1, keepdims=True) + p_cur)
    p = p * inv_l
    p_cur = p_cur * inv_l
    amap = p - lam * pltpu.roll(p, n_heads - 1, 0)
    amap = jnp.concatenate([amap, jnp.zeros_like(amap)], axis=0).astype(BF16)
    p_cur = jnp.broadcast_to(p_cur, (n_heads, LANES))
    amap_cur = p_cur - lam * pltpu.roll(p_cur, n_heads - 1, 0)
    outs = []
    for g in range(G):
        acc = jnp.zeros((R, V_DIM), F32)
        for j in range(n_pages):
            vjg = v_refs[j][pl.ds(g, P, stride=G), :].astype(BF16)
            acc = acc + jnp.dot(amap[:, j * P:(j + 1) * P], vjg, preferred_element_type=F32)
        o = acc[2 * g:2 * g + 1, :] + amap_cur[2 * g:2 * g + 1, :] * vc_ref[:, g * V_DIM:(g + 1) * V_DIM]
        outs.append(_subln(o, g_ref[...], out_scale))
    o_ref[...] = jnp.concatenate(outs, axis=1).astype(o_ref.dtype)


def _attn_sample_kernel(pt_ref, *refs, n_pages, out_scale):
    _attn_sample_body(pl.program_id(0), *refs, n_pages=n_pages, out_scale=out_scale)


def _attn_sample(pt_flat, lam, qt_s, kt_s, v_cur, subln_g, ck, cv, layer, out_scale):
    d_att, n = qt_s.shape
    n_pages = pt_flat.shape[0] // n
    P = ck.shape[-1]
    nl = min(n, LANES)
    vc3 = v_cur.reshape(n, 1, d_att)
    cols = lambda: pl.BlockSpec((d_att, nl), lambda b, pt: (0, b // nl))
    k_specs = [pl.BlockSpec((None, None, d_att, P), functools.partial(
        lambda b, pt, j: (layer, pt[b * n_pages + j], 0, 0), j=j)) for j in range(n_pages)]
    v_specs = [pl.BlockSpec((None, None, cv.shape[2], V_DIM), functools.partial(
        lambda b, pt, j: (layer, pt[b * n_pages + j], 0, 0), j=j)) for j in range(n_pages)]
    grid_spec = pltpu.PrefetchScalarGridSpec(
        num_scalar_prefetch=1,
        grid=(n,),
        in_specs=[pl.BlockSpec(memory_space=pltpu.SMEM), cols(), cols(),
                  pl.BlockSpec((None, 1, d_att), lambda b, pt: (b, 0, 0)),
                  pl.BlockSpec((1, V_DIM), lambda b, pt: (0, 0))] + k_specs + v_specs,
        out_specs=pl.BlockSpec((None, 1, d_att), lambda b, pt: (b, 0, 0)),
    )
    out = pl.pallas_call(
        functools.partial(_attn_sample_kernel, n_pages=n_pages, out_scale=out_scale),
        grid_spec=grid_spec,
        out_shape=jax.ShapeDtypeStruct((n, 1, d_att), BF16),
        compiler_params=_cparams(("arbitrary",)),
        name="diff_attn_sample",
    )(pt_flat, lam, qt_s, kt_s, vc3, subln_g, *([ck] * n_pages), *([cv] * n_pages))
    return out.reshape(n, d_att)


def _attn_fused_kernel(pt_ref, lam_ref, qt_ref, kb_ref, vt_ref, gcol_ref, qts_ref, kts_ref, vc_ref, g_ref, *rest,
                       n_pages, out_scale):
    pages = rest[:2 * n_pages]
    o_ref, os_ref, qcat, m, acc, s_even, s_odd = rest[2 * n_pages:]
    seq = pl.program_id(0) * pl.num_programs(1) + pl.program_id(1)
    sample = functools.partial(_attn_sample_body, seq, lam_ref, qts_ref, kts_ref, vc_ref, g_ref, *pages, os_ref,
                               n_pages=n_pages, out_scale=out_scale)
    _attn_kernel(lam_ref, qt_ref, kb_ref, vt_ref, gcol_ref, o_ref, qcat, m, acc, s_even, s_odd,
                 out_scale=out_scale, between=sample)


def _attn_prompt_and_sample(qt, kb, vt, subln_g, lam, out_scale, pt_flat, qt_s, kt_s, v_cur, ck, cv, layer):
    B, S, d_att = kb.shape
    G = d_att // V_DIM
    T = min(ATT_TILE, S)
    nq = S // T
    n = qt_s.shape[1]
    if n != B * G * nq:
        return (_attn_prompt(qt, kb, vt, subln_g, lam, out_scale),
                _attn_sample(pt_flat, lam, qt_s, kt_s, v_cur, subln_g, ck, cv, layer, out_scale))
    n_pages = pt_flat.shape[0] // n
    P = ck.shape[-1]
    nl = min(n, LANES)
    seq = lambda bg, i: bg * nq + i
    cols = lambda: pl.BlockSpec((d_att, nl), lambda bg, i, pt: (0, seq(bg, i) // nl))
    k_specs = [pl.BlockSpec((None, None, d_att, P), functools.partial(
        lambda bg, i, pt, j: (layer, pt[seq(bg, i) * n_pages + j], 0, 0), j=j)) for j in range(n_pages)]
    v_specs = [pl.BlockSpec((None, None, cv.shape[2], V_DIM), functools.partial(
        lambda bg, i, pt, j: (layer, pt[seq(bg, i) * n_pages + j], 0, 0), j=j)) for j in range(n_pages)]
    grid_spec = pltpu.PrefetchScalarGridSpec(
        num_scalar_prefetch=1,
        grid=(B * G, nq),
        in_specs=[
            pl.BlockSpec(memory_space=pltpu.SMEM),
            pl.BlockSpec((None, V_DIM, T), lambda bg, i, pt: (bg // G, bg % G, i)),
            pl.BlockSpec((None, S, V_DIM), lambda bg, i, pt: (bg // G, 0, bg % G)),
            pl.BlockSpec((None, V_DIM + ONES_ROWS, S), lambda bg, i, pt: (bg // G, bg % G, 0)),
            pl.BlockSpec((V_DIM, 1), lambda bg, i, pt: (0, 0)),
            cols(), cols(),
            pl.BlockSpec((None, 1, d_att), lambda bg, i, pt: (seq(bg, i), 0, 0)),
            pl.BlockSpec((1, V_DIM), lambda bg, i, pt: (0, 0)),
        ] + k_specs + v_specs,
        out_specs=[pl.BlockSpec((None, T, V_DIM), lambda bg, i, pt: (bg // G, i, bg % G)),
                   pl.BlockSpec((None, 1, d_att), lambda bg, i, pt: (seq(bg, i), 0, 0))],
        scratch_shapes=[pltpu.VMEM((V_DIM, 2 * T), BF16), pltpu.VMEM((1, 2 * T), F32),
                        pltpu.VMEM((V_DIM + ONES_ROWS, 2 * T), F32),
                        pltpu.VMEM((T, 2 * T), F32), pltpu.VMEM((T, 2 * T), F32)],
    )
    att, att_s = pl.pallas_call(
        functools.partial(_attn_fused_kernel, n_pages=n_pages, out_scale=out_scale),
        grid_spec=grid_spec,
        out_shape=[jax.ShapeDtypeStruct((B, S, d_att), BF16), jax.ShapeDtypeStruct((n, 1, d_att), BF16)],
        compiler_params=_cparams(("parallel", "arbitrary")),
        name="diff_attn_fused",
    )(pt_flat, lam, qt, kb, vt, subln_g.reshape(V_DIM, 1),
      qt_s, kt_s, v_cur.reshape(n, 1, d_att), subln_g,
      *([ck] * n_pages), *([cv] * n_pages))
    return att, att_s.reshape(n, d_att)


def _store_token_tiles(ref, x):
    T, D = x.shape
    n = D // LANES
    for c in range(n):
        ref[pl.ds(c, T, stride=n), :] = x[:, c * LANES:(c + 1) * LANES]


def _load_token_tiles(ref, T):
    n = ref.shape[0] // T
    return jnp.concatenate([ref[pl.ds(c, T, stride=n), :] for c in range(n)], axis=1)


def _outproj_kernel(yr_ref, at_ref, x_ref, wo_ref, g2_ref, wr_ref, br_ref, cnt_in_ref,
                    x1_ref, h2_ref, route_ref, route_t_ref, cnt_out_ref, cnt_scr, *, n_groups, per_group):
    step = pl.program_id(0)

    @pl.when(step == 0)
    def _():
        cnt_scr[...] = cnt_in_ref[...]

    cnt = _outproj_rows(pl.ds(0, x_ref.shape[0]), cnt_scr[...], yr_ref, at_ref, x_ref, wo_ref,
                        g2_ref, wr_ref, br_ref, x1_ref, h2_ref, route_ref, route_t_ref,
                        n_groups=n_groups, per_group=per_group)
    cnt_scr[...] = cnt
    cnt_out_ref[...] = cnt


def _outproj_rows(rows, cnt, yr_ref, at_ref, x_ref, wo_ref, g2_ref, wr_ref, br_ref,
                  x1_ref, h2_ref, route_ref, route_t_ref, *, n_groups, per_group):
    d_rnn = yr_ref.shape[1]
    y = (jnp.dot(yr_ref[rows, :], wo_ref[:d_rnn, :], preferred_element_type=F32)
         + jnp.dot(at_ref[rows, :], wo_ref[d_rnn:, :], preferred_element_type=F32))
    x1 = x_ref[rows, :] + y
    x1_ref[rows, :] = x1
    ms = jnp.mean(x1 * x1, axis=-1, keepdims=True)
    h2 = x1 * lax.rsqrt(ms + NORM_EPS) * g2_ref[...]
    T = x1.shape[0]
    n_tile = h2.shape[1] // LANES
    _store_token_tiles(h2_ref.at[pl.ds(rows.start * n_tile, T * n_tile)], h2)
    logits = jnp.dot(h2.astype(BF16), wr_ref[...], preferred_element_type=F32) + br_ref[...]
    lane = lax.broadcasted_iota(jnp.int32, (T, LANES), 1).astype(F32)
    first = lambda hit: jnp.min(jnp.where(hit, lane, float(LANES)), axis=-1, keepdims=True)
    gl = jnp.where(lane < n_groups, logits, NEG_INF)
    gmax = jnp.max(gl, axis=-1, keepdims=True)
    gidx = first(gl == gmax)
    pg = 1.0 / jnp.sum(jnp.exp(gl - gmax), axis=-1, keepdims=True)
    lo = n_groups + per_group * gidx
    el = jnp.where(jnp.logical_and(lane >= lo, lane < lo + per_group), logits, NEG_INF)
    ma = jnp.max(el, axis=-1, keepdims=True)
    ia = first(el == ma)
    el2 = jnp.where(lane == ia, NEG_INF, el)
    mb = jnp.max(el2, axis=-1, keepdims=True)
    ib = first(el2 == mb)
    eb = jnp.exp(mb - ma)
    gate_a = pg / (1.0 + eb)
    gate_b = pg * eb / (1.0 + eb)
    hit_a = lane == ia
    hit_b = lane == ib
    onehot = jnp.where(jnp.logical_or(hit_a, hit_b), 1.0, 0.0)
    r_i = lax.broadcasted_iota(jnp.int32, (T, T), 0)
    c_i = lax.broadcasted_iota(jnp.int32, (T, T), 1)
    lower = jnp.where(c_i < r_i, 1.0, 0.0).astype(BF16)
    before = jnp.dot(lower, onehot.astype(BF16), preferred_element_type=F32) + cnt
    rank_a = jnp.sum(jnp.where(hit_a, before, 0.0), axis=-1, keepdims=True)
    rank_b = jnp.sum(jnp.where(hit_b, before, 0.0), axis=-1, keepdims=True)
    cols = (ia - n_groups, ib - n_groups, gate_a, gate_b, rank_a, rank_b)
    route = jnp.zeros((T, LANES), F32)
    for c, val in enumerate(cols):
        route = jnp.where(lane == float(c), val, route)
    route_ref[rows, :] = route
    route_t_ref[:, rows] = route.T[:SUBLANES]
    return cnt + jnp.sum(onehot, axis=0, keepdims=True)


def _outproj_route(yr, at, x, wo, g2, wr, br, cnt_in, n_groups, per_group):
    N, D = x.shape
    T = min(ROW_TILE, N)
    d_rnn = yr.shape[1]
    row = lambda w: pl.BlockSpec((T, w), lambda i: (i, 0))
    full = lambda a: pl.BlockSpec(a.shape, lambda i: (0,) * a.ndim)
    return pl.pallas_call(
        functools.partial(_outproj_kernel, n_groups=n_groups, per_group=per_group),
        grid=(N // T,),
        in_specs=[row(d_rnn), row(at.shape[1]), row(D), full(wo), full(g2), full(wr), full(br), full(cnt_in)],
        out_specs=[row(D), pl.BlockSpec((T * (D // LANES), LANES), lambda i: (i, 0)), row(LANES),
                   pl.BlockSpec((SUBLANES, T), lambda i: (0, i)),
                   pl.BlockSpec((1, LANES), lambda i: (0, 0))],
        out_shape=[jax.ShapeDtypeStruct((N, D), F32), jax.ShapeDtypeStruct((N * (D // LANES), LANES), F32),
                   jax.ShapeDtypeStruct((N, LANES), F32), jax.ShapeDtypeStruct((SUBLANES, N), F32),
                   jax.ShapeDtypeStruct((1, LANES), F32)],
        scratch_shapes=[pltpu.VMEM((1, LANES), F32)],
        compiler_params=_cparams(("arbitrary",)),
        name="outproj_route",
    )(yr, at, x, wo, g2, wr, br, cnt_in)


def _dest_kernel(seg_ref, rt_ref, o_ref):
    e = rt_ref[0:TOP_K, :].astype(jnp.int32)
    start = jnp.zeros_like(e)
    for ex in range(seg_ref.shape[0]):
        start = jnp.where(e == ex, seg_ref[ex], start)
    o_ref[...] = start + rt_ref[4:4 + TOP_K, :].astype(jnp.int32)


def _dest(seg_start, route_t):
    R, N = route_t.shape
    T = min(ROW_TILE, N)
    grid_spec = pltpu.PrefetchScalarGridSpec(
        num_scalar_prefetch=1,
        grid=(N // T,),
        in_specs=[pl.BlockSpec((R, T), lambda i, seg: (0, i))],
        out_specs=pl.BlockSpec((TOP_K, T), lambda i, seg: (0, i)),
    )
    return pl.pallas_call(
        _dest_kernel,
        grid_spec=grid_spec,
        out_shape=jax.ShapeDtypeStruct((TOP_K, N), jnp.int32),
        compiler_params=_cparams(("arbitrary",)),
        name="moe_dest",
    )(seg_start, route_t)


def _dispatch_kernel(zb_ref, dest_ref, h_ref, *rest, zero_fill):
    xb_hbm, zbuf, sem = rest[-3], rest[-2], rest[-1]
    T = dest_ref.shape[1]
    n = h_ref.shape[0] // T

    if zero_fill:
        @pl.when(pl.program_id(0) == 0)
        def _():
            zbuf[...] = jnp.zeros_like(zbuf)

            def zcopy(i):
                row0 = pl.multiple_of(zb_ref[i] * (EXP_BLOCK * n), EXP_BLOCK * n)
                return pltpu.make_async_copy(zbuf, xb_hbm.at[pl.ds(row0, EXP_BLOCK * n)], sem)

            def zstart(i, c):
                @pl.when(zb_ref[i] >= 0)
                def _():
                    zcopy(i).start()
                return c

            def zwait(i, c):
                @pl.when(zb_ref[i] >= 0)
                def _():
                    zcopy(i).wait()
                return c

            lax.fori_loop(0, zb_ref.shape[0], zstart, 0)
            lax.fori_loop(0, zb_ref.shape[0], zwait, 0)

    def body(k, c):
        t0 = pl.multiple_of(k * SUBLANES, SUBLANES)
        group = h_ref.at[pl.ds(pl.multiple_of(t0 * n, SUBLANES * n), SUBLANES * n)]
        for j in range(SUBLANES):
            for s in range(TOP_K):
                row0 = pl.multiple_of(dest_ref[s, t0 + j] * n, n)
                pltpu.make_async_copy(group.at[pl.ds(j * n, n)],
                                      xb_hbm.at[pl.ds(row0, n)], sem).start(priority=s)
        return c

    lax.fori_loop(0, T // SUBLANES, body, 0, unroll=2)
    for s in range(TOP_K):
        pltpu.make_async_copy(h_ref, xb_hbm.at[pl.ds(0, T * n)], sem).wait()


def _dispatch(zb, dest, h, xb, xb_rows):
    N = dest.shape[1]
    n = h.shape[0] // N
    T = min(DISPATCH_TILE, N)
    dest3 = jnp.transpose(dest.reshape(TOP_K, N // T, T), (1, 0, 2))
    aliased = xb is not None
    in_specs = [pl.BlockSpec((None, TOP_K, T), lambda i, zb: (i, 0, 0), memory_space=pltpu.SMEM),
                pl.BlockSpec((T * n, LANES), lambda i, zb: (i, 0))]
    args = [zb, dest3, h]
    if aliased:
        in_specs.append(pl.BlockSpec(memory_space=pl.ANY))
        args.append(xb)
    grid_spec = pltpu.PrefetchScalarGridSpec(
        num_scalar_prefetch=1,
        grid=(N // T,),
        in_specs=in_specs,
        out_specs=pl.BlockSpec(memory_space=pl.ANY),
        scratch_shapes=[pltpu.VMEM((EXP_BLOCK * n, LANES), h.dtype), pltpu.SemaphoreType.DMA(())],
    )
    return pl.pallas_call(
        functools.partial(_dispatch_kernel, zero_fill=not aliased),
        grid_spec=grid_spec,
        out_shape=jax.ShapeDtypeStruct((xb_rows * n, LANES), h.dtype),
        input_output_aliases={3: 0} if aliased else {},
        compiler_params=pltpu.CompilerParams(dimension_semantics=("arbitrary",), has_side_effects=True),
        name="moe_dispatch",
    )(*args)


def _expert_kernel(be_ref, bi_ref, br_ref, bf_ref, x_ref, w1_ref, w2_ref, y_ref, w1b, w2b):
    b = pl.program_id(0)
    rows = br_ref[b]

    @pl.when(bf_ref[b] == 1)
    def _():
        w1b[...] = w1_ref[...].astype(BF16)
        w2b[...] = w2_ref[...].astype(BF16)

    @pl.when(rows > 0)
    def _():
        x = _load_token_tiles(x_ref, EXP_BLOCK).astype(BF16)
        h = jnp.dot(x, w1b[...], preferred_element_type=F32)
        de = h.shape[1] // 2
        act = (jax.nn.silu(h[:, :de]) * h[:, de:]).astype(BF16)
        _store_token_tiles(y_ref, jnp.dot(act, w2b[...], preferred_element_type=F32))

    @pl.when(rows == 0)
    def _():
        y_ref[...] = jnp.zeros_like(y_ref)


def _experts(be, bi, br, bf, xb, w1, w2, layer):
    nb = be.shape[0]
    D, d2 = w1.shape[-2:]
    de = w2.shape[-2]
    blk = EXP_BLOCK * (D // LANES)
    grid_spec = pltpu.PrefetchScalarGridSpec(
        num_scalar_prefetch=4,
        grid=(nb,),
        in_specs=[pl.BlockSpec((blk, LANES), lambda b, be, bi, br, bf: (bi[b], 0)),
                  pl.BlockSpec((None, None, D, d2), lambda b, be, bi, br, bf: (layer, be[b], 0, 0)),
                  pl.BlockSpec((None, None, de, D), lambda b, be, bi, br, bf: (layer, be[b], 0, 0))],
        out_specs=pl.BlockSpec((blk, LANES), lambda b, be, bi, br, bf: (b, 0)),
        scratch_shapes=[pltpu.VMEM((D, d2), BF16), pltpu.VMEM((de, D), BF16)],
    )
    return pl.pallas_call(
        _expert_kernel,
        grid_spec=grid_spec,
        out_shape=jax.ShapeDtypeStruct(xb.shape, F32),
        compiler_params=_cparams(("arbitrary",)),
        name="moe_experts",
    )(be, bi, br, bf, xb, w1, w2)


def _combine_kernel(dest_ref, dest_next_ref, x1_ref, route_ref, gf_ref, yb_hbm, o_ref, ybuf, sem, *, final):
    T = x1_ref.shape[0]
    n = ybuf.shape[2] // T
    i = pl.program_id(0)
    slot = i % 2

    def issue(dref, sl):
        def body(k, c):
            t0 = pl.multiple_of(k * SUBLANES, SUBLANES)
            for s in range(TOP_K):
                group = ybuf.at[sl, s, pl.ds(pl.multiple_of(t0 * n, SUBLANES * n), SUBLANES * n)]
                for j in range(SUBLANES):
                    row0 = pl.multiple_of(dref[s, t0 + j] * n, n)
                    pltpu.make_async_copy(yb_hbm.at[pl.ds(row0, n)],
                                          group.at[pl.ds(j * n, n)], sem.at[sl]).start(priority=s)
            return c

        lax.fori_loop(0, T // SUBLANES, body, 0, unroll=2)

    @pl.when(i == 0)
    def _():
        issue(dest_ref, 0)

    @pl.when(i + 1 < pl.num_programs(0))
    def _():
        issue(dest_next_ref, 1 - slot)

    for s in range(TOP_K):
        pltpu.make_async_copy(yb_hbm.at[pl.ds(0, T * n)], ybuf.at[slot, s], sem.at[slot]).wait()
    route = route_ref[...]
    out = (x1_ref[...] + _load_token_tiles(ybuf.at[slot, 0], T) * route[:, 2:3]
           + _load_token_tiles(ybuf.at[slot, 1], T) * route[:, 3:4])
    if final:
        ms = jnp.mean(out * out, axis=-1, keepdims=True)
        out = out * lax.rsqrt(ms + NORM_EPS) * gf_ref[...]
    o_ref[...] = out


def _combine(dest, x1, route, gf, yb, final):
    N, D = x1.shape
    T = min(COMBINE_TILE, N)
    dest3 = jnp.transpose(dest.reshape(TOP_K, N // T, T), (1, 0, 2))
    last = N // T - 1
    return pl.pallas_call(
        functools.partial(_combine_kernel, final=final),
        grid=(N // T,),
        in_specs=[pl.BlockSpec((None, TOP_K, T), lambda i: (i, 0, 0), memory_space=pltpu.SMEM),
                  pl.BlockSpec((None, TOP_K, T), lambda i: (jnp.minimum(i + 1, last), 0, 0),
                               memory_space=pltpu.SMEM),
                  pl.BlockSpec((T, D), lambda i: (i, 0)),
                  pl.BlockSpec((T, LANES), lambda i: (i, 0)),
                  pl.BlockSpec((1, D), lambda i: (0, 0)),
                  pl.BlockSpec(memory_space=pl.ANY)],
        out_specs=pl.BlockSpec((T, D), lambda i: (i, 0)),
        out_shape=jax.ShapeDtypeStruct((N, D), F32),
        scratch_shapes=[pltpu.VMEM((2, TOP_K, T * (D // LANES), LANES), F32), pltpu.SemaphoreType.DMA((2,))],
        compiler_params=_cparams(("arbitrary",)),
        name="moe_combine",
    )(dest3, dest3, x1, route, gf, yb)


def _rope_tables(pos):
    inv = ROPE_THETA ** (-jnp.arange(HALF, dtype=F32) / HALF)
    ang = pos.astype(F32)[:, None] * inv[None, :]
    cos, sin = jnp.cos(ang), jnp.sin(ang)
    cq = jnp.tile(cos, (1, LANES // HALF))
    sq = jnp.concatenate([-sin, -sin, sin, sin], axis=1)
    return cq, sq, cos.T, sin.T


def _attention_order(w):
    D, d_att = w.shape
    w = w.reshape(D, d_att // V_DIM, 2, 2, HALF)
    return jnp.swapaxes(w, 2, 3).reshape(D, d_att)


def _gate_weights(w_ga, w_gx):
    nb, bw, _ = w_ga.shape
    half = nb // 2
    eye = jnp.eye(half, dtype=w_ga.dtype)

    def dense(w):
        return jnp.einsum('nij,nm->nimj', w, eye).reshape(half * bw, half * bw)

    return jnp.stack([jnp.concatenate([dense(w_ga[h * half:(h + 1) * half]),
                                       dense(w_gx[h * half:(h + 1) * half])], axis=1)
                      for h in range(2)]).astype(BF16)


def _block_tables(counts, nb):
    ne = counts.shape[0]
    nblk = (counts + EXP_BLOCK - 1) // EXP_BLOCK
    cum = jnp.sum(jnp.where(jnp.arange(ne)[None, :] <= jnp.arange(ne)[:, None], nblk[None, :], 0), axis=1)
    total = cum[-1]
    b = jnp.arange(nb, dtype=jnp.int32)
    bc = jnp.minimum(b, total - 1)
    e = jnp.minimum(jnp.sum(bc[:, None] >= cum[None, :], axis=1), ne - 1).astype(jnp.int32)
    pick = lambda tab: jnp.sum(jnp.where(e[:, None] == jnp.arange(ne)[None, :], tab[None, :], 0), axis=1)
    off = bc - (pick(cum) - pick(nblk))
    valid = b < total
    rows = jnp.where(valid, jnp.clip(pick(counts) - off * EXP_BLOCK, 0, EXP_BLOCK), 0)
    first = jnp.where(jnp.logical_and(valid, off == 0), 1, 0)
    last_blk = jnp.where(nblk > 0, cum - 1, -1)
    trail = total + jnp.arange(ne, dtype=jnp.int32)
    trail = jnp.where(trail < nb, trail, -1)
    zb = jnp.concatenate([last_blk, trail]).astype(jnp.int32)
    seg_start = ((cum - nblk) * EXP_BLOCK).astype(jnp.int32)
    return seg_start, e, bc.astype(jnp.int32), rows.astype(jnp.int32), first.astype(jnp.int32), zb


def kernel(x_prompt, x_sample, cache_k, cache_v, state_h, state_conv, page_table, ln1_g, w_in, conv_w, conv_b,
           w_ga, b_ga, w_gx, b_gx, lru_lambda, lam_q1, lam_k1, lam_q2, lam_k2, subln_g, w_out, ln2_g,
           w_route_group, b_route_group, w_route_expert, b_route_expert, w_exp_in, w_exp_out, ln_f):
    Bp, Sp, D = x_prompt.shape
    Bs = x_sample.shape[0]
    depth = w_in.shape[0]
    d_rnn = state_h.shape[-1]
    d_att = cache_k.shape[-2] * cache_k.shape[-1]
    n_groups = w_route_group.shape[-1]
    n_experts = w_route_expert.shape[-1]
    per_group = n_experts // n_groups
    n_pool, page = cache_k.shape[1], cache_k.shape[2]
    n_pages = page_table.shape[1]
    past_len = n_pages * page
    Np = Bp * Sp
    assert x_sample.shape[1] == 1 and n_groups + n_experts <= LANES

    cq, sq, ct, st = _rope_tables(jnp.arange(Sp))
    cq_s, sq_s, ct_s, st_s = _rope_tables(jnp.full((Bs,), past_len))
    qscale = HEAD_DIM ** -0.5
    ck = jnp.transpose(cache_k, (0, 1, 3, 4, 2)).reshape(depth, n_pool, d_att, page)
    cv = cache_v.reshape(depth, n_pool, page * (d_att // V_DIM), V_DIM)
    pt_flat = page_table.reshape(-1).astype(jnp.int32)
    conv_s = jnp.transpose(state_conv, (0, 2, 1, 3))

    n_blocks = (TOP_K * (Np + Bs)) // EXP_BLOCK + n_experts

    xp = x_prompt
    xs = x_sample.reshape(Bs, D)
    kt_all = v_all = None
    hp_l, cp_l, ks_l, vs_l, hs_l, cs_l = [], [], [], [], [], []
    for l in range(depth):
        lambda_init = 0.8 - 0.6 * math.exp(-0.3 * l)
        out_scale = 1.0 - lambda_init
        lam = (jnp.exp(jnp.sum(lam_q1[l] * lam_k1[l])) - jnp.exp(jnp.sum(lam_q2[l] * lam_k2[l]))
               + lambda_init).reshape(1, 1).astype(F32)
        wl = w_in[l]
        w_xg = wl[:, :2 * d_rnn]
        w_q = wl[:, 2 * d_rnn:2 * d_rnn + d_att] * qscale
        w_k = wl[:, 2 * d_rnn + d_att:2 * d_rnn + 2 * d_att]
        w_v = wl[:, 2 * d_rnn + 2 * d_att:]
        wm_p = jnp.concatenate([w_xg, _attention_order(w_k), w_v], axis=1).astype(BF16)
        wt_p = jnp.concatenate([_attention_order(w_q), w_v], axis=1).T.astype(BF16)
        g1 = ln1_g[l].reshape(1, D)
        wg = _gate_weights(w_ga[l], w_gx[l])
        bga = b_ga[l].reshape(1, d_rnn)
        bgx = b_gx[l].reshape(1, d_rnn)
        lru_l = lru_lambda[l].reshape(1, d_rnn)
        cw = conv_w[l]
        cb = conv_b[l].reshape(1, d_rnn)
        sg = subln_g[l].reshape(1, V_DIM)
        wo = w_out[l].astype(BF16)
        g2 = ln2_g[l].reshape(1, D)
        n_route = n_groups + n_experts
        wr = jnp.concatenate([w_route_group[l], w_route_expert[l], jnp.zeros((D, LANES - n_route), F32)],
                             axis=1).astype(BF16)
        br = jnp.concatenate([b_route_group[l], b_route_expert[l], jnp.zeros((LANES - n_route,), F32)]
                             ).reshape(1, LANES)

        xr, gr, kb, v_new, qt, kt_new, vt = _inproj_prompt(xp, g1, wm_p, wt_p, cq, sq, ct, st, None, None)
        if kt_all is None:
            y_rnn, h_last, c_tail = _lru_prompt(xr, gr, cw, cb, wg, bga, bgx, lru_l)
            kt_all, v_all = kt_new, v_new
        else:
            y_rnn, h_last, c_tail, kt_all, v_all = _lru_prompt(xr, gr, cw, cb, wg, bga, bgx, lru_l,
                                                               (kt_all, v_all, kt_new, v_new))
        xr_s, gr_s, v_s, kt_s, qt_s = _inproj_sample(xs, g1, wm_p, wt_p, cq_s, sq_s, ct_s, st_s)
        y_rnn_s, h_s, c_s = _lru_step(xr_s, gr_s, conv_s[l], state_h[l], cw, cb, wg, bga, bgx, lru_l)
        k_s = kt_s.T
        att, att_s = _attn_prompt_and_sample(qt, kb, vt, sg, lam, out_scale,
                                             pt_flat, qt_s, kt_s, v_s, ck, cv, l)
        cnt0 = jnp.zeros((1, LANES), F32)
        x1, h2, route, route_t, cnt1 = _outproj_route(
            y_rnn.reshape(Np, d_rnn), att.reshape(Np, d_att), xp.reshape(Np, D), wo, g2, wr, br, cnt0,
            n_groups, per_group)
        x1_s, h2_s, route_s, route_ts, cnt2 = _outproj_route(y_rnn_s, att_s, xs, wo, g2, wr, br, cnt1,
                                                             n_groups, per_group)
        counts = cnt2[0, n_groups:n_groups + n_experts].astype(jnp.int32)
        seg_start, be, bi, brows, bfirst, zb = _block_tables(counts, n_blocks)

        dest_p, dest_s = _dest(seg_start, route_t), _dest(seg_start, route_ts)
        xb = _dispatch(zb, dest_p, h2, None, n_blocks * EXP_BLOCK)
        xb = _dispatch(zb, dest_s, h2_s, xb, n_blocks * EXP_BLOCK)
        yb = _experts(be, bi, brows, bfirst, xb, w_exp_in, w_exp_out, l)
        final = l == depth - 1
        gf = ln_f.reshape(1, D)
        xp_new = _combine(dest_p, x1, route, gf, yb, final)
        xs = _combine(dest_s, x1_s, route_s, gf, yb, final)
        xp = xp_new.reshape(Bp, Sp, D)

        n_qk = d_att // HEAD_DIM
        hp_l.append(h_last.reshape(Bp, d_rnn))
        cp_l.append(c_tail)
        ks_l.append(k_s.reshape(Bs, 1, n_qk, HEAD_DIM))
        vs_l.append(v_s.reshape(Bs, 1, d_att // V_DIM, V_DIM))
        hs_l.append(h_s)
        cs_l.append(jnp.transpose(c_s, (1, 0, 2)))
    k_prompt = jnp.transpose(kt_all.reshape(depth, Bp, n_qk, HEAD_DIM, Sp), (0, 1, 4, 2, 3))
    v_prompt = v_all.reshape(depth, Bp, Sp, d_att // V_DIM, V_DIM)
    return (xp, xs.reshape(Bs, 1, D),
            k_prompt, v_prompt, jnp.stack(hp_l), jnp.stack(cp_l),
            jnp.stack(ks_l), jnp.stack(vs_l), jnp.stack(hs_l), jnp.stack(cs_l))
```
